```python
import math, functools
import jax, jax.numpy as jnp
from jax import lax
import numpy as np

D_MODEL = 1024
BATCH = 4
SEQ = 4096
DEPTH = 1
DEC_BATCH = 32
DEC_SEQ = 8
PAST_LEN = 8192
PAGE_SIZE = 128

N_META = 16
DH_SB = 64
H_SB = (D_MODEL // 2) // DH_SB
W_SB = H_SB * DH_SB
DH_DIFF = 64
DV_DIFF = 2 * DH_DIFF
H_DIFF = (D_MODEL // 2) // DV_DIFF
W_DIFF = H_DIFF * DV_DIFF
W_MIX = W_SB + W_DIFF
IN_COLS = 3 * W_SB + 3 * W_DIFF
N_BUCKETS = 32
MAX_DISTANCE = 128
Q_BLOCK = 128
N_EXPERTS = 64
TOP_K = 8
N_GROUPS = 8
TOPK_GROUPS = 4
D_EXPERT = 256
D_SHARED = 256
ROUTED_SCALE = 2.5
MOE_BLOCK = 512
EPS = 1e-6

kernel_name = 'hymba_stickbreak_diffattn_moe_step'

F32 = jnp.float32


def rms_norm(x, g):
    xf = x.astype(F32)
    y = xf * lax.rsqrt(jnp.mean(xf * xf, axis=-1, keepdims=True) + EPS)
    return (y * g.astype(F32)).astype(x.dtype)


def rel_bucket(rel):
    n = jnp.maximum(rel, 0)
    max_exact = N_BUCKETS // 2
    nf = jnp.maximum(n, 1).astype(F32)
    large = max_exact + (jnp.log(nf / max_exact) / math.log(MAX_DISTANCE / max_exact)
                         * (N_BUCKETS - max_exact)).astype(jnp.int32)
    large = jnp.minimum(large, N_BUCKETS - 1)
    return jnp.where(n < max_exact, n, large)


def sb_block(q, q_pos, kv, k_pos):
    k, v = kv
    z = jnp.einsum('bqhd,bkhd->bhqk', q, k, preferred_element_type=F32) * (DH_SB ** -0.5)
    vis = (k_pos[None, :] < q_pos[:, None])[None, None]
    log_keep = jnp.where(vis, jax.nn.log_sigmoid(-z), 0.0)
    between = lax.cumsum(log_keep, axis=3, reverse=True) - log_keep
    w = jnp.where(vis, jnp.exp(jax.nn.log_sigmoid(z) + between), 0.0)
    return jnp.einsum('bhqk,bkhd->bqhd', w, v.astype(F32))


def diff_block(q, q_pos, kv, k_pos, bias_table, lam):
    k, v = kv
    s = jnp.einsum('bqhmd,bkhmd->bhmqk', q, k, preferred_element_type=F32) * (DH_DIFF ** -0.5)
    rel = q_pos[:, None] - k_pos[None, :]
    bias = jnp.transpose(bias_table.astype(F32)[rel_bucket(rel)], (2, 0, 1))
    s = jnp.where((rel >= 0)[None, None, None], s + bias[None, :, None], -jnp.inf)
    p = jax.nn.softmax(s, axis=-1)
    a = p[:, :, 0] - lam * p[:, :, 1]
    return jnp.einsum('bhqk,bkhd->bqhd', a, v.astype(F32))


def sweep(block_fn, q, q_pos, kv, k_pos):
    b, tq = q.shape[:2]
    qb = Q_BLOCK if tq % Q_BLOCK == 0 else tq
    nb = tq // qb
    qs = jnp.moveaxis(q.reshape((b, nb, qb) + q.shape[2:]), 1, 0)
    ps = q_pos.reshape(nb, qb)
    out = lax.map(lambda a: block_fn(a[0], a[1], kv, k_pos), (qs, ps))
    out = jnp.moveaxis(out, 0, 1)
    return out.reshape((b, tq) + out.shape[3:])


def attention_core(q_sb, q_df, q_pos, k_sb, v_sb, k_df, v_df, k_pos, bias_table, lam):
    o_sb = sweep(sb_block, q_sb, q_pos, (k_sb, v_sb), k_pos)
    o_df = sweep(functools.partial(diff_block, bias_table=bias_table, lam=lam),
                 q_df, q_pos, (k_df, v_df), k_pos)
    return o_sb, o_df


def project(h, l, p):
    b, t, _ = h.shape
    proj = jnp.einsum('btd,dc->btc', h, p['w_in'][l])
    cuts = [W_SB, 2 * W_SB, 3 * W_SB, 3 * W_SB + W_DIFF, 3 * W_SB + 2 * W_DIFF]
    q_sb, k_sb, v_sb, q_df, k_df, v_df = jnp.split(proj, cuts, axis=-1)
    q_df = rms_norm(q_df.reshape(b, t, H_DIFF, 2, DH_DIFF), p['q_norm_g'][l])
    k_df = rms_norm(k_df.reshape(b, t, H_DIFF, 2, DH_DIFF), p['k_norm_g'][l])
    return (q_sb.reshape(b, t, H_SB, DH_SB), k_sb.reshape(b, t, H_SB, DH_SB),
            v_sb.reshape(b, t, H_SB, DH_SB), q_df, k_df, v_df.reshape(b, t, H_DIFF, DV_DIFF))


def moe_ffn(x, l, p):
    n = x.shape[0]
    scores = jax.nn.sigmoid(jnp.dot(x, p['router_w'][l], preferred_element_type=F32))
    sel = scores + p['router_bias'][l].astype(F32)
    grp_score = lax.top_k(sel.reshape(n, N_GROUPS, N_EXPERTS // N_GROUPS), 2)[0].sum(-1)
    _, gidx = lax.top_k(grp_score, TOPK_GROUPS)
    gmask = jax.nn.one_hot(gidx, N_GROUPS, dtype=F32).sum(axis=1) > 0
    emask = jnp.repeat(gmask, N_EXPERTS // N_GROUPS, axis=1)
    _, eidx = lax.top_k(jnp.where(emask, sel, -jnp.inf), TOP_K)
    w = jnp.take_along_axis(scores, eidx, axis=1)
    w = w / jnp.sum(w, axis=-1, keepdims=True) * ROUTED_SCALE
    gates = jnp.einsum('nk,nke->ne', w, jax.nn.one_hot(eidx, N_EXPERTS, dtype=F32)).astype(x.dtype)
    blk = MOE_BLOCK if n >= MOE_BLOCK else n
    pad = (-n) % blk
    xp = jnp.pad(x, ((0, pad), (0, 0))).reshape(-1, blk, D_MODEL)
    gp = jnp.pad(gates, ((0, pad), (0, 0))).reshape(-1, blk, N_EXPERTS)
    w_gate, w_up, w_down = p['w_gate'][l], p['w_up'][l], p['w_down'][l]

    def run(args):
        xb, gb = args
        g = jnp.einsum('md,edf->mef', xb, w_gate)
        u = jnp.einsum('md,edf->mef', xb, w_up)
        a = jax.nn.silu(g) * u * gb[..., None]
        return jnp.einsum('mef,efd->md', a, w_down)

    routed = lax.map(run, (xp, gp)).reshape(-1, D_MODEL)[:n]
    shared = jnp.dot(jax.nn.silu(jnp.dot(x, p['w_shared_gate'][l])) * jnp.dot(x, p['w_shared_up'][l]),
                     p['w_shared_down'][l])
    return routed + shared


def layer_forward(x, l, past, n_lead, p):
    b, t, _ = x.shape
    p0 = 0 if past is None else past[0].shape[1]
    q_pos = p0 + jnp.arange(t, dtype=jnp.int32)
    k_pos = jnp.arange(p0 + t, dtype=jnp.int32)
    h = rms_norm(x, p['norm1_g'][l])
    q_sb, k_sb, v_sb, q_df, k_df, v_df = project(h, l, p)
    new_rows = (k_sb, v_sb, k_df.reshape(b, t, H_DIFF, 2 * DH_DIFF), v_df)
    if past is None:
        keys = new_rows
    else:
        keys = tuple(jnp.concatenate([pr.astype(nr.dtype), nr], axis=1) for pr, nr in zip(past, new_rows))
    ks, vs, kd, vd = keys
    kd = kd.reshape(b, -1, H_DIFF, 2, DH_DIFF)
    lam_init = 0.8 - 0.6 * math.exp(-0.3 * l)
    lam = (jnp.exp(jnp.sum(p['lambda_q1'][l].astype(F32) * p['lambda_k1'][l].astype(F32)))
           - jnp.exp(jnp.sum(p['lambda_q2'][l].astype(F32) * p['lambda_k2'][l].astype(F32))) + lam_init)
    bias_table = p['rel_bias']

    def core(lo, hi, kmax):
        return attention_core(q_sb[:, lo:hi], q_df[:, lo:hi], q_pos[lo:hi], ks[:, :kmax], vs[:, :kmax],
                              kd[:, :kmax], vd[:, :kmax], k_pos[:kmax], bias_table, lam)

    if n_lead > 0:
        lead_sb, lead_df = core(0, n_lead, p0 + n_lead)
        rest_sb, rest_df = core(n_lead, t, p0 + t)
        o_sb = jnp.concatenate([lead_sb, rest_sb], axis=1)
        o_df = jnp.concatenate([lead_df, rest_df], axis=1)
    else:
        o_sb, o_df = core(0, t, p0 + t)
    o_sb = rms_norm(o_sb, p['sb_out_g'][l].reshape(H_SB, DH_SB)).astype(x.dtype)
    o_df = rms_norm(o_df, p['diff_subln_g'][l] * (1.0 - lam_init)).astype(x.dtype)
    mixed = jnp.concatenate([o_sb.reshape(b, t, W_SB), o_df.reshape(b, t, W_DIFF)], axis=-1)
    x = x + jnp.einsum('btc,cd->btd', mixed, p['w_out'][l])
    h2 = rms_norm(x, p['norm2_g'][l]).reshape(b * t, D_MODEL)
    x = x + moe_ffn(h2, l, p).reshape(b, t, D_MODEL)
    return x, new_rows


def setup_inputs(seed: int = 0) -> dict:
    key = jax.random.key(seed)
    ks = jax.random.split(key, 40)
    n_pages = PAST_LEN // PAGE_SIZE
    n_pool = (DEC_BATCH * n_pages * 5) // 4
    nrm = lambda k, shape, s=1.0: jax.random.normal(k, shape, F32) * s
    gain = lambda k, shape: 1.0 + 0.01 * jax.random.normal(k, shape, F32)
    page_table = jax.random.permutation(ks[6], n_pool)[:DEC_BATCH * n_pages].reshape(DEC_BATCH, n_pages).astype(jnp.int32)
    return {
        'x_prompt': nrm(ks[0], (BATCH, SEQ, D_MODEL)),
        'x_sample': nrm(ks[1], (DEC_BATCH, DEC_SEQ, D_MODEL)),
        'cache_k_sb': nrm(ks[2], (DEPTH, n_pool, PAGE_SIZE, H_SB, DH_SB)),
        'cache_v_sb': nrm(ks[3], (DEPTH, n_pool, PAGE_SIZE, H_SB, DH_SB)),
        'cache_k_diff': nrm(ks[4], (DEPTH, n_pool, PAGE_SIZE, H_DIFF, 2 * DH_DIFF)),
        'cache_v_diff': nrm(ks[5], (DEPTH, n_pool, PAGE_SIZE, H_DIFF, DV_DIFF)),
        'page_table': page_table,
        'meta_tokens': nrm(ks[7], (N_META, D_MODEL)),
        'rel_bias': nrm(ks[8], (N_BUCKETS, H_DIFF), 0.5),
        'norm1_g': gain(ks[9], (DEPTH, D_MODEL)),
        'w_in': nrm(ks[10], (DEPTH, D_MODEL, IN_COLS), D_MODEL ** -0.5),
        'q_norm_g': gain(ks[11], (DEPTH, DH_DIFF)),
        'k_norm_g': gain(ks[12], (DEPTH, DH_DIFF)),
        'lambda_q1': nrm(ks[13], (DEPTH, DH_DIFF), 0.1),
        'lambda_k1': nrm(ks[14], (DEPTH, DH_DIFF), 0.1),
        'lambda_q2': nrm(ks[15], (DEPTH, DH_DIFF), 0.1),
        'lambda_k2': nrm(ks[16], (DEPTH, DH_DIFF), 0.1),
        'sb_out_g': gain(ks[17], (DEPTH, W_SB)),
        'diff_subln_g': gain(ks[18], (DEPTH, DV_DIFF)),
        'w_out': nrm(ks[19], (DEPTH, W_MIX, D_MODEL), W_MIX ** -0.5),
        'norm2_g': gain(ks[20], (DEPTH, D_MODEL)),
        'router_w': nrm(ks[21], (DEPTH, D_MODEL, N_EXPERTS), D_MODEL ** -0.5),
        'router_bias': nrm(ks[22], (DEPTH, N_EXPERTS), 0.01),
        'w_gate': nrm(ks[23], (DEPTH, N_EXPERTS, D_MODEL, D_EXPERT), D_MODEL ** -0.5),
        'w_up': nrm(ks[24], (DEPTH, N_EXPERTS, D_MODEL, D_EXPERT), D_MODEL ** -0.5),
        'w_down': nrm(ks[25], (DEPTH, N_EXPERTS, D_EXPERT, D_MODEL), D_EXPERT ** -0.5),
        'w_shared_gate': nrm(ks[26], (DEPTH, D_MODEL, D_SHARED), D_MODEL ** -0.5),
        'w_shared_up': nrm(ks[27], (DEPTH, D_MODEL, D_SHARED), D_MODEL ** -0.5),
        'w_shared_down': nrm(ks[28], (DEPTH, D_SHARED, D_MODEL), D_SHARED ** -0.5),
    }


def reference(x_prompt, x_sample, cache_k_sb, cache_v_sb, cache_k_diff, cache_v_diff, page_table,
              meta_tokens, rel_bias, norm1_g, w_in, q_norm_g, k_norm_g, lambda_q1, lambda_k1,
              lambda_q2, lambda_k2, sb_out_g, diff_subln_g, w_out, norm2_g, router_w, router_bias,
              w_gate, w_up, w_down, w_shared_gate, w_shared_up, w_shared_down):
    p = dict(rel_bias=rel_bias, norm1_g=norm1_g, w_in=w_in, q_norm_g=q_norm_g, k_norm_g=k_norm_g,
             lambda_q1=lambda_q1, lambda_k1=lambda_k1, lambda_q2=lambda_q2, lambda_k2=lambda_k2,
             sb_out_g=sb_out_g, diff_subln_g=diff_subln_g, w_out=w_out, norm2_g=norm2_g,
             router_w=router_w, router_bias=router_bias, w_gate=w_gate, w_up=w_up, w_down=w_down,
             w_shared_gate=w_shared_gate, w_shared_up=w_shared_up, w_shared_down=w_shared_down)
    b = x_prompt.shape[0]
    dec_b, n_pages = page_table.shape

    def gather(cache_l):
        g = cache_l[page_table]
        return g.reshape((dec_b, n_pages * g.shape[2]) + g.shape[3:])

    meta = jnp.broadcast_to(meta_tokens.astype(x_prompt.dtype)[None], (b, N_META, D_MODEL))
    xp = jnp.concatenate([meta, x_prompt], axis=1)
    xs = x_sample
    rows_p = [[], [], [], []]
    rows_s = [[], [], [], []]
    for l in range(DEPTH):
        xp, new_p = layer_forward(xp, l, None, N_META, p)
        past = (gather(cache_k_sb[l]), gather(cache_v_sb[l]), gather(cache_k_diff[l]), gather(cache_v_diff[l]))
        xs, new_s = layer_forward(xs, l, past, 0, p)
        for i in range(4):
            rows_p[i].append(new_p[i])
            rows_s[i].append(new_s[i])
    y_prompt = xp[:, N_META:]
    y_sample = xs
    nk_sb_p, nv_sb_p, nk_df_p, nv_df_p = [jnp.stack(r, axis=0) for r in rows_p]
    nk_sb_s, nv_sb_s, nk_df_s, nv_df_s = [jnp.stack(r, axis=0) for r in rows_s]
    return (y_prompt, y_sample, nk_sb_p, nv_sb_p, nk_df_p, nv_df_p, nk_sb_s, nv_sb_s, nk_df_s, nv_df_s)
```

```python
import functools
import math

import jax
import jax.numpy as jnp
from jax import lax
from jax.experimental import pallas as pl
from jax.experimental.pallas import tpu as pltpu

F32 = jnp.float32
BF16 = jnp.bfloat16
I32 = jnp.int32

EPS = 1e-6
DH = 64
LANES = 128
MAX_DISTANCE = 128
TOP_K = 8
N_GROUPS = 8
TOPK_GROUPS = 4
ROUTED_SCALE = 2.5
NEG = -1e30
ATT_BLOCK = 256
VMEM_LIMIT = 48 * 1024 * 1024

_TRANS_B = (((1,), (1,)), ((), ()))


def _dot(a, b):
    return jnp.dot(a, b, preferred_element_type=F32)


def _dot_tb(a, b):
    return lax.dot_general(a, b, _TRANS_B, preferred_element_type=F32)


def _split_bf16(x):
    hi = x.astype(BF16)
    lo = (x - hi.astype(F32)).astype(BF16)
    return hi, lo


def _round_up(n, m):
    return (n + m - 1) // m * m


def _proj_body(x_ref, g1_ref, w_ref, qg_ref, kg_ref, gm_ref,
               ksb_o, vsb_o, kdf_o, vdf_o,
               qsb_b, ksb_b, vsb_b, qdf_b, kdf_b, vdf_b, *, width):
    x = x_ref[0]
    ms = jnp.mean(x * x, axis=-1, keepdims=True)
    h = (x * lax.rsqrt(ms + EPS)) * g1_ref[...]
    proj = _dot(h.astype(BF16), w_ref[...])
    w = width
    q_sb, k_sb, v_sb = proj[:, 0:w], proj[:, w:2 * w], proj[:, 2 * w:3 * w]
    q_df, k_df, v_df = proj[:, 3 * w:4 * w], proj[:, 4 * w:5 * w], proj[:, 5 * w:6 * w]

    def map_norm(t, g_ref):
        msq = _dot((t * t).astype(BF16), gm_ref[...])
        return (t * lax.rsqrt(msq + EPS)) * g_ref[...]

    q_df = map_norm(q_df, qg_ref)
    k_df = map_norm(k_df, kg_ref)
    ksb_o[0] = k_sb
    vsb_o[0] = v_sb
    kdf_o[0] = k_df
    vdf_o[0] = v_df
    scale = DH ** -0.5
    qsb_b[0] = (q_sb * scale).astype(BF16)
    ksb_b[0] = k_sb.astype(BF16)
    vsb_b[0] = v_sb.astype(BF16)
    qdf_b[0] = (q_df * scale).astype(BF16)
    kdf_b[0] = k_df.astype(BF16)
    vdf_b[0] = v_df.astype(BF16)


def _project(x_pad, t_valid, g1, w_in_b, qg, kg, gm, tm):
    b, tp, d = x_pad.shape
    width = w_in_b.shape[1] // 6
    grid = (b, tp // tm)
    row = lambda bi, i: (bi, i, 0)
    const = lambda bi, i: (0, 0)
    f32_out = jax.ShapeDtypeStruct((b, t_valid, width), F32)
    bf_out = jax.ShapeDtypeStruct((b, tp, width), BF16)
    out_spec = pl.BlockSpec((1, tm, width), row)
    return pl.pallas_call(
        functools.partial(_proj_body, width=width),
        grid=grid,
        in_specs=[
            pl.BlockSpec((1, tm, d), row),
            pl.BlockSpec((1, d), const),
            pl.BlockSpec(w_in_b.shape, const),
            pl.BlockSpec((1, width), const),
            pl.BlockSpec((1, width), const),
            pl.BlockSpec((width, width), const),
        ],
        out_specs=[out_spec] * 10,
        out_shape=[f32_out] * 4 + [bf_out] * 6,
        compiler_params=pltpu.CompilerParams(
            dimension_semantics=("arbitrary", "arbitrary"),
            vmem_limit_bytes=VMEM_LIMIT),
        name="proj",
    )(x_pad, g1, w_in_b, qg, kg, gm)


def _stack_halves(q):
    lane = lax.broadcasted_iota(I32, (1, LANES), 1)
    zero = jnp.zeros_like(q)
    return jnp.concatenate(
        [jnp.where(lane < DH, q, zero), jnp.where(lane >= DH, q, zero)], axis=0)


def _neg_softplus(z):
    return -(jnp.maximum(z, 0.0) + jnp.log(1.0 + jnp.exp(-jnp.abs(z))))


def _sb_block(qq, k, v, tri, acc, carry, vis):
    z = _dot_tb(qq, k)
    log_keep = _neg_softplus(z)
    if vis is not None:
        log_keep = jnp.where(vis, log_keep, 0.0)
    hi, lo = _split_bf16(log_keep)
    csum = _dot(hi, tri) + _dot(lo, tri) + carry
    w = jnp.exp(z + csum)
    if vis is not None:
        w = jnp.where(vis, w, 0.0)
    acc = acc + _dot(w.astype(BF16), v)
    return acc, csum[:, 0:1]


def _sb_finish(acc, t, g):
    lane = lax.broadcasted_iota(I32, (1, LANES), 1)
    lo_half = lane < DH
    o = jnp.where(lo_half, acc[:t], acc[t:])
    o2 = o * o
    s_lo = jnp.sum(jnp.where(lo_half, o2, 0.0), axis=-1, keepdims=True)
    s_hi = jnp.sum(jnp.where(lo_half, 0.0, o2), axis=-1, keepdims=True)
    ms = jnp.where(lo_half, s_lo, s_hi) * (1.0 / DH)
    return (o * lax.rsqrt(ms + EPS)) * g


def _df_block(qq, k, v, bias, m, l, acc):
    s = _dot_tb(qq, k) + bias
    m_new = jnp.maximum(m, jnp.max(s, axis=-1, keepdims=True))
    alpha = jnp.exp(m - m_new)
    p = jnp.exp(s - m_new)
    l = alpha * l + jnp.sum(p, axis=-1, keepdims=True)
    acc = alpha * acc + _dot(p.astype(BF16), v)
    return m_new, l, acc


def _lambda(lq1, lk1, lq2, lk2, lam_init):
    s1 = jnp.sum(lq1[...] * lk1[...], axis=-1, keepdims=True)
    s2 = jnp.sum(lq2[...] * lk2[...], axis=-1, keepdims=True)
    return jnp.exp(s1) - jnp.exp(s2) + lam_init


def _df_finish(acc, l, t, lam, g, lam_init):
    o = acc[:t] / l[:t] - lam * (acc[t:] / l[t:])
    ms = jnp.mean(o * o, axis=-1, keepdims=True)
    return (o * lax.rsqrt(ms + EPS)) * (g * (1.0 - lam_init))


def _bias_tile(tab_ref, h, rel, n_buckets):
    max_exact = n_buckets // 2
    n = jnp.maximum(rel, 0)
    nf = jnp.maximum(n, 1).astype(F32)
    large = max_exact + (jnp.log(nf / max_exact) / math.log(MAX_DISTANCE / max_exact)
                         * (n_buckets - max_exact)).astype(I32)
    large = jnp.minimum(large, n_buckets - 1)
    bucket = jnp.where(n < max_exact, n, large)
    bias = jnp.zeros(rel.shape, F32)
    for b in range(n_buckets):
        bias = jnp.where(bucket == b, tab_ref[b, h], bias)
    return jnp.where(rel >= 0, bias, NEG)


def _bias_body(tab_ref, bp_ref, bs_ref, *, n_buckets, n_heads, blk, dec_t, page):
    r = lax.broadcasted_iota(I32, (blk, blk), 0)
    c = lax.broadcasted_iota(I32, (blk, blk), 1)
    rs = lax.broadcasted_iota(I32, (dec_t, page), 0)
    cs = lax.broadcasted_iota(I32, (dec_t, page), 1)
    for h in range(n_heads):
        for d in range(2):
            bp_ref[h, d] = _bias_tile(tab_ref, h, d * blk + r - c, n_buckets)
        new = _bias_tile(tab_ref, h, rs - cs, n_buckets)
        last = _bias_tile(tab_ref, h, page + rs - cs, n_buckets)
        far = jnp.full((dec_t, page), tab_ref[n_buckets - 1, h], F32)
        for idx, tile in enumerate((new, last, far)):
            for mp in range(2):
                bs_ref[idx, pl.ds((2 * h + mp) * dec_t, dec_t), :] = tile


def _bias_tiles(rel_bias, blk, dec_t, page):
    n_buckets, n_heads = rel_bias.shape
    return pl.pallas_call(
        functools.partial(_bias_body, n_buckets=n_buckets, n_heads=n_heads, blk=blk,
                          dec_t=dec_t, page=page),
        in_specs=[pl.BlockSpec(memory_space=pltpu.SMEM)],
        out_shape=[jax.ShapeDtypeStruct((n_heads, 2, blk, blk), F32),
                   jax.ShapeDtypeStruct((3, 2 * n_heads * dec_t, page), F32)],
        name="bias_tiles",
    )(rel_bias)


def _sb_prompt_body(q_ref, k_ref, v_ref, tri_ref, g_ref, o_ref, *, blk):
    i = pl.program_id(2)
    qq = _stack_halves(q_ref[0])
    tri = tri_ref[...]
    r = lax.broadcasted_iota(I32, (2 * blk, blk), 0)
    r = jnp.where(r >= blk, r - blk, r)
    c = lax.broadcasted_iota(I32, (2 * blk, blk), 1)
    vis = c < r

    def load(j):
        start = pl.multiple_of(j * blk, blk)
        return k_ref[0, pl.ds(start, blk), :], v_ref[0, pl.ds(start, blk), :]

    acc = jnp.zeros((2 * blk, LANES), F32)
    carry = jnp.zeros((2 * blk, 1), F32)
    k, v = load(i)
    acc, carry = _sb_block(qq, k, v, tri, acc, carry, vis)

    def body(jj, state):
        k, v = load(i - 1 - jj)
        return _sb_block(qq, k, v, tri, state[0], state[1], None)

    acc, carry = lax.fori_loop(0, i, body, (acc, carry))
    o_ref[0] = _sb_finish(acc, blk, g_ref[...]).astype(o_ref.dtype)


def _sb_prompt(q_b, k_b, v_b, tri, g, blk):
    b, tp, width = q_b.shape
    n_pairs = width // LANES
    grid = (b, n_pairs, tp // blk)
    return pl.pallas_call(
        functools.partial(_sb_prompt_body, blk=blk),
        grid=grid,
        in_specs=[
            pl.BlockSpec((1, blk, LANES), lambda bi, p, i: (bi, i, p)),
            pl.BlockSpec((1, tp, LANES), lambda bi, p, i: (bi, 0, p)),
            pl.BlockSpec((1, tp, LANES), lambda bi, p, i: (bi, 0, p)),
            pl.BlockSpec((blk, blk), lambda bi, p, i: (0, 0)),
            pl.BlockSpec((1, LANES), lambda bi, p, i: (0, p)),
        ],
        out_specs=pl.BlockSpec((1, blk, LANES), lambda bi, p, i: (bi, i, p)),
        out_shape=jax.ShapeDtypeStruct((b, tp, width), BF16),
        compiler_params=pltpu.CompilerParams(
            dimension_semantics=("arbitrary", "arbitrary", "arbitrary"),
            vmem_limit_bytes=VMEM_LIMIT),
        name="sb_prompt",
    )(q_b, k_b, v_b, tri, g)


def _df_prompt_body(tab_ref, q_ref, k_ref, v_ref, bias_ref, g_ref, lq1, lk1, lq2, lk2,
                    o_ref, *, blk, lam_init, n_buckets):
    h = pl.program_id(1)
    i = pl.program_id(2)
    qq = _stack_halves(q_ref[0])

    def load(j):
        start = pl.multiple_of(j * blk, blk)
        return k_ref[0, pl.ds(start, blk), :], v_ref[0, pl.ds(start, blk), :]

    def tile_bias(d):
        t = bias_ref[0, d]
        return jnp.concatenate([t, t], axis=0)

    m = jnp.full((2 * blk, 1), NEG, F32)
    l = jnp.zeros((2 * blk, 1), F32)
    acc = jnp.zeros((2 * blk, LANES), F32)
    k, v = load(i)
    m, l, acc = _df_block(qq, k, v, tile_bias(0), m, l, acc)

    def near(state):
        k, v = load(i - 1)
        return _df_block(qq, k, v, tile_bias(1), *state)

    m, l, acc = lax.cond(i >= 1, near, lambda s: s, (m, l, acc))
    far_bias = tab_ref[n_buckets - 1, h]

    def body(jj, state):
        k, v = load(i - 2 - jj)
        return _df_block(qq, k, v, far_bias, *state)

    m, l, acc = lax.fori_loop(0, jnp.maximum(i - 1, 0), body, (m, l, acc))
    lam = _lambda(lq1, lk1, lq2, lk2, lam_init)
    o_ref[0] = _df_finish(acc, l, blk, lam, g_ref[...], lam_init).astype(o_ref.dtype)


def _df_prompt(rel_bias, q_b, k_b, v_b, bias_p, g, lams, blk, lam_init):
    b, tp, width = q_b.shape
    n_heads = width // LANES
    grid = (b, n_heads, tp // blk)
    vec = pl.BlockSpec((1, DH), lambda bi, h, i: (0, 0))
    return pl.pallas_call(
        functools.partial(_df_prompt_body, blk=blk, lam_init=lam_init,
                          n_buckets=rel_bias.shape[0]),
        grid=grid,
        in_specs=[
            pl.BlockSpec(memory_space=pltpu.SMEM),
            pl.BlockSpec((1, blk, LANES), lambda bi, h, i: (bi, i, h)),
            pl.BlockSpec((1, tp, LANES), lambda bi, h, i: (bi, 0, h)),
            pl.BlockSpec((1, tp, LANES), lambda bi, h, i: (bi, 0, h)),
            pl.BlockSpec((1, 2, blk, blk), lambda bi, h, i: (h, 0, 0, 0)),
            pl.BlockSpec((1, LANES), lambda bi, h, i: (0, 0)),
            vec, vec, vec, vec,
        ],
        out_specs=pl.BlockSpec((1, blk, LANES), lambda bi, h, i: (bi, i, h)),
        out_shape=jax.ShapeDtypeStruct((b, tp, width), BF16),
        compiler_params=pltpu.CompilerParams(
            dimension_semantics=("arbitrary", "arbitrary", "arbitrary"),
            vmem_limit_bytes=VMEM_LIMIT),
        name="df_prompt",
    )(rel_bias, q_b, k_b, v_b, bias_p, g, *lams)


def _stack_decode_queries(q, n_blocks):
    q = q.astype(F32)
    return jnp.concatenate(
        [_stack_halves(q[:, p * LANES:(p + 1) * LANES]) for p in range(n_blocks)], axis=0).astype(BF16)


def _sb_decode_body(pt_ref, q_ref, kn_ref, vn_ref, kc_ref, vc_ref, tri_ref, g_ref, o_ref,
                    qq_ref, acc_ref, carry_ref, *, dec_t, n_blocks, page):
    jj = pl.program_id(1)
    rows = 2 * dec_t

    @pl.when(jj == 0)
    def _():
        qq_ref[...] = _stack_decode_queries(q_ref[0], n_blocks)
        acc_ref[...] = jnp.zeros_like(acc_ref)
        carry_ref[...] = jnp.zeros_like(carry_ref)

    def process(k_ref, v_ref, masked):
        qq = qq_ref[...]
        z = jnp.concatenate(
            [_dot_tb(qq[p * rows:(p + 1) * rows], k_ref[0, :, p * LANES:(p + 1) * LANES].astype(BF16))
             for p in range(n_blocks)], axis=0)
        log_keep = _neg_softplus(z)
        if masked:
            r = lax.broadcasted_iota(I32, (dec_t, page), 0)
            c = lax.broadcasted_iota(I32, (dec_t, page), 1)
            vis = jnp.concatenate([c < r] * (2 * n_blocks), axis=0)
            log_keep = jnp.where(vis, log_keep, 0.0)
        hi, lo = _split_bf16(log_keep)
        tri = tri_ref[...]
        csum = _dot(hi, tri) + _dot(lo, tri) + carry_ref[...]
        w = jnp.exp(z + csum)
        if masked:
            w = jnp.where(vis, w, 0.0)
        w = w.astype(BF16)
        for p in range(n_blocks):
            acc_ref[p * rows:(p + 1) * rows, :] += _dot(
                w[p * rows:(p + 1) * rows], v_ref[0, :, p * LANES:(p + 1) * LANES].astype(BF16))
        carry_ref[...] = csum[:, 0:1]

    @pl.when(jj == 0)
    def _():
        process(kn_ref, vn_ref, True)

    @pl.when(jj > 0)
    def _():
        process(kc_ref, vc_ref, False)

    @pl.when(jj == pl.num_programs(1) - 1)
    def _():
        for p in range(n_blocks):
            o_ref[0, :, p * LANES:(p + 1) * LANES] = _sb_finish(
                acc_ref[p * rows:(p + 1) * rows, :], dec_t, g_ref[:, p * LANES:(p + 1) * LANES])


def _page_index(n_pages):
    def index(bi, jj, pt):
        return (pt[bi, n_pages - jnp.maximum(jj, 1)], 0, 0)
    return index


def _sb_decode(page_table, q, k_new, v_new, k_cache, v_cache, tri, g):
    b, dec_t, width = q.shape
    n_pages = page_table.shape[1]
    page = k_cache.shape[1]
    n_blocks = width // LANES
    rows = n_blocks * 2 * dec_t
    per_b = lambda bi, jj, pt: (bi, 0, 0)
    const = lambda bi, jj, pt: (0, 0)
    grid_spec = pltpu.PrefetchScalarGridSpec(
        num_scalar_prefetch=1,
        grid=(b, n_pages + 1),
        in_specs=[
            pl.BlockSpec((1, dec_t, width), per_b),
            pl.BlockSpec((1, page, width), per_b),
            pl.BlockSpec((1, page, width), per_b),
            pl.BlockSpec((1, page, width), _page_index(n_pages)),
            pl.BlockSpec((1, page, width), _page_index(n_pages)),
            pl.BlockSpec((page, page), const),
            pl.BlockSpec((1, width), const),
        ],
        out_specs=pl.BlockSpec((1, dec_t, width), per_b),
        scratch_shapes=[pltpu.VMEM((rows, LANES), BF16),
                        pltpu.VMEM((rows, LANES), F32),
                        pltpu.VMEM((rows, 1), F32)],
    )
    return pl.pallas_call(
        functools.partial(_sb_decode_body, dec_t=dec_t, n_blocks=n_blocks, page=page),
        grid_spec=grid_spec,
        out_shape=jax.ShapeDtypeStruct((b, dec_t, width), F32),
        compiler_params=pltpu.CompilerParams(
            dimension_semantics=("arbitrary", "arbitrary"),
            vmem_limit_bytes=VMEM_LIMIT),
        name="sb_decode",
    )(page_table, q, k_new, v_new, k_cache, v_cache, tri, g)


def _df_decode_body(pt_ref, q_ref, kn_ref, vn_ref, kc_ref, vc_ref, bias_ref, g_ref,
                    lq1, lk1, lq2, lk2, o_ref, qq_ref, m_ref, l_ref, acc_ref,
                    *, dec_t, n_blocks, lam_init):
    jj = pl.program_id(1)
    rows = 2 * dec_t

    @pl.when(jj == 0)
    def _():
        qq_ref[...] = _stack_decode_queries(q_ref[0], n_blocks)
        m_ref[...] = jnp.full(m_ref.shape, NEG, F32)
        l_ref[...] = jnp.zeros_like(l_ref)
        acc_ref[...] = jnp.zeros_like(acc_ref)

    def process(k_ref, v_ref):
        qq = qq_ref[...]
        s = jnp.concatenate(
            [_dot_tb(qq[p * rows:(p + 1) * rows], k_ref[0, :, p * LANES:(p + 1) * LANES].astype(BF16))
             for p in range(n_blocks)], axis=0) + bias_ref[0]
        m = m_ref[...]
        m_new = jnp.maximum(m, jnp.max(s, axis=-1, keepdims=True))
        alpha = jnp.exp(m - m_new)
        p_all = jnp.exp(s - m_new)
        l_ref[...] = alpha * l_ref[...] + jnp.sum(p_all, axis=-1, keepdims=True)
        m_ref[...] = m_new
        p_all = p_all.astype(BF16)
        for p in range(n_blocks):
            sl = slice(p * rows, (p + 1) * rows)
            acc_ref[sl, :] = alpha[sl] * acc_ref[sl, :] + _dot(
                p_all[sl], v_ref[0, :, p * LANES:(p + 1) * LANES].astype(BF16))

    @pl.when(jj == 0)
    def _():
        process(kn_ref, vn_ref)

    @pl.when(jj > 0)
    def _():
        process(kc_ref, vc_ref)

    @pl.when(jj == pl.num_programs(1) - 1)
    def _():
        lam = _lambda(lq1, lk1, lq2, lk2, lam_init)
        for p in range(n_blocks):
            sl = slice(p * rows, (p + 1) * rows)
            o_ref[0, :, p * LANES:(p + 1) * LANES] = _df_finish(
                acc_ref[sl, :], l_ref[sl, :], dec_t, lam, g_ref[...], lam_init)


def _df_decode(page_table, q, k_new, v_new, k_cache, v_cache, bias_s, g, lams, lam_init):
    b, dec_t, width = q.shape
    n_pages = page_table.shape[1]
    page = k_cache.shape[1]
    n_blocks = width // LANES
    rows = n_blocks * 2 * dec_t
    per_b = lambda bi, jj, pt: (bi, 0, 0)
    const = lambda bi, jj, pt: (0, 0)
    vec = pl.BlockSpec((1, DH), const)
    grid_spec = pltpu.PrefetchScalarGridSpec(
        num_scalar_prefetch=1,
        grid=(b, n_pages + 1),
        in_specs=[
            pl.BlockSpec((1, dec_t, width), per_b),
            pl.BlockSpec((1, page, width), per_b),
            pl.BlockSpec((1, page, width), per_b),
            pl.BlockSpec((1, page, width), _page_index(n_pages)),
            pl.BlockSpec((1, page, width), _page_index(n_pages)),
            pl.BlockSpec((1, rows, page), lambda bi, jj, pt: (jnp.minimum(jj, 2), 0, 0)),
            pl.BlockSpec((1, LANES), const),
            vec, vec, vec, vec,
        ],
        out_specs=pl.BlockSpec((1, dec_t, width), per_b),
        scratch_shapes=[pltpu.VMEM((rows, LANES), BF16),
                        pltpu.VMEM((rows, 1), F32),
                        pltpu.VMEM((rows, 1), F32),
                        pltpu.VMEM((rows, LANES), F32)],
    )
    return pl.pallas_call(
        functools.partial(_df_decode_body, dec_t=dec_t, n_blocks=n_blocks, lam_init=lam_init),
        grid_spec=grid_spec,
        out_shape=jax.ShapeDtypeStruct((b, dec_t, width), F32),
        compiler_params=pltpu.CompilerParams(
            dimension_semantics=("arbitrary", "arbitrary"),
            vmem_limit_bytes=VMEM_LIMIT),
        name="df_decode",
    )(page_table, q, k_new, v_new, k_cache, v_cache, bias_s, g, *lams)


def _router_gates(logits_t, bias_col, n_experts):
    tm = logits_t.shape[1]
    gsz = n_experts // N_GROUPS
    scores = 1.0 / (1.0 + jnp.exp(-logits_t))
    sel = scores + bias_col
    sub = lax.broadcasted_iota(I32, (gsz, tm), 0)
    group_scores = []
    for g in range(N_GROUPS):
        blk = sel[g * gsz:(g + 1) * gsz]
        m1 = jnp.max(blk, axis=0, keepdims=True)
        first = jnp.min(jnp.where(blk == m1, sub, gsz), axis=0, keepdims=True)
        m2 = jnp.max(jnp.where(sub == first, -jnp.inf, blk), axis=0, keepdims=True)
        group_scores.append(m1 + m2)
    gs = jnp.concatenate(group_scores, axis=0)
    gidx = lax.broadcasted_iota(I32, (N_GROUPS, tm), 0)
    grank = jnp.zeros((N_GROUPS, tm), I32)
    for g in range(N_GROUPS):
        row = gs[g:g + 1]
        ahead = (row > gs) | ((row == gs) & (gidx > g))
        grank = grank + ahead.astype(I32)
    gkeep = grank < TOPK_GROUPS
    masked = jnp.concatenate(
        [jnp.where(jnp.broadcast_to(gkeep[g:g + 1], (gsz, tm)), sel[g * gsz:(g + 1) * gsz], -jnp.inf)
         for g in range(N_GROUPS)], axis=0)
    eidx = lax.broadcasted_iota(I32, (n_experts, tm), 0)
    erank = jnp.zeros((n_experts, tm), I32)
    for e in range(n_experts):
        row = masked[e:e + 1]
        ahead = (row > masked) | ((row == masked) & (eidx > e))
        erank = erank + ahead.astype(I32)
    w = jnp.where(erank < TOP_K, scores, 0.0)
    return w / jnp.sum(w, axis=0, keepdims=True) * ROUTED_SCALE


def _post_body(msb_ref, mdf_ref, x_ref, wo_sb_ref, wo_df_ref, g2_ref, rw_hi_ref, rw_lo_ref,
               rb_ref, x1_ref, h2_ref, gates_ref, *, n_experts):
    att = _dot(msb_ref[...].astype(BF16), wo_sb_ref[...]) + _dot(mdf_ref[...].astype(BF16), wo_df_ref[...])
    x1 = x_ref[...] + att
    x1_ref[...] = x1
    ms = jnp.mean(x1 * x1, axis=-1, keepdims=True)
    h2 = (x1 * lax.rsqrt(ms + EPS)) * g2_ref[...]
    h2_hi, h2_lo = _split_bf16(h2)
    h2_ref[...] = h2_hi
    rw_hi = rw_hi_ref[...]
    logits_t = _dot_tb(rw_hi, h2_hi) + _dot_tb(rw_hi, h2_lo) + _dot_tb(rw_lo_ref[...], h2_hi)
    gates_t = _router_gates(logits_t, rb_ref[...], n_experts)
    tm = gates_t.shape[1]
    pad = gates_ref.shape[1] - n_experts
    gates_t = jnp.concatenate([gates_t, jnp.zeros((pad, tm), F32)], axis=0)
    gates_ref[...] = gates_t.T


def _post_attention(msb, mdf, x, wo_sb, wo_df, g2, rw_hi, rw_lo, rb, tm):
    n, d = x.shape
    n_experts = rw_hi.shape[0]
    ge = _round_up(n_experts, LANES)
    row = lambda i: (i, 0)
    const = lambda i: (0, 0)
    return pl.pallas_call(
        functools.partial(_post_body, n_experts=n_experts),
        grid=(n // tm,),
        in_specs=[
            pl.BlockSpec((tm, msb.shape[1]), row),
            pl.BlockSpec((tm, mdf.shape[1]), row),
            pl.BlockSpec((tm, d), row),
            pl.BlockSpec(wo_sb.shape, const),
            pl.BlockSpec(wo_df.shape, const),
            pl.BlockSpec((1, d), const),
            pl.BlockSpec(rw_hi.shape, const),
            pl.BlockSpec(rw_lo.shape, const),
            pl.BlockSpec((n_experts, 1), const),
        ],
        out_specs=[pl.BlockSpec((tm, d), row), pl.BlockSpec((tm, d), row), pl.BlockSpec((tm, ge), row)],
        out_shape=[jax.ShapeDtypeStruct((n, d), F32), jax.ShapeDtypeStruct((n, d), BF16),
                   jax.ShapeDtypeStruct((n, ge), F32)],
        compiler_params=pltpu.CompilerParams(
            dimension_semantics=("arbitrary",), vmem_limit_bytes=VMEM_LIMIT),
        name="post_attention",
    )(msb, mdf, x, wo_sb, wo_df, g2, rw_hi, rw_lo, rb)


def _silu(x):
    return x / (1.0 + jnp.exp(-x))


def _moe_body(h2_ref, x1_ref, gates_ref, wg_ref, wu_ref, wd_ref, sg_ref, su_ref, sd_ref,
              o_ref, acc_ref):
    e = pl.program_id(1)
    h2 = h2_ref[...]

    @pl.when(e == 0)
    def _():
        a = _silu(_dot(h2, sg_ref[...])) * _dot(h2, su_ref[...])
        acc_ref[...] = _dot(a.astype(BF16), sd_ref[...])

    g = _dot(h2, wg_ref[0].astype(BF16))
    u = _dot(h2, wu_ref[0].astype(BF16))
    f = g.shape[1]
    ge = gates_ref.shape[1]
    pick = (lax.broadcasted_iota(I32, (ge, f), 0) == e).astype(BF16)
    gate_hi, gate_lo = _split_bf16(gates_ref[...])
    gate_col = _dot(gate_hi, pick) + _dot(gate_lo, pick)
    a = _silu(g) * u * gate_col
    acc_ref[...] += _dot(a.astype(BF16), wd_ref[0].astype(BF16))

    @pl.when(e == pl.num_programs(1) - 1)
    def _():
        o_ref[...] = x1_ref[...] + acc_ref[...]


def _moe(h2, x1, gates, w_gate, w_up, w_down, sg, su, sd, tm):
    n, d = x1.shape
    n_experts, _, f = w_gate.shape
    row = lambda i, e: (i, 0)
    const = lambda i, e: (0, 0)
    return pl.pallas_call(
        _moe_body,
        grid=(n // tm, n_experts),
        in_specs=[
            pl.BlockSpec((tm, d), row),
            pl.BlockSpec((tm, d), row),
            pl.BlockSpec((tm, gates.shape[1]), row),
            pl.BlockSpec((1, d, f), lambda i, e: (e, 0, 0)),
            pl.BlockSpec((1, d, f), lambda i, e: (e, 0, 0)),
            pl.BlockSpec((1, f, d), lambda i, e: (e, 0, 0)),
            pl.BlockSpec(sg.shape, const),
            pl.BlockSpec(su.shape, const),
            pl.BlockSpec(sd.shape, const),
        ],
        out_specs=pl.BlockSpec((tm, d), row),
        out_shape=jax.ShapeDtypeStruct((n, d), F32),
        scratch_shapes=[pltpu.VMEM((tm, d), F32)],
        compiler_params=pltpu.CompilerParams(
            dimension_semantics=("arbitrary", "arbitrary"), vmem_limit_bytes=VMEM_LIMIT),
        name="moe",
    )(h2, x1, gates, w_gate, w_up, w_down, sg, su, sd)


def _largest_tile(n, cap):
    t = cap
    while n % t:
        t //= 2
    return t


def kernel(x_prompt, x_sample, cache_k_sb, cache_v_sb, cache_k_diff, cache_v_diff, page_table,
           meta_tokens, rel_bias, norm1_g, w_in, q_norm_g, k_norm_g, lambda_q1, lambda_k1,
           lambda_q2, lambda_k2, sb_out_g, diff_subln_g, w_out, norm2_g, router_w, router_bias,
           w_gate, w_up, w_down, w_shared_gate, w_shared_up, w_shared_down):
    assert w_in.shape[0] == 1, "single-layer step"
    b, seq, d = x_prompt.shape
    dec_b, dec_t, _ = x_sample.shape
    n_meta = meta_tokens.shape[0]
    width = sb_out_g.shape[1]
    n_pool, page = cache_k_sb.shape[1], cache_k_sb.shape[2]
    n_experts = router_w.shape[2]
    t = seq + n_meta
    blk = ATT_BLOCK
    tp = _round_up(t, blk)
    lam_init = 0.8 - 0.6 * math.exp(-0.3 * 0)

    w_in_b = w_in[0].astype(BF16)
    g1 = norm1_g
    reps = width // DH
    qg = jnp.tile(q_norm_g, (1, reps))
    kg = jnp.tile(k_norm_g, (1, reps))
    lane = jnp.arange(width)
    gm = ((lane[:, None] // DH) == (lane[None, :] // DH)).astype(BF16) * (1.0 / DH)
    tri_p = (jnp.arange(blk)[:, None] >= jnp.arange(blk)[None, :]).astype(BF16)
    tri_d = (jnp.arange(page)[:, None] >= jnp.arange(page)[None, :]).astype(BF16)
    lams = (lambda_q1, lambda_k1, lambda_q2, lambda_k2)
    wo_sb = w_out[0, :width].astype(BF16)
    wo_df = w_out[0, width:].astype(BF16)
    rw_t = router_w[0].T
    rw_hi = rw_t.astype(BF16)
    rw_lo = (rw_t - rw_hi.astype(F32)).astype(BF16)
    rb = router_bias[0][:, None]
    sg = w_shared_gate[0].astype(BF16)
    su = w_shared_up[0].astype(BF16)
    sd = w_shared_down[0].astype(BF16)

    bias_p, bias_s = _bias_tiles(rel_bias, blk, dec_t, page)

    meta = jnp.broadcast_to(meta_tokens[None], (b, n_meta, d))
    xp = jnp.concatenate([meta, x_prompt, jnp.zeros((b, tp - t, d), F32)], axis=1)
    (k_sb_p, v_sb_p, k_df_p, v_df_p,
     qsb_b, ksb_b, vsb_b, qdf_b, kdf_b, vdf_b) = _project(xp, t, g1, w_in_b, qg, kg, gm, blk)
    msb_p = _sb_prompt(qsb_b, ksb_b, vsb_b, tri_p, sb_out_g, blk)
    mdf_p = _df_prompt(rel_bias, qdf_b, kdf_b, vdf_b, bias_p, diff_subln_g, lams, blk, lam_init)

    n_p = b * tp
    tm_p = _largest_tile(n_p, 512)
    x1_p, h2_p, gates_p = _post_attention(
        msb_p.reshape(n_p, width), mdf_p.reshape(n_p, width), xp.reshape(n_p, d),
        wo_sb, wo_df, norm2_g, rw_hi, rw_lo, rb, tm_p)
    y_p = _moe(h2_p, x1_p, gates_p, w_gate[0], w_up[0], w_down[0], sg, su, sd,
               _largest_tile(n_p, 1024))
    y_prompt = y_p.reshape(b, tp, d)[:, n_meta:t]

    n_s = dec_b * dec_t
    xs = x_sample.reshape(1, n_s, d)
    (k_sb_s, v_sb_s, k_df_s, v_df_s,
     qsb_s, _, _, qdf_s, _, _) = _project(xs, n_s, g1, w_in_b, qg, kg, gm, n_s)

    def new_page(rows):
        r = rows.reshape(dec_b, dec_t, width)
        return jnp.pad(r, ((0, 0), (0, page - dec_t), (0, 0)))

    pool = lambda c: c.reshape(n_pool, page, width)
    msb_s = _sb_decode(page_table, qsb_s.reshape(dec_b, dec_t, width),
                       new_page(k_sb_s), new_page(v_sb_s),
                       pool(cache_k_sb), pool(cache_v_sb), tri_d, sb_out_g)
    mdf_s = _df_decode(page_table, qdf_s.reshape(dec_b, dec_t, width),
                       new_page(k_df_s), new_page(v_df_s),
                       pool(cache_k_diff), pool(cache_v_diff), bias_s, diff_subln_g, lams, lam_init)
    x1_s, h2_s, gates_s = _post_attention(
        msb_s.reshape(n_s, width), mdf_s.reshape(n_s, width), x_sample.reshape(n_s, d),
        wo_sb, wo_df, norm2_g, rw_hi, rw_lo, rb, n_s)
    y_s = _moe(h2_s, x1_s, gates_s, w_gate[0], w_up[0], w_down[0], sg, su, sd, n_s)
    y_sample = y_s.reshape(dec_b, dec_t, d)

    n_h_sb = width // DH
    n_h_df = width // LANES
    sb_rows = lambda r, bb, tt: r.reshape(1, bb, tt, n_h_sb, DH)
    df_rows = lambda r, bb, tt: r.reshape(1, bb, tt, n_h_df, LANES)
    return (y_prompt, y_sample,
            sb_rows(k_sb_p, b, t), sb_rows(v_sb_p, b, t), df_rows(k_df_p, b, t), df_rows(v_df_p, b, t),
            sb_rows(k_sb_s, dec_b, dec_t), sb_rows(v_sb_s, dec_b, dec_t),
            df_rows(k_df_s, dec_b, dec_t), df_rows(v_df_s, dec_b, dec_t))
```

```python
import functools
import math

import jax
import jax.numpy as jnp
from jax import lax
from jax.experimental import pallas as pl
from jax.experimental.pallas import tpu as pltpu

F32 = jnp.float32
BF16 = jnp.bfloat16
I32 = jnp.int32

EPS = 1e-6
DH = 64
LANES = 128
MAX_DISTANCE = 128
TOP_K = 8
N_GROUPS = 8
TOPK_GROUPS = 4
ROUTED_SCALE = 2.5
NEG = -1e30
SB_CUTOFF = -104.0
ATT_BLOCK = 256
DECODE_PAGES_PER_STEP = 4
VMEM_LIMIT = 48 * 1024 * 1024

_TRANS_B = (((1,), (1,)), ((), ()))


def _dot(a, b):
    return jnp.dot(a, b, preferred_element_type=F32)


def _dot_tb(a, b):
    return lax.dot_general(a, b, _TRANS_B, preferred_element_type=F32)


def _split_bf16(x):
    hi = x.astype(BF16)
    lo = (x - hi.astype(F32)).astype(BF16)
    return hi, lo


def _round_up(n, m):
    return (n + m - 1) // m * m


def _proj_body(x_ref, g1_ref, w_ref, qg_ref, kg_ref, gm_ref,
               ksb_o, vsb_o, kdf_o, vdf_o,
               qsb_b, ksb_b, vsb_b, qdf_b, kdf_b, vdf_b, *, width):
    x = x_ref[0]
    ms = jnp.mean(x * x, axis=-1, keepdims=True)
    h = (x * lax.rsqrt(ms + EPS)) * g1_ref[...]
    proj = _dot(h.astype(BF16), w_ref[...])
    w = width
    q_sb, k_sb, v_sb = proj[:, 0:w], proj[:, w:2 * w], proj[:, 2 * w:3 * w]
    q_df, k_df, v_df = proj[:, 3 * w:4 * w], proj[:, 4 * w:5 * w], proj[:, 5 * w:6 * w]

    def map_norm(t, g_ref):
        msq = _dot((t * t).astype(BF16), gm_ref[...])
        return (t * lax.rsqrt(msq + EPS)) * g_ref[...]

    q_df = map_norm(q_df, qg_ref)
    k_df = map_norm(k_df, kg_ref)
    ksb_o[0] = k_sb
    vsb_o[0] = v_sb
    kdf_o[0] = k_df
    vdf_o[0] = v_df
    scale = DH ** -0.5
    qsb_b[0] = (q_sb * scale).astype(BF16)
    ksb_b[0] = k_sb.astype(BF16)
    vsb_b[0] = v_sb.astype(BF16)
    qdf_b[0] = (q_df * scale).astype(BF16)
    kdf_b[0] = k_df.astype(BF16)
    vdf_b[0] = v_df.astype(BF16)


def _project(x_pad, t_valid, g1, w_in_b, qg, kg, gm, tm):
    b, tp, d = x_pad.shape
    width = w_in_b.shape[1] // 6
    grid = (b, tp // tm)
    row = lambda bi, i: (bi, i, 0)
    const = lambda bi, i: (0, 0)
    f32_out = jax.ShapeDtypeStruct((b, t_valid, width), F32)
    bf_out = jax.ShapeDtypeStruct((b, tp, width), BF16)
    out_spec = pl.BlockSpec((1, tm, width), row)
    return pl.pallas_call(
        functools.partial(_proj_body, width=width),
        grid=grid,
        in_specs=[
            pl.BlockSpec((1, tm, d), row),
            pl.BlockSpec((1, d), const),
            pl.BlockSpec(w_in_b.shape, const),
            pl.BlockSpec((1, width), const),
            pl.BlockSpec((1, width), const),
            pl.BlockSpec((width, width), const),
        ],
        out_specs=[out_spec] * 10,
        out_shape=[f32_out] * 4 + [bf_out] * 6,
        compiler_params=pltpu.CompilerParams(
            dimension_semantics=("arbitrary", "arbitrary"),
            vmem_limit_bytes=VMEM_LIMIT),
        name="proj",
    )(x_pad, g1, w_in_b, qg, kg, gm)


def _stack_halves(q):
    lane = lax.broadcasted_iota(I32, (1, LANES), 1)
    zero = jnp.zeros_like(q)
    return jnp.concatenate(
        [jnp.where(lane < DH, q, zero), jnp.where(lane >= DH, q, zero)], axis=0)


def _neg_softplus(z):
    return -(jnp.maximum(z, 0.0) + jnp.log(1.0 + jnp.exp(-jnp.abs(z))))


def _sb_block(qq, k, v, tri, acc, carry, vis):
    z = _dot_tb(qq, k)
    log_keep = _neg_softplus(z)
    if vis is not None:
        log_keep = jnp.where(vis, log_keep, 0.0)
    hi, lo = _split_bf16(log_keep)
    csum = _dot(hi, tri) + _dot(lo, tri) + carry
    w = jnp.exp(z + csum)
    if vis is not None:
        w = jnp.where(vis, w, 0.0)
    acc = acc + _dot(w.astype(BF16), v)
    return acc, csum[:, 0:1]


def _sb_finish(acc, t, g):
    lane = lax.broadcasted_iota(I32, (1, LANES), 1)
    lo_half = lane < DH
    o = jnp.where(lo_half, acc[:t], acc[t:])
    o2 = o * o
    s_lo = jnp.sum(jnp.where(lo_half, o2, 0.0), axis=-1, keepdims=True)
    s_hi = jnp.sum(jnp.where(lo_half, 0.0, o2), axis=-1, keepdims=True)
    ms = jnp.where(lo_half, s_lo, s_hi) * (1.0 / DH)
    return (o * lax.rsqrt(ms + EPS)) * g


def _df_block(qq, k, v, bias, m, l, acc):
    s = _dot_tb(qq, k) + bias
    m_new = jnp.maximum(m, jnp.max(s, axis=-1, keepdims=True))
    alpha = jnp.exp(m - m_new)
    p = jnp.exp(s - m_new)
    l = alpha * l + jnp.sum(p, axis=-1, keepdims=True)
    acc = alpha * acc + _dot(p.astype(BF16), v)
    return m_new, l, acc


def _lambda(lq1, lk1, lq2, lk2, lam_init):
    s1 = jnp.sum(lq1[...] * lk1[...], axis=-1, keepdims=True)
    s2 = jnp.sum(lq2[...] * lk2[...], axis=-1, keepdims=True)
    return jnp.exp(s1) - jnp.exp(s2) + lam_init


def _df_finish(acc, l, t, lam, g, lam_init):
    o = acc[:t] / l[:t] - lam * (acc[t:] / l[t:])
    ms = jnp.mean(o * o, axis=-1, keepdims=True)
    return (o * lax.rsqrt(ms + EPS)) * (g * (1.0 - lam_init))


def _bias_tile(tab_ref, h, rel, n_buckets):
    max_exact = n_buckets // 2
    n = jnp.maximum(rel, 0)
    nf = jnp.maximum(n, 1).astype(F32)
    large = max_exact + (jnp.log(nf / max_exact) / math.log(MAX_DISTANCE / max_exact)
                         * (n_buckets - max_exact)).astype(I32)
    large = jnp.minimum(large, n_buckets - 1)
    bucket = jnp.where(n < max_exact, n, large)
    bias = jnp.zeros(rel.shape, F32)
    for b in range(n_buckets):
        bias = jnp.where(bucket == b, tab_ref[b, h], bias)
    return jnp.where(rel >= 0, bias, NEG)


def _bias_body(tab_ref, bp_ref, bs_ref, bn_ref, *, n_buckets, n_heads, blk, dec_t, page, n_slots):
    r = lax.broadcasted_iota(I32, (blk, blk), 0)
    c = lax.broadcasted_iota(I32, (blk, blk), 1)
    cols = page * n_heads
    rs = lax.broadcasted_iota(I32, (dec_t, cols), 0)
    cs = lax.broadcasted_iota(I32, (dec_t, cols), 1)
    pos, head = cs // n_heads, cs % n_heads
    rn = lax.broadcasted_iota(I32, (dec_t, page), 0)
    cn = lax.broadcasted_iota(I32, (dec_t, page), 1)
    pos_n, head_n = cn // n_heads, cn % n_heads
    for h in range(n_heads):
        for d in range(2):
            bp_ref[h, d] = _bias_tile(tab_ref, h, d * blk + r - c, n_buckets)
        far = jnp.where(head == h, tab_ref[n_buckets - 1, h], NEG)
        last = jnp.where(head == h, _bias_tile(tab_ref, h, page + rs - pos, n_buckets), NEG)
        new = jnp.where((head_n == h) & (pos_n < dec_t),
                        _bias_tile(tab_ref, h, rn - pos_n, n_buckets), NEG)
        for mp in range(2):
            rows = pl.ds((2 * h + mp) * dec_t, dec_t)
            bn_ref[rows, :] = new
            for u in range(n_slots):
                bs_ref[0, rows, u * cols:(u + 1) * cols] = far
                bs_ref[1, rows, u * cols:(u + 1) * cols] = last if u == n_slots - 1 else far


def _bias_tiles(rel_bias, blk, dec_t, page, n_slots):
    n_buckets, n_heads = rel_bias.shape
    rows = 2 * n_heads * dec_t
    return pl.pallas_call(
        functools.partial(_bias_body, n_buckets=n_buckets, n_heads=n_heads, blk=blk,
                          dec_t=dec_t, page=page, n_slots=n_slots),
        in_specs=[pl.BlockSpec(memory_space=pltpu.SMEM)],
        out_shape=[jax.ShapeDtypeStruct((n_heads, 2, blk, blk), F32),
                   jax.ShapeDtypeStruct((2, rows, n_slots * page * n_heads), F32),
                   jax.ShapeDtypeStruct((rows, page), F32)],
        name="bias_tiles",
    )(rel_bias)


def _sb_prompt_body(q_ref, k_ref, v_ref, tri_ref, g_ref, o_ref, *, blk):
    i = pl.program_id(2)
    qq = _stack_halves(q_ref[0])
    tri = tri_ref[...]
    r = lax.broadcasted_iota(I32, (2 * blk, blk), 0)
    r = jnp.where(r >= blk, r - blk, r)
    c = lax.broadcasted_iota(I32, (2 * blk, blk), 1)
    vis = c < r

    def load(j):
        start = pl.multiple_of(j * blk, blk)
        return k_ref[0, pl.ds(start, blk), :], v_ref[0, pl.ds(start, blk), :]

    acc = jnp.zeros((2 * blk, LANES), F32)
    carry = jnp.zeros((2 * blk, 1), F32)
    k, v = load(i)
    acc, carry = _sb_block(qq, k, v, tri, acc, carry, vis)

    def more(state):
        jj, _, carry = state
        return (jj < i) & (jnp.max(carry) > SB_CUTOFF)

    def body(state):
        jj, acc, carry = state
        k, v = load(i - 1 - jj)
        acc, carry = _sb_block(qq, k, v, tri, acc, carry, None)
        return jj + 1, acc, carry

    _, acc, _ = lax.while_loop(more, body, (jnp.int32(0), acc, carry))
    o_ref[0] = _sb_finish(acc, blk, g_ref[...]).astype(o_ref.dtype)


def _sb_prompt(q_b, k_b, v_b, tri, g, blk):
    b, tp, width = q_b.shape
    n_pairs = width // LANES
    grid = (b, n_pairs, tp // blk)
    return pl.pallas_call(
        functools.partial(_sb_prompt_body, blk=blk),
        grid=grid,
        in_specs=[
            pl.BlockSpec((1, blk, LANES), lambda bi, p, i: (bi, i, p)),
            pl.BlockSpec((1, tp, LANES), lambda bi, p, i: (bi, 0, p)),
            pl.BlockSpec((1, tp, LANES), lambda bi, p, i: (bi, 0, p)),
            pl.BlockSpec((blk, blk), lambda bi, p, i: (0, 0)),
            pl.BlockSpec((1, LANES), lambda bi, p, i: (0, p)),
        ],
        out_specs=pl.BlockSpec((1, blk, LANES), lambda bi, p, i: (bi, i, p)),
        out_shape=jax.ShapeDtypeStruct((b, tp, width), BF16),
        compiler_params=pltpu.CompilerParams(
            dimension_semantics=("arbitrary", "arbitrary", "arbitrary"),
            vmem_limit_bytes=VMEM_LIMIT),
        name="sb_prompt",
    )(q_b, k_b, v_b, tri, g)


def _df_prompt_body(tab_ref, q_ref, k_ref, v_ref, bias_ref, g_ref, lq1, lk1, lq2, lk2,
                    o_ref, *, blk, lam_init, n_buckets):
    h = pl.program_id(1)
    i = pl.program_id(2)
    qq = _stack_halves(q_ref[0])

    def load(j):
        start = pl.multiple_of(j * blk, blk)
        return k_ref[0, pl.ds(start, blk), :], v_ref[0, pl.ds(start, blk), :]

    def tile_bias(d):
        t = bias_ref[0, d]
        return jnp.concatenate([t, t], axis=0)

    m = jnp.full((2 * blk, 1), NEG, F32)
    l = jnp.zeros((2 * blk, 1), F32)
    acc = jnp.zeros((2 * blk, LANES), F32)
    k, v = load(i)
    m, l, acc = _df_block(qq, k, v, tile_bias(0), m, l, acc)

    def near(state):
        k, v = load(i - 1)
        return _df_block(qq, k, v, tile_bias(1), *state)

    m, l, acc = lax.cond(i >= 1, near, lambda s: s, (m, l, acc))
    far_bias = tab_ref[n_buckets - 1, h]

    def body(jj, state):
        k, v = load(i - 2 - jj)
        return _df_block(qq, k, v, far_bias, *state)

    m, l, acc = lax.fori_loop(0, jnp.maximum(i - 1, 0), body, (m, l, acc))
    lam = _lambda(lq1, lk1, lq2, lk2, lam_init)
    o_ref[0] = _df_finish(acc, l, blk, lam, g_ref[...], lam_init).astype(o_ref.dtype)


def _df_prompt(rel_bias, q_b, k_b, v_b, bias_p, g, lams, blk, lam_init):
    b, tp, width = q_b.shape
    n_heads = width // LANES
    grid = (b, n_heads, tp // blk)
    vec = pl.BlockSpec((1, DH), lambda bi, h, i: (0, 0))
    return pl.pallas_call(
        functools.partial(_df_prompt_body, blk=blk, lam_init=lam_init,
                          n_buckets=rel_bias.shape[0]),
        grid=grid,
        in_specs=[
            pl.BlockSpec(memory_space=pltpu.SMEM),
            pl.BlockSpec((1, blk, LANES), lambda bi, h, i: (bi, i, h)),
            pl.BlockSpec((1, tp, LANES), lambda bi, h, i: (bi, 0, h)),
            pl.BlockSpec((1, tp, LANES), lambda bi, h, i: (bi, 0, h)),
            pl.BlockSpec((1, 2, blk, blk), lambda bi, h, i: (h, 0, 0, 0)),
            pl.BlockSpec((1, LANES), lambda bi, h, i: (0, 0)),
            vec, vec, vec, vec,
        ],
        out_specs=pl.BlockSpec((1, blk, LANES), lambda bi, h, i: (bi, i, h)),
        out_shape=jax.ShapeDtypeStruct((b, tp, width), BF16),
        compiler_params=pltpu.CompilerParams(
            dimension_semantics=("arbitrary", "arbitrary", "arbitrary"),
            vmem_limit_bytes=VMEM_LIMIT),
        name="df_prompt",
    )(rel_bias, q_b, k_b, v_b, bias_p, g, *lams)


def _stack_decode_queries(q, n_blocks):
    q = q.astype(F32)
    return jnp.concatenate(
        [_stack_halves(q[:, p * LANES:(p + 1) * LANES]) for p in range(n_blocks)], axis=0).astype(BF16)


def _sb_decode_body(pt_ref, q_ref, kn_ref, vn_ref, kc_hbm, vc_hbm, tri_ref, g_ref, o_ref,
                    kbuf, vbuf, sem, *, dec_t, n_blocks, page, n_pages):
    bi = pl.program_id(0)
    rows = 2 * dec_t
    qq = _stack_decode_queries(q_ref[0], n_blocks)
    tri = tri_ref[...]

    def page_copies(slot, jj):
        pg = pt_ref[bi, n_pages - 1 - jj]
        return (pltpu.make_async_copy(kc_hbm.at[pg], kbuf.at[slot], sem.at[0, slot]),
                pltpu.make_async_copy(vc_hbm.at[pg], vbuf.at[slot], sem.at[1, slot]))

    def start(slot, jj):
        for cp in page_copies(slot, jj):
            cp.start()

    def wait(slot, jj):
        for cp in page_copies(slot, jj):
            cp.wait()

    start(0, 0)

    def weights(z, carry, vis):
        log_keep = _neg_softplus(z)
        if vis is not None:
            log_keep = jnp.where(vis, log_keep, 0.0)
        hi, lo = _split_bf16(log_keep)
        csum = _dot(hi, tri) + _dot(lo, tri) + carry
        w = jnp.exp(z + csum)
        if vis is not None:
            w = jnp.where(vis, w, 0.0)
        return w.astype(BF16), csum[:, 0:1]

    r = lax.broadcasted_iota(I32, (dec_t, page), 0)
    c = lax.broadcasted_iota(I32, (dec_t, page), 1)
    vis = jnp.concatenate([c < r] * (2 * n_blocks), axis=0)
    z = jnp.concatenate(
        [_dot_tb(qq[p * rows:(p + 1) * rows], kn_ref[0, :, p * LANES:(p + 1) * LANES].astype(BF16))
         for p in range(n_blocks)], axis=0)
    w, carry = weights(z, jnp.zeros((n_blocks * rows, 1), F32), vis)
    acc = jnp.concatenate(
        [_dot(w[p * rows:(p + 1) * rows], vn_ref[0, :, p * LANES:(p + 1) * LANES].astype(BF16))
         for p in range(n_blocks)], axis=0)

    def more(state):
        jj, _, carry = state
        return (jj < n_pages) & (jnp.max(carry) > SB_CUTOFF)

    def body(state):
        jj, acc, carry = state
        slot = jj & 1
        wait(slot, jj)

        @pl.when(jj + 1 < n_pages)
        def _():
            start(1 - slot, jj + 1)

        kt = kbuf[slot]
        vt = vbuf[slot]
        z = jnp.concatenate(
            [_dot(qq[p * rows:(p + 1) * rows], kt[p * LANES:(p + 1) * LANES, :].astype(BF16))
             for p in range(n_blocks)], axis=0)
        w, carry = weights(z, carry, None)
        acc = acc + jnp.concatenate(
            [_dot_tb(w[p * rows:(p + 1) * rows], vt[p * LANES:(p + 1) * LANES, :].astype(BF16))
             for p in range(n_blocks)], axis=0)
        return jj + 1, acc, carry

    jj, acc, _ = lax.while_loop(more, body, (jnp.int32(0), acc, carry))

    @pl.when(jj < n_pages)
    def _():
        wait(jj & 1, jj)

    for p in range(n_blocks):
        o_ref[0, :, p * LANES:(p + 1) * LANES] = _sb_finish(
            acc[p * rows:(p + 1) * rows, :], dec_t, g_ref[:, p * LANES:(p + 1) * LANES])


def _sb_decode(page_table, q, k_new, v_new, kt_cache, vt_cache, tri, g):
    b, dec_t, width = q.shape
    n_pages = page_table.shape[1]
    page = kt_cache.shape[2]
    n_blocks = width // LANES
    per_b = lambda bi, pt: (bi, 0, 0)
    const = lambda bi, pt: (0, 0)
    grid_spec = pltpu.PrefetchScalarGridSpec(
        num_scalar_prefetch=1,
        grid=(b,),
        in_specs=[
            pl.BlockSpec((1, dec_t, width), per_b),
            pl.BlockSpec((1, page, width), per_b),
            pl.BlockSpec((1, page, width), per_b),
            pl.BlockSpec(memory_space=pl.ANY),
            pl.BlockSpec(memory_space=pl.ANY),
            pl.BlockSpec((page, page), const),
            pl.BlockSpec((1, width), const),
        ],
        out_specs=pl.BlockSpec((1, dec_t, width), per_b),
        scratch_shapes=[pltpu.VMEM((2, width, page), F32),
                        pltpu.VMEM((2, width, page), F32),
                        pltpu.SemaphoreType.DMA((2, 2))],
    )
    return pl.pallas_call(
        functools.partial(_sb_decode_body, dec_t=dec_t, n_blocks=n_blocks, page=page, n_pages=n_pages),
        grid_spec=grid_spec,
        out_shape=jax.ShapeDtypeStruct((b, dec_t, width), F32),
        compiler_params=pltpu.CompilerParams(
            dimension_semantics=("arbitrary",), vmem_limit_bytes=VMEM_LIMIT),
        name="sb_decode",
    )(page_table, q, k_new, v_new, kt_cache, vt_cache, tri, g)


def _df_decode_body(pt_ref, q_ref, kn_ref, vn_ref, *rest, dec_t, n_blocks, n_slots, lam_init):
    kc_refs, vc_refs = rest[:n_slots], rest[n_slots:2 * n_slots]
    (bias_ref, bias_new_ref, g_ref, lq1, lk1, lq2, lk2, o_ref,
     qq_ref, m_ref, l_ref, acc_ref) = rest[2 * n_slots:]
    jj = pl.program_id(1)
    rows = 2 * dec_t

    def update(k, v, bias):
        m, l, acc = _df_block(qq_ref[...], k, v, bias, m_ref[...], l_ref[...], acc_ref[...])
        m_ref[...] = m
        l_ref[...] = l
        acc_ref[...] = acc

    @pl.when(jj == 0)
    def _():
        qq_ref[...] = _stack_decode_queries(q_ref[0], n_blocks)
        m_ref[...] = jnp.full(m_ref.shape, NEG, F32)
        l_ref[...] = jnp.zeros_like(l_ref)
        acc_ref[...] = jnp.zeros_like(acc_ref)
        update(kn_ref[0].astype(BF16), vn_ref[0].astype(BF16), bias_new_ref[...])

    k = jnp.concatenate([r[0].astype(BF16) for r in kc_refs], axis=0)
    v = jnp.concatenate([r[0].astype(BF16) for r in vc_refs], axis=0)
    update(k, v, bias_ref[0])

    @pl.when(jj == pl.num_programs(1) - 1)
    def _():
        lam = _lambda(lq1, lk1, lq2, lk2, lam_init)
        for p in range(n_blocks):
            sl = slice(p * rows, (p + 1) * rows)
            o_ref[0, :, p * LANES:(p + 1) * LANES] = _df_finish(
                acc_ref[sl, :], l_ref[sl, :], dec_t, lam, g_ref[...], lam_init)


def _df_decode(page_table, q, k_new, v_new, k_cache, v_cache, bias_s, bias_new, g, lams, lam_init,
               n_slots):
    b, dec_t, width = q.shape
    n_pages = page_table.shape[1]
    page_rows = k_cache.shape[1]
    n_blocks = width // LANES
    rows = n_blocks * 2 * dec_t
    n_steps = n_pages // n_slots
    per_b = lambda bi, jj, pt: (bi, 0, 0)
    const = lambda bi, jj, pt: (0, 0)
    vec = pl.BlockSpec((1, DH), const)

    def slot_spec(u):
        return pl.BlockSpec((1, page_rows, LANES), lambda bi, jj, pt: (pt[bi, jj * n_slots + u], 0, 0))

    grid_spec = pltpu.PrefetchScalarGridSpec(
        num_scalar_prefetch=1,
        grid=(b, n_steps),
        in_specs=[
            pl.BlockSpec((1, dec_t, width), per_b),
            pl.BlockSpec((1,) + k_new.shape[1:], per_b),
            pl.BlockSpec((1,) + v_new.shape[1:], per_b),
            *[slot_spec(u) for u in range(n_slots)],
            *[slot_spec(u) for u in range(n_slots)],
            pl.BlockSpec((1, rows, n_slots * page_rows),
                         lambda bi, jj, pt: (jnp.where(jj == n_steps - 1, 1, 0), 0, 0)),
            pl.BlockSpec(bias_new.shape, const),
            pl.BlockSpec((1, LANES), const),
            vec, vec, vec, vec,
        ],
        out_specs=pl.BlockSpec((1, dec_t, width), per_b),
        scratch_shapes=[pltpu.VMEM((rows, LANES), BF16),
                        pltpu.VMEM((rows, 1), F32),
                        pltpu.VMEM((rows, 1), F32),
                        pltpu.VMEM((rows, LANES), F32)],
    )
    return pl.pallas_call(
        functools.partial(_df_decode_body, dec_t=dec_t, n_blocks=n_blocks, n_slots=n_slots,
                          lam_init=lam_init),
        grid_spec=grid_spec,
        out_shape=jax.ShapeDtypeStruct((b, dec_t, width), F32),
        compiler_params=pltpu.CompilerParams(
            dimension_semantics=("arbitrary", "arbitrary"),
            vmem_limit_bytes=VMEM_LIMIT),
        name="df_decode",
    )(page_table, q, k_new, v_new, *([k_cache] * n_slots), *([v_cache] * n_slots),
      bias_s, bias_new, g, *lams)


def _router_gates(logits_t, bias_col, n_experts):
    tm = logits_t.shape[1]
    gsz = n_experts // N_GROUPS
    scores = 1.0 / (1.0 + jnp.exp(-logits_t))
    sel = scores + bias_col
    sub = lax.broadcasted_iota(I32, (gsz, tm), 0)
    group_scores = []
    for g in range(N_GROUPS):
        blk = sel[g * gsz:(g + 1) * gsz]
        m1 = jnp.max(blk, axis=0, keepdims=True)
        first = jnp.min(jnp.where(blk == m1, sub, gsz), axis=0, keepdims=True)
        m2 = jnp.max(jnp.where(sub == first, -jnp.inf, blk), axis=0, keepdims=True)
        group_scores.append(m1 + m2)
    gs = jnp.concatenate(group_scores, axis=0)
    gidx = lax.broadcasted_iota(I32, (N_GROUPS, tm), 0)
    grank = jnp.zeros((N_GROUPS, tm), I32)
    for g in range(N_GROUPS):
        row = gs[g:g + 1]
        ahead = (row > gs) | ((row == gs) & (gidx > g))
        grank = grank + ahead.astype(I32)
    gkeep = grank < TOPK_GROUPS
    masked = jnp.concatenate(
        [jnp.where(jnp.broadcast_to(gkeep[g:g + 1], (gsz, tm)), sel[g * gsz:(g + 1) * gsz], -jnp.inf)
         for g in range(N_GROUPS)], axis=0)
    eidx = lax.broadcasted_iota(I32, (n_experts, tm), 0)
    erank = jnp.zeros((n_experts, tm), I32)
    for e in range(n_experts):
        row = masked[e:e + 1]
        ahead = (row > masked) | ((row == masked) & (eidx > e))
        erank = erank + ahead.astype(I32)
    w = jnp.where(erank < TOP_K, scores, 0.0)
    return w / jnp.sum(w, axis=0, keepdims=True) * ROUTED_SCALE


def _post_body(msb_ref, mdf_ref, x_ref, wo_sb_ref, wo_df_ref, g2_ref, rw_hi_ref, rw_lo_ref,
               rb_ref, x1_ref, h2_ref, gates_ref, *, n_experts):
    att = _dot(msb_ref[...].astype(BF16), wo_sb_ref[...]) + _dot(mdf_ref[...].astype(BF16), wo_df_ref[...])
    x1 = x_ref[...] + att
    x1_ref[...] = x1
    ms = jnp.mean(x1 * x1, axis=-1, keepdims=True)
    h2 = (x1 * lax.rsqrt(ms + EPS)) * g2_ref[...]
    h2_hi, h2_lo = _split_bf16(h2)
    h2_ref[...] = h2_hi
    rw_hi = rw_hi_ref[...]
    logits_t = _dot_tb(rw_hi, h2_hi) + _dot_tb(rw_hi, h2_lo) + _dot_tb(rw_lo_ref[...], h2_hi)
    gates_t = _router_gates(logits_t, rb_ref[...], n_experts)
    tm = gates_t.shape[1]
    pad = gates_ref.shape[1] - n_experts
    gates_t = jnp.concatenate([gates_t, jnp.zeros((pad, tm), F32)], axis=0)
    gates_ref[...] = gates_t.T


def _post_attention(msb, mdf, x, wo_sb, wo_df, g2, rw_hi, rw_lo, rb, tm):
    n, d = x.shape
    n_experts = rw_hi.shape[0]
    ge = _round_up(n_experts, LANES)
    row = lambda i: (i, 0)
    const = lambda i: (0, 0)
    return pl.pallas_call(
        functools.partial(_post_body, n_experts=n_experts),
        grid=(n // tm,),
        in_specs=[
            pl.BlockSpec((tm, msb.shape[1]), row),
            pl.BlockSpec((tm, mdf.shape[1]), row),
            pl.BlockSpec((tm, d), row),
            pl.BlockSpec(wo_sb.shape, const),
            pl.BlockSpec(wo_df.shape, const),
            pl.BlockSpec((1, d), const),
            pl.BlockSpec(rw_hi.shape, const),
            pl.BlockSpec(rw_lo.shape, const),
            pl.BlockSpec((n_experts, 1), const),
        ],
        out_specs=[pl.BlockSpec((tm, d), row), pl.BlockSpec((tm, d), row), pl.BlockSpec((tm, ge), row)],
        out_shape=[jax.ShapeDtypeStruct((n, d), F32), jax.ShapeDtypeStruct((n, d), BF16),
                   jax.ShapeDtypeStruct((n, ge), F32)],
        compiler_params=pltpu.CompilerParams(
            dimension_semantics=("arbitrary",), vmem_limit_bytes=VMEM_LIMIT),
        name="post_attention",
    )(msb, mdf, x, wo_sb, wo_df, g2, rw_hi, rw_lo, rb)


def _silu(x):
    return x / (1.0 + jnp.exp(-x))


def _moe_body(h2_ref, x1_ref, gates_ref, wg_ref, wu_ref, wd_ref, sg_ref, su_ref, sd_ref,
              o_ref, acc_ref):
    e = pl.program_id(1)
    h2 = h2_ref[...]

    @pl.when(e == 0)
    def _():
        a = _silu(_dot(h2, sg_ref[...])) * _dot(h2, su_ref[...])
        acc_ref[...] = _dot(a.astype(BF16), sd_ref[...])

    g = _dot(h2, wg_ref[0].astype(BF16))
    u = _dot(h2, wu_ref[0].astype(BF16))
    f = g.shape[1]
    ge = gates_ref.shape[1]
    pick = (lax.broadcasted_iota(I32, (ge, f), 0) == e).astype(BF16)
    gate_hi, gate_lo = _split_bf16(gates_ref[...])
    gate_col = _dot(gate_hi, pick) + _dot(gate_lo, pick)
    a = _silu(g) * u * gate_col
    acc_ref[...] += _dot(a.astype(BF16), wd_ref[0].astype(BF16))

    @pl.when(e == pl.num_programs(1) - 1)
    def _():
        o_ref[...] = x1_ref[...] + acc_ref[...]


def _moe(h2, x1, gates, w_gate, w_up, w_down, sg, su, sd, tm):
    n, d = x1.shape
    n_experts, _, f = w_gate.shape
    row = lambda i, e: (i, 0)
    const = lambda i, e: (0, 0)
    return pl.pallas_call(
        _moe_body,
        grid=(n // tm, n_experts),
        in_specs=[
            pl.BlockSpec((tm, d), row),
            pl.BlockSpec((tm, d), row),
            pl.BlockSpec((tm, gates.shape[1]), row),
            pl.BlockSpec((1, d, f), lambda i, e: (e, 0, 0)),
            pl.BlockSpec((1, d, f), lambda i, e: (e, 0, 0)),
            pl.BlockSpec((1, f, d), lambda i, e: (e, 0, 0)),
            pl.BlockSpec(sg.shape, const),
            pl.BlockSpec(su.shape, const),
            pl.BlockSpec(sd.shape, const),
        ],
        out_specs=pl.BlockSpec((tm, d), row),
        out_shape=jax.ShapeDtypeStruct((n, d), F32),
        scratch_shapes=[pltpu.VMEM((tm, d), F32)],
        compiler_params=pltpu.CompilerParams(
            dimension_semantics=("arbitrary", "arbitrary"), vmem_limit_bytes=VMEM_LIMIT),
        name="moe",
    )(h2, x1, gates, w_gate, w_up, w_down, sg, su, sd)


def _largest_tile(n, cap):
    t = cap
    while n % t:
        t //= 2
    return t


def kernel(x_prompt, x_sample, cache_k_sb, cache_v_sb, cache_k_diff, cache_v_diff, page_table,
           meta_tokens, rel_bias, norm1_g, w_in, q_norm_g, k_norm_g, lambda_q1, lambda_k1,
           lambda_q2, lambda_k2, sb_out_g, diff_subln_g, w_out, norm2_g, router_w, router_bias,
           w_gate, w_up, w_down, w_shared_gate, w_shared_up, w_shared_down):
    assert w_in.shape[0] == 1, "single-layer step"
    b, seq, d = x_prompt.shape
    dec_b, dec_t, _ = x_sample.shape
    n_meta = meta_tokens.shape[0]
    width = sb_out_g.shape[1]
    n_pool, page = cache_k_sb.shape[1], cache_k_sb.shape[2]
    n_experts = router_w.shape[2]
    t = seq + n_meta
    blk = ATT_BLOCK
    tp = _round_up(t, blk)
    lam_init = 0.8 - 0.6 * math.exp(-0.3 * 0)

    w_in_b = w_in[0].astype(BF16)
    g1 = norm1_g
    reps = width // DH
    qg = jnp.tile(q_norm_g, (1, reps))
    kg = jnp.tile(k_norm_g, (1, reps))
    lane = jnp.arange(width)
    gm = ((lane[:, None] // DH) == (lane[None, :] // DH)).astype(BF16) * (1.0 / DH)
    tri_p = (jnp.arange(blk)[:, None] >= jnp.arange(blk)[None, :]).astype(BF16)
    tri_d = (jnp.arange(page)[:, None] >= jnp.arange(page)[None, :]).astype(BF16)
    lams = (lambda_q1, lambda_k1, lambda_q2, lambda_k2)
    wo_sb = w_out[0, :width].astype(BF16)
    wo_df = w_out[0, width:].astype(BF16)
    rw_t = router_w[0].T
    rw_hi = rw_t.astype(BF16)
    rw_lo = (rw_t - rw_hi.astype(F32)).astype(BF16)
    rb = router_bias[0][:, None]
    sg = w_shared_gate[0].astype(BF16)
    su = w_shared_up[0].astype(BF16)
    sd = w_shared_down[0].astype(BF16)

    n_h_sb = width // DH
    n_h_df = width // LANES
    n_slots = DECODE_PAGES_PER_STEP
    assert blk >= MAX_DISTANCE and page >= MAX_DISTANCE
    assert page_table.shape[1] % n_slots == 0 and dec_t * n_h_df <= page
    bias_p, bias_s, bias_n = _bias_tiles(rel_bias, blk, dec_t, page, n_slots)

    meta = jnp.broadcast_to(meta_tokens[None], (b, n_meta, d))
    xp = jnp.concatenate([meta, x_prompt, jnp.zeros((b, tp - t, d), F32)], axis=1)
    (k_sb_p, v_sb_p, k_df_p, v_df_p,
     qsb_b, ksb_b, vsb_b, qdf_b, kdf_b, vdf_b) = _project(xp, t, g1, w_in_b, qg, kg, gm, blk)
    msb_p = _sb_prompt(qsb_b, ksb_b, vsb_b, tri_p, sb_out_g, blk)
    mdf_p = _df_prompt(rel_bias, qdf_b, kdf_b, vdf_b, bias_p, diff_subln_g, lams, blk, lam_init)

    n_p = b * tp
    tm_p = _largest_tile(n_p, 512)
    x1_p, h2_p, gates_p = _post_attention(
        msb_p.reshape(n_p, width), mdf_p.reshape(n_p, width), xp.reshape(n_p, d),
        wo_sb, wo_df, norm2_g, rw_hi, rw_lo, rb, tm_p)
    y_p = _moe(h2_p, x1_p, gates_p, w_gate[0], w_up[0], w_down[0], sg, su, sd,
               _largest_tile(n_p, 1024))
    y_prompt = y_p.reshape(b, tp, d)[:, n_meta:t]

    n_s = dec_b * dec_t
    xs = x_sample.reshape(1, n_s, d)
    (k_sb_s, v_sb_s, k_df_s, v_df_s,
     qsb_s, _, _, qdf_s, _, _) = _project(xs, n_s, g1, w_in_b, qg, kg, gm, n_s)

    def new_page(rows, n_rows):
        r = rows.reshape(dec_b, n_rows, -1)
        return jnp.pad(r, ((0, 0), (0, page - n_rows), (0, 0)))

    transposed_pages = lambda c: jnp.transpose(c[0], (0, 2, 3, 1)).reshape(n_pool, width, page)
    interleaved_rows = lambda c: c.reshape(n_pool, page * n_h_df, LANES)
    msb_s = _sb_decode(page_table, qsb_s.reshape(dec_b, dec_t, width),
                       new_page(k_sb_s, dec_t), new_page(v_sb_s, dec_t),
                       transposed_pages(cache_k_sb), transposed_pages(cache_v_sb), tri_d, sb_out_g)
    mdf_s = _df_decode(page_table, qdf_s.reshape(dec_b, dec_t, width),
                       new_page(k_df_s, dec_t * n_h_df), new_page(v_df_s, dec_t * n_h_df),
                       interleaved_rows(cache_k_diff), interleaved_rows(cache_v_diff),
                       bias_s, bias_n, diff_subln_g, lams, lam_init, n_slots)
    x1_s, h2_s, gates_s = _post_attention(
        msb_s.reshape(n_s, width), mdf_s.reshape(n_s, width), x_sample.reshape(n_s, d),
        wo_sb, wo_df, norm2_g, rw_hi, rw_lo, rb, n_s)
    y_s = _moe(h2_s, x1_s, gates_s, w_gate[0], w_up[0], w_down[0], sg, su, sd, n_s)
    y_sample = y_s.reshape(dec_b, dec_t, d)

    sb_rows = lambda r, bb, tt: r.reshape(1, bb, tt, n_h_sb, DH)
    df_rows = lambda r, bb, tt: r.reshape(1, bb, tt, n_h_df, LANES)
    return (y_prompt, y_sample,
            sb_rows(k_sb_p, b, t), sb_rows(v_sb_p, b, t), df_rows(k_df_p, b, t), df_rows(v_df_p, b, t),
            sb_rows(k_sb_s, dec_b, dec_t), sb_rows(v_sb_s, dec_b, dec_t),
            df_rows(k_df_s, dec_b, dec_t), df_rows(v_df_s, dec_b, dec_t))
```

```python
import functools
import math

import jax
import jax.numpy as jnp
import numpy as np
from jax import lax
from jax.experimental import pallas as pl
from jax.experimental.pallas import tpu as pltpu

F32 = jnp.float32
BF16 = jnp.bfloat16
I32 = jnp.int32

EPS = 1e-6
DH = 64
LANES = 128
MAX_DISTANCE = 128
TOP_K = 8
N_GROUPS = 8
TOPK_GROUPS = 4
ROUTED_SCALE = 2.5
NEG = -1e30
SB_CUTOFF = -104.0
ATT_BLOCK = 256
DECODE_PAGES_PER_STEP = 4
MOE_TILE = 256
VMEM_LIMIT = 48 * 1024 * 1024

_TRANS_B = (((1,), (1,)), ((), ()))


def _dot(a, b):
    return jnp.dot(a, b, preferred_element_type=F32)


def _dot_tb(a, b):
    return lax.dot_general(a, b, _TRANS_B, preferred_element_type=F32)


def _split_bf16(x):
    hi = x.astype(BF16)
    lo = (x - hi.astype(F32)).astype(BF16)
    return hi, lo


def _round_up(n, m):
    return (n + m - 1) // m * m


def _proj_body(x_ref, g1_ref, w_ref, qg_ref, kg_ref, gm_ref,
               ksb_o, vsb_o, kdf_o, vdf_o,
               qsb_b, ksb_b, vsb_b, qdf_b, kdf_b, vdf_b, *, width):
    x = x_ref[0]
    ms = jnp.mean(x * x, axis=-1, keepdims=True)
    h = (x * lax.rsqrt(ms + EPS)) * g1_ref[...]
    proj = _dot(h.astype(BF16), w_ref[...])
    w = width
    q_sb, k_sb, v_sb = proj[:, 0:w], proj[:, w:2 * w], proj[:, 2 * w:3 * w]
    q_df, k_df, v_df = proj[:, 3 * w:4 * w], proj[:, 4 * w:5 * w], proj[:, 5 * w:6 * w]

    def map_norm(t, g_ref):
        msq = _dot((t * t).astype(BF16), gm_ref[...])
        return (t * lax.rsqrt(msq + EPS)) * g_ref[...]

    q_df = map_norm(q_df, qg_ref)
    k_df = map_norm(k_df, kg_ref)
    ksb_o[0] = k_sb
    vsb_o[0] = v_sb
    kdf_o[0] = k_df
    vdf_o[0] = v_df
    scale = DH ** -0.5
    qsb_b[0] = (q_sb * scale).astype(BF16)
    ksb_b[0] = k_sb.astype(BF16)
    vsb_b[0] = v_sb.astype(BF16)
    qdf_b[0] = (q_df * scale).astype(BF16)
    kdf_b[0] = k_df.astype(BF16)
    vdf_b[0] = v_df.astype(BF16)


def _project(x_pad, t_valid, g1, w_in_b, qg, kg, gm, tm):
    b, tp, d = x_pad.shape
    width = w_in_b.shape[1] // 6
    grid = (b, tp // tm)
    row = lambda bi, i: (bi, i, 0)
    const = lambda bi, i: (0, 0)
    f32_out = jax.ShapeDtypeStruct((b, t_valid, width), F32)
    bf_out = jax.ShapeDtypeStruct((b, tp, width), BF16)
    out_spec = pl.BlockSpec((1, tm, width), row)
    return pl.pallas_call(
        functools.partial(_proj_body, width=width),
        grid=grid,
        in_specs=[
            pl.BlockSpec((1, tm, d), row),
            pl.BlockSpec((1, d), const),
            pl.BlockSpec(w_in_b.shape, const),
            pl.BlockSpec((1, width), const),
            pl.BlockSpec((1, width), const),
            pl.BlockSpec((width, width), const),
        ],
        out_specs=[out_spec] * 10,
        out_shape=[f32_out] * 4 + [bf_out] * 6,
        compiler_params=pltpu.CompilerParams(
            dimension_semantics=("arbitrary", "arbitrary"),
            vmem_limit_bytes=VMEM_LIMIT),
        name="proj",
    )(x_pad, g1, w_in_b, qg, kg, gm)


def _stack_halves(q):
    lane = lax.broadcasted_iota(I32, (1, LANES), 1)
    zero = jnp.zeros_like(q)
    return jnp.concatenate(
        [jnp.where(lane < DH, q, zero), jnp.where(lane >= DH, q, zero)], axis=0)


def _neg_softplus(z):
    return -(jnp.maximum(z, 0.0) + jnp.log(1.0 + jnp.exp(-jnp.abs(z))))


def _sb_block(qq, k, v, tri, acc, carry, vis):
    z = _dot_tb(qq, k)
    log_keep = _neg_softplus(z)
    if vis is not None:
        log_keep = jnp.where(vis, log_keep, 0.0)
    hi, lo = _split_bf16(log_keep)
    csum = _dot(hi, tri) + _dot(lo, tri) + carry
    w = jnp.exp(z + csum)
    if vis is not None:
        w = jnp.where(vis, w, 0.0)
    acc = acc + _dot(w.astype(BF16), v)
    return acc, csum[:, 0:1]


def _sb_finish(acc, t, g):
    lane = lax.broadcasted_iota(I32, (1, LANES), 1)
    lo_half = lane < DH
    o = jnp.where(lo_half, acc[:t], acc[t:])
    o2 = o * o
    s_lo = jnp.sum(jnp.where(lo_half, o2, 0.0), axis=-1, keepdims=True)
    s_hi = jnp.sum(jnp.where(lo_half, 0.0, o2), axis=-1, keepdims=True)
    ms = jnp.where(lo_half, s_lo, s_hi) * (1.0 / DH)
    return (o * lax.rsqrt(ms + EPS)) * g


def _df_block(qq, k, v, bias, m, l, acc):
    s = _dot_tb(qq, k) + bias
    m_new = jnp.maximum(m, jnp.max(s, axis=-1, keepdims=True))
    alpha = jnp.exp(m - m_new)
    p = jnp.exp(s - m_new)
    l = alpha * l + jnp.sum(p, axis=-1, keepdims=True)
    acc = alpha * acc + _dot(p.astype(BF16), v)
    return m_new, l, acc


def _lambda(lq1, lk1, lq2, lk2, lam_init):
    s1 = jnp.sum(lq1[...] * lk1[...], axis=-1, keepdims=True)
    s2 = jnp.sum(lq2[...] * lk2[...], axis=-1, keepdims=True)
    return jnp.exp(s1) - jnp.exp(s2) + lam_init


def _df_finish(acc, l, t, lam, g, lam_init):
    o = acc[:t] / l[:t] - lam * (acc[t:] / l[t:])
    ms = jnp.mean(o * o, axis=-1, keepdims=True)
    return (o * lax.rsqrt(ms + EPS)) * (g * (1.0 - lam_init))


def _bias_tile(tab_ref, h, rel, n_buckets):
    max_exact = n_buckets // 2
    n = jnp.maximum(rel, 0)
    nf = jnp.maximum(n, 1).astype(F32)
    large = max_exact + (jnp.log(nf / max_exact) / math.log(MAX_DISTANCE / max_exact)
                         * (n_buckets - max_exact)).astype(I32)
    large = jnp.minimum(large, n_buckets - 1)
    bucket = jnp.where(n < max_exact, n, large)
    bias = jnp.zeros(rel.shape, F32)
    for b in range(n_buckets):
        bias = jnp.where(bucket == b, tab_ref[b, h], bias)
    return jnp.where(rel >= 0, bias, NEG)


def _bias_body(tab_ref, bp_ref, bs_ref, bn_ref, *, n_buckets, n_heads, blk, dec_t, page, n_slots):
    r = lax.broadcasted_iota(I32, (blk, blk), 0)
    c = lax.broadcasted_iota(I32, (blk, blk), 1)
    cols = page * n_heads
    rs = lax.broadcasted_iota(I32, (dec_t, cols), 0)
    cs = lax.broadcasted_iota(I32, (dec_t, cols), 1)
    pos, head = cs // n_heads, cs % n_heads
    rn = lax.broadcasted_iota(I32, (dec_t, page), 0)
    cn = lax.broadcasted_iota(I32, (dec_t, page), 1)
    pos_n, head_n = cn // n_heads, cn % n_heads
    for h in range(n_heads):
        for d in range(2):
            bp_ref[h, d] = _bias_tile(tab_ref, h, d * blk + r - c, n_buckets)
        far = jnp.where(head == h, tab_ref[n_buckets - 1, h], NEG)
        last = jnp.where(head == h, _bias_tile(tab_ref, h, page + rs - pos, n_buckets), NEG)
        new = jnp.where((head_n == h) & (pos_n < dec_t),
                        _bias_tile(tab_ref, h, rn - pos_n, n_buckets), NEG)
        for mp in range(2):
            rows = pl.ds((2 * h + mp) * dec_t, dec_t)
            bn_ref[rows, :] = new
            for u in range(n_slots):
                bs_ref[0, rows, u * cols:(u + 1) * cols] = far
                bs_ref[1, rows, u * cols:(u + 1) * cols] = last if u == n_slots - 1 else far


def _bias_tiles(rel_bias, blk, dec_t, page, n_slots):
    n_buckets, n_heads = rel_bias.shape
    rows = 2 * n_heads * dec_t
    return pl.pallas_call(
        functools.partial(_bias_body, n_buckets=n_buckets, n_heads=n_heads, blk=blk,
                          dec_t=dec_t, page=page, n_slots=n_slots),
        in_specs=[pl.BlockSpec(memory_space=pltpu.SMEM)],
        out_shape=[jax.ShapeDtypeStruct((n_heads, 2, blk, blk), F32),
                   jax.ShapeDtypeStruct((2, rows, n_slots * page * n_heads), F32),
                   jax.ShapeDtypeStruct((rows, page), F32)],
        name="bias_tiles",
    )(rel_bias)


def _sb_prompt_body(q_ref, k_ref, v_ref, tri_ref, g_ref, o_ref, *, blk):
    i = pl.program_id(2)
    qq = _stack_halves(q_ref[0])
    tri = tri_ref[...]
    r = lax.broadcasted_iota(I32, (2 * blk, blk), 0)
    r = jnp.where(r >= blk, r - blk, r)
    c = lax.broadcasted_iota(I32, (2 * blk, blk), 1)
    vis = c < r

    def load(j):
        start = pl.multiple_of(j * blk, blk)
        return k_ref[0, pl.ds(start, blk), :], v_ref[0, pl.ds(start, blk), :]

    acc = jnp.zeros((2 * blk, LANES), F32)
    carry = jnp.zeros((2 * blk, 1), F32)
    k, v = load(i)
    acc, carry = _sb_block(qq, k, v, tri, acc, carry, vis)

    def more(state):
        jj, _, carry = state
        return (jj < i) & (jnp.max(carry) > SB_CUTOFF)

    def body(state):
        jj, acc, carry = state
        k, v = load(i - 1 - jj)
        acc, carry = _sb_block(qq, k, v, tri, acc, carry, None)
        return jj + 1, acc, carry

    _, acc, _ = lax.while_loop(more, body, (jnp.int32(0), acc, carry))
    o_ref[0] = _sb_finish(acc, blk, g_ref[...]).astype(o_ref.dtype)


def _sb_prompt(q_b, k_b, v_b, tri, g, blk):
    b, tp, width = q_b.shape
    n_pairs = width // LANES
    grid = (b, n_pairs, tp // blk)
    return pl.pallas_call(
        functools.partial(_sb_prompt_body, blk=blk),
        grid=grid,
        in_specs=[
            pl.BlockSpec((1, blk, LANES), lambda bi, p, i: (bi, i, p)),
            pl.BlockSpec((1, tp, LANES), lambda bi, p, i: (bi, 0, p)),
            pl.BlockSpec((1, tp, LANES), lambda bi, p, i: (bi, 0, p)),
            pl.BlockSpec((blk, blk), lambda bi, p, i: (0, 0)),
            pl.BlockSpec((1, LANES), lambda bi, p, i: (0, p)),
        ],
        out_specs=pl.BlockSpec((1, blk, LANES), lambda bi, p, i: (bi, i, p)),
        out_shape=jax.ShapeDtypeStruct((b, tp, width), BF16),
        compiler_params=pltpu.CompilerParams(
            dimension_semantics=("arbitrary", "arbitrary", "arbitrary"),
            vmem_limit_bytes=VMEM_LIMIT),
        name="sb_prompt",
    )(q_b, k_b, v_b, tri, g)


def _df_prompt_body(tab_ref, q_ref, k_ref, v_ref, bias_ref, g_ref, lq1, lk1, lq2, lk2,
                    o_ref, *, blk, lam_init, n_buckets):
    h = pl.program_id(1)
    i = pl.program_id(2)
    qq = _stack_halves(q_ref[0])

    def load(j):
        start = pl.multiple_of(j * blk, blk)
        return k_ref[0, pl.ds(start, blk), :], v_ref[0, pl.ds(start, blk), :]

    def tile_bias(d):
        t = bias_ref[0, d]
        return jnp.concatenate([t, t], axis=0)

    m = jnp.full((2 * blk, 1), NEG, F32)
    l = jnp.zeros((2 * blk, 1), F32)
    acc = jnp.zeros((2 * blk, LANES), F32)
    k, v = load(i)
    m, l, acc = _df_block(qq, k, v, tile_bias(0), m, l, acc)

    def near(state):
        k, v = load(i - 1)
        return _df_block(qq, k, v, tile_bias(1), *state)

    m, l, acc = lax.cond(i >= 1, near, lambda s: s, (m, l, acc))
    far_bias = tab_ref[n_buckets - 1, h]

    def body(jj, state):
        k, v = load(i - 2 - jj)
        return _df_block(qq, k, v, far_bias, *state)

    m, l, acc = lax.fori_loop(0, jnp.maximum(i - 1, 0), body, (m, l, acc))
    lam = _lambda(lq1, lk1, lq2, lk2, lam_init)
    o_ref[0] = _df_finish(acc, l, blk, lam, g_ref[...], lam_init).astype(o_ref.dtype)


def _df_prompt(rel_bias, q_b, k_b, v_b, bias_p, g, lams, blk, lam_init):
    b, tp, width = q_b.shape
    n_heads = width // LANES
    grid = (b, n_heads, tp // blk)
    vec = pl.BlockSpec((1, DH), lambda bi, h, i: (0, 0))
    return pl.pallas_call(
        functools.partial(_df_prompt_body, blk=blk, lam_init=lam_init,
                          n_buckets=rel_bias.shape[0]),
        grid=grid,
        in_specs=[
            pl.BlockSpec(memory_space=pltpu.SMEM),
            pl.BlockSpec((1, blk, LANES), lambda bi, h, i: (bi, i, h)),
            pl.BlockSpec((1, tp, LANES), lambda bi, h, i: (bi, 0, h)),
            pl.BlockSpec((1, tp, LANES), lambda bi, h, i: (bi, 0, h)),
            pl.BlockSpec((1, 2, blk, blk), lambda bi, h, i: (h, 0, 0, 0)),
            pl.BlockSpec((1, LANES), lambda bi, h, i: (0, 0)),
            vec, vec, vec, vec,
        ],
        out_specs=pl.BlockSpec((1, blk, LANES), lambda bi, h, i: (bi, i, h)),
        out_shape=jax.ShapeDtypeStruct((b, tp, width), BF16),
        compiler_params=pltpu.CompilerParams(
            dimension_semantics=("arbitrary", "arbitrary", "arbitrary"),
            vmem_limit_bytes=VMEM_LIMIT),
        name="df_prompt",
    )(rel_bias, q_b, k_b, v_b, bias_p, g, *lams)


def _stack_decode_queries(q, n_blocks):
    q = q.astype(F32)
    return jnp.concatenate(
        [_stack_halves(q[:, p * LANES:(p + 1) * LANES]) for p in range(n_blocks)], axis=0).astype(BF16)


def _sb_decode_body(pt_ref, q_ref, kn_ref, vn_ref, kc_hbm, vc_hbm, tri_ref, g_ref, o_ref,
                    kbuf, vbuf, sem, *, dec_t, n_blocks, page, n_pages):
    bi = pl.program_id(0)
    rows = 2 * dec_t
    qq = _stack_decode_queries(q_ref[0], n_blocks)
    tri = tri_ref[...]

    def page_copies(slot, jj):
        pg = pt_ref[bi, n_pages - 1 - jj]
        return (pltpu.make_async_copy(kc_hbm.at[pg], kbuf.at[slot], sem.at[0, slot]),
                pltpu.make_async_copy(vc_hbm.at[pg], vbuf.at[slot], sem.at[1, slot]))

    def start(slot, jj):
        for cp in page_copies(slot, jj):
            cp.start()

    def wait(slot, jj):
        for cp in page_copies(slot, jj):
            cp.wait()

    start(0, 0)

    def weights(z, carry, vis):
        log_keep = _neg_softplus(z)
        if vis is not None:
            log_keep = jnp.where(vis, log_keep, 0.0)
        hi, lo = _split_bf16(log_keep)
        csum = _dot(hi, tri) + _dot(lo, tri) + carry
        w = jnp.exp(z + csum)
        if vis is not None:
            w = jnp.where(vis, w, 0.0)
        return w.astype(BF16), csum[:, 0:1]

    r = lax.broadcasted_iota(I32, (dec_t, page), 0)
    c = lax.broadcasted_iota(I32, (dec_t, page), 1)
    vis = jnp.concatenate([c < r] * (2 * n_blocks), axis=0)
    z = jnp.concatenate(
        [_dot_tb(qq[p * rows:(p + 1) * rows], kn_ref[0, :, p * LANES:(p + 1) * LANES].astype(BF16))
         for p in range(n_blocks)], axis=0)
    w, carry = weights(z, jnp.zeros((n_blocks * rows, 1), F32), vis)
    acc = jnp.concatenate(
        [_dot(w[p * rows:(p + 1) * rows], vn_ref[0, :, p * LANES:(p + 1) * LANES].astype(BF16))
         for p in range(n_blocks)], axis=0)

    def more(state):
        jj, _, carry = state
        return (jj < n_pages) & (jnp.max(carry) > SB_CUTOFF)

    def body(state):
        jj, acc, carry = state
        slot = jj & 1
        wait(slot, jj)

        @pl.when(jj + 1 < n_pages)
        def _():
            start(1 - slot, jj + 1)

        kt = kbuf[slot]
        vt = vbuf[slot]
        z = jnp.concatenate(
            [_dot(qq[p * rows:(p + 1) * rows], kt[p * LANES:(p + 1) * LANES, :].astype(BF16))
             for p in range(n_blocks)], axis=0)
        w, carry = weights(z, carry, None)
        acc = acc + jnp.concatenate(
            [_dot_tb(w[p * rows:(p + 1) * rows], vt[p * LANES:(p + 1) * LANES, :].astype(BF16))
             for p in range(n_blocks)], axis=0)
        return jj + 1, acc, carry

    jj, acc, _ = lax.while_loop(more, body, (jnp.int32(0), acc, carry))

    @pl.when(jj < n_pages)
    def _():
        wait(jj & 1, jj)

    for p in range(n_blocks):
        o_ref[0, :, p * LANES:(p + 1) * LANES] = _sb_finish(
            acc[p * rows:(p + 1) * rows, :], dec_t, g_ref[:, p * LANES:(p + 1) * LANES])


def _sb_decode(page_table, q, k_new, v_new, kt_cache, vt_cache, tri, g):
    b, dec_t, width = q.shape
    n_pages = page_table.shape[1]
    page = kt_cache.shape[2]
    n_blocks = width // LANES
    per_b = lambda bi, pt: (bi, 0, 0)
    const = lambda bi, pt: (0, 0)
    grid_spec = pltpu.PrefetchScalarGridSpec(
        num_scalar_prefetch=1,
        grid=(b,),
        in_specs=[
            pl.BlockSpec((1, dec_t, width), per_b),
            pl.BlockSpec((1, page, width), per_b),
            pl.BlockSpec((1, page, width), per_b),
            pl.BlockSpec(memory_space=pl.ANY),
            pl.BlockSpec(memory_space=pl.ANY),
            pl.BlockSpec((page, page), const),
            pl.BlockSpec((1, width), const),
        ],
        out_specs=pl.BlockSpec((1, dec_t, width), per_b),
        scratch_shapes=[pltpu.VMEM((2, width, page), F32),
                        pltpu.VMEM((2, width, page), F32),
                        pltpu.SemaphoreType.DMA((2, 2))],
    )
    return pl.pallas_call(
        functools.partial(_sb_decode_body, dec_t=dec_t, n_blocks=n_blocks, page=page, n_pages=n_pages),
        grid_spec=grid_spec,
        out_shape=jax.ShapeDtypeStruct((b, dec_t, width), F32),
        compiler_params=pltpu.CompilerParams(
            dimension_semantics=("arbitrary",), vmem_limit_bytes=VMEM_LIMIT),
        name="sb_decode",
    )(page_table, q, k_new, v_new, kt_cache, vt_cache, tri, g)


def _df_decode_body(pt_ref, q_ref, kn_ref, vn_ref, *rest, dec_t, n_blocks, n_slots, lam_init):
    kc_refs, vc_refs = rest[:n_slots], rest[n_slots:2 * n_slots]
    (bias_ref, bias_new_ref, g_ref, lq1, lk1, lq2, lk2, o_ref,
     qq_ref, m_ref, l_ref, acc_ref) = rest[2 * n_slots:]
    jj = pl.program_id(1)
    rows = 2 * dec_t

    def update(k, v, bias):
        m, l, acc = _df_block(qq_ref[...], k, v, bias, m_ref[...], l_ref[...], acc_ref[...])
        m_ref[...] = m
        l_ref[...] = l
        acc_ref[...] = acc

    @pl.when(jj == 0)
    def _():
        qq_ref[...] = _stack_decode_queries(q_ref[0], n_blocks)
        m_ref[...] = jnp.full(m_ref.shape, NEG, F32)
        l_ref[...] = jnp.zeros_like(l_ref)
        acc_ref[...] = jnp.zeros_like(acc_ref)
        update(kn_ref[0].astype(BF16), vn_ref[0].astype(BF16), bias_new_ref[...])

    k = jnp.concatenate([r[0].astype(BF16) for r in kc_refs], axis=0)
    v = jnp.concatenate([r[0].astype(BF16) for r in vc_refs], axis=0)
    update(k, v, bias_ref[0])

    @pl.when(jj == pl.num_programs(1) - 1)
    def _():
        lam = _lambda(lq1, lk1, lq2, lk2, lam_init)
        for p in range(n_blocks):
            sl = slice(p * rows, (p + 1) * rows)
            o_ref[0, :, p * LANES:(p + 1) * LANES] = _df_finish(
                acc_ref[sl, :], l_ref[sl, :], dec_t, lam, g_ref[...], lam_init)


def _df_decode(page_table, q, k_new, v_new, k_cache, v_cache, bias_s, bias_new, g, lams, lam_init,
               n_slots):
    b, dec_t, width = q.shape
    n_pages = page_table.shape[1]
    page_rows = k_cache.shape[1]
    n_blocks = width // LANES
    rows = n_blocks * 2 * dec_t
    n_steps = n_pages // n_slots
    per_b = lambda bi, jj, pt: (bi, 0, 0)
    const = lambda bi, jj, pt: (0, 0)
    vec = pl.BlockSpec((1, DH), const)

    def slot_spec(u):
        return pl.BlockSpec((1, page_rows, LANES), lambda bi, jj, pt: (pt[bi, jj * n_slots + u], 0, 0))

    grid_spec = pltpu.PrefetchScalarGridSpec(
        num_scalar_prefetch=1,
        grid=(b, n_steps),
        in_specs=[
            pl.BlockSpec((1, dec_t, width), per_b),
            pl.BlockSpec((1,) + k_new.shape[1:], per_b),
            pl.BlockSpec((1,) + v_new.shape[1:], per_b),
            *[slot_spec(u) for u in range(n_slots)],
            *[slot_spec(u) for u in range(n_slots)],
            pl.BlockSpec((1, rows, n_slots * page_rows),
                         lambda bi, jj, pt: (jnp.where(jj == n_steps - 1, 1, 0), 0, 0)),
            pl.BlockSpec(bias_new.shape, const),
            pl.BlockSpec((1, LANES), const),
            vec, vec, vec, vec,
        ],
        out_specs=pl.BlockSpec((1, dec_t, width), per_b),
        scratch_shapes=[pltpu.VMEM((rows, LANES), BF16),
                        pltpu.VMEM((rows, 1), F32),
                        pltpu.VMEM((rows, 1), F32),
                        pltpu.VMEM((rows, LANES), F32)],
    )
    return pl.pallas_call(
        functools.partial(_df_decode_body, dec_t=dec_t, n_blocks=n_blocks, n_slots=n_slots,
                          lam_init=lam_init),
        grid_spec=grid_spec,
        out_shape=jax.ShapeDtypeStruct((b, dec_t, width), F32),
        compiler_params=pltpu.CompilerParams(
            dimension_semantics=("arbitrary", "arbitrary"),
            vmem_limit_bytes=VMEM_LIMIT),
        name="df_decode",
    )(page_table, q, k_new, v_new, *([k_cache] * n_slots), *([v_cache] * n_slots),
      bias_s, bias_new, g, *lams)


def _router_gates(logits_t, bias_col, n_experts):
    tm = logits_t.shape[1]
    gsz = n_experts // N_GROUPS
    scores = 1.0 / (1.0 + jnp.exp(-logits_t))
    sel = scores + bias_col
    sub = lax.broadcasted_iota(I32, (gsz, tm), 0)
    group_scores = []
    for g in range(N_GROUPS):
        blk = sel[g * gsz:(g + 1) * gsz]
        m1 = jnp.max(blk, axis=0, keepdims=True)
        first = jnp.min(jnp.where(blk == m1, sub, gsz), axis=0, keepdims=True)
        m2 = jnp.max(jnp.where(sub == first, -jnp.inf, blk), axis=0, keepdims=True)
        group_scores.append(m1 + m2)
    gs = jnp.concatenate(group_scores, axis=0)
    gidx = lax.broadcasted_iota(I32, (N_GROUPS, tm), 0)
    grank = jnp.zeros((N_GROUPS, tm), I32)
    for g in range(N_GROUPS):
        row = gs[g:g + 1]
        ahead = (row > gs) | ((row == gs) & (gidx > g))
        grank = grank + ahead.astype(I32)
    gkeep = grank < TOPK_GROUPS
    masked = jnp.concatenate(
        [jnp.where(jnp.broadcast_to(gkeep[g:g + 1], (gsz, tm)), sel[g * gsz:(g + 1) * gsz], -jnp.inf)
         for g in range(N_GROUPS)], axis=0)
    eidx = lax.broadcasted_iota(I32, (n_experts, tm), 0)
    erank = jnp.zeros((n_experts, tm), I32)
    for e in range(n_experts):
        row = masked[e:e + 1]
        ahead = (row > masked) | ((row == masked) & (eidx > e))
        erank = erank + ahead.astype(I32)
    w = jnp.where(erank < TOP_K, scores, 0.0)
    return w / jnp.sum(w, axis=0, keepdims=True) * ROUTED_SCALE, erank


def _post_body(msb_p, mdf_p, x_p, msb_s, mdf_s, x_s, wo_sb_ref, wo_df_ref, g2_ref, rw_hi_ref,
               rw_lo_ref, rb_ref, upper_ref,
               x1_ref, h2_ref, h2g_ref, gates_ref, expert_ref, rank_ref, count_ref, base_ref,
               *, n_experts, tiles_p, t_pad, t_valid, n_dec):
    i = pl.program_id(0)
    tm = x_p.shape[0]
    is_dec = i >= tiles_p
    msb = jnp.where(is_dec, msb_s[...].astype(BF16), msb_p[...])
    mdf = jnp.where(is_dec, mdf_s[...].astype(BF16), mdf_p[...])
    x = jnp.where(is_dec, x_s[...], x_p[...])
    x1 = x + _dot(msb, wo_sb_ref[...]) + _dot(mdf, wo_df_ref[...])
    x1_ref[...] = x1
    ms = jnp.mean(x1 * x1, axis=-1, keepdims=True)
    h2 = (x1 * lax.rsqrt(ms + EPS)) * g2_ref[...]
    h2_hi, h2_lo = _split_bf16(h2)
    h2_ref[...] = h2_hi
    for s in range(h2g_ref.shape[1]):
        h2g_ref[:, s, :] = h2_hi[:, s * LANES:(s + 1) * LANES]
    rw_hi = rw_hi_ref[...]
    logits_t = _dot_tb(rw_hi, h2_hi) + _dot_tb(rw_hi, h2_lo) + _dot_tb(rw_lo_ref[...], h2_hi)
    gates_t, erank = _router_gates(logits_t, rb_ref[...], n_experts)

    lane = lax.broadcasted_iota(I32, (1, tm), 1)
    row0 = jnp.where(is_dec, (i - tiles_p) * tm, lax.rem(i * tm, t_pad))
    limit = jnp.where(is_dec, n_dec, t_valid)
    chosen = (erank < TOP_K) & ((row0 + lane) < limit)

    @pl.when(i == 0)
    def _():
        base_ref[...] = jnp.zeros_like(base_ref)

    chosen_f = chosen.astype(F32)
    running = _dot(chosen_f.astype(BF16), upper_ref[...])
    slot = base_ref[...] + running - chosen_f
    base_ref[...] = base_ref[...] + running[:, tm - 1:tm]
    count_ref[...] = jnp.broadcast_to(base_ref[...], count_ref.shape)

    eidx = lax.broadcasted_iota(I32, (n_experts, tm), 0).astype(F32)
    experts, slots, gates = [], [], []
    for k in range(TOP_K):
        hit = chosen & (erank == k)
        experts.append(jnp.sum(jnp.where(hit, eidx, 0.0), axis=0, keepdims=True))
        slots.append(jnp.sum(jnp.where(hit, slot, 0.0), axis=0, keepdims=True))
        gates.append(jnp.sum(jnp.where(hit, gates_t, 0.0), axis=0, keepdims=True))
    expert_ref[...] = jnp.concatenate(experts, axis=0).astype(I32)
    rank_ref[...] = jnp.concatenate(slots, axis=0).astype(I32)
    pad = gates_ref.shape[1] - TOP_K
    gates_k = jnp.concatenate(gates + [jnp.zeros((pad, tm), F32)], axis=0)
    gates_ref[...] = gates_k.T


def _post_attention(msb_p, mdf_p, x_p, msb_s, mdf_s, x_s, wo_sb, wo_df, g2, rw_hi, rw_lo, rb,
                    tm, t_pad, t_valid, n_dec):
    n_p, d = x_p.shape
    n_s = x_s.shape[0]
    n_experts = rw_hi.shape[0]
    tiles_p, tiles_s = n_p // tm, n_s // tm
    n = n_p + n_s
    upper = (jnp.arange(tm)[:, None] <= jnp.arange(tm)[None, :]).astype(BF16)
    prow = lambda i: (jnp.minimum(i, tiles_p - 1), 0)
    srow = lambda i: (jnp.maximum(i - tiles_p, 0), 0)
    row = lambda i: (i, 0)
    col = lambda i: (0, i)
    const = lambda i: (0, 0)
    w = msb_p.shape[1]
    return pl.pallas_call(
        functools.partial(_post_body, n_experts=n_experts, tiles_p=tiles_p, t_pad=t_pad,
                          t_valid=t_valid, n_dec=n_dec),
        grid=(tiles_p + tiles_s,),
        in_specs=[
            pl.BlockSpec((tm, w), prow), pl.BlockSpec((tm, w), prow), pl.BlockSpec((tm, d), prow),
            pl.BlockSpec((tm, w), srow), pl.BlockSpec((tm, w), srow), pl.BlockSpec((tm, d), srow),
            pl.BlockSpec(wo_sb.shape, const),
            pl.BlockSpec(wo_df.shape, const),
            pl.BlockSpec((1, d), const),
            pl.BlockSpec(rw_hi.shape, const),
            pl.BlockSpec(rw_lo.shape, const),
            pl.BlockSpec((n_experts, 1), const),
            pl.BlockSpec((tm, tm), const),
        ],
        out_specs=[pl.BlockSpec((tm, d), row), pl.BlockSpec((tm, d), row),
                   pl.BlockSpec((tm, d // LANES, LANES), lambda i: (i, 0, 0)),
                   pl.BlockSpec((tm, LANES), row),
                   pl.BlockSpec((TOP_K, tm), col), pl.BlockSpec((TOP_K, tm), col),
                   pl.BlockSpec((n_experts, LANES), const)],
        out_shape=[jax.ShapeDtypeStruct((n, d), F32), jax.ShapeDtypeStruct((n, d), BF16),
                   jax.ShapeDtypeStruct((n, d // LANES, LANES), BF16),
                   jax.ShapeDtypeStruct((n, LANES), F32),
                   jax.ShapeDtypeStruct((TOP_K, n), I32), jax.ShapeDtypeStruct((TOP_K, n), I32),
                   jax.ShapeDtypeStruct((n_experts, LANES), F32)],
        scratch_shapes=[pltpu.VMEM((n_experts, 1), F32)],
        compiler_params=pltpu.CompilerParams(
            dimension_semantics=("arbitrary",), vmem_limit_bytes=VMEM_LIMIT),
        name="post_attention",
    )(msb_p, mdf_p, x_p, msb_s, mdf_s, x_s, wo_sb, wo_df, g2, rw_hi, rw_lo, rb, upper)


def _silu(x):
    return x / (1.0 + jnp.exp(-x))


def _dispatch_body(start_ref, end_ref, nvalid_ref, expert_ref, rank_ref, src_hbm, dst_hbm,
                   zero_ref, zsem, sem, *, tile, chunk, n_experts):
    i = pl.program_id(0)

    @pl.when(i == 0)
    def _():
        zero_ref[...] = jnp.zeros_like(zero_ref)

        def zero_copy(e):
            return pltpu.make_async_copy(zero_ref, dst_hbm.at[pl.ds(end_ref[e] - tile, tile)], zsem)

        def fill(e, c):
            @pl.when(end_ref[e] > start_ref[e])
            def _():
                zero_copy(e).start()
            return c

        def drain(e, c):
            @pl.when(end_ref[e] > start_ref[e])
            def _():
                zero_copy(e).wait()
            return c

        lax.fori_loop(0, n_experts, fill, 0)
        lax.fori_loop(0, n_experts, drain, 0)

    def row_copy(r, k, slot):
        dst = start_ref[expert_ref[k, r]] + rank_ref[k, r]
        return pltpu.make_async_copy(src_hbm.at[i * chunk + r], dst_hbm.at[dst], sem.at[slot])

    slot = i & 1

    def issue(r, c):
        for k in range(TOP_K):
            row_copy(r, k, slot).start()
        return c

    lax.fori_loop(0, nvalid_ref[i], issue, 0)

    def wait_rows(n_rows, slot):
        def wait_one(r, c):
            for k in range(TOP_K):
                pltpu.make_async_copy(src_hbm.at[0], dst_hbm.at[0], sem.at[slot]).wait()
            return c
        lax.fori_loop(0, n_rows, wait_one, 0)

    @pl.when(i > 0)
    def _():
        wait_rows(nvalid_ref[jnp.maximum(i - 1, 0)], 1 - slot)

    @pl.when(i == pl.num_programs(0) - 1)
    def _():
        wait_rows(nvalid_ref[i], slot)


def _dispatch(starts, ends, nvalid, experts, ranks, h2g, n_rows, tile, chunk):
    n, s, l = h2g.shape
    n_experts = starts.shape[0]
    grid_spec = pltpu.PrefetchScalarGridSpec(
        num_scalar_prefetch=3,
        grid=(n // chunk,),
        in_specs=[
            pl.BlockSpec((TOP_K, chunk), lambda i, *_: (0, i), memory_space=pltpu.SMEM),
            pl.BlockSpec((TOP_K, chunk), lambda i, *_: (0, i), memory_space=pltpu.SMEM),
            pl.BlockSpec(memory_space=pl.ANY),
        ],
        out_specs=pl.BlockSpec(memory_space=pl.ANY),
        scratch_shapes=[pltpu.VMEM((tile, s, l), h2g.dtype),
                        pltpu.SemaphoreType.DMA(()),
                        pltpu.SemaphoreType.DMA((2,))],
    )
    return pl.pallas_call(
        functools.partial(_dispatch_body, tile=tile, chunk=chunk, n_experts=n_experts),
        grid_spec=grid_spec,
        out_shape=jax.ShapeDtypeStruct((n_rows, s, l), h2g.dtype),
        compiler_params=pltpu.CompilerParams(
            dimension_semantics=("arbitrary",), vmem_limit_bytes=VMEM_LIMIT),
        name="moe_dispatch",
    )(starts, ends, nvalid, experts, ranks, h2g)


def _experts_body(te_ref, used_ref, x_ref, wg_ref, wu_ref, wd_ref, y_ref, wg_b, wu_b, wd_b):
    t = pl.program_id(0)

    @pl.when(t < used_ref[0])
    def _():
        prev = te_ref[jnp.maximum(t - 1, 0)]

        @pl.when((t == 0) | (te_ref[t] != prev))
        def _():
            wg_b[...] = wg_ref[0].astype(BF16)
            wu_b[...] = wu_ref[0].astype(BF16)
            wd_b[...] = wd_ref[0].astype(BF16)

        n_s = x_ref.shape[1]
        x = jnp.concatenate([x_ref[:, s, :] for s in range(n_s)], axis=1)
        a = _silu(_dot(x, wg_b[...])) * _dot(x, wu_b[...])
        y = _dot(a.astype(BF16), wd_b[...]).astype(y_ref.dtype)
        for s in range(n_s):
            y_ref[:, s, :] = y[:, s * LANES:(s + 1) * LANES]


def _experts(tile_expert, n_used, xs, w_gate, w_up, w_down, tile):
    n_rows, s, l = xs.shape
    n_experts, d, f = w_gate.shape
    live = lambda t, te, used: jnp.minimum(t, used[0] - 1)
    grid_spec = pltpu.PrefetchScalarGridSpec(
        num_scalar_prefetch=2,
        grid=(n_rows // tile,),
        in_specs=[
            pl.BlockSpec((tile, s, l), lambda t, te, used: (live(t, te, used), 0, 0)),
            pl.BlockSpec((1, d, f), lambda t, te, used: (te[live(t, te, used)], 0, 0)),
            pl.BlockSpec((1, d, f), lambda t, te, used: (te[live(t, te, used)], 0, 0)),
            pl.BlockSpec((1, f, d), lambda t, te, used: (te[live(t, te, used)], 0, 0)),
        ],
        out_specs=pl.BlockSpec((tile, s, l), lambda t, te, used: (live(t, te, used), 0, 0)),
        scratch_shapes=[pltpu.VMEM((d, f), BF16), pltpu.VMEM((d, f), BF16), pltpu.VMEM((f, d), BF16)],
    )
    return pl.pallas_call(
        _experts_body,
        grid_spec=grid_spec,
        out_shape=jax.ShapeDtypeStruct((n_rows, s, l), BF16),
        compiler_params=pltpu.CompilerParams(
            dimension_semantics=("arbitrary",), vmem_limit_bytes=VMEM_LIMIT),
        name="moe_experts",
    )(tile_expert, n_used, xs, w_gate, w_up, w_down)


def _combine_body(start_ref, expert_ref, rank_ref, expert_nx, rank_nx, ys_hbm, x1_ref, h2_ref,
                  gates_ref, sg_ref, su_ref, sd_ref, o_ref, buf, sem, *, chunk):
    i = pl.program_id(0)
    n = pl.num_programs(0)
    slot = i & 1

    def issue(e_ref, r_ref, slot):
        def body(r, c):
            for k in range(TOP_K):
                src = start_ref[e_ref[k, r]] + r_ref[k, r]
                pltpu.make_async_copy(ys_hbm.at[src], buf.at[slot, k, r], sem.at[slot]).start()
            return c
        lax.fori_loop(0, chunk, body, 0)

    @pl.when(i == 0)
    def _():
        issue(expert_ref, rank_ref, slot)

    @pl.when(i + 1 < n)
    def _():
        issue(expert_nx, rank_nx, 1 - slot)

    h2 = h2_ref[...]
    acc = x1_ref[...] + _dot(
        (_silu(_dot(h2, sg_ref[...])) * _dot(h2, su_ref[...])).astype(BF16), sd_ref[...])

    for k in range(TOP_K):
        pltpu.make_async_copy(ys_hbm.at[pl.ds(0, chunk)], buf.at[slot, k], sem.at[slot]).wait()

    n_s = buf.shape[3]
    gates = gates_ref[...]
    for k in range(TOP_K):
        y = jnp.concatenate([buf[slot, k, :, s, :] for s in range(n_s)], axis=1).astype(F32)
        acc = acc + gates[:, k:k + 1] * y
    o_ref[...] = acc


def _combine(starts, experts, ranks, ys, x1, h2, gates, sg, su, sd, chunk):
    n, d = x1.shape
    _, s, l = ys.shape
    n_chunks = n // chunk
    row = lambda i, *_: (i, 0)
    const = lambda i, *_: (0, 0)
    here = lambda i, *_: (0, i)
    ahead = lambda i, *_: (0, jnp.minimum(i + 1, n_chunks - 1))
    grid_spec = pltpu.PrefetchScalarGridSpec(
        num_scalar_prefetch=1,
        grid=(n_chunks,),
        in_specs=[
            pl.BlockSpec((TOP_K, chunk), here, memory_space=pltpu.SMEM),
            pl.BlockSpec((TOP_K, chunk), here, memory_space=pltpu.SMEM),
            pl.BlockSpec((TOP_K, chunk), ahead, memory_space=pltpu.SMEM),
            pl.BlockSpec((TOP_K, chunk), ahead, memory_space=pltpu.SMEM),
            pl.BlockSpec(memory_space=pl.ANY),
            pl.BlockSpec((chunk, d), row),
            pl.BlockSpec((chunk, d), row),
            pl.BlockSpec((chunk, gates.shape[1]), row),
            pl.BlockSpec(sg.shape, const),
            pl.BlockSpec(su.shape, const),
            pl.BlockSpec(sd.shape, const),
        ],
        out_specs=pl.BlockSpec((chunk, d), row),
        scratch_shapes=[pltpu.VMEM((2, TOP_K, chunk, s, l), ys.dtype),
                        pltpu.SemaphoreType.DMA((2,))],
    )
    return pl.pallas_call(
        functools.partial(_combine_body, chunk=chunk),
        grid_spec=grid_spec,
        out_shape=jax.ShapeDtypeStruct((n, d), F32),
        compiler_params=pltpu.CompilerParams(
            dimension_semantics=("arbitrary",), vmem_limit_bytes=VMEM_LIMIT),
        name="moe_combine",
    )(starts, experts, ranks, experts, ranks, ys, x1, h2, gates, sg, su, sd)


def kernel(x_prompt, x_sample, cache_k_sb, cache_v_sb, cache_k_diff, cache_v_diff, page_table,
           meta_tokens, rel_bias, norm1_g, w_in, q_norm_g, k_norm_g, lambda_q1, lambda_k1,
           lambda_q2, lambda_k2, sb_out_g, diff_subln_g, w_out, norm2_g, router_w, router_bias,
           w_gate, w_up, w_down, w_shared_gate, w_shared_up, w_shared_down):
    assert w_in.shape[0] == 1, "single-layer step"
    b, seq, d = x_prompt.shape
    dec_b, dec_t, _ = x_sample.shape
    n_meta = meta_tokens.shape[0]
    width = sb_out_g.shape[1]
    n_pool, page = cache_k_sb.shape[1], cache_k_sb.shape[2]
    n_experts = router_w.shape[2]
    t = seq + n_meta
    blk = ATT_BLOCK
    tp = _round_up(t, blk)
    lam_init = 0.8 - 0.6 * math.exp(-0.3 * 0)

    w_in_b = w_in[0].astype(BF16)
    g1 = norm1_g
    reps = width // DH
    qg = jnp.tile(q_norm_g, (1, reps))
    kg = jnp.tile(k_norm_g, (1, reps))
    lane = jnp.arange(width)
    gm = ((lane[:, None] // DH) == (lane[None, :] // DH)).astype(BF16) * (1.0 / DH)
    tri_p = (jnp.arange(blk)[:, None] >= jnp.arange(blk)[None, :]).astype(BF16)
    tri_d = (jnp.arange(page)[:, None] >= jnp.arange(page)[None, :]).astype(BF16)
    lams = (lambda_q1, lambda_k1, lambda_q2, lambda_k2)
    wo_sb = w_out[0, :width].astype(BF16)
    wo_df = w_out[0, width:].astype(BF16)
    rw_t = router_w[0].T
    rw_hi = rw_t.astype(BF16)
    rw_lo = (rw_t - rw_hi.astype(F32)).astype(BF16)
    rb = router_bias[0][:, None]
    sg = w_shared_gate[0].astype(BF16)
    su = w_shared_up[0].astype(BF16)
    sd = w_shared_down[0].astype(BF16)

    n_h_sb = width // DH
    n_h_df = width // LANES
    n_slots = DECODE_PAGES_PER_STEP
    assert blk >= MAX_DISTANCE and page >= MAX_DISTANCE
    assert page_table.shape[1] % n_slots == 0 and dec_t * n_h_df <= page
    bias_p, bias_s, bias_n = _bias_tiles(rel_bias, blk, dec_t, page, n_slots)

    meta = jnp.broadcast_to(meta_tokens[None], (b, n_meta, d))
    xp = jnp.concatenate([meta, x_prompt, jnp.zeros((b, tp - t, d), F32)], axis=1)
    (k_sb_p, v_sb_p, k_df_p, v_df_p,
     qsb_b, ksb_b, vsb_b, qdf_b, kdf_b, vdf_b) = _project(xp, t, g1, w_in_b, qg, kg, gm, blk)
    msb_p = _sb_prompt(qsb_b, ksb_b, vsb_b, tri_p, sb_out_g, blk)
    mdf_p = _df_prompt(rel_bias, qdf_b, kdf_b, vdf_b, bias_p, diff_subln_g, lams, blk, lam_init)

    n_s = dec_b * dec_t
    xs = x_sample.reshape(1, n_s, d)
    (k_sb_s, v_sb_s, k_df_s, v_df_s,
     qsb_s, _, _, qdf_s, _, _) = _project(xs, n_s, g1, w_in_b, qg, kg, gm, n_s)

    def new_page(rows, n_rows):
        r = rows.reshape(dec_b, n_rows, -1)
        return jnp.pad(r, ((0, 0), (0, page - n_rows), (0, 0)))

    transposed_pages = lambda c: jnp.transpose(c[0], (0, 2, 3, 1)).reshape(n_pool, width, page)
    interleaved_rows = lambda c: c.reshape(n_pool, page * n_h_df, LANES)
    msb_s = _sb_decode(page_table, qsb_s.reshape(dec_b, dec_t, width),
                       new_page(k_sb_s, dec_t), new_page(v_sb_s, dec_t),
                       transposed_pages(cache_k_sb), transposed_pages(cache_v_sb), tri_d, sb_out_g)
    mdf_s = _df_decode(page_table, qdf_s.reshape(dec_b, dec_t, width),
                       new_page(k_df_s, dec_t * n_h_df), new_page(v_df_s, dec_t * n_h_df),
                       interleaved_rows(cache_k_diff), interleaved_rows(cache_v_diff),
                       bias_s, bias_n, diff_subln_g, lams, lam_init, n_slots)

    chunk = MOE_TILE
    n_p = b * tp
    n_sp = _round_up(n_s, chunk)
    pad_rows = lambda a: jnp.pad(a.reshape(n_s, -1), ((0, n_sp - n_s), (0, 0)))
    x1, h2, h2g, gates, experts, ranks, counts = _post_attention(
        msb_p.reshape(n_p, width), mdf_p.reshape(n_p, width), xp.reshape(n_p, d),
        pad_rows(msb_s), pad_rows(mdf_s), pad_rows(x_sample),
        wo_sb, wo_df, norm2_g, rw_hi, rw_lo, rb, chunk, tp, t, n_s)

    counts = counts[:, 0].astype(I32)
    padded = (counts + (chunk - 1)) // chunk * chunk
    ends = jnp.cumsum(padded).astype(I32)
    starts = ends - padded
    max_tiles = -(-(TOP_K * (b * t + n_s)) // chunk) + n_experts
    tile_expert = jnp.minimum(
        jnp.sum((jnp.arange(max_tiles) * chunk)[:, None] >= ends[None, :], axis=1), n_experts - 1).astype(I32)
    n_used = (ends[-1:] // chunk).astype(I32)
    first = np.arange((n_p + n_sp) // chunk) * chunk
    nvalid = np.where(first < n_p, t - first % tp, n_s - (first - n_p))
    nvalid = jnp.asarray(np.clip(nvalid, 0, chunk), I32)

    sorted_rows = _dispatch(starts, ends, nvalid, experts, ranks, h2g, max_tiles * chunk, chunk, chunk)
    expert_rows = _experts(tile_expert, n_used, sorted_rows, w_gate[0], w_up[0], w_down[0], chunk)
    y = _combine(starts, experts, ranks, expert_rows, x1, h2, gates, sg, su, sd, chunk)
    y_prompt = y[:n_p].reshape(b, tp, d)[:, n_meta:t]
    y_sample = y[n_p:n_p + n_s].reshape(dec_b, dec_t, d)

    sb_rows = lambda r, bb, tt: r.reshape(1, bb, tt, n_h_sb, DH)
    df_rows = lambda r, bb, tt: r.reshape(1, bb, tt, n_h_df, LANES)
    return (y_prompt, y_sample,
            sb_rows(k_sb_p, b, t), sb_rows(v_sb_p, b, t), df_rows(k_df_p, b, t), df_rows(v_df_p, b, t),
            sb_rows(k_sb_s, dec_b, dec_t), sb_rows(v_sb_s, dec_b, dec_t),
            df_rows(k_df_s, dec_b, dec_t), df_rows(v_df_s, dec_b, dec_t))
```

```python
import functools
import math

import jax
import jax.numpy as jnp
from jax import lax
from jax.experimental import pallas as pl
from jax.experimental.pallas import tpu as pltpu

F32 = jnp.float32
BF16 = jnp.bfloat16
I32 = jnp.int32

EPS = 1e-6
DH = 64
LANES = 128
MAX_DISTANCE = 128
TOP_K = 8
N_GROUPS = 8
TOPK_GROUPS = 4
ROUTED_SCALE = 2.5
NEG = -1e30
SB_CUTOFF = -104.0
ATT_BLOCK = 256
DECODE_PAGES_PER_STEP = 4
MOE_TOKEN_TILE = 1024
MOE_EXPERT_GROUP = 4
VMEM_LIMIT = 48 * 1024 * 1024

_TRANS_B = (((1,), (1,)), ((), ()))


def _dot(a, b):
    return jnp.dot(a, b, preferred_element_type=F32)


def _dot_tb(a, b):
    return lax.dot_general(a, b, _TRANS_B, preferred_element_type=F32)


def _split_bf16(x):
    hi = x.astype(BF16)
    lo = (x - hi.astype(F32)).astype(BF16)
    return hi, lo


def _round_up(n, m):
    return (n + m - 1) // m * m


def _proj_body(x_ref, g1_ref, w_ref, qg_ref, kg_ref, gm_ref,
               ksb_o, vsb_o, kdf_o, vdf_o,
               qsb_b, ksb_b, vsb_b, qdf_b, kdf_b, vdf_b, *, width):
    x = x_ref[0]
    ms = jnp.mean(x * x, axis=-1, keepdims=True)
    h = (x * lax.rsqrt(ms + EPS)) * g1_ref[...]
    proj = _dot(h.astype(BF16), w_ref[...])
    w = width
    q_sb, k_sb, v_sb = proj[:, 0:w], proj[:, w:2 * w], proj[:, 2 * w:3 * w]
    q_df, k_df, v_df = proj[:, 3 * w:4 * w], proj[:, 4 * w:5 * w], proj[:, 5 * w:6 * w]

    def map_norm(t, g_ref):
        msq = _dot((t * t).astype(BF16), gm_ref[...])
        return (t * lax.rsqrt(msq + EPS)) * g_ref[...]

    q_df = map_norm(q_df, qg_ref)
    k_df = map_norm(k_df, kg_ref)
    ksb_o[0] = k_sb
    vsb_o[0] = v_sb
    kdf_o[0] = k_df
    vdf_o[0] = v_df
    scale = DH ** -0.5
    qsb_b[0] = (q_sb * scale).astype(BF16)
    ksb_b[0] = k_sb.astype(BF16)
    vsb_b[0] = v_sb.astype(BF16)
    qdf_b[0] = (q_df * scale).astype(BF16)
    kdf_b[0] = k_df.astype(BF16)
    vdf_b[0] = v_df.astype(BF16)


def _project(x_pad, t_valid, g1, w_in_b, qg, kg, gm, tm):
    b, tp, d = x_pad.shape
    width = w_in_b.shape[1] // 6
    grid = (b, tp // tm)
    row = lambda bi, i: (bi, i, 0)
    const = lambda bi, i: (0, 0)
    f32_out = jax.ShapeDtypeStruct((b, t_valid, width), F32)
    bf_out = jax.ShapeDtypeStruct((b, tp, width), BF16)
    out_spec = pl.BlockSpec((1, tm, width), row)
    return pl.pallas_call(
        functools.partial(_proj_body, width=width),
        grid=grid,
        in_specs=[
            pl.BlockSpec((1, tm, d), row),
            pl.BlockSpec((1, d), const),
            pl.BlockSpec(w_in_b.shape, const),
            pl.BlockSpec((1, width), const),
            pl.BlockSpec((1, width), const),
            pl.BlockSpec((width, width), const),
        ],
        out_specs=[out_spec] * 10,
        out_shape=[f32_out] * 4 + [bf_out] * 6,
        compiler_params=pltpu.CompilerParams(
            dimension_semantics=("arbitrary", "arbitrary"),
            vmem_limit_bytes=VMEM_LIMIT),
        name="proj",
    )(x_pad, g1, w_in_b, qg, kg, gm)


def _stack_halves(q):
    lane = lax.broadcasted_iota(I32, (1, LANES), 1)
    zero = jnp.zeros_like(q)
    return jnp.concatenate(
        [jnp.where(lane < DH, q, zero), jnp.where(lane >= DH, q, zero)], axis=0)


def _neg_softplus(z):
    return -(jnp.maximum(z, 0.0) + jnp.log(1.0 + jnp.exp(-jnp.abs(z))))


def _sb_block(qq, k, v, tri, acc, carry, vis):
    z = _dot_tb(qq, k)
    log_keep = _neg_softplus(z)
    if vis is not None:
        log_keep = jnp.where(vis, log_keep, 0.0)
    hi, lo = _split_bf16(log_keep)
    csum = _dot(hi, tri) + _dot(lo, tri) + carry
    w = jnp.exp(z + csum)
    if vis is not None:
        w = jnp.where(vis, w, 0.0)
    acc = acc + _dot(w.astype(BF16), v)
    return acc, csum[:, 0:1]


def _sb_finish(acc, t, g):
    lane = lax.broadcasted_iota(I32, (1, LANES), 1)
    lo_half = lane < DH
    o = jnp.where(lo_half, acc[:t], acc[t:])
    o2 = o * o
    s_lo = jnp.sum(jnp.where(lo_half, o2, 0.0), axis=-1, keepdims=True)
    s_hi = jnp.sum(jnp.where(lo_half, 0.0, o2), axis=-1, keepdims=True)
    ms = jnp.where(lo_half, s_lo, s_hi) * (1.0 / DH)
    return (o * lax.rsqrt(ms + EPS)) * g


def _df_block(qq, k, v, bias, m, l, acc):
    s = _dot_tb(qq, k) + bias
    m_new = jnp.maximum(m, jnp.max(s, axis=-1, keepdims=True))
    alpha = jnp.exp(m - m_new)
    p = jnp.exp(s - m_new)
    l = alpha * l + jnp.sum(p, axis=-1, keepdims=True)
    acc = alpha * acc + _dot(p.astype(BF16), v)
    return m_new, l, acc


def _lambda(lq1, lk1, lq2, lk2, lam_init):
    s1 = jnp.sum(lq1[...] * lk1[...], axis=-1, keepdims=True)
    s2 = jnp.sum(lq2[...] * lk2[...], axis=-1, keepdims=True)
    return jnp.exp(s1) - jnp.exp(s2) + lam_init


def _df_finish(acc, l, t, lam, g, lam_init):
    o = acc[:t] / l[:t] - lam * (acc[t:] / l[t:])
    ms = jnp.mean(o * o, axis=-1, keepdims=True)
    return (o * lax.rsqrt(ms + EPS)) * (g * (1.0 - lam_init))


def _bias_tile(tab_ref, h, rel, n_buckets):
    max_exact = n_buckets // 2
    n = jnp.maximum(rel, 0)
    nf = jnp.maximum(n, 1).astype(F32)
    large = max_exact + (jnp.log(nf / max_exact) / math.log(MAX_DISTANCE / max_exact)
                         * (n_buckets - max_exact)).astype(I32)
    large = jnp.minimum(large, n_buckets - 1)
    bucket = jnp.where(n < max_exact, n, large)
    bias = jnp.zeros(rel.shape, F32)
    for b in range(n_buckets):
        bias = jnp.where(bucket == b, tab_ref[b, h], bias)
    return jnp.where(rel >= 0, bias, NEG)


def _bias_body(tab_ref, bp_ref, bs_ref, bn_ref, *, n_buckets, n_heads, blk, dec_t, page, n_slots):
    r = lax.broadcasted_iota(I32, (blk, blk), 0)
    c = lax.broadcasted_iota(I32, (blk, blk), 1)
    cols = page * n_heads
    rs = lax.broadcasted_iota(I32, (dec_t, cols), 0)
    cs = lax.broadcasted_iota(I32, (dec_t, cols), 1)
    pos, head = cs // n_heads, cs % n_heads
    rn = lax.broadcasted_iota(I32, (dec_t, page), 0)
    cn = lax.broadcasted_iota(I32, (dec_t, page), 1)
    pos_n, head_n = cn // n_heads, cn % n_heads
    for h in range(n_heads):
        for d in range(2):
            bp_ref[h, d] = _bias_tile(tab_ref, h, d * blk + r - c, n_buckets)
        far = jnp.where(head == h, tab_ref[n_buckets - 1, h], NEG)
        last = jnp.where(head == h, _bias_tile(tab_ref, h, page + rs - pos, n_buckets), NEG)
        new = jnp.where((head_n == h) & (pos_n < dec_t),
                        _bias_tile(tab_ref, h, rn - pos_n, n_buckets), NEG)
        for mp in range(2):
            rows = pl.ds((2 * h + mp) * dec_t, dec_t)
            bn_ref[rows, :] = new
            for u in range(n_slots):
                bs_ref[0, rows, u * cols:(u + 1) * cols] = far
                bs_ref[1, rows, u * cols:(u + 1) * cols] = last if u == n_slots - 1 else far


def _bias_tiles(rel_bias, blk, dec_t, page, n_slots):
    n_buckets, n_heads = rel_bias.shape
    rows = 2 * n_heads * dec_t
    return pl.pallas_call(
        functools.partial(_bias_body, n_buckets=n_buckets, n_heads=n_heads, blk=blk,
                          dec_t=dec_t, page=page, n_slots=n_slots),
        in_specs=[pl.BlockSpec(memory_space=pltpu.SMEM)],
        out_shape=[jax.ShapeDtypeStruct((n_heads, 2, blk, blk), F32),
                   jax.ShapeDtypeStruct((2, rows, n_slots * page * n_heads), F32),
                   jax.ShapeDtypeStruct((rows, page), F32)],
        name="bias_tiles",
    )(rel_bias)


def _sb_prompt_body(q_ref, k_ref, v_ref, tri_ref, g_ref, o_ref, *, blk):
    i = pl.program_id(2)
    qq = _stack_halves(q_ref[0])
    tri = tri_ref[...]
    r = lax.broadcasted_iota(I32, (2 * blk, blk), 0)
    r = jnp.where(r >= blk, r - blk, r)
    c = lax.broadcasted_iota(I32, (2 * blk, blk), 1)
    vis = c < r

    def load(j):
        start = pl.multiple_of(j * blk, blk)
        return k_ref[0, pl.ds(start, blk), :], v_ref[0, pl.ds(start, blk), :]

    acc = jnp.zeros((2 * blk, LANES), F32)
    carry = jnp.zeros((2 * blk, 1), F32)
    k, v = load(i)
    acc, carry = _sb_block(qq, k, v, tri, acc, carry, vis)

    def more(state):
        jj, _, carry = state
        return (jj < i) & (jnp.max(carry) > SB_CUTOFF)

    def body(state):
        jj, acc, carry = state
        k, v = load(i - 1 - jj)
        acc, carry = _sb_block(qq, k, v, tri, acc, carry, None)
        return jj + 1, acc, carry

    _, acc, _ = lax.while_loop(more, body, (jnp.int32(0), acc, carry))
    o_ref[0] = _sb_finish(acc, blk, g_ref[...]).astype(o_ref.dtype)


def _sb_prompt(q_b, k_b, v_b, tri, g, blk):
    b, tp, width = q_b.shape
    n_pairs = width // LANES
    grid = (b, n_pairs, tp // blk)
    return pl.pallas_call(
        functools.partial(_sb_prompt_body, blk=blk),
        grid=grid,
        in_specs=[
            pl.BlockSpec((1, blk, LANES), lambda bi, p, i: (bi, i, p)),
            pl.BlockSpec((1, tp, LANES), lambda bi, p, i: (bi, 0, p)),
            pl.BlockSpec((1, tp, LANES), lambda bi, p, i: (bi, 0, p)),
            pl.BlockSpec((blk, blk), lambda bi, p, i: (0, 0)),
            pl.BlockSpec((1, LANES), lambda bi, p, i: (0, p)),
        ],
        out_specs=pl.BlockSpec((1, blk, LANES), lambda bi, p, i: (bi, i, p)),
        out_shape=jax.ShapeDtypeStruct((b, tp, width), BF16),
        compiler_params=pltpu.CompilerParams(
            dimension_semantics=("arbitrary", "arbitrary", "arbitrary"),
            vmem_limit_bytes=VMEM_LIMIT),
        name="sb_prompt",
    )(q_b, k_b, v_b, tri, g)


def _df_prompt_body(tab_ref, q_ref, k_ref, v_ref, bias_ref, g_ref, lq1, lk1, lq2, lk2,
                    o_ref, *, blk, lam_init, n_buckets):
    h = pl.program_id(1)
    i = pl.program_id(2)
    qq = _stack_halves(q_ref[0])

    def load(j):
        start = pl.multiple_of(j * blk, blk)
        return k_ref[0, pl.ds(start, blk), :], v_ref[0, pl.ds(start, blk), :]

    def tile_bias(d):
        t = bias_ref[0, d]
        return jnp.concatenate([t, t], axis=0)

    m = jnp.full((2 * blk, 1), NEG, F32)
    l = jnp.zeros((2 * blk, 1), F32)
    acc = jnp.zeros((2 * blk, LANES), F32)
    k, v = load(i)
    m, l, acc = _df_block(qq, k, v, tile_bias(0), m, l, acc)

    def near(state):
        k, v = load(i - 1)
        return _df_block(qq, k, v, tile_bias(1), *state)

    m, l, acc = lax.cond(i >= 1, near, lambda s: s, (m, l, acc))
    far_bias = tab_ref[n_buckets - 1, h]

    def body(jj, state):
        k, v = load(i - 2 - jj)
        return _df_block(qq, k, v, far_bias, *state)

    m, l, acc = lax.fori_loop(0, jnp.maximum(i - 1, 0), body, (m, l, acc))
    lam = _lambda(lq1, lk1, lq2, lk2, lam_init)
    o_ref[0] = _df_finish(acc, l, blk, lam, g_ref[...], lam_init).astype(o_ref.dtype)


def _df_prompt(rel_bias, q_b, k_b, v_b, bias_p, g, lams, blk, lam_init):
    b, tp, width = q_b.shape
    n_heads = width // LANES
    grid = (b, n_heads, tp // blk)
    vec = pl.BlockSpec((1, DH), lambda bi, h, i: (0, 0))
    return pl.pallas_call(
        functools.partial(_df_prompt_body, blk=blk, lam_init=lam_init,
                          n_buckets=rel_bias.shape[0]),
        grid=grid,
        in_specs=[
            pl.BlockSpec(memory_space=pltpu.SMEM),
            pl.BlockSpec((1, blk, LANES), lambda bi, h, i: (bi, i, h)),
            pl.BlockSpec((1, tp, LANES), lambda bi, h, i: (bi, 0, h)),
            pl.BlockSpec((1, tp, LANES), lambda bi, h, i: (bi, 0, h)),
            pl.BlockSpec((1, 2, blk, blk), lambda bi, h, i: (h, 0, 0, 0)),
            pl.BlockSpec((1, LANES), lambda bi, h, i: (0, 0)),
            vec, vec, vec, vec,
        ],
        out_specs=pl.BlockSpec((1, blk, LANES), lambda bi, h, i: (bi, i, h)),
        out_shape=jax.ShapeDtypeStruct((b, tp, width), BF16),
        compiler_params=pltpu.CompilerParams(
            dimension_semantics=("arbitrary", "arbitrary", "arbitrary"),
            vmem_limit_bytes=VMEM_LIMIT),
        name="df_prompt",
    )(rel_bias, q_b, k_b, v_b, bias_p, g, *lams)


def _stack_decode_queries(q, n_blocks):
    q = q.astype(F32)
    return jnp.concatenate(
        [_stack_halves(q[:, p * LANES:(p + 1) * LANES]) for p in range(n_blocks)], axis=0).astype(BF16)


def _sb_decode_body(pt_ref, q_ref, kn_ref, vn_ref, kc_hbm, vc_hbm, tri_ref, g_ref, o_ref,
                    kbuf, vbuf, sem, *, dec_t, n_blocks, page, n_pages):
    bi = pl.program_id(0)
    rows = 2 * dec_t
    qq = _stack_decode_queries(q_ref[0], n_blocks)
    tri = tri_ref[...]

    def page_copies(slot, jj):
        pg = pt_ref[bi, n_pages - 1 - jj]
        return (pltpu.make_async_copy(kc_hbm.at[pg], kbuf.at[slot], sem.at[0, slot]),
                pltpu.make_async_copy(vc_hbm.at[pg], vbuf.at[slot], sem.at[1, slot]))

    def start(slot, jj):
        for cp in page_copies(slot, jj):
            cp.start()

    def wait(slot, jj):
        for cp in page_copies(slot, jj):
            cp.wait()

    start(0, 0)

    def weights(z, carry, vis):
        log_keep = _neg_softplus(z)
        if vis is not None:
            log_keep = jnp.where(vis, log_keep, 0.0)
        hi, lo = _split_bf16(log_keep)
        csum = _dot(hi, tri) + _dot(lo, tri) + carry
        w = jnp.exp(z + csum)
        if vis is not None:
            w = jnp.where(vis, w, 0.0)
        return w.astype(BF16), csum[:, 0:1]

    r = lax.broadcasted_iota(I32, (dec_t, page), 0)
    c = lax.broadcasted_iota(I32, (dec_t, page), 1)
    vis = jnp.concatenate([c < r] * (2 * n_blocks), axis=0)
    z = jnp.concatenate(
        [_dot_tb(qq[p * rows:(p + 1) * rows], kn_ref[0, :, p * LANES:(p + 1) * LANES].astype(BF16))
         for p in range(n_blocks)], axis=0)
    w, carry = weights(z, jnp.zeros((n_blocks * rows, 1), F32), vis)
    acc = jnp.concatenate(
        [_dot(w[p * rows:(p + 1) * rows], vn_ref[0, :, p * LANES:(p + 1) * LANES].astype(BF16))
         for p in range(n_blocks)], axis=0)

    def more(state):
        jj, _, carry = state
        return (jj < n_pages) & (jnp.max(carry) > SB_CUTOFF)

    def body(state):
        jj, acc, carry = state
        slot = jj & 1
        wait(slot, jj)

        @pl.when(jj + 1 < n_pages)
        def _():
            start(1 - slot, jj + 1)

        kt = kbuf[slot]
        vt = vbuf[slot]
        z = jnp.concatenate(
            [_dot(qq[p * rows:(p + 1) * rows], kt[p * LANES:(p + 1) * LANES, :].astype(BF16))
             for p in range(n_blocks)], axis=0)
        w, carry = weights(z, carry, None)
        acc = acc + jnp.concatenate(
            [_dot_tb(w[p * rows:(p + 1) * rows], vt[p * LANES:(p + 1) * LANES, :].astype(BF16))
             for p in range(n_blocks)], axis=0)
        return jj + 1, acc, carry

    jj, acc, _ = lax.while_loop(more, body, (jnp.int32(0), acc, carry))

    @pl.when(jj < n_pages)
    def _():
        wait(jj & 1, jj)

    for p in range(n_blocks):
        o_ref[0, :, p * LANES:(p + 1) * LANES] = _sb_finish(
            acc[p * rows:(p + 1) * rows, :], dec_t, g_ref[:, p * LANES:(p + 1) * LANES])


def _sb_decode(page_table, q, k_new, v_new, kt_cache, vt_cache, tri, g):
    b, dec_t, width = q.shape
    n_pages = page_table.shape[1]
    page = kt_cache.shape[2]
    n_blocks = width // LANES
    per_b = lambda bi, pt: (bi, 0, 0)
    const = lambda bi, pt: (0, 0)
    grid_spec = pltpu.PrefetchScalarGridSpec(
        num_scalar_prefetch=1,
        grid=(b,),
        in_specs=[
            pl.BlockSpec((1, dec_t, width), per_b),
            pl.BlockSpec((1, page, width), per_b),
            pl.BlockSpec((1, page, width), per_b),
            pl.BlockSpec(memory_space=pl.ANY),
            pl.BlockSpec(memory_space=pl.ANY),
            pl.BlockSpec((page, page), const),
            pl.BlockSpec((1, width), const),
        ],
        out_specs=pl.BlockSpec((1, dec_t, width), per_b),
        scratch_shapes=[pltpu.VMEM((2, width, page), F32),
                        pltpu.VMEM((2, width, page), F32),
                        pltpu.SemaphoreType.DMA((2, 2))],
    )
    return pl.pallas_call(
        functools.partial(_sb_decode_body, dec_t=dec_t, n_blocks=n_blocks, page=page, n_pages=n_pages),
        grid_spec=grid_spec,
        out_shape=jax.ShapeDtypeStruct((b, dec_t, width), F32),
        compiler_params=pltpu.CompilerParams(
            dimension_semantics=("arbitrary",), vmem_limit_bytes=VMEM_LIMIT),
        name="sb_decode",
    )(page_table, q, k_new, v_new, kt_cache, vt_cache, tri, g)


def _df_decode_body(pt_ref, q_ref, kn_ref, vn_ref, *rest, dec_t, n_blocks, n_slots, lam_init):
    kc_refs, vc_refs = rest[:n_slots], rest[n_slots:2 * n_slots]
    (bias_ref, bias_new_ref, g_ref, lq1, lk1, lq2, lk2, o_ref,
     qq_ref, m_ref, l_ref, acc_ref) = rest[2 * n_slots:]
    jj = pl.program_id(1)
    rows = 2 * dec_t

    def update(k, v, bias):
        m, l, acc = _df_block(qq_ref[...], k, v, bias, m_ref[...], l_ref[...], acc_ref[...])
        m_ref[...] = m
        l_ref[...] = l
        acc_ref[...] = acc

    @pl.when(jj == 0)
    def _():
        qq_ref[...] = _stack_decode_queries(q_ref[0], n_blocks)
        m_ref[...] = jnp.full(m_ref.shape, NEG, F32)
        l_ref[...] = jnp.zeros_like(l_ref)
        acc_ref[...] = jnp.zeros_like(acc_ref)
        update(kn_ref[0].astype(BF16), vn_ref[0].astype(BF16), bias_new_ref[...])

    k = jnp.concatenate([r[0].astype(BF16) for r in kc_refs], axis=0)
    v = jnp.concatenate([r[0].astype(BF16) for r in vc_refs], axis=0)
    update(k, v, bias_ref[0])

    @pl.when(jj == pl.num_programs(1) - 1)
    def _():
        lam = _lambda(lq1, lk1, lq2, lk2, lam_init)
        for p in range(n_blocks):
            sl = slice(p * rows, (p + 1) * rows)
            o_ref[0, :, p * LANES:(p + 1) * LANES] = _df_finish(
                acc_ref[sl, :], l_ref[sl, :], dec_t, lam, g_ref[...], lam_init)


def _df_decode(page_table, q, k_new, v_new, k_cache, v_cache, bias_s, bias_new, g, lams, lam_init,
               n_slots):
    b, dec_t, width = q.shape
    n_pages = page_table.shape[1]
    page_rows = k_cache.shape[1]
    n_blocks = width // LANES
    rows = n_blocks * 2 * dec_t
    n_steps = n_pages // n_slots
    per_b = lambda bi, jj, pt: (bi, 0, 0)
    const = lambda bi, jj, pt: (0, 0)
    vec = pl.BlockSpec((1, DH), const)

    def slot_spec(u):
        return pl.BlockSpec((1, page_rows, LANES), lambda bi, jj, pt: (pt[bi, jj * n_slots + u], 0, 0))

    grid_spec = pltpu.PrefetchScalarGridSpec(
        num_scalar_prefetch=1,
        grid=(b, n_steps),
        in_specs=[
            pl.BlockSpec((1, dec_t, width), per_b),
            pl.BlockSpec((1,) + k_new.shape[1:], per_b),
            pl.BlockSpec((1,) + v_new.shape[1:], per_b),
            *[slot_spec(u) for u in range(n_slots)],
            *[slot_spec(u) for u in range(n_slots)],
            pl.BlockSpec((1, rows, n_slots * page_rows),
                         lambda bi, jj, pt: (jnp.where(jj == n_steps - 1, 1, 0), 0, 0)),
            pl.BlockSpec(bias_new.shape, const),
            pl.BlockSpec((1, LANES), const),
            vec, vec, vec, vec,
        ],
        out_specs=pl.BlockSpec((1, dec_t, width), per_b),
        scratch_shapes=[pltpu.VMEM((rows, LANES), BF16),
                        pltpu.VMEM((rows, 1), F32),
                        pltpu.VMEM((rows, 1), F32),
                        pltpu.VMEM((rows, LANES), F32)],
    )
    return pl.pallas_call(
        functools.partial(_df_decode_body, dec_t=dec_t, n_blocks=n_blocks, n_slots=n_slots,
                          lam_init=lam_init),
        grid_spec=grid_spec,
        out_shape=jax.ShapeDtypeStruct((b, dec_t, width), F32),
        compiler_params=pltpu.CompilerParams(
            dimension_semantics=("arbitrary", "arbitrary"),
            vmem_limit_bytes=VMEM_LIMIT),
        name="df_decode",
    )(page_table, q, k_new, v_new, *([k_cache] * n_slots), *([v_cache] * n_slots),
      bias_s, bias_new, g, *lams)


def _router_gates(logits_t, bias_col, n_experts):
    tm = logits_t.shape[1]
    gsz = n_experts // N_GROUPS
    scores = 1.0 / (1.0 + jnp.exp(-logits_t))
    sel = scores + bias_col
    sub = lax.broadcasted_iota(I32, (gsz, tm), 0)
    group_scores = []
    for g in range(N_GROUPS):
        blk = sel[g * gsz:(g + 1) * gsz]
        m1 = jnp.max(blk, axis=0, keepdims=True)
        first = jnp.min(jnp.where(blk == m1, sub, gsz), axis=0, keepdims=True)
        m2 = jnp.max(jnp.where(sub == first, -jnp.inf, blk), axis=0, keepdims=True)
        group_scores.append(m1 + m2)
    gs = jnp.concatenate(group_scores, axis=0)
    gidx = lax.broadcasted_iota(I32, (N_GROUPS, tm), 0)
    grank = jnp.zeros((N_GROUPS, tm), I32)
    for g in range(N_GROUPS):
        row = gs[g:g + 1]
        ahead = (row > gs) | ((row == gs) & (gidx > g))
        grank = grank + ahead.astype(I32)
    gkeep = grank < TOPK_GROUPS
    masked = jnp.concatenate(
        [jnp.where(jnp.broadcast_to(gkeep[g:g + 1], (gsz, tm)), sel[g * gsz:(g + 1) * gsz], -jnp.inf)
         for g in range(N_GROUPS)], axis=0)
    eidx = lax.broadcasted_iota(I32, (n_experts, tm), 0)
    erank = jnp.zeros((n_experts, tm), I32)
    for e in range(n_experts):
        row = masked[e:e + 1]
        ahead = (row > masked) | ((row == masked) & (eidx > e))
        erank = erank + ahead.astype(I32)
    w = jnp.where(erank < TOP_K, scores, 0.0)
    return w / jnp.sum(w, axis=0, keepdims=True) * ROUTED_SCALE


def _post_body(msb_ref, mdf_ref, x_ref, wo_sb_ref, wo_df_ref, g2_ref, rw_hi_ref, rw_lo_ref,
               rb_ref, x1_ref, h2_ref, gates_ref, *, n_experts):
    att = _dot(msb_ref[...].astype(BF16), wo_sb_ref[...]) + _dot(mdf_ref[...].astype(BF16), wo_df_ref[...])
    x1 = x_ref[...] + att
    x1_ref[...] = x1
    ms = jnp.mean(x1 * x1, axis=-1, keepdims=True)
    h2 = (x1 * lax.rsqrt(ms + EPS)) * g2_ref[...]
    h2_hi, h2_lo = _split_bf16(h2)
    h2_ref[...] = h2_hi
    rw_hi = rw_hi_ref[...]
    logits_t = _dot_tb(rw_hi, h2_hi) + _dot_tb(rw_hi, h2_lo) + _dot_tb(rw_lo_ref[...], h2_hi)
    gates_t = _router_gates(logits_t, rb_ref[...], n_experts)
    tm = gates_t.shape[1]
    pad = gates_ref.shape[1] - n_experts
    gates_t = jnp.concatenate([gates_t, jnp.zeros((pad, tm), F32)], axis=0)
    gates_ref[...] = gates_t.T.astype(gates_ref.dtype)


def _post_attention(msb, mdf, x, wo_sb, wo_df, g2, rw_hi, rw_lo, rb, tm):
    n, d = x.shape
    n_experts = rw_hi.shape[0]
    ge = _round_up(n_experts, LANES)
    row = lambda i: (i, 0)
    const = lambda i: (0, 0)
    return pl.pallas_call(
        functools.partial(_post_body, n_experts=n_experts),
        grid=(n // tm,),
        in_specs=[
            pl.BlockSpec((tm, msb.shape[1]), row),
            pl.BlockSpec((tm, mdf.shape[1]), row),
            pl.BlockSpec((tm, d), row),
            pl.BlockSpec(wo_sb.shape, const),
            pl.BlockSpec(wo_df.shape, const),
            pl.BlockSpec((1, d), const),
            pl.BlockSpec(rw_hi.shape, const),
            pl.BlockSpec(rw_lo.shape, const),
            pl.BlockSpec((n_experts, 1), const),
        ],
        out_specs=[pl.BlockSpec((tm, d), row), pl.BlockSpec((tm, d), row), pl.BlockSpec((tm, ge), row)],
        out_shape=[jax.ShapeDtypeStruct((n, d), F32), jax.ShapeDtypeStruct((n, d), BF16),
                   jax.ShapeDtypeStruct((n, ge), BF16)],
        compiler_params=pltpu.CompilerParams(
            dimension_semantics=("arbitrary",), vmem_limit_bytes=VMEM_LIMIT),
        name="post_attention",
    )(msb, mdf, x, wo_sb, wo_df, g2, rw_hi, rw_lo, rb)


def _silu(x):
    return x / (1.0 + jnp.exp(-x))


def _moe_body(h2_ref, x1_ref, gates_ref, wg_ref, wu_ref, wd_ref, sg_ref, su_ref, sd_ref,
              o_ref, acc_ref, *, group):
    eg = pl.program_id(1)
    h2 = h2_ref[...]

    @pl.when(eg == 0)
    def _():
        a = _silu(_dot(h2, sg_ref[...])) * _dot(h2, su_ref[...])
        acc_ref[...] = _dot(a.astype(BF16), sd_ref[...])

    f = wg_ref.shape[2]
    g = jnp.concatenate([_dot(h2, wg_ref[j]) for j in range(group)], axis=1)
    u = jnp.concatenate([_dot(h2, wu_ref[j]) for j in range(group)], axis=1)
    ge = gates_ref.shape[1]
    row = lax.broadcasted_iota(I32, (ge, group * f), 0)
    col_expert = eg * group + lax.broadcasted_iota(I32, (ge, group * f), 1) // f
    gate = _dot(gates_ref[...], (row == col_expert).astype(BF16))
    a = (_silu(g) * u * gate).astype(BF16)
    wd = wd_ref[...].reshape(group * f, wd_ref.shape[2])
    acc_ref[...] += _dot(a, wd)

    @pl.when(eg == pl.num_programs(1) - 1)
    def _():
        o_ref[...] = x1_ref[...] + acc_ref[...]


def _moe(h2, x1, gates, w_gate, w_up, w_down, sg, su, sd, tm, group):
    n, d = x1.shape
    n_experts, _, f = w_gate.shape
    row = lambda i, e: (i, 0)
    const = lambda i, e: (0, 0)
    return pl.pallas_call(
        functools.partial(_moe_body, group=group),
        grid=(n // tm, n_experts // group),
        in_specs=[
            pl.BlockSpec((tm, d), row),
            pl.BlockSpec((tm, d), row),
            pl.BlockSpec((tm, gates.shape[1]), row),
            pl.BlockSpec((group, d, f), lambda i, e: (e, 0, 0)),
            pl.BlockSpec((group, d, f), lambda i, e: (e, 0, 0)),
            pl.BlockSpec((group, f, d), lambda i, e: (e, 0, 0)),
            pl.BlockSpec(sg.shape, const),
            pl.BlockSpec(su.shape, const),
            pl.BlockSpec(sd.shape, const),
        ],
        out_specs=pl.BlockSpec((tm, d), row),
        out_shape=jax.ShapeDtypeStruct((n, d), F32),
        scratch_shapes=[pltpu.VMEM((tm, d), F32)],
        compiler_params=pltpu.CompilerParams(
            dimension_semantics=("arbitrary", "arbitrary"), vmem_limit_bytes=VMEM_LIMIT),
        name="moe",
    )(h2, x1, gates, w_gate, w_up, w_down, sg, su, sd)


def _largest_tile(n, cap):
    t = cap
    while n % t:
        t //= 2
    return t


def kernel(x_prompt, x_sample, cache_k_sb, cache_v_sb, cache_k_diff, cache_v_diff, page_table,
           meta_tokens, rel_bias, norm1_g, w_in, q_norm_g, k_norm_g, lambda_q1, lambda_k1,
           lambda_q2, lambda_k2, sb_out_g, diff_subln_g, w_out, norm2_g, router_w, router_bias,
           w_gate, w_up, w_down, w_shared_gate, w_shared_up, w_shared_down):
    assert w_in.shape[0] == 1, "single-layer step"
    b, seq, d = x_prompt.shape
    dec_b, dec_t, _ = x_sample.shape
    n_meta = meta_tokens.shape[0]
    width = sb_out_g.shape[1]
    n_pool, page = cache_k_sb.shape[1], cache_k_sb.shape[2]
    n_experts = router_w.shape[2]
    t = seq + n_meta
    blk = ATT_BLOCK
    tp = _round_up(t, blk)
    lam_init = 0.8 - 0.6 * math.exp(-0.3 * 0)

    w_in_b = w_in[0].astype(BF16)
    g1 = norm1_g
    reps = width // DH
    qg = jnp.tile(q_norm_g, (1, reps))
    kg = jnp.tile(k_norm_g, (1, reps))
    lane = jnp.arange(width)
    gm = ((lane[:, None] // DH) == (lane[None, :] // DH)).astype(BF16) * (1.0 / DH)
    tri_p = (jnp.arange(blk)[:, None] >= jnp.arange(blk)[None, :]).astype(BF16)
    tri_d = (jnp.arange(page)[:, None] >= jnp.arange(page)[None, :]).astype(BF16)
    lams = (lambda_q1, lambda_k1, lambda_q2, lambda_k2)
    wo_sb = w_out[0, :width].astype(BF16)
    wo_df = w_out[0, width:].astype(BF16)
    rw_t = router_w[0].T
    rw_hi = rw_t.astype(BF16)
    rw_lo = (rw_t - rw_hi.astype(F32)).astype(BF16)
    rb = router_bias[0][:, None]
    sg = w_shared_gate[0].astype(BF16)
    su = w_shared_up[0].astype(BF16)
    sd = w_shared_down[0].astype(BF16)

    n_h_sb = width // DH
    n_h_df = width // LANES
    n_slots = DECODE_PAGES_PER_STEP
    assert blk >= MAX_DISTANCE and page >= MAX_DISTANCE
    assert page_table.shape[1] % n_slots == 0 and dec_t * n_h_df <= page
    bias_p, bias_s, bias_n = _bias_tiles(rel_bias, blk, dec_t, page, n_slots)

    meta = jnp.broadcast_to(meta_tokens[None], (b, n_meta, d))
    xp = jnp.concatenate([meta, x_prompt, jnp.zeros((b, tp - t, d), F32)], axis=1)
    (k_sb_p, v_sb_p, k_df_p, v_df_p,
     qsb_b, ksb_b, vsb_b, qdf_b, kdf_b, vdf_b) = _project(xp, t, g1, w_in_b, qg, kg, gm, blk)
    msb_p = _sb_prompt(qsb_b, ksb_b, vsb_b, tri_p, sb_out_g, blk)
    mdf_p = _df_prompt(rel_bias, qdf_b, kdf_b, vdf_b, bias_p, diff_subln_g, lams, blk, lam_init)

    n_s = dec_b * dec_t
    xs = x_sample.reshape(1, n_s, d)
    (k_sb_s, v_sb_s, k_df_s, v_df_s,
     qsb_s, _, _, qdf_s, _, _) = _project(xs, n_s, g1, w_in_b, qg, kg, gm, n_s)

    def new_page(rows, n_rows):
        r = rows.reshape(dec_b, n_rows, -1)
        return jnp.pad(r, ((0, 0), (0, page - n_rows), (0, 0)))

    transposed_pages = lambda c: jnp.transpose(c[0], (0, 2, 3, 1)).reshape(n_pool, width, page)
    interleaved_rows = lambda c: c.reshape(n_pool, page * n_h_df, LANES)
    msb_s = _sb_decode(page_table, qsb_s.reshape(dec_b, dec_t, width),
                       new_page(k_sb_s, dec_t), new_page(v_sb_s, dec_t),
                       transposed_pages(cache_k_sb), transposed_pages(cache_v_sb), tri_d, sb_out_g)
    mdf_s = _df_decode(page_table, qdf_s.reshape(dec_b, dec_t, width),
                       new_page(k_df_s, dec_t * n_h_df), new_page(v_df_s, dec_t * n_h_df),
                       interleaved_rows(cache_k_diff), interleaved_rows(cache_v_diff),
                       bias_s, bias_n, diff_subln_g, lams, lam_init, n_slots)

    wg_b, wu_b, wd_b = w_gate[0].astype(BF16), w_up[0].astype(BF16), w_down[0].astype(BF16)
    n_p = b * tp
    x1_p, h2_p, gates_p = _post_attention(
        msb_p.reshape(n_p, width), mdf_p.reshape(n_p, width), xp.reshape(n_p, d),
        wo_sb, wo_df, norm2_g, rw_hi, rw_lo, rb, _largest_tile(n_p, 512))
    y_p = _moe(h2_p, x1_p, gates_p, wg_b, wu_b, wd_b, sg, su, sd,
               _largest_tile(n_p, MOE_TOKEN_TILE), MOE_EXPERT_GROUP)
    y_prompt = y_p.reshape(b, tp, d)[:, n_meta:t]
    x1_s, h2_s, gates_s = _post_attention(
        msb_s.reshape(n_s, width), mdf_s.reshape(n_s, width), x_sample.reshape(n_s, d),
        wo_sb, wo_df, norm2_g, rw_hi, rw_lo, rb, n_s)
    y_s = _moe(h2_s, x1_s, gates_s, wg_b, wu_b, wd_b, sg, su, sd, n_s, MOE_EXPERT_GROUP)
    y_sample = y_s.reshape(dec_b, dec_t, d)

    sb_rows = lambda r, bb, tt: r.reshape(1, bb, tt, n_h_sb, DH)
    df_rows = lambda r, bb, tt: r.reshape(1, bb, tt, n_h_df, LANES)
    return (y_prompt, y_sample,
            sb_rows(k_sb_p, b, t), sb_rows(v_sb_p, b, t), df_rows(k_df_p, b, t), df_rows(v_df_p, b, t),
            sb_rows(k_sb_s, dec_b, dec_t), sb_rows(v_sb_s, dec_b, dec_t),
            df_rows(k_df_s, dec_b, dec_t), df_rows(v_df_s, dec_b, dec_t))
```

```python
import functools
import math

import jax
import jax.numpy as jnp
from jax import lax
from jax.experimental import pallas as pl
from jax.experimental.pallas import tpu as pltpu

F32 = jnp.float32
BF16 = jnp.bfloat16
I32 = jnp.int32

EPS = 1e-6
DH = 64
LANES = 128
MAX_DISTANCE = 128
TOP_K = 8
N_GROUPS = 8
TOPK_GROUPS = 4
ROUTED_SCALE = 2.5
NEG = -1e30
SB_CUTOFF = -104.0
ATT_BLOCK = 256
FAR_BLOCKS = 4
DECODE_PAGES_PER_STEP = 8
MOE_TOKEN_TILE = 1024
MOE_EXPERT_GROUP = 4
VMEM_LIMIT = 48 * 1024 * 1024

_TRANS_B = (((1,), (1,)), ((), ()))


def _dot(a, b):
    return jnp.dot(a, b, preferred_element_type=F32)


def _dot_tb(a, b):
    return lax.dot_general(a, b, _TRANS_B, preferred_element_type=F32)


def _split_bf16(x):
    hi = x.astype(BF16)
    lo = (x - hi.astype(F32)).astype(BF16)
    return hi, lo


def _round_up(n, m):
    return (n + m - 1) // m * m


def _proj_body(x_ref, g1_ref, w_ref, qg_ref, kg_ref, gm_ref,
               ksb_o, vsb_o, kdf_o, vdf_o,
               qsb_b, ksb_b, vsb_b, qdf_b, kdf_b, vdf_b, *, width):
    x = x_ref[0]
    ms = jnp.mean(x * x, axis=-1, keepdims=True)
    h = (x * lax.rsqrt(ms + EPS)) * g1_ref[...]
    proj = _dot(h.astype(BF16), w_ref[...])
    w = width
    q_sb, k_sb, v_sb = proj[:, 0:w], proj[:, w:2 * w], proj[:, 2 * w:3 * w]
    q_df, k_df, v_df = proj[:, 3 * w:4 * w], proj[:, 4 * w:5 * w], proj[:, 5 * w:6 * w]

    def map_norm(t, g_ref):
        msq = _dot((t * t).astype(BF16), gm_ref[...])
        return (t * lax.rsqrt(msq + EPS)) * g_ref[...]

    q_df = map_norm(q_df, qg_ref)
    k_df = map_norm(k_df, kg_ref)
    ksb_o[0] = k_sb
    vsb_o[0] = v_sb
    kdf_o[0] = k_df
    vdf_o[0] = v_df
    scale = DH ** -0.5
    qsb_b[0] = (q_sb * scale).astype(BF16)
    ksb_b[0] = k_sb.astype(BF16)
    vsb_b[0] = v_sb.astype(BF16)
    qdf_b[0] = (q_df * scale).astype(BF16)
    kdf_b[0] = k_df.astype(BF16)
    vdf_b[0] = v_df.astype(BF16)


def _project(x_pad, t_valid, g1, w_in_b, qg, kg, gm, tm):
    b, tp, d = x_pad.shape
    width = w_in_b.shape[1] // 6
    grid = (b, tp // tm)
    row = lambda bi, i: (bi, i, 0)
    const = lambda bi, i: (0, 0)
    f32_out = jax.ShapeDtypeStruct((b, t_valid, width), F32)
    bf_out = jax.ShapeDtypeStruct((b, tp, width), BF16)
    out_spec = pl.BlockSpec((1, tm, width), row)
    return pl.pallas_call(
        functools.partial(_proj_body, width=width),
        grid=grid,
        in_specs=[
            pl.BlockSpec((1, tm, d), row),
            pl.BlockSpec((1, d), const),
            pl.BlockSpec(w_in_b.shape, const),
            pl.BlockSpec((1, width), const),
            pl.BlockSpec((1, width), const),
            pl.BlockSpec((width, width), const),
        ],
        out_specs=[out_spec] * 10,
        out_shape=[f32_out] * 4 + [bf_out] * 6,
        compiler_params=pltpu.CompilerParams(
            dimension_semantics=("arbitrary", "arbitrary"),
            vmem_limit_bytes=VMEM_LIMIT),
        name="proj",
    )(x_pad, g1, w_in_b, qg, kg, gm)


def _stack_halves(q):
    lane = lax.broadcasted_iota(I32, (1, LANES), 1)
    zero = jnp.zeros_like(q)
    return jnp.concatenate(
        [jnp.where(lane < DH, q, zero), jnp.where(lane >= DH, q, zero)], axis=0)


def _neg_softplus(z):
    return -(jnp.maximum(z, 0.0) + jnp.log(1.0 + jnp.exp(-jnp.abs(z))))


def _sb_block(qq, k, v, tri, acc, carry, vis):
    z = _dot_tb(qq, k)
    log_keep = _neg_softplus(z)
    if vis is not None:
        log_keep = jnp.where(vis, log_keep, 0.0)
    hi, lo = _split_bf16(log_keep)
    csum = _dot(hi, tri) + _dot(lo, tri) + carry
    w = jnp.exp(z + csum)
    if vis is not None:
        w = jnp.where(vis, w, 0.0)
    acc = acc + _dot(w.astype(BF16), v)
    return acc, csum[:, 0:1]


def _sb_finish(acc, t, g):
    lane = lax.broadcasted_iota(I32, (1, LANES), 1)
    lo_half = lane < DH
    o = jnp.where(lo_half, acc[:t], acc[t:])
    o2 = o * o
    s_lo = jnp.sum(jnp.where(lo_half, o2, 0.0), axis=-1, keepdims=True)
    s_hi = jnp.sum(jnp.where(lo_half, 0.0, o2), axis=-1, keepdims=True)
    ms = jnp.where(lo_half, s_lo, s_hi) * (1.0 / DH)
    return (o * lax.rsqrt(ms + EPS)) * g


def _df_block(qq, k, v, bias, m, l, acc):
    s = _dot_tb(qq, k) + bias
    m_new = jnp.maximum(m, jnp.max(s, axis=-1, keepdims=True))
    alpha = jnp.exp(m - m_new)
    p = jnp.exp(s - m_new)
    l = alpha * l + jnp.sum(p, axis=-1, keepdims=True)
    acc = alpha * acc + _dot(p.astype(BF16), v)
    return m_new, l, acc


def _lambda(lq1, lk1, lq2, lk2, lam_init):
    s1 = jnp.sum(lq1[...] * lk1[...], axis=-1, keepdims=True)
    s2 = jnp.sum(lq2[...] * lk2[...], axis=-1, keepdims=True)
    return jnp.exp(s1) - jnp.exp(s2) + lam_init


def _df_finish(acc, l, t, lam, g, lam_init):
    o = acc[:t] / l[:t] - lam * (acc[t:] / l[t:])
    ms = jnp.mean(o * o, axis=-1, keepdims=True)
    return (o * lax.rsqrt(ms + EPS)) * (g * (1.0 - lam_init))


def _bias_tile(tab_ref, h, rel, n_buckets):
    max_exact = n_buckets // 2
    n = jnp.maximum(rel, 0)
    nf = jnp.maximum(n, 1).astype(F32)
    large = max_exact + (jnp.log(nf / max_exact) / math.log(MAX_DISTANCE / max_exact)
                         * (n_buckets - max_exact)).astype(I32)
    large = jnp.minimum(large, n_buckets - 1)
    bucket = jnp.where(n < max_exact, n, large)
    bias = jnp.zeros(rel.shape, F32)
    for b in range(n_buckets):
        bias = jnp.where(bucket == b, tab_ref[b, h], bias)
    return jnp.where(rel >= 0, bias, NEG)


def _bias_body(tab_ref, bp_ref, bs_ref, bn_ref, *, n_buckets, n_heads, blk, dec_t, page, n_slots):
    r = lax.broadcasted_iota(I32, (blk, blk), 0)
    c = lax.broadcasted_iota(I32, (blk, blk), 1)
    cols = page * n_heads
    rs = lax.broadcasted_iota(I32, (dec_t, cols), 0)
    cs = lax.broadcasted_iota(I32, (dec_t, cols), 1)
    pos, head = cs // n_heads, cs % n_heads
    rn = lax.broadcasted_iota(I32, (dec_t, page), 0)
    cn = lax.broadcasted_iota(I32, (dec_t, page), 1)
    pos_n, head_n = cn // n_heads, cn % n_heads
    for h in range(n_heads):
        for d in range(2):
            bp_ref[h, d] = _bias_tile(tab_ref, h, d * blk + r - c, n_buckets)
        far = jnp.where(head == h, tab_ref[n_buckets - 1, h], NEG)
        last = jnp.where(head == h, _bias_tile(tab_ref, h, page + rs - pos, n_buckets), NEG)
        new = jnp.where((head_n == h) & (pos_n < dec_t),
                        _bias_tile(tab_ref, h, rn - pos_n, n_buckets), NEG)
        for mp in range(2):
            rows = pl.ds((2 * h + mp) * dec_t, dec_t)
            bn_ref[rows, :] = new
            for u in range(n_slots):
                bs_ref[0, rows, u * cols:(u + 1) * cols] = far
                bs_ref[1, rows, u * cols:(u + 1) * cols] = last if u == n_slots - 1 else far


def _bias_tiles(rel_bias, blk, dec_t, page, n_slots):
    n_buckets, n_heads = rel_bias.shape
    rows = 2 * n_heads * dec_t
    return pl.pallas_call(
        functools.partial(_bias_body, n_buckets=n_buckets, n_heads=n_heads, blk=blk,
                          dec_t=dec_t, page=page, n_slots=n_slots),
        in_specs=[pl.BlockSpec(memory_space=pltpu.SMEM)],
        out_shape=[jax.ShapeDtypeStruct((n_heads, 2, blk, blk), F32),
                   jax.ShapeDtypeStruct((2, rows, n_slots * page * n_heads), F32),
                   jax.ShapeDtypeStruct((rows, page), F32)],
        name="bias_tiles",
    )(rel_bias)


def _sb_prompt_body(q_ref, k_ref, v_ref, tri_ref, g_ref, o_ref, *, blk):
    i = pl.program_id(2)
    qq = _stack_halves(q_ref[0])
    tri = tri_ref[...]
    r = lax.broadcasted_iota(I32, (2 * blk, blk), 0)
    r = jnp.where(r >= blk, r - blk, r)
    c = lax.broadcasted_iota(I32, (2 * blk, blk), 1)
    vis = c < r

    def load(j):
        start = pl.multiple_of(j * blk, blk)
        return k_ref[0, pl.ds(start, blk), :], v_ref[0, pl.ds(start, blk), :]

    acc = jnp.zeros((2 * blk, LANES), F32)
    carry = jnp.zeros((2 * blk, 1), F32)
    k, v = load(i)
    acc, carry = _sb_block(qq, k, v, tri, acc, carry, vis)

    def more(state):
        jj, _, carry = state
        return (jj < i) & (jnp.max(carry) > SB_CUTOFF)

    def body(state):
        jj, acc, carry = state
        k, v = load(i - 1 - jj)
        acc, carry = _sb_block(qq, k, v, tri, acc, carry, None)
        return jj + 1, acc, carry

    _, acc, _ = lax.while_loop(more, body, (jnp.int32(0), acc, carry))
    o_ref[0] = _sb_finish(acc, blk, g_ref[...]).astype(o_ref.dtype)


def _sb_prompt(q_b, k_b, v_b, tri, g, blk):
    b, tp, width = q_b.shape
    n_pairs = width // LANES
    grid = (b, n_pairs, tp // blk)
    return pl.pallas_call(
        functools.partial(_sb_prompt_body, blk=blk),
        grid=grid,
        in_specs=[
            pl.BlockSpec((1, blk, LANES), lambda bi, p, i: (bi, i, p)),
            pl.BlockSpec((1, tp, LANES), lambda bi, p, i: (bi, 0, p)),
            pl.BlockSpec((1, tp, LANES), lambda bi, p, i: (bi, 0, p)),
            pl.BlockSpec((blk, blk), lambda bi, p, i: (0, 0)),
            pl.BlockSpec((1, LANES), lambda bi, p, i: (0, p)),
        ],
        out_specs=pl.BlockSpec((1, blk, LANES), lambda bi, p, i: (bi, i, p)),
        out_shape=jax.ShapeDtypeStruct((b, tp, width), BF16),
        compiler_params=pltpu.CompilerParams(
            dimension_semantics=("arbitrary", "arbitrary", "arbitrary"),
            vmem_limit_bytes=VMEM_LIMIT),
        name="sb_prompt",
    )(q_b, k_b, v_b, tri, g)


def _df_prompt_body(tab_ref, q_ref, k_ref, v_ref, bias_ref, g_ref, lq1, lk1, lq2, lk2,
                    o_ref, m_ref, l_ref, acc_ref, *, blk, lam_init, n_buckets):
    h = pl.program_id(1)
    i = pl.program_id(2)
    qq = _stack_halves(q_ref[0])
    far_bias = tab_ref[n_buckets - 1, h]

    def scores(j, n, d):
        keys = pl.ds(pl.multiple_of(j * blk, blk), n * blk)
        s = _dot_tb(qq, k_ref[0, keys, :])
        if d is None:
            return s + far_bias, keys
        t = bias_ref[0, d]
        return s + jnp.concatenate([t, t], axis=0), keys

    def sweep(visit):
        visit(i, 1, 0)

        @pl.when(i >= 1)
        def _():
            visit(i - 1, 1, 1)

        n_far = jnp.maximum(i - 1, 0)
        n_wide = n_far // FAR_BLOCKS

        def wide(g, carry):
            visit(g * FAR_BLOCKS, FAR_BLOCKS, None)
            return carry

        def single(j, carry):
            visit(j, 1, None)
            return carry

        lax.fori_loop(0, n_wide, wide, 0)
        lax.fori_loop(n_wide * FAR_BLOCKS, n_far, single, 0)

    def lane_chunks(x):
        return [x[:, c * LANES:(c + 1) * LANES] for c in range(x.shape[1] // LANES)]

    m_ref[...] = jnp.full(m_ref.shape, NEG, F32)

    def visit_max(j, n, d):
        s, _ = scores(j, n, d)
        m_ref[...] = functools.reduce(jnp.maximum, lane_chunks(s), m_ref[...])

    sweep(visit_max)
    m = jnp.max(m_ref[...], axis=-1, keepdims=True)

    l_ref[...] = jnp.zeros_like(l_ref)
    acc_ref[...] = jnp.zeros_like(acc_ref)

    def visit_sum(j, n, d):
        s, keys = scores(j, n, d)
        p = jnp.exp(s - m)
        l_ref[...] = functools.reduce(jnp.add, lane_chunks(p), l_ref[...])
        acc_ref[...] += _dot(p.astype(BF16), v_ref[0, keys, :])

    sweep(visit_sum)
    l = jnp.sum(l_ref[...], axis=-1, keepdims=True)
    lam = _lambda(lq1, lk1, lq2, lk2, lam_init)
    o_ref[0] = _df_finish(acc_ref[...], l, blk, lam, g_ref[...], lam_init).astype(o_ref.dtype)


def _df_prompt(rel_bias, q_b, k_b, v_b, bias_p, g, lams, blk, lam_init):
    b, tp, width = q_b.shape
    n_heads = width // LANES
    grid = (b, n_heads, tp // blk)
    vec = pl.BlockSpec((1, DH), lambda bi, h, i: (0, 0))
    return pl.pallas_call(
        functools.partial(_df_prompt_body, blk=blk, lam_init=lam_init,
                          n_buckets=rel_bias.shape[0]),
        grid=grid,
        in_specs=[
            pl.BlockSpec(memory_space=pltpu.SMEM),
            pl.BlockSpec((1, blk, LANES), lambda bi, h, i: (bi, i, h)),
            pl.BlockSpec((1, tp, LANES), lambda bi, h, i: (bi, 0, h)),
            pl.BlockSpec((1, tp, LANES), lambda bi, h, i: (bi, 0, h)),
            pl.BlockSpec((1, 2, blk, blk), lambda bi, h, i: (h, 0, 0, 0)),
            pl.BlockSpec((1, LANES), lambda bi, h, i: (0, 0)),
            vec, vec, vec, vec,
        ],
        out_specs=pl.BlockSpec((1, blk, LANES), lambda bi, h, i: (bi, i, h)),
        out_shape=jax.ShapeDtypeStruct((b, tp, width), BF16),
        scratch_shapes=[pltpu.VMEM((2 * blk, LANES), F32)] * 3,
        compiler_params=pltpu.CompilerParams(
            dimension_semantics=("arbitrary", "arbitrary", "arbitrary"),
            vmem_limit_bytes=VMEM_LIMIT),
        name="df_prompt",
    )(rel_bias, q_b, k_b, v_b, bias_p, g, *lams)


def _stack_decode_queries(q, n_blocks):
    q = q.astype(F32)
    return jnp.concatenate(
        [_stack_halves(q[:, p * LANES:(p + 1) * LANES]) for p in range(n_blocks)], axis=0).astype(BF16)


def _sb_decode_body(pt_ref, q_ref, kn_ref, vn_ref, kc_hbm, vc_hbm, tri_ref, g_ref, o_ref,
                    kbuf, vbuf, sem, *, dec_t, n_blocks, page, n_pages):
    bi = pl.program_id(0)
    rows = 2 * dec_t
    qq = _stack_decode_queries(q_ref[0], n_blocks)
    tri = tri_ref[...]

    def page_copies(slot, jj):
        pg = pt_ref[bi, n_pages - 1 - jj]
        return (pltpu.make_async_copy(kc_hbm.at[pg], kbuf.at[slot], sem.at[0, slot]),
                pltpu.make_async_copy(vc_hbm.at[pg], vbuf.at[slot], sem.at[1, slot]))

    def start(slot, jj):
        for cp in page_copies(slot, jj):
            cp.start()

    def wait(slot, jj):
        for cp in page_copies(slot, jj):
            cp.wait()

    start(0, 0)

    def weights(z, carry, vis):
        log_keep = _neg_softplus(z)
        if vis is not None:
            log_keep = jnp.where(vis, log_keep, 0.0)
        hi, lo = _split_bf16(log_keep)
        csum = _dot(hi, tri) + _dot(lo, tri) + carry
        w = jnp.exp(z + csum)
        if vis is not None:
            w = jnp.where(vis, w, 0.0)
        return w.astype(BF16), csum[:, 0:1]

    r = lax.broadcasted_iota(I32, (dec_t, page), 0)
    c = lax.broadcasted_iota(I32, (dec_t, page), 1)
    vis = jnp.concatenate([c < r] * (2 * n_blocks), axis=0)
    z = jnp.concatenate(
        [_dot_tb(qq[p * rows:(p + 1) * rows], kn_ref[0, :, p * LANES:(p + 1) * LANES].astype(BF16))
         for p in range(n_blocks)], axis=0)
    w, carry = weights(z, jnp.zeros((n_blocks * rows, 1), F32), vis)
    acc = jnp.concatenate(
        [_dot(w[p * rows:(p + 1) * rows], vn_ref[0, :, p * LANES:(p + 1) * LANES].astype(BF16))
         for p in range(n_blocks)], axis=0)

    def more(state):
        jj, _, carry = state
        return (jj < n_pages) & (jnp.max(carry) > SB_CUTOFF)

    def body(state):
        jj, acc, carry = state
        slot = jj & 1
        wait(slot, jj)

        @pl.when(jj + 1 < n_pages)
        def _():
            start(1 - slot, jj + 1)

        kt = kbuf[slot]
        vt = vbuf[slot]
        z = jnp.concatenate(
            [_dot(qq[p * rows:(p + 1) * rows], kt[p * LANES:(p + 1) * LANES, :].astype(BF16))
             for p in range(n_blocks)], axis=0)
        w, carry = weights(z, carry, None)
        acc = acc + jnp.concatenate(
            [_dot_tb(w[p * rows:(p + 1) * rows], vt[p * LANES:(p + 1) * LANES, :].astype(BF16))
             for p in range(n_blocks)], axis=0)
        return jj + 1, acc, carry

    jj, acc, _ = lax.while_loop(more, body, (jnp.int32(0), acc, carry))

    @pl.when(jj < n_pages)
    def _():
        wait(jj & 1, jj)

    for p in range(n_blocks):
        o_ref[0, :, p * LANES:(p + 1) * LANES] = _sb_finish(
            acc[p * rows:(p + 1) * rows, :], dec_t, g_ref[:, p * LANES:(p + 1) * LANES])


def _sb_decode(page_table, q, k_new, v_new, kt_cache, vt_cache, tri, g):
    b, dec_t, width = q.shape
    n_pages = page_table.shape[1]
    page = kt_cache.shape[2]
    n_blocks = width // LANES
    per_b = lambda bi, pt: (bi, 0, 0)
    const = lambda bi, pt: (0, 0)
    grid_spec = pltpu.PrefetchScalarGridSpec(
        num_scalar_prefetch=1,
        grid=(b,),
        in_specs=[
            pl.BlockSpec((1, dec_t, width), per_b),
            pl.BlockSpec((1, page, width), per_b),
            pl.BlockSpec((1, page, width), per_b),
            pl.BlockSpec(memory_space=pl.ANY),
            pl.BlockSpec(memory_space=pl.ANY),
            pl.BlockSpec((page, page), const),
            pl.BlockSpec((1, width), const),
        ],
        out_specs=pl.BlockSpec((1, dec_t, width), per_b),
        scratch_shapes=[pltpu.VMEM((2, width, page), F32),
                        pltpu.VMEM((2, width, page), F32),
                        pltpu.SemaphoreType.DMA((2, 2))],
    )
    return pl.pallas_call(
        functools.partial(_sb_decode_body, dec_t=dec_t, n_blocks=n_blocks, page=page, n_pages=n_pages),
        grid_spec=grid_spec,
        out_shape=jax.ShapeDtypeStruct((b, dec_t, width), F32),
        compiler_params=pltpu.CompilerParams(
            dimension_semantics=("arbitrary",), vmem_limit_bytes=VMEM_LIMIT),
        name="sb_decode",
    )(page_table, q, k_new, v_new, kt_cache, vt_cache, tri, g)


def _df_decode_body(pt_ref, q_ref, kn_ref, vn_ref, *rest, dec_t, n_blocks, n_slots, lam_init):
    kc_refs, vc_refs = rest[:n_slots], rest[n_slots:2 * n_slots]
    (bias_ref, bias_new_ref, g_ref, lq1, lk1, lq2, lk2, o_ref,
     qq_ref, m_ref, l_ref, acc_ref) = rest[2 * n_slots:]
    jj = pl.program_id(1)
    rows = 2 * dec_t

    def update(k, v, bias):
        m, l, acc = _df_block(qq_ref[...], k, v, bias, m_ref[...], l_ref[...], acc_ref[...])
        m_ref[...] = m
        l_ref[...] = l
        acc_ref[...] = acc

    @pl.when(jj == 0)
    def _():
        qq_ref[...] = _stack_decode_queries(q_ref[0], n_blocks)
        m_ref[...] = jnp.full(m_ref.shape, NEG, F32)
        l_ref[...] = jnp.zeros_like(l_ref)
        acc_ref[...] = jnp.zeros_like(acc_ref)
        update(kn_ref[0].astype(BF16), vn_ref[0].astype(BF16), bias_new_ref[...])

    k = jnp.concatenate([r[0].astype(BF16) for r in kc_refs], axis=0)
    v = jnp.concatenate([r[0].astype(BF16) for r in vc_refs], axis=0)
    update(k, v, bias_ref[0])

    @pl.when(jj == pl.num_programs(1) - 1)
    def _():
        lam = _lambda(lq1, lk1, lq2, lk2, lam_init)
        for p in range(n_blocks):
            sl = slice(p * rows, (p + 1) * rows)
            o_ref[0, :, p * LANES:(p + 1) * LANES] = _df_finish(
                acc_ref[sl, :], l_ref[sl, :], dec_t, lam, g_ref[...], lam_init)


def _df_decode(page_table, q, k_new, v_new, k_cache, v_cache, bias_s, bias_new, g, lams, lam_init,
               n_slots):
    b, dec_t, width = q.shape
    n_pages = page_table.shape[1]
    page_rows = k_cache.shape[1]
    n_blocks = width // LANES
    rows = n_blocks * 2 * dec_t
    n_steps = n_pages // n_slots
    per_b = lambda bi, jj, pt: (bi, 0, 0)
    const = lambda bi, jj, pt: (0, 0)
    vec = pl.BlockSpec((1, DH), const)

    def slot_spec(u):
        return pl.BlockSpec((1, page_rows, LANES), lambda bi, jj, pt: (pt[bi, jj * n_slots + u], 0, 0))

    grid_spec = pltpu.PrefetchScalarGridSpec(
        num_scalar_prefetch=1,
        grid=(b, n_steps),
        in_specs=[
            pl.BlockSpec((1, dec_t, width), per_b),
            pl.BlockSpec((1,) + k_new.shape[1:], per_b),
            pl.BlockSpec((1,) + v_new.shape[1:], per_b),
            *[slot_spec(u) for u in range(n_slots)],
            *[slot_spec(u) for u in range(n_slots)],
            pl.BlockSpec((1, rows, n_slots * page_rows),
                         lambda bi, jj, pt: (jnp.where(jj == n_steps - 1, 1, 0), 0, 0)),
            pl.BlockSpec(bias_new.shape, const),
            pl.BlockSpec((1, LANES), const),
            vec, vec, vec, vec,
        ],
        out_specs=pl.BlockSpec((1, dec_t, width), per_b),
        scratch_shapes=[pltpu.VMEM((rows, LANES), BF16),
                        pltpu.VMEM((rows, 1), F32),
                        pltpu.VMEM((rows, 1), F32),
                        pltpu.VMEM((rows, LANES), F32)],
    )
    return pl.pallas_call(
        functools.partial(_df_decode_body, dec_t=dec_t, n_blocks=n_blocks, n_slots=n_slots,
                          lam_init=lam_init),
        grid_spec=grid_spec,
        out_shape=jax.ShapeDtypeStruct((b, dec_t, width), F32),
        compiler_params=pltpu.CompilerParams(
            dimension_semantics=("arbitrary", "arbitrary"),
            vmem_limit_bytes=VMEM_LIMIT),
        name="df_decode",
    )(page_table, q, k_new, v_new, *([k_cache] * n_slots), *([v_cache] * n_slots),
      bias_s, bias_new, g, *lams)


def _router_gates(logits_t, bias_col, n_experts):
    tm = logits_t.shape[1]
    gsz = n_experts // N_GROUPS
    scores = 1.0 / (1.0 + jnp.exp(-logits_t))
    sel = scores + bias_col
    sub = lax.broadcasted_iota(I32, (gsz, tm), 0)
    group_scores = []
    for g in range(N_GROUPS):
        blk = sel[g * gsz:(g + 1) * gsz]
        m1 = jnp.max(blk, axis=0, keepdims=True)
        first = jnp.min(jnp.where(blk == m1, sub, gsz), axis=0, keepdims=True)
        m2 = jnp.max(jnp.where(sub == first, -jnp.inf, blk), axis=0, keepdims=True)
        group_scores.append(m1 + m2)
    gs = jnp.concatenate(group_scores, axis=0)
    gidx = lax.broadcasted_iota(I32, (N_GROUPS, tm), 0)
    grank = jnp.zeros((N_GROUPS, tm), I32)
    for g in range(N_GROUPS):
        row = gs[g:g + 1]
        ahead = (row > gs) | ((row == gs) & (gidx > g))
        grank = grank + ahead.astype(I32)
    gkeep = grank < TOPK_GROUPS
    masked = jnp.concatenate(
        [jnp.where(jnp.broadcast_to(gkeep[g:g + 1], (gsz, tm)), sel[g * gsz:(g + 1) * gsz], -jnp.inf)
         for g in range(N_GROUPS)], axis=0)
    eidx = lax.broadcasted_iota(I32, (n_experts, tm), 0)
    erank = jnp.zeros((n_experts, tm), I32)
    for e in range(n_experts):
        row = masked[e:e + 1]
        ahead = (row > masked) | ((row == masked) & (eidx > e))
        erank = erank + ahead.astype(I32)
    w = jnp.where(erank < TOP_K, scores, 0.0)
    return w / jnp.sum(w, axis=0, keepdims=True) * ROUTED_SCALE


def _post_body(msb_ref, mdf_ref, x_ref, wo_sb_ref, wo_df_ref, g2_ref, rw_hi_ref, rw_lo_ref,
               rb_ref, x1_ref, h2_ref, gates_ref, *, n_experts):
    att = _dot(msb_ref[...].astype(BF16), wo_sb_ref[...]) + _dot(mdf_ref[...].astype(BF16), wo_df_ref[...])
    x1 = x_ref[...] + att
    x1_ref[...] = x1
    ms = jnp.mean(x1 * x1, axis=-1, keepdims=True)
    h2 = (x1 * lax.rsqrt(ms + EPS)) * g2_ref[...]
    h2_hi, h2_lo = _split_bf16(h2)
    h2_ref[...] = h2_hi
    rw_hi = rw_hi_ref[...]
    logits_t = _dot_tb(rw_hi, h2_hi) + _dot_tb(rw_hi, h2_lo) + _dot_tb(rw_lo_ref[...], h2_hi)
    gates_t = _router_gates(logits_t, rb_ref[...], n_experts)
    tm = gates_t.shape[1]
    pad = gates_ref.shape[1] - n_experts
    gates_t = jnp.concatenate([gates_t, jnp.zeros((pad, tm), F32)], axis=0)
    gates_ref[...] = gates_t.T.astype(gates_ref.dtype)


def _post_attention(msb, mdf, x, wo_sb, wo_df, g2, rw_hi, rw_lo, rb, tm):
    n, d = x.shape
    n_experts = rw_hi.shape[0]
    ge = _round_up(n_experts, LANES)
    row = lambda i: (i, 0)
    const = lambda i: (0, 0)
    return pl.pallas_call(
        functools.partial(_post_body, n_experts=n_experts),
        grid=(n // tm,),
        in_specs=[
            pl.BlockSpec((tm, msb.shape[1]), row),
            pl.BlockSpec((tm, mdf.shape[1]), row),
            pl.BlockSpec((tm, d), row),
            pl.BlockSpec(wo_sb.shape, const),
            pl.BlockSpec(wo_df.shape, const),
            pl.BlockSpec((1, d), const),
            pl.BlockSpec(rw_hi.shape, const),
            pl.BlockSpec(rw_lo.shape, const),
            pl.BlockSpec((n_experts, 1), const),
        ],
        out_specs=[pl.BlockSpec((tm, d), row), pl.BlockSpec((tm, d), row), pl.BlockSpec((tm, ge), row)],
        out_shape=[jax.ShapeDtypeStruct((n, d), F32), jax.ShapeDtypeStruct((n, d), BF16),
                   jax.ShapeDtypeStruct((n, ge), F32)],
        compiler_params=pltpu.CompilerParams(
            dimension_semantics=("arbitrary",), vmem_limit_bytes=VMEM_LIMIT),
        name="post_attention",
    )(msb, mdf, x, wo_sb, wo_df, g2, rw_hi, rw_lo, rb)


def _silu(x):
    return x / (1.0 + jnp.exp(-x))


def _moe_body(h2_ref, x1_ref, gates_ref, wg_ref, wu_ref, wd_ref, sg_ref, su_ref, sd_ref,
              o_ref, acc_ref, *, group):
    eg = pl.program_id(1)
    h2 = h2_ref[...]

    @pl.when(eg == 0)
    def _():
        a = _silu(_dot(h2, sg_ref[...])) * _dot(h2, su_ref[...])
        acc_ref[...] = _dot(a.astype(BF16), sd_ref[...])

    f = wg_ref.shape[2]
    gates = gates_ref[...]
    lane = lax.broadcasted_iota(I32, gates.shape, 1)
    acts = []
    for j in range(group):
        gate = jnp.sum(jnp.where(lane == eg * group + j, gates, 0.0), axis=-1, keepdims=True)
        acts.append((_silu(_dot(h2, wg_ref[j])) * _dot(h2, wu_ref[j]) * gate).astype(BF16))
    a = jnp.concatenate(acts, axis=1)
    wd = wd_ref[...].reshape(group * f, wd_ref.shape[2])
    acc_ref[...] += _dot(a, wd)

    @pl.when(eg == pl.num_programs(1) - 1)
    def _():
        o_ref[...] = x1_ref[...] + acc_ref[...]


def _moe(h2, x1, gates, w_gate, w_up, w_down, sg, su, sd, tm, group):
    n, d = x1.shape
    n_experts, _, f = w_gate.shape
    row = lambda i, e: (i, 0)
    const = lambda i, e: (0, 0)
    return pl.pallas_call(
        functools.partial(_moe_body, group=group),
        grid=(n // tm, n_experts // group),
        in_specs=[
            pl.BlockSpec((tm, d), row),
            pl.BlockSpec((tm, d), row),
            pl.BlockSpec((tm, gates.shape[1]), row),
            pl.BlockSpec((group, d, f), lambda i, e: (e, 0, 0)),
            pl.BlockSpec((group, d, f), lambda i, e: (e, 0, 0)),
            pl.BlockSpec((group, f, d), lambda i, e: (e, 0, 0)),
            pl.BlockSpec(sg.shape, const),
            pl.BlockSpec(su.shape, const),
            pl.BlockSpec(sd.shape, const),
        ],
        out_specs=pl.BlockSpec((tm, d), row),
        out_shape=jax.ShapeDtypeStruct((n, d), F32),
        scratch_shapes=[pltpu.VMEM((tm, d), F32)],
        compiler_params=pltpu.CompilerParams(
            dimension_semantics=("arbitrary", "arbitrary"), vmem_limit_bytes=VMEM_LIMIT),
        name="moe",
    )(h2, x1, gates, w_gate, w_up, w_down, sg, su, sd)


def _largest_tile(n, cap):
    t = cap
    while n % t:
        t //= 2
    return t


def kernel(x_prompt, x_sample, cache_k_sb, cache_v_sb, cache_k_diff, cache_v_diff, page_table,
           meta_tokens, rel_bias, norm1_g, w_in, q_norm_g, k_norm_g, lambda_q1, lambda_k1,
           lambda_q2, lambda_k2, sb_out_g, diff_subln_g, w_out, norm2_g, router_w, router_bias,
           w_gate, w_up, w_down, w_shared_gate, w_shared_up, w_shared_down):
    assert w_in.shape[0] == 1, "single-layer step"
    b, seq, d = x_prompt.shape
    dec_b, dec_t, _ = x_sample.shape
    n_meta = meta_tokens.shape[0]
    width = sb_out_g.shape[1]
    n_pool, page = cache_k_sb.shape[1], cache_k_sb.shape[2]
    n_experts = router_w.shape[2]
    t = seq + n_meta
    blk = ATT_BLOCK
    tp = _round_up(t, blk)
    lam_init = 0.8 - 0.6 * math.exp(-0.3 * 0)

    w_in_b = w_in[0].astype(BF16)
    g1 = norm1_g
    reps = width // DH
    qg = jnp.tile(q_norm_g, (1, reps))
    kg = jnp.tile(k_norm_g, (1, reps))
    lane = jnp.arange(width)
    gm = ((lane[:, None] // DH) == (lane[None, :] // DH)).astype(BF16) * (1.0 / DH)
    tri_p = (jnp.arange(blk)[:, None] >= jnp.arange(blk)[None, :]).astype(BF16)
    tri_d = (jnp.arange(page)[:, None] >= jnp.arange(page)[None, :]).astype(BF16)
    lams = (lambda_q1, lambda_k1, lambda_q2, lambda_k2)
    wo_sb = w_out[0, :width].astype(BF16)
    wo_df = w_out[0, width:].astype(BF16)
    rw_t = router_w[0].T
    rw_hi = rw_t.astype(BF16)
    rw_lo = (rw_t - rw_hi.astype(F32)).astype(BF16)
    rb = router_bias[0][:, None]
    sg = w_shared_gate[0].astype(BF16)
    su = w_shared_up[0].astype(BF16)
    sd = w_shared_down[0].astype(BF16)

    n_h_sb = width // DH
    n_h_df = width // LANES
    n_slots = DECODE_PAGES_PER_STEP
    assert blk >= MAX_DISTANCE and page >= MAX_DISTANCE
    assert page_table.shape[1] % n_slots == 0 and dec_t * n_h_df <= page
    bias_p, bias_s, bias_n = _bias_tiles(rel_bias, blk, dec_t, page, n_slots)

    meta = jnp.broadcast_to(meta_tokens[None], (b, n_meta, d))
    xp = jnp.concatenate([meta, x_prompt, jnp.zeros((b, tp - t, d), F32)], axis=1)
    (k_sb_p, v_sb_p, k_df_p, v_df_p,
     qsb_b, ksb_b, vsb_b, qdf_b, kdf_b, vdf_b) = _project(xp, t, g1, w_in_b, qg, kg, gm, blk)
    msb_p = _sb_prompt(qsb_b, ksb_b, vsb_b, tri_p, sb_out_g, blk)
    mdf_p = _df_prompt(rel_bias, qdf_b, kdf_b, vdf_b, bias_p, diff_subln_g, lams, blk, lam_init)

    n_s = dec_b * dec_t
    xs = x_sample.reshape(1, n_s, d)
    (k_sb_s, v_sb_s, k_df_s, v_df_s,
     qsb_s, _, _, qdf_s, _, _) = _project(xs, n_s, g1, w_in_b, qg, kg, gm, n_s)

    def new_page(rows, n_rows):
        r = rows.reshape(dec_b, n_rows, -1)
        return jnp.pad(r, ((0, 0), (0, page - n_rows), (0, 0)))

    transposed_pages = lambda c: jnp.transpose(c[0], (0, 2, 3, 1)).reshape(n_pool, width, page)
    interleaved_rows = lambda c: c.reshape(n_pool, page * n_h_df, LANES)
    msb_s = _sb_decode(page_table, qsb_s.reshape(dec_b, dec_t, width),
                       new_page(k_sb_s, dec_t), new_page(v_sb_s, dec_t),
                       transposed_pages(cache_k_sb), transposed_pages(cache_v_sb), tri_d, sb_out_g)
    mdf_s = _df_decode(page_table, qdf_s.reshape(dec_b, dec_t, width),
                       new_page(k_df_s, dec_t * n_h_df), new_page(v_df_s, dec_t * n_h_df),
                       interleaved_rows(cache_k_diff), interleaved_rows(cache_v_diff),
                       bias_s, bias_n, diff_subln_g, lams, lam_init, n_slots)

    wg_b, wu_b, wd_b = w_gate[0].astype(BF16), w_up[0].astype(BF16), w_down[0].astype(BF16)
    n_p = b * tp
    x1_p, h2_p, gates_p = _post_attention(
        msb_p.reshape(n_p, width), mdf_p.reshape(n_p, width), xp.reshape(n_p, d),
        wo_sb, wo_df, norm2_g, rw_hi, rw_lo, rb, _largest_tile(n_p, 512))
    y_p = _moe(h2_p, x1_p, gates_p, wg_b, wu_b, wd_b, sg, su, sd,
               _largest_tile(n_p, MOE_TOKEN_TILE), MOE_EXPERT_GROUP)
    y_prompt = y_p.reshape(b, tp, d)[:, n_meta:t]
    x1_s, h2_s, gates_s = _post_attention(
        msb_s.reshape(n_s, width), mdf_s.reshape(n_s, width), x_sample.reshape(n_s, d),
        wo_sb, wo_df, norm2_g, rw_hi, rw_lo, rb, n_s)
    y_s = _moe(h2_s, x1_s, gates_s, wg_b, wu_b, wd_b, sg, su, sd, n_s, MOE_EXPERT_GROUP)
    y_sample = y_s.reshape(dec_b, dec_t, d)

    sb_rows = lambda r, bb, tt: r.reshape(1, bb, tt, n_h_sb, DH)
    df_rows = lambda r, bb, tt: r.reshape(1, bb, tt, n_h_df, LANES)
    return (y_prompt, y_sample,
            sb_rows(k_sb_p, b, t), sb_rows(v_sb_p, b, t), df_rows(k_df_p, b, t), df_rows(v_df_p, b, t),
            sb_rows(k_sb_s, dec_b, dec_t), sb_rows(v_sb_s, dec_b, dec_t),
            df_rows(k_df_s, dec_b, dec_t), df_rows(v_df_s, dec_b, dec_t))
```

```python
import functools
import math

import jax
import jax.numpy as jnp
from jax import lax
from jax.experimental import pallas as pl
from jax.experimental.pallas import tpu as pltpu

F32 = jnp.float32
BF16 = jnp.bfloat16
I32 = jnp.int32

EPS = 1e-6
DH = 64
LANES = 128
MAX_DISTANCE = 128
TOP_K = 8
N_GROUPS = 8
TOPK_GROUPS = 4
ROUTED_SCALE = 2.5
NEG = -1e30
SB_CUTOFF = -104.0
ATT_BLOCK = 256
DF_HEADS_PER_STEP = 2
DECODE_PAGES_PER_STEP = 8
MOE_TOKEN_TILE = 1024
MOE_EXPERT_GROUP = 4
VMEM_LIMIT = 48 * 1024 * 1024

_TRANS_B = (((1,), (1,)), ((), ()))


def _dot(a, b):
    return jnp.dot(a, b, preferred_element_type=F32)


def _dot_tb(a, b):
    return lax.dot_general(a, b, _TRANS_B, preferred_element_type=F32)


def _split_bf16(x):
    hi = x.astype(BF16)
    lo = (x - hi.astype(F32)).astype(BF16)
    return hi, lo


def _round_up(n, m):
    return (n + m - 1) // m * m


def _proj_body(x_ref, g1_ref, w_ref, qg_ref, kg_ref, gm_ref,
               ksb_o, vsb_o, kdf_o, vdf_o,
               qsb_b, ksb_b, vsb_b, qdf_b, kdf_b, vdf_b, *, width):
    x = x_ref[0]
    ms = jnp.mean(x * x, axis=-1, keepdims=True)
    h = (x * lax.rsqrt(ms + EPS)) * g1_ref[...]
    proj = _dot(h.astype(BF16), w_ref[...])
    w = width
    q_sb, k_sb, v_sb = proj[:, 0:w], proj[:, w:2 * w], proj[:, 2 * w:3 * w]
    q_df, k_df, v_df = proj[:, 3 * w:4 * w], proj[:, 4 * w:5 * w], proj[:, 5 * w:6 * w]

    def map_norm(t, g_ref):
        msq = _dot((t * t).astype(BF16), gm_ref[...])
        return (t * lax.rsqrt(msq + EPS)) * g_ref[...]

    q_df = map_norm(q_df, qg_ref)
    k_df = map_norm(k_df, kg_ref)
    ksb_o[0] = k_sb
    vsb_o[0] = v_sb
    kdf_o[0] = k_df
    vdf_o[0] = v_df
    scale = DH ** -0.5
    qsb_b[0] = (q_sb * scale).astype(BF16)
    ksb_b[0] = k_sb.astype(BF16)
    vsb_b[0] = v_sb.astype(BF16)
    qdf_b[0] = (q_df * scale).astype(BF16)
    kdf_b[0] = k_df.astype(BF16)
    vdf_b[0] = v_df.astype(BF16)


def _project(x_pad, t_valid, g1, w_in_b, qg, kg, gm, tm):
    b, tp, d = x_pad.shape
    width = w_in_b.shape[1] // 6
    grid = (b, tp // tm)
    row = lambda bi, i: (bi, i, 0)
    const = lambda bi, i: (0, 0)
    f32_out = jax.ShapeDtypeStruct((b, t_valid, width), F32)
    bf_out = jax.ShapeDtypeStruct((b, tp, width), BF16)
    out_spec = pl.BlockSpec((1, tm, width), row)
    return pl.pallas_call(
        functools.partial(_proj_body, width=width),
        grid=grid,
        in_specs=[
            pl.BlockSpec((1, tm, d), row),
            pl.BlockSpec((1, d), const),
            pl.BlockSpec(w_in_b.shape, const),
            pl.BlockSpec((1, width), const),
            pl.BlockSpec((1, width), const),
            pl.BlockSpec((width, width), const),
        ],
        out_specs=[out_spec] * 10,
        out_shape=[f32_out] * 4 + [bf_out] * 6,
        compiler_params=pltpu.CompilerParams(
            dimension_semantics=("arbitrary", "arbitrary"),
            vmem_limit_bytes=VMEM_LIMIT),
        name="proj",
    )(x_pad, g1, w_in_b, qg, kg, gm)


def _stack_halves(q):
    lane = lax.broadcasted_iota(I32, (1, LANES), 1)
    zero = jnp.zeros_like(q)
    return jnp.concatenate(
        [jnp.where(lane < DH, q, zero), jnp.where(lane >= DH, q, zero)], axis=0)


def _neg_softplus(z):
    return -(jnp.maximum(z, 0.0) + jnp.log(1.0 + jnp.exp(-jnp.abs(z))))


def _sb_block(qq, k, v, tri, acc, carry, vis):
    z = _dot_tb(qq, k)
    log_keep = _neg_softplus(z)
    if vis is not None:
        log_keep = jnp.where(vis, log_keep, 0.0)
    hi, lo = _split_bf16(log_keep)
    csum = _dot(hi, tri) + _dot(lo, tri) + carry
    w = jnp.exp(z + csum)
    if vis is not None:
        w = jnp.where(vis, w, 0.0)
    acc = acc + _dot(w.astype(BF16), v)
    return acc, csum[:, 0:1]


def _sb_finish(acc, t, g):
    lane = lax.broadcasted_iota(I32, (1, LANES), 1)
    lo_half = lane < DH
    o = jnp.where(lo_half, acc[:t], acc[t:])
    o2 = o * o
    s_lo = jnp.sum(jnp.where(lo_half, o2, 0.0), axis=-1, keepdims=True)
    s_hi = jnp.sum(jnp.where(lo_half, 0.0, o2), axis=-1, keepdims=True)
    ms = jnp.where(lo_half, s_lo, s_hi) * (1.0 / DH)
    return (o * lax.rsqrt(ms + EPS)) * g


def _df_block(qq, k, v, bias, m, l, acc):
    s = _dot_tb(qq, k) + bias
    m_new = jnp.maximum(m, jnp.max(s, axis=-1, keepdims=True))
    alpha = jnp.exp(m - m_new)
    p = jnp.exp(s - m_new)
    l = alpha * l + jnp.sum(p, axis=-1, keepdims=True)
    acc = alpha * acc + _dot(p.astype(BF16), v)
    return m_new, l, acc


def _lambda(lq1, lk1, lq2, lk2, lam_init):
    s1 = jnp.sum(lq1[...] * lk1[...], axis=-1, keepdims=True)
    s2 = jnp.sum(lq2[...] * lk2[...], axis=-1, keepdims=True)
    return jnp.exp(s1) - jnp.exp(s2) + lam_init


def _df_finish(acc, l, t, lam, g, lam_init):
    o = acc[:t] / l[:t] - lam * (acc[t:] / l[t:])
    ms = jnp.mean(o * o, axis=-1, keepdims=True)
    return (o * lax.rsqrt(ms + EPS)) * (g * (1.0 - lam_init))


def _bias_tile(tab_ref, h, rel, n_buckets):
    max_exact = n_buckets // 2
    n = jnp.maximum(rel, 0)
    nf = jnp.maximum(n, 1).astype(F32)
    large = max_exact + (jnp.log(nf / max_exact) / math.log(MAX_DISTANCE / max_exact)
                         * (n_buckets - max_exact)).astype(I32)
    large = jnp.minimum(large, n_buckets - 1)
    bucket = jnp.where(n < max_exact, n, large)
    bias = jnp.zeros(rel.shape, F32)
    for b in range(n_buckets):
        bias = jnp.where(bucket == b, tab_ref[b, h], bias)
    return jnp.where(rel >= 0, bias, NEG)


def _bias_body(tab_ref, bp_ref, bs_ref, bn_ref, *, n_buckets, n_heads, blk, dec_t, page, n_slots):
    r = lax.broadcasted_iota(I32, (blk, 2 * blk), 0)
    c = lax.broadcasted_iota(I32, (blk, 2 * blk), 1)
    cols = page * n_heads
    rs = lax.broadcasted_iota(I32, (dec_t, cols), 0)
    cs = lax.broadcasted_iota(I32, (dec_t, cols), 1)
    pos, head = cs // n_heads, cs % n_heads
    rn = lax.broadcasted_iota(I32, (dec_t, page), 0)
    cn = lax.broadcasted_iota(I32, (dec_t, page), 1)
    pos_n, head_n = cn // n_heads, cn % n_heads
    for h in range(n_heads):
        bp_ref[h] = _bias_tile(tab_ref, h, blk + r - c, n_buckets)
        far = jnp.where(head == h, tab_ref[n_buckets - 1, h], NEG)
        last = jnp.where(head == h, _bias_tile(tab_ref, h, page + rs - pos, n_buckets), NEG)
        new = jnp.where((head_n == h) & (pos_n < dec_t),
                        _bias_tile(tab_ref, h, rn - pos_n, n_buckets), NEG)
        for mp in range(2):
            rows = pl.ds((2 * h + mp) * dec_t, dec_t)
            bn_ref[rows, :] = new
            for u in range(n_slots):
                bs_ref[0, rows, u * cols:(u + 1) * cols] = far
                bs_ref[1, rows, u * cols:(u + 1) * cols] = last if u == n_slots - 1 else far


def _bias_tiles(rel_bias, blk, dec_t, page, n_slots):
    n_buckets, n_heads = rel_bias.shape
    rows = 2 * n_heads * dec_t
    return pl.pallas_call(
        functools.partial(_bias_body, n_buckets=n_buckets, n_heads=n_heads, blk=blk,
                          dec_t=dec_t, page=page, n_slots=n_slots),
        in_specs=[pl.BlockSpec(memory_space=pltpu.SMEM)],
        out_shape=[jax.ShapeDtypeStruct((n_heads, blk, 2 * blk), F32),
                   jax.ShapeDtypeStruct((2, rows, n_slots * page * n_heads), F32),
                   jax.ShapeDtypeStruct((rows, page), F32)],
        name="bias_tiles",
    )(rel_bias)


def _sb_prompt_body(q_ref, k_ref, v_ref, tri_ref, g_ref, o_ref, *, blk):
    i = pl.program_id(2)
    qq = _stack_halves(q_ref[0])
    tri = tri_ref[...]
    r = lax.broadcasted_iota(I32, (2 * blk, blk), 0)
    r = jnp.where(r >= blk, r - blk, r)
    c = lax.broadcasted_iota(I32, (2 * blk, blk), 1)
    vis = c < r

    def load(j):
        start = pl.multiple_of(j * blk, blk)
        return k_ref[0, pl.ds(start, blk), :], v_ref[0, pl.ds(start, blk), :]

    acc = jnp.zeros((2 * blk, LANES), F32)
    carry = jnp.zeros((2 * blk, 1), F32)
    k, v = load(i)
    acc, carry = _sb_block(qq, k, v, tri, acc, carry, vis)

    def more(state):
        jj, _, carry = state
        return (jj < i) & (jnp.max(carry) > SB_CUTOFF)

    def body(state):
        jj, acc, carry = state
        k, v = load(i - 1 - jj)
        acc, carry = _sb_block(qq, k, v, tri, acc, carry, None)
        return jj + 1, acc, carry

    _, acc, _ = lax.while_loop(more, body, (jnp.int32(0), acc, carry))
    o_ref[0] = _sb_finish(acc, blk, g_ref[...]).astype(o_ref.dtype)


def _sb_prompt(q_b, k_b, v_b, tri, g, blk):
    b, tp, width = q_b.shape
    n_pairs = width // LANES
    grid = (b, n_pairs, tp // blk)
    return pl.pallas_call(
        functools.partial(_sb_prompt_body, blk=blk),
        grid=grid,
        in_specs=[
            pl.BlockSpec((1, blk, LANES), lambda bi, p, i: (bi, i, p)),
            pl.BlockSpec((1, tp, LANES), lambda bi, p, i: (bi, 0, p)),
            pl.BlockSpec((1, tp, LANES), lambda bi, p, i: (bi, 0, p)),
            pl.BlockSpec((blk, blk), lambda bi, p, i: (0, 0)),
            pl.BlockSpec((1, LANES), lambda bi, p, i: (0, p)),
        ],
        out_specs=pl.BlockSpec((1, blk, LANES), lambda bi, p, i: (bi, i, p)),
        out_shape=jax.ShapeDtypeStruct((b, tp, width), BF16),
        compiler_params=pltpu.CompilerParams(
            dimension_semantics=("arbitrary", "arbitrary", "arbitrary"),
            vmem_limit_bytes=VMEM_LIMIT),
        name="sb_prompt",
    )(q_b, k_b, v_b, tri, g)


def _df_prompt_body(tab_ref, q_ref, k_ref, v_ref, bias_ref, g_ref, lq1, lk1, lq2, lk2,
                    o_ref, m_ref, l_ref, acc_ref, *, blk, lam_init, n_buckets):
    hg = pl.program_id(1)
    i = pl.program_id(2)
    heads = range(DF_HEADS_PER_STEP)
    lanes = lambda hh: slice(hh * LANES, (hh + 1) * LANES)
    qq = [_stack_halves(q_ref[0, :, lanes(hh)]) for hh in heads]
    far_bias = [tab_ref[n_buckets - 1, hg * DF_HEADS_PER_STEP + hh] for hh in heads]

    def scores(hh, j, n, near):
        keys = pl.ds(pl.multiple_of(j * blk, blk), n * blk)
        s = _dot_tb(qq[hh], k_ref[0, keys, lanes(hh)])
        if near is None:
            return s + far_bias[hh], keys
        t = bias_ref[hh, :, near]
        return s + jnp.concatenate([t, t], axis=0), keys

    def sweep(visit):
        @pl.when(i == 0)
        def _():
            visit(0, 1, slice(blk, 2 * blk))

        @pl.when(i >= 1)
        def _():
            visit(i - 1, 2, slice(0, 2 * blk))

        n_far = jnp.maximum(i - 1, 0)
        n4 = lax.shift_right_logical(n_far, 2)

        def wide(g, carry):
            visit(g * 4, 4, None)
            return carry

        lax.fori_loop(0, n4, wide, 0)

        @pl.when((n_far & 2) != 0)
        def _():
            visit(n4 * 4, 2, None)

        @pl.when((n_far & 1) != 0)
        def _():
            visit(n_far - 1, 1, None)

    def lane_chunks(x):
        return [x[:, c * LANES:(c + 1) * LANES] for c in range(x.shape[1] // LANES)]

    m_ref[...] = jnp.full(m_ref.shape, NEG, F32)

    def visit_max(j, n, near):
        for hh in heads:
            s, _ = scores(hh, j, n, near)
            m_ref[hh] = functools.reduce(jnp.maximum, lane_chunks(s), m_ref[hh])

    sweep(visit_max)
    m = [jnp.max(m_ref[hh], axis=-1, keepdims=True) for hh in heads]

    l_ref[...] = jnp.zeros_like(l_ref)
    acc_ref[...] = jnp.zeros_like(acc_ref)

    def visit_sum(j, n, near):
        for hh in heads:
            s, keys = scores(hh, j, n, near)
            p = jnp.exp(s - m[hh])
            l_ref[hh] = functools.reduce(jnp.add, lane_chunks(p), l_ref[hh])
            acc_ref[hh] += _dot(p.astype(BF16), v_ref[0, keys, lanes(hh)])

    sweep(visit_sum)
    lam = _lambda(lq1, lk1, lq2, lk2, lam_init)
    for hh in heads:
        l = jnp.sum(l_ref[hh], axis=-1, keepdims=True)
        o_ref[0, :, lanes(hh)] = _df_finish(acc_ref[hh], l, blk, lam, g_ref[...], lam_init).astype(o_ref.dtype)


def _df_prompt(rel_bias, q_b, k_b, v_b, bias_p, g, lams, blk, lam_init):
    b, tp, width = q_b.shape
    hps = DF_HEADS_PER_STEP
    w = hps * LANES
    grid = (b, width // w, tp // blk)
    vec = pl.BlockSpec((1, DH), lambda bi, h, i: (0, 0))
    return pl.pallas_call(
        functools.partial(_df_prompt_body, blk=blk, lam_init=lam_init,
                          n_buckets=rel_bias.shape[0]),
        grid=grid,
        in_specs=[
            pl.BlockSpec(memory_space=pltpu.SMEM),
            pl.BlockSpec((1, blk, w), lambda bi, h, i: (bi, i, h)),
            pl.BlockSpec((1, tp, w), lambda bi, h, i: (bi, 0, h)),
            pl.BlockSpec((1, tp, w), lambda bi, h, i: (bi, 0, h)),
            pl.BlockSpec((hps, blk, 2 * blk), lambda bi, h, i: (h, 0, 0)),
            pl.BlockSpec((1, LANES), lambda bi, h, i: (0, 0)),
            vec, vec, vec, vec,
        ],
        out_specs=pl.BlockSpec((1, blk, w), lambda bi, h, i: (bi, i, h)),
        out_shape=jax.ShapeDtypeStruct((b, tp, width), BF16),
        scratch_shapes=[pltpu.VMEM((hps, 2 * blk, LANES), F32)] * 3,
        compiler_params=pltpu.CompilerParams(
            dimension_semantics=("arbitrary", "arbitrary", "arbitrary"),
            vmem_limit_bytes=VMEM_LIMIT),
        name="df_prompt",
    )(rel_bias, q_b, k_b, v_b, bias_p, g, *lams)


def _stack_decode_queries(q, n_blocks):
    q = q.astype(F32)
    return jnp.concatenate(
        [_stack_halves(q[:, p * LANES:(p + 1) * LANES]) for p in range(n_blocks)], axis=0).astype(BF16)


def _sb_decode_body(pt_ref, q_ref, kn_ref, vn_ref, kc_hbm, vc_hbm, tri_ref, g_ref, o_ref,
                    kbuf, vbuf, sem, *, dec_t, n_blocks, page, n_pages):
    bi = pl.program_id(0)
    rows = 2 * dec_t
    qq = _stack_decode_queries(q_ref[0], n_blocks)
    tri = tri_ref[...]

    def page_copies(slot, jj):
        pg = pt_ref[bi, n_pages - 1 - jj]
        return (pltpu.make_async_copy(kc_hbm.at[pg], kbuf.at[slot], sem.at[0, slot]),
                pltpu.make_async_copy(vc_hbm.at[pg], vbuf.at[slot], sem.at[1, slot]))

    def start(slot, jj):
        for cp in page_copies(slot, jj):
            cp.start()

    def wait(slot, jj):
        for cp in page_copies(slot, jj):
            cp.wait()

    start(0, 0)

    def weights(z, carry, vis):
        log_keep = _neg_softplus(z)
        if vis is not None:
            log_keep = jnp.where(vis, log_keep, 0.0)
        hi, lo = _split_bf16(log_keep)
        csum = _dot(hi, tri) + _dot(lo, tri) + carry
        w = jnp.exp(z + csum)
        if vis is not None:
            w = jnp.where(vis, w, 0.0)
        return w.astype(BF16), csum[:, 0:1]

    r = lax.broadcasted_iota(I32, (dec_t, page), 0)
    c = lax.broadcasted_iota(I32, (dec_t, page), 1)
    vis = jnp.concatenate([c < r] * (2 * n_blocks), axis=0)
    z = jnp.concatenate(
        [_dot_tb(qq[p * rows:(p + 1) * rows], kn_ref[0, :, p * LANES:(p + 1) * LANES].astype(BF16))
         for p in range(n_blocks)], axis=0)
    w, carry = weights(z, jnp.zeros((n_blocks * rows, 1), F32), vis)
    acc = jnp.concatenate(
        [_dot(w[p * rows:(p + 1) * rows], vn_ref[0, :, p * LANES:(p + 1) * LANES].astype(BF16))
         for p in range(n_blocks)], axis=0)

    def more(state):
        jj, _, carry = state
        return (jj < n_pages) & (jnp.max(carry) > SB_CUTOFF)

    def body(state):
        jj, acc, carry = state
        slot = jj & 1
        wait(slot, jj)

        @pl.when(jj + 1 < n_pages)
        def _():
            start(1 - slot, jj + 1)

        kt = kbuf[slot]
        vt = vbuf[slot]
        z = jnp.concatenate(
            [_dot(qq[p * rows:(p + 1) * rows], kt[p * LANES:(p + 1) * LANES, :].astype(BF16))
             for p in range(n_blocks)], axis=0)
        w, carry = weights(z, carry, None)
        acc = acc + jnp.concatenate(
            [_dot_tb(w[p * rows:(p + 1) * rows], vt[p * LANES:(p + 1) * LANES, :].astype(BF16))
             for p in range(n_blocks)], axis=0)
        return jj + 1, acc, carry

    jj, acc, _ = lax.while_loop(more, body, (jnp.int32(0), acc, carry))

    @pl.when(jj < n_pages)
    def _():
        wait(jj & 1, jj)

    for p in range(n_blocks):
        o_ref[0, :, p * LANES:(p + 1) * LANES] = _sb_finish(
            acc[p * rows:(p + 1) * rows, :], dec_t, g_ref[:, p * LANES:(p + 1) * LANES])


def _sb_decode(page_table, q, k_new, v_new, kt_cache, vt_cache, tri, g):
    b, dec_t, width = q.shape
    n_pages = page_table.shape[1]
    page = kt_cache.shape[2]
    n_blocks = width // LANES
    per_b = lambda bi, pt: (bi, 0, 0)
    const = lambda bi, pt: (0, 0)
    grid_spec = pltpu.PrefetchScalarGridSpec(
        num_scalar_prefetch=1,
        grid=(b,),
        in_specs=[
            pl.BlockSpec((1, dec_t, width), per_b),
            pl.BlockSpec((1, page, width), per_b),
            pl.BlockSpec((1, page, width), per_b),
            pl.BlockSpec(memory_space=pl.ANY),
            pl.BlockSpec(memory_space=pl.ANY),
            pl.BlockSpec((page, page), const),
            pl.BlockSpec((1, width), const),
        ],
        out_specs=pl.BlockSpec((1, dec_t, width), per_b),
        scratch_shapes=[pltpu.VMEM((2, width, page), F32),
                        pltpu.VMEM((2, width, page), F32),
                        pltpu.SemaphoreType.DMA((2, 2))],
    )
    return pl.pallas_call(
        functools.partial(_sb_decode_body, dec_t=dec_t, n_blocks=n_blocks, page=page, n_pages=n_pages),
        grid_spec=grid_spec,
        out_shape=jax.ShapeDtypeStruct((b, dec_t, width), F32),
        compiler_params=pltpu.CompilerParams(
            dimension_semantics=("arbitrary",), vmem_limit_bytes=VMEM_LIMIT),
        name="sb_decode",
    )(page_table, q, k_new, v_new, kt_cache, vt_cache, tri, g)


def _df_decode_body(pt_ref, q_ref, kn_ref, vn_ref, *rest, dec_t, n_blocks, n_slots, lam_init):
    kc_refs, vc_refs = rest[:n_slots], rest[n_slots:2 * n_slots]
    (bias_ref, bias_new_ref, g_ref, lq1, lk1, lq2, lk2, o_ref,
     qq_ref, m_ref, l_ref, acc_ref) = rest[2 * n_slots:]
    jj = pl.program_id(1)
    rows = 2 * dec_t

    def update(k, v, bias):
        m, l, acc = _df_block(qq_ref[...], k, v, bias, m_ref[...], l_ref[...], acc_ref[...])
        m_ref[...] = m
        l_ref[...] = l
        acc_ref[...] = acc

    @pl.when(jj == 0)
    def _():
        qq_ref[...] = _stack_decode_queries(q_ref[0], n_blocks)
        m_ref[...] = jnp.full(m_ref.shape, NEG, F32)
        l_ref[...] = jnp.zeros_like(l_ref)
        acc_ref[...] = jnp.zeros_like(acc_ref)
        update(kn_ref[0].astype(BF16), vn_ref[0].astype(BF16), bias_new_ref[...])

    k = jnp.concatenate([r[0].astype(BF16) for r in kc_refs], axis=0)
    v = jnp.concatenate([r[0].astype(BF16) for r in vc_refs], axis=0)
    update(k, v, bias_ref[0])

    @pl.when(jj == pl.num_programs(1) - 1)
    def _():
        lam = _lambda(lq1, lk1, lq2, lk2, lam_init)
        for p in range(n_blocks):
            sl = slice(p * rows, (p + 1) * rows)
            o_ref[0, :, p * LANES:(p + 1) * LANES] = _df_finish(
                acc_ref[sl, :], l_ref[sl, :], dec_t, lam, g_ref[...], lam_init)


def _df_decode(page_table, q, k_new, v_new, k_cache, v_cache, bias_s, bias_new, g, lams, lam_init,
               n_slots):
    b, dec_t, width = q.shape
    n_pages = page_table.shape[1]
    page_rows = k_cache.shape[1]
    n_blocks = width // LANES
    rows = n_blocks * 2 * dec_t
    n_steps = n_pages // n_slots
    per_b = lambda bi, jj, pt: (bi, 0, 0)
    const = lambda bi, jj, pt: (0, 0)
    vec = pl.BlockSpec((1, DH), const)

    def slot_spec(u):
        return pl.BlockSpec((1, page_rows, LANES), lambda bi, jj, pt: (pt[bi, jj * n_slots + u], 0, 0))

    grid_spec = pltpu.PrefetchScalarGridSpec(
        num_scalar_prefetch=1,
        grid=(b, n_steps),
        in_specs=[
            pl.BlockSpec((1, dec_t, width), per_b),
            pl.BlockSpec((1,) + k_new.shape[1:], per_b),
            pl.BlockSpec((1,) + v_new.shape[1:], per_b),
            *[slot_spec(u) for u in range(n_slots)],
            *[slot_spec(u) for u in range(n_slots)],
            pl.BlockSpec((1, rows, n_slots * page_rows),
                         lambda bi, jj, pt: (jnp.where(jj == n_steps - 1, 1, 0), 0, 0)),
            pl.BlockSpec(bias_new.shape, const),
            pl.BlockSpec((1, LANES), const),
            vec, vec, vec, vec,
        ],
        out_specs=pl.BlockSpec((1, dec_t, width), per_b),
        scratch_shapes=[pltpu.VMEM((rows, LANES), BF16),
                        pltpu.VMEM((rows, 1), F32),
                        pltpu.VMEM((rows, 1), F32),
                        pltpu.VMEM((rows, LANES), F32)],
    )
    return pl.pallas_call(
        functools.partial(_df_decode_body, dec_t=dec_t, n_blocks=n_blocks, n_slots=n_slots,
                          lam_init=lam_init),
        grid_spec=grid_spec,
        out_shape=jax.ShapeDtypeStruct((b, dec_t, width), F32),
        compiler_params=pltpu.CompilerParams(
            dimension_semantics=("arbitrary", "arbitrary"),
            vmem_limit_bytes=VMEM_LIMIT),
        name="df_decode",
    )(page_table, q, k_new, v_new, *([k_cache] * n_slots), *([v_cache] * n_slots),
      bias_s, bias_new, g, *lams)


def _router_gates(logits_t, bias_col, n_experts):
    tm = logits_t.shape[1]
    gsz = n_experts // N_GROUPS
    scores = 1.0 / (1.0 + jnp.exp(-logits_t))
    sel = scores + bias_col
    sub = lax.broadcasted_iota(I32, (gsz, tm), 0)
    group_scores = []
    for g in range(N_GROUPS):
        blk = sel[g * gsz:(g + 1) * gsz]
        m1 = jnp.max(blk, axis=0, keepdims=True)
        first = jnp.min(jnp.where(blk == m1, sub, gsz), axis=0, keepdims=True)
        m2 = jnp.max(jnp.where(sub == first, -jnp.inf, blk), axis=0, keepdims=True)
        group_scores.append(m1 + m2)
    gs = jnp.concatenate(group_scores, axis=0)
    gidx = lax.broadcasted_iota(I32, (N_GROUPS, tm), 0)
    grank = jnp.zeros((N_GROUPS, tm), I32)
    for g in range(N_GROUPS):
        row = gs[g:g + 1]
        ahead = (row > gs) | ((row == gs) & (gidx > g))
        grank = grank + ahead.astype(I32)
    gkeep = grank < TOPK_GROUPS
    masked = jnp.concatenate(
        [jnp.where(jnp.broadcast_to(gkeep[g:g + 1], (gsz, tm)), sel[g * gsz:(g + 1) * gsz], -jnp.inf)
         for g in range(N_GROUPS)], axis=0)
    eidx = lax.broadcasted_iota(I32, (n_experts, tm), 0)
    erank = jnp.zeros((n_experts, tm), I32)
    for e in range(n_experts):
        row = masked[e:e + 1]
        ahead = (row > masked) | ((row == masked) & (eidx > e))
        erank = erank + ahead.astype(I32)
    w = jnp.where(erank < TOP_K, scores, 0.0)
    return w / jnp.sum(w, axis=0, keepdims=True) * ROUTED_SCALE


def _post_body(msb_ref, mdf_ref, x_ref, wo_sb_ref, wo_df_ref, g2_ref, rw_hi_ref, rw_lo_ref,
               rb_ref, x1_ref, h2_ref, gates_ref, *, n_experts):
    att = _dot(msb_ref[...].astype(BF16), wo_sb_ref[...]) + _dot(mdf_ref[...].astype(BF16), wo_df_ref[...])
    x1 = x_ref[...] + att
    x1_ref[...] = x1
    ms = jnp.mean(x1 * x1, axis=-1, keepdims=True)
    h2 = (x1 * lax.rsqrt(ms + EPS)) * g2_ref[...]
    h2_hi, h2_lo = _split_bf16(h2)
    h2_ref[...] = h2_hi
    rw_hi = rw_hi_ref[...]
    logits_t = _dot_tb(rw_hi, h2_hi) + _dot_tb(rw_hi, h2_lo) + _dot_tb(rw_lo_ref[...], h2_hi)
    gates_t = _router_gates(logits_t, rb_ref[...], n_experts)
    tm = gates_t.shape[1]
    pad = gates_ref.shape[1] - n_experts
    gates_t = jnp.concatenate([gates_t, jnp.zeros((pad, tm), F32)], axis=0)
    gates_ref[...] = gates_t.T.astype(gates_ref.dtype)


def _post_attention(msb, mdf, x, wo_sb, wo_df, g2, rw_hi, rw_lo, rb, tm):
    n, d = x.shape
    n_experts = rw_hi.shape[0]
    ge = _round_up(n_experts, LANES)
    row = lambda i: (i, 0)
    const = lambda i: (0, 0)
    return pl.pallas_call(
        functools.partial(_post_body, n_experts=n_experts),
        grid=(n // tm,),
        in_specs=[
            pl.BlockSpec((tm, msb.shape[1]), row),
            pl.BlockSpec((tm, mdf.shape[1]), row),
            pl.BlockSpec((tm, d), row),
            pl.BlockSpec(wo_sb.shape, const),
            pl.BlockSpec(wo_df.shape, const),
            pl.BlockSpec((1, d), const),
            pl.BlockSpec(rw_hi.shape, const),
            pl.BlockSpec(rw_lo.shape, const),
            pl.BlockSpec((n_experts, 1), const),
        ],
        out_specs=[pl.BlockSpec((tm, d), row), pl.BlockSpec((tm, d), row), pl.BlockSpec((tm, ge), row)],
        out_shape=[jax.ShapeDtypeStruct((n, d), F32), jax.ShapeDtypeStruct((n, d), BF16),
                   jax.ShapeDtypeStruct((n, ge), F32)],
        compiler_params=pltpu.CompilerParams(
            dimension_semantics=("arbitrary",), vmem_limit_bytes=VMEM_LIMIT),
        name="post_attention",
    )(msb, mdf, x, wo_sb, wo_df, g2, rw_hi, rw_lo, rb)


def _silu(x):
    return x / (1.0 + jnp.exp(-x))


def _moe_body(h2_ref, x1_ref, gates_ref, wg_ref, wu_ref, wd_ref, sg_ref, su_ref, sd_ref,
              o_ref, acc_ref, *, group):
    eg = pl.program_id(1)
    h2 = h2_ref[...]

    @pl.when(eg == 0)
    def _():
        a = _silu(_dot(h2, sg_ref[...])) * _dot(h2, su_ref[...])
        acc_ref[...] = _dot(a.astype(BF16), sd_ref[...])

    f = wg_ref.shape[2]
    gates = gates_ref[...]
    lane = lax.broadcasted_iota(I32, gates.shape, 1)
    acts = []
    for j in range(group):
        gate = jnp.sum(jnp.where(lane == eg * group + j, gates, 0.0), axis=-1, keepdims=True)
        acts.append((_silu(_dot(h2, wg_ref[j])) * _dot(h2, wu_ref[j]) * gate).astype(BF16))
    a = jnp.concatenate(acts, axis=1)
    wd = wd_ref[...].reshape(group * f, wd_ref.shape[2])
    acc_ref[...] += _dot(a, wd)

    @pl.when(eg == pl.num_programs(1) - 1)
    def _():
        o_ref[...] = x1_ref[...] + acc_ref[...]


def _moe(h2, x1, gates, w_gate, w_up, w_down, sg, su, sd, tm, group):
    n, d = x1.shape
    n_experts, _, f = w_gate.shape
    row = lambda i, e: (i, 0)
    const = lambda i, e: (0, 0)
    return pl.pallas_call(
        functools.partial(_moe_body, group=group),
        grid=(n // tm, n_experts // group),
        in_specs=[
            pl.BlockSpec((tm, d), row),
            pl.BlockSpec((tm, d), row),
            pl.BlockSpec((tm, gates.shape[1]), row),
            pl.BlockSpec((group, d, f), lambda i, e: (e, 0, 0)),
            pl.BlockSpec((group, d, f), lambda i, e: (e, 0, 0)),
            pl.BlockSpec((group, f, d), lambda i, e: (e, 0, 0)),
            pl.BlockSpec(sg.shape, const),
            pl.BlockSpec(su.shape, const),
            pl.BlockSpec(sd.shape, const),
        ],
        out_specs=pl.BlockSpec((tm, d), row),
        out_shape=jax.ShapeDtypeStruct((n, d), F32),
        scratch_shapes=[pltpu.VMEM((tm, d), F32)],
        compiler_params=pltpu.CompilerParams(
            dimension_semantics=("arbitrary", "arbitrary"), vmem_limit_bytes=VMEM_LIMIT),
        name="moe",
    )(h2, x1, gates, w_gate, w_up, w_down, sg, su, sd)


def _largest_tile(n, cap):
    t = cap
    while n % t:
        t //= 2
    return t


def kernel(x_prompt, x_sample, cache_k_sb, cache_v_sb, cache_k_diff, cache_v_diff, page_table,
           meta_tokens, rel_bias, norm1_g, w_in, q_norm_g, k_norm_g, lambda_q1, lambda_k1,
           lambda_q2, lambda_k2, sb_out_g, diff_subln_g, w_out, norm2_g, router_w, router_bias,
           w_gate, w_up, w_down, w_shared_gate, w_shared_up, w_shared_down):
    assert w_in.shape[0] == 1, "single-layer step"
    b, seq, d = x_prompt.shape
    dec_b, dec_t, _ = x_sample.shape
    n_meta = meta_tokens.shape[0]
    width = sb_out_g.shape[1]
    n_pool, page = cache_k_sb.shape[1], cache_k_sb.shape[2]
    n_experts = router_w.shape[2]
    t = seq + n_meta
    blk = ATT_BLOCK
    tp = _round_up(t, blk)
    lam_init = 0.8 - 0.6 * math.exp(-0.3 * 0)

    w_in_b = w_in[0].astype(BF16)
    g1 = norm1_g
    reps = width // DH
    qg = jnp.tile(q_norm_g, (1, reps))
    kg = jnp.tile(k_norm_g, (1, reps))
    lane = jnp.arange(width)
    gm = ((lane[:, None] // DH) == (lane[None, :] // DH)).astype(BF16) * (1.0 / DH)
    tri_p = (jnp.arange(blk)[:, None] >= jnp.arange(blk)[None, :]).astype(BF16)
    tri_d = (jnp.arange(page)[:, None] >= jnp.arange(page)[None, :]).astype(BF16)
    lams = (lambda_q1, lambda_k1, lambda_q2, lambda_k2)
    wo_sb = w_out[0, :width].astype(BF16)
    wo_df = w_out[0, width:].astype(BF16)
    rw_t = router_w[0].T
    rw_hi = rw_t.astype(BF16)
    rw_lo = (rw_t - rw_hi.astype(F32)).astype(BF16)
    rb = router_bias[0][:, None]
    sg = w_shared_gate[0].astype(BF16)
    su = w_shared_up[0].astype(BF16)
    sd = w_shared_down[0].astype(BF16)

    n_h_sb = width // DH
    n_h_df = width // LANES
    n_slots = DECODE_PAGES_PER_STEP
    assert blk >= MAX_DISTANCE and page >= MAX_DISTANCE
    assert page_table.shape[1] % n_slots == 0 and dec_t * n_h_df <= page
    bias_p, bias_s, bias_n = _bias_tiles(rel_bias, blk, dec_t, page, n_slots)

    meta = jnp.broadcast_to(meta_tokens[None], (b, n_meta, d))
    xp = jnp.concatenate([meta, x_prompt, jnp.zeros((b, tp - t, d), F32)], axis=1)
    (k_sb_p, v_sb_p, k_df_p, v_df_p,
     qsb_b, ksb_b, vsb_b, qdf_b, kdf_b, vdf_b) = _project(xp, t, g1, w_in_b, qg, kg, gm, blk)
    msb_p = _sb_prompt(qsb_b, ksb_b, vsb_b, tri_p, sb_out_g, blk)
    mdf_p = _df_prompt(rel_bias, qdf_b, kdf_b, vdf_b, bias_p, diff_subln_g, lams, blk, lam_init)

    n_s = dec_b * dec_t
    xs = x_sample.reshape(1, n_s, d)
    (k_sb_s, v_sb_s, k_df_s, v_df_s,
     qsb_s, _, _, qdf_s, _, _) = _project(xs, n_s, g1, w_in_b, qg, kg, gm, n_s)

    def new_page(rows, n_rows):
        r = rows.reshape(dec_b, n_rows, -1)
        return jnp.pad(r, ((0, 0), (0, page - n_rows), (0, 0)))

    transposed_pages = lambda c: jnp.transpose(c[0], (0, 2, 3, 1)).reshape(n_pool, width, page)
    interleaved_rows = lambda c: c.reshape(n_pool, page * n_h_df, LANES)
    msb_s = _sb_decode(page_table, qsb_s.reshape(dec_b, dec_t, width),
                       new_page(k_sb_s, dec_t), new_page(v_sb_s, dec_t),
                       transposed_pages(cache_k_sb), transposed_pages(cache_v_sb), tri_d, sb_out_g)
    mdf_s = _df_decode(page_table, qdf_s.reshape(dec_b, dec_t, width),
                       new_page(k_df_s, dec_t * n_h_df), new_page(v_df_s, dec_t * n_h_df),
                       interleaved_rows(cache_k_diff), interleaved_rows(cache_v_diff),
                       bias_s, bias_n, diff_subln_g, lams, lam_init, n_slots)

    wg_b, wu_b, wd_b = w_gate[0].astype(BF16), w_up[0].astype(BF16), w_down[0].astype(BF16)
    n_p = b * tp
    x1_p, h2_p, gates_p = _post_attention(
        msb_p.reshape(n_p, width), mdf_p.reshape(n_p, width), xp.reshape(n_p, d),
        wo_sb, wo_df, norm2_g, rw_hi, rw_lo, rb, _largest_tile(n_p, 512))
    y_p = _moe(h2_p, x1_p, gates_p, wg_b, wu_b, wd_b, sg, su, sd,
               _largest_tile(n_p, MOE_TOKEN_TILE), MOE_EXPERT_GROUP)
    y_prompt = y_p.reshape(b, tp, d)[:, n_meta:t]
    x1_s, h2_s, gates_s = _post_attention(
        msb_s.reshape(n_s, width), mdf_s.reshape(n_s, width), x_sample.reshape(n_s, d),
        wo_sb, wo_df, norm2_g, rw_hi, rw_lo, rb, n_s)
    y_s = _moe(h2_s, x1_s, gates_s, wg_b, wu_b, wd_b, sg, su, sd, n_s, MOE_EXPERT_GROUP)
    y_sample = y_s.reshape(dec_b, dec_t, d)

    sb_rows = lambda r, bb, tt: r.reshape(1, bb, tt, n_h_sb, DH)
    df_rows = lambda r, bb, tt: r.reshape(1, bb, tt, n_h_df, LANES)
    return (y_prompt, y_sample,
            sb_rows(k_sb_p, b, t), sb_rows(v_sb_p, b, t), df_rows(k_df_p, b, t), df_rows(v_df_p, b, t),
            sb_rows(k_sb_s, dec_b, dec_t), sb_rows(v_sb_s, dec_b, dec_t),
            df_rows(k_df_s, dec_b, dec_t), df_rows(v_df_s, dec_b, dec_t))
```

```python
import functools
import math

import jax
import jax.numpy as jnp
from jax import lax
from jax.experimental import pallas as pl
from jax.experimental.pallas import tpu as pltpu

F32 = jnp.float32
BF16 = jnp.bfloat16
I32 = jnp.int32

EPS = 1e-6
DH = 64
LANES = 128
MAX_DISTANCE = 128
TOP_K = 8
N_GROUPS = 8
TOPK_GROUPS = 4
ROUTED_SCALE = 2.5
NEG = -1e30
SB_CUTOFF = -104.0
ATT_BLOCK = 256
DF_HEADS_PER_STEP = 4
SB_PAIRS_PER_STEP = 4
DECODE_PAGES_PER_STEP = 8
MOE_TOKEN_TILE = 1024
MOE_EXPERT_GROUP = 4
VMEM_LIMIT = 48 * 1024 * 1024

_TRANS_B = (((1,), (1,)), ((), ()))


def _dot(a, b):
    return jnp.dot(a, b, preferred_element_type=F32)


def _dot_tb(a, b):
    return lax.dot_general(a, b, _TRANS_B, preferred_element_type=F32)


def _split_bf16(x):
    hi = x.astype(BF16)
    lo = (x - hi.astype(F32)).astype(BF16)
    return hi, lo


def _round_up(n, m):
    return (n + m - 1) // m * m


def _proj_body(x_ref, g1_ref, w_ref, qg_ref, kg_ref, gm_ref,
               ksb_o, vsb_o, kdf_o, vdf_o,
               qsb_b, ksb_b, vsb_b, qdf_b, kdf_b, vdf_b, *, width):
    x = x_ref[0]
    ms = jnp.mean(x * x, axis=-1, keepdims=True)
    h = (x * lax.rsqrt(ms + EPS)) * g1_ref[...]
    proj = _dot(h.astype(BF16), w_ref[...])
    w = width
    q_sb, k_sb, v_sb = proj[:, 0:w], proj[:, w:2 * w], proj[:, 2 * w:3 * w]
    q_df, k_df, v_df = proj[:, 3 * w:4 * w], proj[:, 4 * w:5 * w], proj[:, 5 * w:6 * w]

    def map_norm(t, g_ref):
        msq = _dot((t * t).astype(BF16), gm_ref[...])
        return (t * lax.rsqrt(msq + EPS)) * g_ref[...]

    q_df = map_norm(q_df, qg_ref)
    k_df = map_norm(k_df, kg_ref)
    ksb_o[0] = k_sb
    vsb_o[0] = v_sb
    kdf_o[0] = k_df
    vdf_o[0] = v_df
    scale = DH ** -0.5
    qsb_b[0] = (q_sb * scale).astype(BF16)
    ksb_b[0] = k_sb.astype(BF16)
    vsb_b[0] = v_sb.astype(BF16)
    qdf_b[0] = (q_df * scale).astype(BF16)
    kdf_b[0] = k_df.astype(BF16)
    vdf_b[0] = v_df.astype(BF16)


def _project(x_pad, t_valid, g1, w_in_b, qg, kg, gm, tm):
    b, tp, d = x_pad.shape
    width = w_in_b.shape[1] // 6
    grid = (b, tp // tm)
    row = lambda bi, i: (bi, i, 0)
    const = lambda bi, i: (0, 0)
    f32_out = jax.ShapeDtypeStruct((b, t_valid, width), F32)
    bf_out = jax.ShapeDtypeStruct((b, tp, width), BF16)
    out_spec = pl.BlockSpec((1, tm, width), row)
    return pl.pallas_call(
        functools.partial(_proj_body, width=width),
        grid=grid,
        in_specs=[
            pl.BlockSpec((1, tm, d), row),
            pl.BlockSpec((1, d), const),
            pl.BlockSpec(w_in_b.shape, const),
            pl.BlockSpec((1, width), const),
            pl.BlockSpec((1, width), const),
            pl.BlockSpec((width, width), const),
        ],
        out_specs=[out_spec] * 10,
        out_shape=[f32_out] * 4 + [bf_out] * 6,
        compiler_params=pltpu.CompilerParams(
            dimension_semantics=("arbitrary", "arbitrary"),
            vmem_limit_bytes=VMEM_LIMIT),
        name="proj",
    )(x_pad, g1, w_in_b, qg, kg, gm)


def _stack_halves(q):
    lane = lax.broadcasted_iota(I32, (1, LANES), 1)
    zero = jnp.zeros_like(q)
    return jnp.concatenate(
        [jnp.where(lane < DH, q, zero), jnp.where(lane >= DH, q, zero)], axis=0)


def _softplus(z):
    return jnp.maximum(z, 0.0) + jnp.log(1.0 + jnp.exp(-jnp.abs(z)))


def _sb_block(qq, k, v, tri, acc, carry, vis):
    z = _dot_tb(qq, k)
    drop = _softplus(z)
    if vis is not None:
        drop = jnp.where(vis, drop, 0.0)
    csum = carry - _dot(drop.astype(BF16), tri)
    w = jnp.exp(z + csum)
    if vis is not None:
        w = jnp.where(vis, w, 0.0)
    acc = acc + _dot(w.astype(BF16), v)
    return acc, csum[:, 0:1]


def _sb_finish(acc, t, g):
    lane = lax.broadcasted_iota(I32, (1, LANES), 1)
    lo_half = lane < DH
    o = jnp.where(lo_half, acc[:t], acc[t:])
    o2 = o * o
    s_lo = jnp.sum(jnp.where(lo_half, o2, 0.0), axis=-1, keepdims=True)
    s_hi = jnp.sum(jnp.where(lo_half, 0.0, o2), axis=-1, keepdims=True)
    ms = jnp.where(lo_half, s_lo, s_hi) * (1.0 / DH)
    return (o * lax.rsqrt(ms + EPS)) * g


def _df_block(qq, k, v, bias, m, l, acc):
    s = _dot_tb(qq, k) + bias
    m_new = jnp.maximum(m, jnp.max(s, axis=-1, keepdims=True))
    alpha = jnp.exp(m - m_new)
    p = jnp.exp(s - m_new)
    l = alpha * l + jnp.sum(p, axis=-1, keepdims=True)
    acc = alpha * acc + _dot(p.astype(BF16), v)
    return m_new, l, acc


def _lambda(lq1, lk1, lq2, lk2, lam_init):
    s1 = jnp.sum(lq1[...] * lk1[...], axis=-1, keepdims=True)
    s2 = jnp.sum(lq2[...] * lk2[...], axis=-1, keepdims=True)
    return jnp.exp(s1) - jnp.exp(s2) + lam_init


def _df_finish(acc, l, t, lam, g, lam_init):
    o = acc[:t] / l[:t] - lam * (acc[t:] / l[t:])
    ms = jnp.mean(o * o, axis=-1, keepdims=True)
    return (o * lax.rsqrt(ms + EPS)) * (g * (1.0 - lam_init))


def _bias_tile(tab_ref, h, rel, n_buckets):
    max_exact = n_buckets // 2
    n = jnp.maximum(rel, 0)
    nf = jnp.maximum(n, 1).astype(F32)
    large = max_exact + (jnp.log(nf / max_exact) / math.log(MAX_DISTANCE / max_exact)
                         * (n_buckets - max_exact)).astype(I32)
    large = jnp.minimum(large, n_buckets - 1)
    bucket = jnp.where(n < max_exact, n, large)
    bias = jnp.zeros(rel.shape, F32)
    for b in range(n_buckets):
        bias = jnp.where(bucket == b, tab_ref[b, h], bias)
    return jnp.where(rel >= 0, bias, NEG)


def _bias_body(tab_ref, bp_ref, bs_ref, bn_ref, *, n_buckets, n_heads, blk, dec_t, page, n_slots):
    r = lax.broadcasted_iota(I32, (blk, 2 * blk), 0)
    c = lax.broadcasted_iota(I32, (blk, 2 * blk), 1)
    cols = page * n_heads
    rs = lax.broadcasted_iota(I32, (dec_t, cols), 0)
    cs = lax.broadcasted_iota(I32, (dec_t, cols), 1)
    pos, head = cs // n_heads, cs % n_heads
    rn = lax.broadcasted_iota(I32, (dec_t, page), 0)
    cn = lax.broadcasted_iota(I32, (dec_t, page), 1)
    pos_n, head_n = cn // n_heads, cn % n_heads
    for h in range(n_heads):
        bp_ref[h] = _bias_tile(tab_ref, h, blk + r - c, n_buckets)
        far = jnp.where(head == h, tab_ref[n_buckets - 1, h], NEG)
        last = jnp.where(head == h, _bias_tile(tab_ref, h, page + rs - pos, n_buckets), NEG)
        new = jnp.where((head_n == h) & (pos_n < dec_t),
                        _bias_tile(tab_ref, h, rn - pos_n, n_buckets), NEG)
        for mp in range(2):
            rows = pl.ds((2 * h + mp) * dec_t, dec_t)
            bn_ref[rows, :] = new
            for u in range(n_slots):
                bs_ref[0, rows, u * cols:(u + 1) * cols] = far
                bs_ref[1, rows, u * cols:(u + 1) * cols] = last if u == n_slots - 1 else far


def _bias_tiles(rel_bias, blk, dec_t, page, n_slots):
    n_buckets, n_heads = rel_bias.shape
    rows = 2 * n_heads * dec_t
    return pl.pallas_call(
        functools.partial(_bias_body, n_buckets=n_buckets, n_heads=n_heads, blk=blk,
                          dec_t=dec_t, page=page, n_slots=n_slots),
        in_specs=[pl.BlockSpec(memory_space=pltpu.SMEM)],
        out_shape=[jax.ShapeDtypeStruct((n_heads, blk, 2 * blk), F32),
                   jax.ShapeDtypeStruct((2, rows, n_slots * page * n_heads), F32),
                   jax.ShapeDtypeStruct((rows, page), F32)],
        name="bias_tiles",
    )(rel_bias)


def _sb_prompt_body(q_ref, k_ref, v_ref, tri_ref, g_ref, o_ref, *, blk):
    i = pl.program_id(2)
    pairs = range(SB_PAIRS_PER_STEP)
    lanes = lambda p: slice(p * LANES, (p + 1) * LANES)
    qq = [_stack_halves(q_ref[0, :, lanes(p)]) for p in pairs]
    tri = tri_ref[...]
    r = lax.broadcasted_iota(I32, (2 * blk, blk), 0)
    r = jnp.where(r >= blk, r - blk, r)
    c = lax.broadcasted_iota(I32, (2 * blk, blk), 1)
    vis = c < r

    def block(j, state, vis):
        keys = pl.ds(pl.multiple_of(j * blk, blk), blk)
        return [_sb_block(qq[p], k_ref[0, keys, lanes(p)], v_ref[0, keys, lanes(p)], tri,
                          state[p][0], state[p][1], vis) for p in pairs]

    zero = (jnp.zeros((2 * blk, LANES), F32), jnp.zeros((2 * blk, 1), F32))
    state = block(i, [zero for _ in pairs], vis)

    def more(loop):
        jj, state = loop
        top = functools.reduce(jnp.maximum, [jnp.max(carry) for _, carry in state])
        return (jj < i) & (top > SB_CUTOFF)

    def body(loop):
        jj, state = loop
        return jj + 1, block(i - 1 - jj, state, None)

    _, state = lax.while_loop(more, body, (jnp.int32(0), state))
    for p in pairs:
        o_ref[0, :, lanes(p)] = _sb_finish(state[p][0], blk, g_ref[:, lanes(p)]).astype(o_ref.dtype)


def _sb_prompt(q_b, k_b, v_b, tri, g, blk):
    b, tp, width = q_b.shape
    w = SB_PAIRS_PER_STEP * LANES
    grid = (b, width // w, tp // blk)
    return pl.pallas_call(
        functools.partial(_sb_prompt_body, blk=blk),
        grid=grid,
        in_specs=[
            pl.BlockSpec((1, blk, w), lambda bi, p, i: (bi, i, p)),
            pl.BlockSpec((1, tp, w), lambda bi, p, i: (bi, 0, p)),
            pl.BlockSpec((1, tp, w), lambda bi, p, i: (bi, 0, p)),
            pl.BlockSpec((blk, blk), lambda bi, p, i: (0, 0)),
            pl.BlockSpec((1, w), lambda bi, p, i: (0, p)),
        ],
        out_specs=pl.BlockSpec((1, blk, w), lambda bi, p, i: (bi, i, p)),
        out_shape=jax.ShapeDtypeStruct((b, tp, width), BF16),
        compiler_params=pltpu.CompilerParams(
            dimension_semantics=("arbitrary", "arbitrary", "arbitrary"),
            vmem_limit_bytes=VMEM_LIMIT),
        name="sb_prompt",
    )(q_b, k_b, v_b, tri, g)


def _df_prompt_body(tab_ref, q_ref, k_ref, v_ref, bias_ref, g_ref, lq1, lk1, lq2, lk2,
                    o_ref, m_ref, l_ref, acc_ref, *, blk, lam_init, n_buckets):
    hg = pl.program_id(1)
    i = pl.program_id(2)
    heads = range(DF_HEADS_PER_STEP)
    lanes = lambda hh: slice(hh * LANES, (hh + 1) * LANES)
    qq = [_stack_halves(q_ref[0, :, lanes(hh)]) for hh in heads]
    far_bias = [tab_ref[n_buckets - 1, hg * DF_HEADS_PER_STEP + hh] for hh in heads]

    def scores(hh, j, n, near):
        keys = pl.ds(pl.multiple_of(j * blk, blk), n * blk)
        s = _dot_tb(qq[hh], k_ref[0, keys, lanes(hh)])
        if near is None:
            return s + far_bias[hh], keys
        t = bias_ref[hh, :, near]
        return s + jnp.concatenate([t, t], axis=0), keys

    def sweep(visit):
        @pl.when(i == 0)
        def _():
            visit(0, 1, slice(blk, 2 * blk))

        @pl.when(i >= 1)
        def _():
            visit(i - 1, 2, slice(0, 2 * blk))

        n_far = jnp.maximum(i - 1, 0)
        n4 = lax.shift_right_logical(n_far, 2)

        def wide(g, carry):
            visit(g * 4, 4, None)
            return carry

        lax.fori_loop(0, n4, wide, 0)

        @pl.when((n_far & 2) != 0)
        def _():
            visit(n4 * 4, 2, None)

        @pl.when((n_far & 1) != 0)
        def _():
            visit(n_far - 1, 1, None)

    def lane_chunks(x):
        return [x[:, c * LANES:(c + 1) * LANES] for c in range(x.shape[1] // LANES)]

    m_ref[...] = jnp.full(m_ref.shape, NEG, F32)

    def visit_max(j, n, near):
        for hh in heads:
            s, _ = scores(hh, j, n, near)
            m_ref[hh] = functools.reduce(jnp.maximum, lane_chunks(s), m_ref[hh])

    sweep(visit_max)
    m = [jnp.max(m_ref[hh], axis=-1, keepdims=True) for hh in heads]

    l_ref[...] = jnp.zeros_like(l_ref)
    acc_ref[...] = jnp.zeros_like(acc_ref)

    def visit_sum(j, n, near):
        for hh in heads:
            s, keys = scores(hh, j, n, near)
            p = jnp.exp(s - m[hh])
            l_ref[hh] = functools.reduce(jnp.add, lane_chunks(p), l_ref[hh])
            acc_ref[hh] += _dot(p.astype(BF16), v_ref[0, keys, lanes(hh)])

    sweep(visit_sum)
    lam = _lambda(lq1, lk1, lq2, lk2, lam_init)
    for hh in heads:
        l = jnp.sum(l_ref[hh], axis=-1, keepdims=True)
        o_ref[0, :, lanes(hh)] = _df_finish(acc_ref[hh], l, blk, lam, g_ref[...], lam_init).astype(o_ref.dtype)


def _df_prompt(rel_bias, q_b, k_b, v_b, bias_p, g, lams, blk, lam_init):
    b, tp, width = q_b.shape
    hps = DF_HEADS_PER_STEP
    w = hps * LANES
    grid = (b, width // w, tp // blk)
    vec = pl.BlockSpec((1, DH), lambda bi, h, i: (0, 0))
    return pl.pallas_call(
        functools.partial(_df_prompt_body, blk=blk, lam_init=lam_init,
                          n_buckets=rel_bias.shape[0]),
        grid=grid,
        in_specs=[
            pl.BlockSpec(memory_space=pltpu.SMEM),
            pl.BlockSpec((1, blk, w), lambda bi, h, i: (bi, i, h)),
            pl.BlockSpec((1, tp, w), lambda bi, h, i: (bi, 0, h)),
            pl.BlockSpec((1, tp, w), lambda bi, h, i: (bi, 0, h)),
            pl.BlockSpec((hps, blk, 2 * blk), lambda bi, h, i: (h, 0, 0)),
            pl.BlockSpec((1, LANES), lambda bi, h, i: (0, 0)),
            vec, vec, vec, vec,
        ],
        out_specs=pl.BlockSpec((1, blk, w), lambda bi, h, i: (bi, i, h)),
        out_shape=jax.ShapeDtypeStruct((b, tp, width), BF16),
        scratch_shapes=[pltpu.VMEM((hps, 2 * blk, LANES), F32)] * 3,
        compiler_params=pltpu.CompilerParams(
            dimension_semantics=("arbitrary", "arbitrary", "arbitrary"),
            vmem_limit_bytes=VMEM_LIMIT),
        name="df_prompt",
    )(rel_bias, q_b, k_b, v_b, bias_p, g, *lams)


def _stack_decode_queries(q, n_blocks):
    q = q.astype(F32)
    return jnp.concatenate(
        [_stack_halves(q[:, p * LANES:(p + 1) * LANES]) for p in range(n_blocks)], axis=0).astype(BF16)


def _sb_decode_body(pt_ref, q_ref, kn_ref, vn_ref, kc_hbm, vc_hbm, tri_ref, g_ref, o_ref,
                    kbuf, vbuf, sem, *, dec_t, n_blocks, page, n_pages):
    bi = pl.program_id(0)
    rows = 2 * dec_t
    qq = _stack_decode_queries(q_ref[0], n_blocks)
    tri = tri_ref[...]

    def page_copies(slot, jj):
        pg = pt_ref[bi, n_pages - 1 - jj]
        return (pltpu.make_async_copy(kc_hbm.at[pg], kbuf.at[slot], sem.at[0, slot]),
                pltpu.make_async_copy(vc_hbm.at[pg], vbuf.at[slot], sem.at[1, slot]))

    def start(slot, jj):
        for cp in page_copies(slot, jj):
            cp.start()

    def wait(slot, jj):
        for cp in page_copies(slot, jj):
            cp.wait()

    start(0, 0)

    def weights(z, carry, vis):
        drop = _softplus(z)
        if vis is not None:
            drop = jnp.where(vis, drop, 0.0)
        csum = carry - _dot(drop.astype(BF16), tri)
        w = jnp.exp(z + csum)
        if vis is not None:
            w = jnp.where(vis, w, 0.0)
        return w.astype(BF16), csum[:, 0:1]

    r = lax.broadcasted_iota(I32, (dec_t, page), 0)
    c = lax.broadcasted_iota(I32, (dec_t, page), 1)
    vis = jnp.concatenate([c < r] * (2 * n_blocks), axis=0)
    z = jnp.concatenate(
        [_dot_tb(qq[p * rows:(p + 1) * rows], kn_ref[0, :, p * LANES:(p + 1) * LANES].astype(BF16))
         for p in range(n_blocks)], axis=0)
    w, carry = weights(z, jnp.zeros((n_blocks * rows, 1), F32), vis)
    acc = jnp.concatenate(
        [_dot(w[p * rows:(p + 1) * rows], vn_ref[0, :, p * LANES:(p + 1) * LANES].astype(BF16))
         for p in range(n_blocks)], axis=0)

    def more(state):
        jj, _, carry = state
        return (jj < n_pages) & (jnp.max(carry) > SB_CUTOFF)

    def body(state):
        jj, acc, carry = state
        slot = jj & 1
        wait(slot, jj)

        @pl.when(jj + 1 < n_pages)
        def _():
            start(1 - slot, jj + 1)

        kt = kbuf[slot]
        vt = vbuf[slot]
        z = jnp.concatenate(
            [_dot(qq[p * rows:(p + 1) * rows], kt[p * LANES:(p + 1) * LANES, :].astype(BF16))
             for p in range(n_blocks)], axis=0)
        w, carry = weights(z, carry, None)
        acc = acc + jnp.concatenate(
            [_dot_tb(w[p * rows:(p + 1) * rows], vt[p * LANES:(p + 1) * LANES, :].astype(BF16))
             for p in range(n_blocks)], axis=0)
        return jj + 1, acc, carry

    jj, acc, _ = lax.while_loop(more, body, (jnp.int32(0), acc, carry))

    @pl.when(jj < n_pages)
    def _():
        wait(jj & 1, jj)

    for p in range(n_blocks):
        o_ref[0, :, p * LANES:(p + 1) * LANES] = _sb_finish(
            acc[p * rows:(p + 1) * rows, :], dec_t, g_ref[:, p * LANES:(p + 1) * LANES])


def _sb_decode(page_table, q, k_new, v_new, kt_cache, vt_cache, tri, g):
    b, dec_t, width = q.shape
    n_pages = page_table.shape[1]
    page = kt_cache.shape[2]
    n_blocks = width // LANES
    per_b = lambda bi, pt: (bi, 0, 0)
    const = lambda bi, pt: (0, 0)
    grid_spec = pltpu.PrefetchScalarGridSpec(
        num_scalar_prefetch=1,
        grid=(b,),
        in_specs=[
            pl.BlockSpec((1, dec_t, width), per_b),
            pl.BlockSpec((1, page, width), per_b),
            pl.BlockSpec((1, page, width), per_b),
            pl.BlockSpec(memory_space=pl.ANY),
            pl.BlockSpec(memory_space=pl.ANY),
            pl.BlockSpec((page, page), const),
            pl.BlockSpec((1, width), const),
        ],
        out_specs=pl.BlockSpec((1, dec_t, width), per_b),
        scratch_shapes=[pltpu.VMEM((2, width, page), F32),
                        pltpu.VMEM((2, width, page), F32),
                        pltpu.SemaphoreType.DMA((2, 2))],
    )
    return pl.pallas_call(
        functools.partial(_sb_decode_body, dec_t=dec_t, n_blocks=n_blocks, page=page, n_pages=n_pages),
        grid_spec=grid_spec,
        out_shape=jax.ShapeDtypeStruct((b, dec_t, width), F32),
        compiler_params=pltpu.CompilerParams(
            dimension_semantics=("arbitrary",), vmem_limit_bytes=VMEM_LIMIT),
        name="sb_decode",
    )(page_table, q, k_new, v_new, kt_cache, vt_cache, tri, g)


def _df_decode_body(pt_ref, q_ref, kn_ref, vn_ref, *rest, dec_t, n_blocks, n_slots, lam_init):
    kc_refs, vc_refs = rest[:n_slots], rest[n_slots:2 * n_slots]
    (bias_ref, bias_new_ref, g_ref, lq1, lk1, lq2, lk2, o_ref,
     qq_ref, m_ref, l_ref, acc_ref) = rest[2 * n_slots:]
    jj = pl.program_id(1)
    rows = 2 * dec_t

    def update(k, v, bias):
        m, l, acc = _df_block(qq_ref[...], k, v, bias, m_ref[...], l_ref[...], acc_ref[...])
        m_ref[...] = m
        l_ref[...] = l
        acc_ref[...] = acc

    @pl.when(jj == 0)
    def _():
        qq_ref[...] = _stack_decode_queries(q_ref[0], n_blocks)
        m_ref[...] = jnp.full(m_ref.shape, NEG, F32)
        l_ref[...] = jnp.zeros_like(l_ref)
        acc_ref[...] = jnp.zeros_like(acc_ref)
        update(kn_ref[0].astype(BF16), vn_ref[0].astype(BF16), bias_new_ref[...])

    k = jnp.concatenate([r[0].astype(BF16) for r in kc_refs], axis=0)
    v = jnp.concatenate([r[0].astype(BF16) for r in vc_refs], axis=0)
    update(k, v, bias_ref[0])

    @pl.when(jj == pl.num_programs(1) - 1)
    def _():
        lam = _lambda(lq1, lk1, lq2, lk2, lam_init)
        for p in range(n_blocks):
            sl = slice(p * rows, (p + 1) * rows)
            o_ref[0, :, p * LANES:(p + 1) * LANES] = _df_finish(
                acc_ref[sl, :], l_ref[sl, :], dec_t, lam, g_ref[...], lam_init)


def _df_decode(page_table, q, k_new, v_new, k_cache, v_cache, bias_s, bias_new, g, lams, lam_init,
               n_slots):
    b, dec_t, width = q.shape
    n_pages = page_table.shape[1]
    page_rows = k_cache.shape[1]
    n_blocks = width // LANES
    rows = n_blocks * 2 * dec_t
    n_steps = n_pages // n_slots
    per_b = lambda bi, jj, pt: (bi, 0, 0)
    const = lambda bi, jj, pt: (0, 0)
    vec = pl.BlockSpec((1, DH), const)

    def slot_spec(u):
        return pl.BlockSpec((1, page_rows, LANES), lambda bi, jj, pt: (pt[bi, jj * n_slots + u], 0, 0))

    grid_spec = pltpu.PrefetchScalarGridSpec(
        num_scalar_prefetch=1,
        grid=(b, n_steps),
        in_specs=[
            pl.BlockSpec((1, dec_t, width), per_b),
            pl.BlockSpec((1,) + k_new.shape[1:], per_b),
            pl.BlockSpec((1,) + v_new.shape[1:], per_b),
            *[slot_spec(u) for u in range(n_slots)],
            *[slot_spec(u) for u in range(n_slots)],
            pl.BlockSpec((1, rows, n_slots * page_rows),
                         lambda bi, jj, pt: (jnp.where(jj == n_steps - 1, 1, 0), 0, 0)),
            pl.BlockSpec(bias_new.shape, const),
            pl.BlockSpec((1, LANES), const),
            vec, vec, vec, vec,
        ],
        out_specs=pl.BlockSpec((1, dec_t, width), per_b),
        scratch_shapes=[pltpu.VMEM((rows, LANES), BF16),
                        pltpu.VMEM((rows, 1), F32),
                        pltpu.VMEM((rows, 1), F32),
                        pltpu.VMEM((rows, LANES), F32)],
    )
    return pl.pallas_call(
        functools.partial(_df_decode_body, dec_t=dec_t, n_blocks=n_blocks, n_slots=n_slots,
                          lam_init=lam_init),
        grid_spec=grid_spec,
        out_shape=jax.ShapeDtypeStruct((b, dec_t, width), F32),
        compiler_params=pltpu.CompilerParams(
            dimension_semantics=("arbitrary", "arbitrary"),
            vmem_limit_bytes=VMEM_LIMIT),
        name="df_decode",
    )(page_table, q, k_new, v_new, *([k_cache] * n_slots), *([v_cache] * n_slots),
      bias_s, bias_new, g, *lams)


def _router_gates(logits_t, bias_col, n_experts):
    tm = logits_t.shape[1]
    gsz = n_experts // N_GROUPS
    scores = 1.0 / (1.0 + jnp.exp(-logits_t))
    sel = scores + bias_col
    sub = lax.broadcasted_iota(I32, (gsz, tm), 0)
    group_scores = []
    for g in range(N_GROUPS):
        blk = sel[g * gsz:(g + 1) * gsz]
        m1 = jnp.max(blk, axis=0, keepdims=True)
        first = jnp.min(jnp.where(blk == m1, sub, gsz), axis=0, keepdims=True)
        m2 = jnp.max(jnp.where(sub == first, -jnp.inf, blk), axis=0, keepdims=True)
        group_scores.append(m1 + m2)
    gs = jnp.concatenate(group_scores, axis=0)
    gidx = lax.broadcasted_iota(I32, (N_GROUPS, tm), 0)
    grank = jnp.zeros((N_GROUPS, tm), I32)
    for g in range(N_GROUPS):
        row = gs[g:g + 1]
        ahead = (row > gs) | ((row == gs) & (gidx > g))
        grank = grank + ahead.astype(I32)
    gkeep = grank < TOPK_GROUPS
    masked = jnp.concatenate(
        [jnp.where(jnp.broadcast_to(gkeep[g:g + 1], (gsz, tm)), sel[g * gsz:(g + 1) * gsz], -jnp.inf)
         for g in range(N_GROUPS)], axis=0)
    eidx = lax.broadcasted_iota(I32, (n_experts, tm), 0)
    erank = jnp.zeros((n_experts, tm), I32)
    for e in range(n_experts):
        row = masked[e:e + 1]
        ahead = (row > masked) | ((row == masked) & (eidx > e))
        erank = erank + ahead.astype(I32)
    w = jnp.where(erank < TOP_K, scores, 0.0)
    return w / jnp.sum(w, axis=0, keepdims=True) * ROUTED_SCALE


def _post_body(msb_ref, mdf_ref, x_ref, wo_sb_ref, wo_df_ref, g2_ref, rw_hi_ref, rw_lo_ref,
               rb_ref, x1_ref, h2_ref, gates_ref, *, n_experts):
    att = _dot(msb_ref[...].astype(BF16), wo_sb_ref[...]) + _dot(mdf_ref[...].astype(BF16), wo_df_ref[...])
    x1 = x_ref[...] + att
    x1_ref[...] = x1
    ms = jnp.mean(x1 * x1, axis=-1, keepdims=True)
    h2 = (x1 * lax.rsqrt(ms + EPS)) * g2_ref[...]
    h2_hi, h2_lo = _split_bf16(h2)
    h2_ref[...] = h2_hi
    rw_hi = rw_hi_ref[...]
    logits_t = _dot_tb(rw_hi, h2_hi) + _dot_tb(rw_hi, h2_lo) + _dot_tb(rw_lo_ref[...], h2_hi)
    gates_t = _router_gates(logits_t, rb_ref[...], n_experts)
    tm = gates_t.shape[1]
    pad = gates_ref.shape[1] - n_experts
    gates_t = jnp.concatenate([gates_t, jnp.zeros((pad, tm), F32)], axis=0)
    gates_ref[...] = gates_t.T.astype(gates_ref.dtype)


def _post_attention(msb, mdf, x, wo_sb, wo_df, g2, rw_hi, rw_lo, rb, tm):
    n, d = x.shape
    n_experts = rw_hi.shape[0]
    ge = _round_up(n_experts, LANES)
    row = lambda i: (i, 0)
    const = lambda i: (0, 0)
    return pl.pallas_call(
        functools.partial(_post_body, n_experts=n_experts),
        grid=(n // tm,),
        in_specs=[
            pl.BlockSpec((tm, msb.shape[1]), row),
            pl.BlockSpec((tm, mdf.shape[1]), row),
            pl.BlockSpec((tm, d), row),
            pl.BlockSpec(wo_sb.shape, const),
            pl.BlockSpec(wo_df.shape, const),
            pl.BlockSpec((1, d), const),
            pl.BlockSpec(rw_hi.shape, const),
            pl.BlockSpec(rw_lo.shape, const),
            pl.BlockSpec((n_experts, 1), const),
        ],
        out_specs=[pl.BlockSpec((tm, d), row), pl.BlockSpec((tm, d), row), pl.BlockSpec((tm, ge), row)],
        out_shape=[jax.ShapeDtypeStruct((n, d), F32), jax.ShapeDtypeStruct((n, d), BF16),
                   jax.ShapeDtypeStruct((n, ge), F32)],
        compiler_params=pltpu.CompilerParams(
            dimension_semantics=("arbitrary",), vmem_limit_bytes=VMEM_LIMIT),
        name="post_attention",
    )(msb, mdf, x, wo_sb, wo_df, g2, rw_hi, rw_lo, rb)


def _silu(x):
    return x / (1.0 + jnp.exp(-x))


def _moe_body(h2_ref, x1_ref, gates_ref, wg_ref, wu_ref, wd_ref, sg_ref, su_ref, sd_ref,
              o_ref, acc_ref, *, group):
    eg = pl.program_id(1)
    h2 = h2_ref[...]

    @pl.when(eg == 0)
    def _():
        a = _silu(_dot(h2, sg_ref[...])) * _dot(h2, su_ref[...])
        acc_ref[...] = _dot(a.astype(BF16), sd_ref[...])

    f = wg_ref.shape[2]
    gates = gates_ref[...]
    lane = lax.broadcasted_iota(I32, gates.shape, 1)
    acts = []
    for j in range(group):
        gate = jnp.sum(jnp.where(lane == eg * group + j, gates, 0.0), axis=-1, keepdims=True)
        acts.append((_silu(_dot(h2, wg_ref[j])) * _dot(h2, wu_ref[j]) * gate).astype(BF16))
    a = jnp.concatenate(acts, axis=1)
    wd = wd_ref[...].reshape(group * f, wd_ref.shape[2])
    acc_ref[...] += _dot(a, wd)

    @pl.when(eg == pl.num_programs(1) - 1)
    def _():
        o_ref[...] = x1_ref[...] + acc_ref[...]


def _moe(h2, x1, gates, w_gate, w_up, w_down, sg, su, sd, tm, group):
    n, d = x1.shape
    n_experts, _, f = w_gate.shape
    row = lambda i, e: (i, 0)
    const = lambda i, e: (0, 0)
    return pl.pallas_call(
        functools.partial(_moe_body, group=group),
        grid=(n // tm, n_experts // group),
        in_specs=[
            pl.BlockSpec((tm, d), row),
            pl.BlockSpec((tm, d), row),
            pl.BlockSpec((tm, gates.shape[1]), row),
            pl.BlockSpec((group, d, f), lambda i, e: (e, 0, 0)),
            pl.BlockSpec((group, d, f), lambda i, e: (e, 0, 0)),
            pl.BlockSpec((group, f, d), lambda i, e: (e, 0, 0)),
            pl.BlockSpec(sg.shape, const),
            pl.BlockSpec(su.shape, const),
            pl.BlockSpec(sd.shape, const),
        ],
        out_specs=pl.BlockSpec((tm, d), row),
        out_shape=jax.ShapeDtypeStruct((n, d), F32),
        scratch_shapes=[pltpu.VMEM((tm, d), F32)],
        compiler_params=pltpu.CompilerParams(
            dimension_semantics=("arbitrary", "arbitrary"), vmem_limit_bytes=VMEM_LIMIT),
        name="moe",
    )(h2, x1, gates, w_gate, w_up, w_down, sg, su, sd)


def _largest_tile(n, cap):
    t = cap
    while n % t:
        t //= 2
    return t


def kernel(x_prompt, x_sample, cache_k_sb, cache_v_sb, cache_k_diff, cache_v_diff, page_table,
           meta_tokens, rel_bias, norm1_g, w_in, q_norm_g, k_norm_g, lambda_q1, lambda_k1,
           lambda_q2, lambda_k2, sb_out_g, diff_subln_g, w_out, norm2_g, router_w, router_bias,
           w_gate, w_up, w_down, w_shared_gate, w_shared_up, w_shared_down):
    assert w_in.shape[0] == 1, "single-layer step"
    b, seq, d = x_prompt.shape
    dec_b, dec_t, _ = x_sample.shape
    n_meta = meta_tokens.shape[0]
    width = sb_out_g.shape[1]
    n_pool, page = cache_k_sb.shape[1], cache_k_sb.shape[2]
    n_experts = router_w.shape[2]
    t = seq + n_meta
    blk = ATT_BLOCK
    tp = _round_up(t, blk)
    lam_init = 0.8 - 0.6 * math.exp(-0.3 * 0)

    w_in_b = w_in[0].astype(BF16)
    g1 = norm1_g
    reps = width // DH
    qg = jnp.tile(q_norm_g, (1, reps))
    kg = jnp.tile(k_norm_g, (1, reps))
    lane = jnp.arange(width)
    gm = ((lane[:, None] // DH) == (lane[None, :] // DH)).astype(BF16) * (1.0 / DH)
    tri_p = (jnp.arange(blk)[:, None] >= jnp.arange(blk)[None, :]).astype(BF16)
    tri_d = (jnp.arange(page)[:, None] >= jnp.arange(page)[None, :]).astype(BF16)
    lams = (lambda_q1, lambda_k1, lambda_q2, lambda_k2)
    wo_sb = w_out[0, :width].astype(BF16)
    wo_df = w_out[0, width:].astype(BF16)
    rw_t = router_w[0].T
    rw_hi = rw_t.astype(BF16)
    rw_lo = (rw_t - rw_hi.astype(F32)).astype(BF16)
    rb = router_bias[0][:, None]
    sg = w_shared_gate[0].astype(BF16)
    su = w_shared_up[0].astype(BF16)
    sd = w_shared_down[0].astype(BF16)

    n_h_sb = width // DH
    n_h_df = width // LANES
    n_slots = DECODE_PAGES_PER_STEP
    assert blk >= MAX_DISTANCE and page >= MAX_DISTANCE
    assert page_table.shape[1] % n_slots == 0 and dec_t * n_h_df <= page
    bias_p, bias_s, bias_n = _bias_tiles(rel_bias, blk, dec_t, page, n_slots)

    meta = jnp.broadcast_to(meta_tokens[None], (b, n_meta, d))
    xp = jnp.concatenate([meta, x_prompt, jnp.zeros((b, tp - t, d), F32)], axis=1)
    (k_sb_p, v_sb_p, k_df_p, v_df_p,
     qsb_b, ksb_b, vsb_b, qdf_b, kdf_b, vdf_b) = _project(xp, t, g1, w_in_b, qg, kg, gm, blk)
    msb_p = _sb_prompt(qsb_b, ksb_b, vsb_b, tri_p, sb_out_g, blk)
    mdf_p = _df_prompt(rel_bias, qdf_b, kdf_b, vdf_b, bias_p, diff_subln_g, lams, blk, lam_init)

    n_s = dec_b * dec_t
    xs = x_sample.reshape(1, n_s, d)
    (k_sb_s, v_sb_s, k_df_s, v_df_s,
     qsb_s, _, _, qdf_s, _, _) = _project(xs, n_s, g1, w_in_b, qg, kg, gm, n_s)

    def new_page(rows, n_rows):
        r = rows.reshape(dec_b, n_rows, -1)
        return jnp.pad(r, ((0, 0), (0, page - n_rows), (0, 0)))

    transposed_pages = lambda c: jnp.transpose(c[0], (0, 2, 3, 1)).reshape(n_pool, width, page)
    interleaved_rows = lambda c: c.reshape(n_pool, page * n_h_df, LANES)
    msb_s = _sb_decode(page_table, qsb_s.reshape(dec_b, dec_t, width),
                       new_page(k_sb_s, dec_t), new_page(v_sb_s, dec_t),
                       transposed_pages(cache_k_sb), transposed_pages(cache_v_sb), tri_d, sb_out_g)
    mdf_s = _df_decode(page_table, qdf_s.reshape(dec_b, dec_t, width),
                       new_page(k_df_s, dec_t * n_h_df), new_page(v_df_s, dec_t * n_h_df),
                       interleaved_rows(cache_k_diff), interleaved_rows(cache_v_diff),
                       bias_s, bias_n, diff_subln_g, lams, lam_init, n_slots)

    wg_b, wu_b, wd_b = w_gate[0].astype(BF16), w_up[0].astype(BF16), w_down[0].astype(BF16)
    n_p = b * tp
    x1_p, h2_p, gates_p = _post_attention(
        msb_p.reshape(n_p, width), mdf_p.reshape(n_p, width), xp.reshape(n_p, d),
        wo_sb, wo_df, norm2_g, rw_hi, rw_lo, rb, _largest_tile(n_p, 512))
    y_p = _moe(h2_p, x1_p, gates_p, wg_b, wu_b, wd_b, sg, su, sd,
               _largest_tile(n_p, MOE_TOKEN_TILE), MOE_EXPERT_GROUP)
    y_prompt = y_p.reshape(b, tp, d)[:, n_meta:t]
    x1_s, h2_s, gates_s = _post_attention(
        msb_s.reshape(n_s, width), mdf_s.reshape(n_s, width), x_sample.reshape(n_s, d),
        wo_sb, wo_df, norm2_g, rw_hi, rw_lo, rb, n_s)
    y_s = _moe(h2_s, x1_s, gates_s, wg_b, wu_b, wd_b, sg, su, sd, n_s, MOE_EXPERT_GROUP)
    y_sample = y_s.reshape(dec_b, dec_t, d)

    sb_rows = lambda r, bb, tt: r.reshape(1, bb, tt, n_h_sb, DH)
    df_rows = lambda r, bb, tt: r.reshape(1, bb, tt, n_h_df, LANES)
    return (y_prompt, y_sample,
            sb_rows(k_sb_p, b, t), sb_rows(v_sb_p, b, t), df_rows(k_df_p, b, t), df_rows(v_df_p, b, t),
            sb_rows(k_sb_s, dec_b, dec_t), sb_rows(v_sb_s, dec_b, dec_t),
            df_rows(k_df_s, dec_b, dec_t), df_rows(v_df_s, dec_b, dec_t))
```

```python
import functools
import math

import jax
import jax.numpy as jnp
from jax import lax
from jax.experimental import pallas as pl
from jax.experimental.pallas import tpu as pltpu

F32 = jnp.float32
BF16 = jnp.bfloat16
I32 = jnp.int32

EPS = 1e-6
DH = 64
LANES = 128
MAX_DISTANCE = 128
TOP_K = 8
N_GROUPS = 8
TOPK_GROUPS = 4
ROUTED_SCALE = 2.5
NEG = -1e30
SB_CUTOFF = -104.0
ATT_BLOCK = 256
DF_HEADS_PER_STEP = 4
SB_PAIRS_PER_STEP = 4
DECODE_PAGES_PER_STEP = 8
MOE_TOKEN_TILE = 1024
MOE_EXPERT_GROUP = 4
VMEM_LIMIT = 48 * 1024 * 1024

_TRANS_B = (((1,), (1,)), ((), ()))


def _dot(a, b):
    return jnp.dot(a, b, preferred_element_type=F32)


def _dot_tb(a, b):
    return lax.dot_general(a, b, _TRANS_B, preferred_element_type=F32)


def _split_bf16(x):
    hi = x.astype(BF16)
    lo = (x - hi.astype(F32)).astype(BF16)
    return hi, lo


def _round_up(n, m):
    return (n + m - 1) // m * m


def _proj_body(x_ref, g1_ref, w_ref, qg_ref, kg_ref, gm_ref,
               ksb_o, vsb_o, kdf_o, vdf_o,
               qsb_b, ksb_b, vsb_b, qdf_b, kdf_b, vdf_b, *, width):
    x = x_ref[0]
    ms = jnp.mean(x * x, axis=-1, keepdims=True)
    h = (x * lax.rsqrt(ms + EPS)) * g1_ref[...]
    proj = _dot(h.astype(BF16), w_ref[...])
    w = width
    q_sb, k_sb, v_sb = proj[:, 0:w], proj[:, w:2 * w], proj[:, 2 * w:3 * w]
    q_df, k_df, v_df = proj[:, 3 * w:4 * w], proj[:, 4 * w:5 * w], proj[:, 5 * w:6 * w]

    def map_norm(t, g_ref):
        msq = _dot((t * t).astype(BF16), gm_ref[...])
        return (t * lax.rsqrt(msq + EPS)) * g_ref[...]

    q_df = map_norm(q_df, qg_ref)
    k_df = map_norm(k_df, kg_ref)
    ksb_o[0] = k_sb
    vsb_o[0] = v_sb
    kdf_o[0] = k_df
    vdf_o[0] = v_df
    scale = DH ** -0.5
    qsb_b[0] = (q_sb * scale).astype(BF16)
    ksb_b[0] = k_sb.astype(BF16)
    vsb_b[0] = v_sb.astype(BF16)
    qdf_b[0] = (q_df * scale).astype(BF16)
    kdf_b[0] = k_df.astype(BF16)
    vdf_b[0] = v_df.astype(BF16)


def _project(x_pad, t_valid, g1, w_in_b, qg, kg, gm, tm):
    b, tp, d = x_pad.shape
    width = w_in_b.shape[1] // 6
    grid = (b, tp // tm)
    row = lambda bi, i: (bi, i, 0)
    const = lambda bi, i: (0, 0)
    f32_out = jax.ShapeDtypeStruct((b, t_valid, width), F32)
    bf_out = jax.ShapeDtypeStruct((b, tp, width), BF16)
    out_spec = pl.BlockSpec((1, tm, width), row)
    return pl.pallas_call(
        functools.partial(_proj_body, width=width),
        grid=grid,
        in_specs=[
            pl.BlockSpec((1, tm, d), row),
            pl.BlockSpec((1, d), const),
            pl.BlockSpec(w_in_b.shape, const),
            pl.BlockSpec((1, width), const),
            pl.BlockSpec((1, width), const),
            pl.BlockSpec((width, width), const),
        ],
        out_specs=[out_spec] * 10,
        out_shape=[f32_out] * 4 + [bf_out] * 6,
        compiler_params=pltpu.CompilerParams(
            dimension_semantics=("arbitrary", "arbitrary"),
            vmem_limit_bytes=VMEM_LIMIT),
        name="proj",
    )(x_pad, g1, w_in_b, qg, kg, gm)


def _stack_halves(q):
    lane = lax.broadcasted_iota(I32, (1, LANES), 1)
    zero = jnp.zeros_like(q)
    return jnp.concatenate(
        [jnp.where(lane < DH, q, zero), jnp.where(lane >= DH, q, zero)], axis=0)


def _softplus(z):
    return jnp.maximum(z, 0.0) + jnp.log(1.0 + jnp.exp(-jnp.abs(z)))


def _sb_blocks(qqs, ks, vs, tri, state, vis):
    n = range(len(qqs))
    zs = [_dot_tb(qqs[p], ks[p]) for p in n]
    drops = [_softplus(z) for z in zs]
    if vis is not None:
        drops = [jnp.where(vis, d, 0.0) for d in drops]
    csums = [state[p][1] - _dot(drops[p].astype(BF16), tri) for p in n]
    ws = [jnp.exp(zs[p] + csums[p]) for p in n]
    if vis is not None:
        ws = [jnp.where(vis, w, 0.0) for w in ws]
    pvs = [_dot(ws[p].astype(BF16), vs[p]) for p in n]
    return [(state[p][0] + pvs[p], csums[p][:, 0:1]) for p in n]


def _sb_finish(acc, t, g):
    lane = lax.broadcasted_iota(I32, (1, LANES), 1)
    lo_half = lane < DH
    o = jnp.where(lo_half, acc[:t], acc[t:])
    o2 = o * o
    s_lo = jnp.sum(jnp.where(lo_half, o2, 0.0), axis=-1, keepdims=True)
    s_hi = jnp.sum(jnp.where(lo_half, 0.0, o2), axis=-1, keepdims=True)
    ms = jnp.where(lo_half, s_lo, s_hi) * (1.0 / DH)
    return (o * lax.rsqrt(ms + EPS)) * g


def _df_block(qq, k, v, bias, m, l, acc):
    s = _dot_tb(qq, k) + bias
    m_new = jnp.maximum(m, jnp.max(s, axis=-1, keepdims=True))
    alpha = jnp.exp(m - m_new)
    p = jnp.exp(s - m_new)
    l = alpha * l + jnp.sum(p, axis=-1, keepdims=True)
    acc = alpha * acc + _dot(p.astype(BF16), v)
    return m_new, l, acc


def _lambda(lq1, lk1, lq2, lk2, lam_init):
    s1 = jnp.sum(lq1[...] * lk1[...], axis=-1, keepdims=True)
    s2 = jnp.sum(lq2[...] * lk2[...], axis=-1, keepdims=True)
    return jnp.exp(s1) - jnp.exp(s2) + lam_init


def _df_finish(acc, l, t, lam, g, lam_init):
    o = acc[:t] / l[:t] - lam * (acc[t:] / l[t:])
    ms = jnp.mean(o * o, axis=-1, keepdims=True)
    return (o * lax.rsqrt(ms + EPS)) * (g * (1.0 - lam_init))


def _bias_tile(tab_ref, h, rel, n_buckets):
    max_exact = n_buckets // 2
    n = jnp.maximum(rel, 0)
    nf = jnp.maximum(n, 1).astype(F32)
    large = max_exact + (jnp.log(nf / max_exact) / math.log(MAX_DISTANCE / max_exact)
                         * (n_buckets - max_exact)).astype(I32)
    large = jnp.minimum(large, n_buckets - 1)
    bucket = jnp.where(n < max_exact, n, large)
    bias = jnp.zeros(rel.shape, F32)
    for b in range(n_buckets):
        bias = jnp.where(bucket == b, tab_ref[b, h], bias)
    return jnp.where(rel >= 0, bias, NEG)


def _bias_body(tab_ref, bp_ref, bs_ref, bn_ref, *, n_buckets, n_heads, blk, dec_t, page, n_slots):
    r = lax.broadcasted_iota(I32, (blk, 2 * blk), 0)
    c = lax.broadcasted_iota(I32, (blk, 2 * blk), 1)
    rs = lax.broadcasted_iota(I32, (dec_t, page), 0)
    cs = lax.broadcasted_iota(I32, (dec_t, page), 1)
    for h in range(n_heads):
        bp_ref[h] = _bias_tile(tab_ref, h, blk + r - c, n_buckets)
        far = jnp.full((dec_t, page), tab_ref[n_buckets - 1, h], F32)
        last = _bias_tile(tab_ref, h, page + rs - cs, n_buckets)
        new = jnp.where(cs < dec_t, _bias_tile(tab_ref, h, rs - cs, n_buckets), NEG)
        for mp in range(2):
            rows = pl.ds((2 * h + mp) * dec_t, dec_t)
            bn_ref[rows, :] = new
            for u in range(n_slots):
                bs_ref[0, rows, u * page:(u + 1) * page] = far
                bs_ref[1, rows, u * page:(u + 1) * page] = last if u == n_slots - 1 else far


def _bias_tiles(rel_bias, blk, dec_t, page, n_slots):
    n_buckets, n_heads = rel_bias.shape
    rows = 2 * n_heads * dec_t
    return pl.pallas_call(
        functools.partial(_bias_body, n_buckets=n_buckets, n_heads=n_heads, blk=blk,
                          dec_t=dec_t, page=page, n_slots=n_slots),
        in_specs=[pl.BlockSpec(memory_space=pltpu.SMEM)],
        out_shape=[jax.ShapeDtypeStruct((n_heads, blk, 2 * blk), F32),
                   jax.ShapeDtypeStruct((2, rows, n_slots * page), F32),
                   jax.ShapeDtypeStruct((rows, page), F32)],
        name="bias_tiles",
    )(rel_bias)


def _sb_prompt_body(q_ref, k_ref, v_ref, tri_ref, g_ref, o_ref, *, blk):
    i = pl.program_id(2)
    pairs = range(SB_PAIRS_PER_STEP)
    lanes = lambda p: slice(p * LANES, (p + 1) * LANES)
    qq = [_stack_halves(q_ref[0, :, lanes(p)]) for p in pairs]
    tri = tri_ref[...]
    r = lax.broadcasted_iota(I32, (2 * blk, blk), 0)
    r = jnp.where(r >= blk, r - blk, r)
    c = lax.broadcasted_iota(I32, (2 * blk, blk), 1)
    vis = c < r

    def block(j, state, vis):
        keys = pl.ds(pl.multiple_of(j * blk, blk), blk)
        return _sb_blocks(qq, [k_ref[0, keys, lanes(p)] for p in pairs],
                          [v_ref[0, keys, lanes(p)] for p in pairs], tri, state, vis)

    zero = (jnp.zeros((2 * blk, LANES), F32), jnp.zeros((2 * blk, 1), F32))
    state = block(i, [zero for _ in pairs], vis)

    def more(loop):
        jj, state = loop
        top = functools.reduce(jnp.maximum, [jnp.max(carry) for _, carry in state])
        return (jj < i) & (top > SB_CUTOFF)

    def body(loop):
        jj, state = loop
        return jj + 1, block(i - 1 - jj, state, None)

    _, state = lax.while_loop(more, body, (jnp.int32(0), state))
    for p in pairs:
        o_ref[0, :, lanes(p)] = _sb_finish(state[p][0], blk, g_ref[:, lanes(p)]).astype(o_ref.dtype)


def _sb_prompt(q_b, k_b, v_b, tri, g, blk):
    b, tp, width = q_b.shape
    w = SB_PAIRS_PER_STEP * LANES
    grid = (b, width // w, tp // blk)
    return pl.pallas_call(
        functools.partial(_sb_prompt_body, blk=blk),
        grid=grid,
        in_specs=[
            pl.BlockSpec((1, blk, w), lambda bi, p, i: (bi, i, p)),
            pl.BlockSpec((1, tp, w), lambda bi, p, i: (bi, 0, p)),
            pl.BlockSpec((1, tp, w), lambda bi, p, i: (bi, 0, p)),
            pl.BlockSpec((blk, blk), lambda bi, p, i: (0, 0)),
            pl.BlockSpec((1, w), lambda bi, p, i: (0, p)),
        ],
        out_specs=pl.BlockSpec((1, blk, w), lambda bi, p, i: (bi, i, p)),
        out_shape=jax.ShapeDtypeStruct((b, tp, width), BF16),
        compiler_params=pltpu.CompilerParams(
            dimension_semantics=("arbitrary", "arbitrary", "arbitrary"),
            vmem_limit_bytes=VMEM_LIMIT),
        name="sb_prompt",
    )(q_b, k_b, v_b, tri, g)


def _df_prompt_body(tab_ref, q_ref, k_ref, v_ref, bias_ref, g_ref, lq1, lk1, lq2, lk2,
                    o_ref, m_ref, l_ref, acc_ref, *, blk, lam_init, n_buckets):
    hg = pl.program_id(1)
    i = pl.program_id(2)
    heads = range(DF_HEADS_PER_STEP)
    lanes = lambda hh: slice(hh * LANES, (hh + 1) * LANES)
    qq = [_stack_halves(q_ref[0, :, lanes(hh)]) for hh in heads]
    far_bias = [tab_ref[n_buckets - 1, hg * DF_HEADS_PER_STEP + hh] for hh in heads]

    def scores(hh, j, n, near):
        keys = pl.ds(pl.multiple_of(j * blk, blk), n * blk)
        s = _dot_tb(qq[hh], k_ref[0, keys, lanes(hh)])
        if near is None:
            return s + far_bias[hh], keys
        t = bias_ref[hh, :, near]
        return s + jnp.concatenate([t, t], axis=0), keys

    def sweep(visit):
        @pl.when(i == 0)
        def _():
            visit(0, 1, slice(blk, 2 * blk))

        @pl.when(i >= 1)
        def _():
            visit(i - 1, 2, slice(0, 2 * blk))

        n_far = jnp.maximum(i - 1, 0)
        n4 = lax.shift_right_logical(n_far, 2)

        def wide(g, carry):
            visit(g * 4, 4, None)
            return carry

        lax.fori_loop(0, n4, wide, 0)

        @pl.when((n_far & 2) != 0)
        def _():
            visit(n4 * 4, 2, None)

        @pl.when((n_far & 1) != 0)
        def _():
            visit(n_far - 1, 1, None)

    def lane_chunks(x):
        return [x[:, c * LANES:(c + 1) * LANES] for c in range(x.shape[1] // LANES)]

    m_ref[...] = jnp.full(m_ref.shape, NEG, F32)

    def visit_max(j, n, near):
        for hh in heads:
            s, _ = scores(hh, j, n, near)
            m_ref[hh] = functools.reduce(jnp.maximum, lane_chunks(s), m_ref[hh])

    sweep(visit_max)
    m = [jnp.max(m_ref[hh], axis=-1, keepdims=True) for hh in heads]

    l_ref[...] = jnp.zeros_like(l_ref)
    acc_ref[...] = jnp.zeros_like(acc_ref)

    def visit_sum(j, n, near):
        scored = [scores(hh, j, n, near) for hh in heads]
        ps = [jnp.exp(s - m[hh]) for hh, (s, _) in zip(heads, scored)]
        for hh in heads:
            l_ref[hh] = functools.reduce(jnp.add, lane_chunks(ps[hh]), l_ref[hh])
        pv = [_dot(ps[hh].astype(BF16), v_ref[0, scored[hh][1], lanes(hh)]) for hh in heads]
        for hh in heads:
            acc_ref[hh] += pv[hh]

    sweep(visit_sum)
    lam = _lambda(lq1, lk1, lq2, lk2, lam_init)
    for hh in heads:
        l = jnp.sum(l_ref[hh], axis=-1, keepdims=True)
        o_ref[0, :, lanes(hh)] = _df_finish(acc_ref[hh], l, blk, lam, g_ref[...], lam_init).astype(o_ref.dtype)


def _df_prompt(rel_bias, q_b, k_b, v_b, bias_p, g, lams, blk, lam_init):
    b, tp, width = q_b.shape
    hps = DF_HEADS_PER_STEP
    w = hps * LANES
    grid = (b, width // w, tp // blk)
    vec = pl.BlockSpec((1, DH), lambda bi, h, i: (0, 0))
    return pl.pallas_call(
        functools.partial(_df_prompt_body, blk=blk, lam_init=lam_init,
                          n_buckets=rel_bias.shape[0]),
        grid=grid,
        in_specs=[
            pl.BlockSpec(memory_space=pltpu.SMEM),
            pl.BlockSpec((1, blk, w), lambda bi, h, i: (bi, i, h)),
            pl.BlockSpec((1, tp, w), lambda bi, h, i: (bi, 0, h)),
            pl.BlockSpec((1, tp, w), lambda bi, h, i: (bi, 0, h)),
            pl.BlockSpec((hps, blk, 2 * blk), lambda bi, h, i: (h, 0, 0)),
            pl.BlockSpec((1, LANES), lambda bi, h, i: (0, 0)),
            vec, vec, vec, vec,
        ],
        out_specs=pl.BlockSpec((1, blk, w), lambda bi, h, i: (bi, i, h)),
        out_shape=jax.ShapeDtypeStruct((b, tp, width), BF16),
        scratch_shapes=[pltpu.VMEM((hps, 2 * blk, LANES), F32)] * 3,
        compiler_params=pltpu.CompilerParams(
            dimension_semantics=("arbitrary", "arbitrary", "arbitrary"),
            vmem_limit_bytes=VMEM_LIMIT),
        name="df_prompt",
    )(rel_bias, q_b, k_b, v_b, bias_p, g, *lams)


def _stack_decode_queries(q, n_blocks):
    q = q.astype(F32)
    return jnp.concatenate(
        [_stack_halves(q[:, p * LANES:(p + 1) * LANES]) for p in range(n_blocks)], axis=0).astype(BF16)


def _sb_decode_body(pt_ref, q_ref, kn_ref, vn_ref, kc_hbm, vc_hbm, tri_ref, g_ref, o_ref,
                    kbuf, vbuf, sem, *, dec_t, n_blocks, page, n_pages):
    bi = pl.program_id(0)
    rows = 2 * dec_t
    qq = _stack_decode_queries(q_ref[0], n_blocks)
    tri = tri_ref[...]

    def page_copies(slot, jj):
        pg = pt_ref[bi, n_pages - 1 - jj]
        return (pltpu.make_async_copy(kc_hbm.at[pg], kbuf.at[slot], sem.at[0, slot]),
                pltpu.make_async_copy(vc_hbm.at[pg], vbuf.at[slot], sem.at[1, slot]))

    def start(slot, jj):
        for cp in page_copies(slot, jj):
            cp.start()

    def wait(slot, jj):
        for cp in page_copies(slot, jj):
            cp.wait()

    start(0, 0)

    def weights(z, carry, vis):
        drop = _softplus(z)
        if vis is not None:
            drop = jnp.where(vis, drop, 0.0)
        csum = carry - _dot(drop.astype(BF16), tri)
        w = jnp.exp(z + csum)
        if vis is not None:
            w = jnp.where(vis, w, 0.0)
        return w.astype(BF16), csum[:, 0:1]

    r = lax.broadcasted_iota(I32, (dec_t, page), 0)
    c = lax.broadcasted_iota(I32, (dec_t, page), 1)
    vis = jnp.concatenate([c < r] * (2 * n_blocks), axis=0)
    z = jnp.concatenate(
        [_dot_tb(qq[p * rows:(p + 1) * rows], kn_ref[0, :, p * LANES:(p + 1) * LANES].astype(BF16))
         for p in range(n_blocks)], axis=0)
    w, carry = weights(z, jnp.zeros((n_blocks * rows, 1), F32), vis)
    acc = jnp.concatenate(
        [_dot(w[p * rows:(p + 1) * rows], vn_ref[0, :, p * LANES:(p + 1) * LANES].astype(BF16))
         for p in range(n_blocks)], axis=0)

    def more(state):
        jj, _, carry = state
        return (jj < n_pages) & (jnp.max(carry) > SB_CUTOFF)

    def body(state):
        jj, acc, carry = state
        slot = jj & 1
        wait(slot, jj)

        @pl.when(jj + 1 < n_pages)
        def _():
            start(1 - slot, jj + 1)

        kt = kbuf[slot]
        vt = vbuf[slot]
        z = jnp.concatenate(
            [_dot(qq[p * rows:(p + 1) * rows], kt[p * LANES:(p + 1) * LANES, :].astype(BF16))
             for p in range(n_blocks)], axis=0)
        w, carry = weights(z, carry, None)
        acc = acc + jnp.concatenate(
            [_dot_tb(w[p * rows:(p + 1) * rows], vt[p * LANES:(p + 1) * LANES, :].astype(BF16))
             for p in range(n_blocks)], axis=0)
        return jj + 1, acc, carry

    jj, acc, _ = lax.while_loop(more, body, (jnp.int32(0), acc, carry))

    @pl.when(jj < n_pages)
    def _():
        wait(jj & 1, jj)

    for p in range(n_blocks):
        o_ref[0, :, p * LANES:(p + 1) * LANES] = _sb_finish(
            acc[p * rows:(p + 1) * rows, :], dec_t, g_ref[:, p * LANES:(p + 1) * LANES])


def _sb_decode(page_table, q, k_new, v_new, kt_cache, vt_cache, tri, g):
    b, dec_t, width = q.shape
    n_pages = page_table.shape[1]
    page = kt_cache.shape[2]
    n_blocks = width // LANES
    per_b = lambda bi, pt: (bi, 0, 0)
    const = lambda bi, pt: (0, 0)
    grid_spec = pltpu.PrefetchScalarGridSpec(
        num_scalar_prefetch=1,
        grid=(b,),
        in_specs=[
            pl.BlockSpec((1, dec_t, width), per_b),
            pl.BlockSpec((1, page, width), per_b),
            pl.BlockSpec((1, page, width), per_b),
            pl.BlockSpec(memory_space=pl.ANY),
            pl.BlockSpec(memory_space=pl.ANY),
            pl.BlockSpec((page, page), const),
            pl.BlockSpec((1, width), const),
        ],
        out_specs=pl.BlockSpec((1, dec_t, width), per_b),
        scratch_shapes=[pltpu.VMEM((2, width, page), F32),
                        pltpu.VMEM((2, width, page), F32),
                        pltpu.SemaphoreType.DMA((2, 2))],
    )
    return pl.pallas_call(
        functools.partial(_sb_decode_body, dec_t=dec_t, n_blocks=n_blocks, page=page, n_pages=n_pages),
        grid_spec=grid_spec,
        out_shape=jax.ShapeDtypeStruct((b, dec_t, width), F32),
        compiler_params=pltpu.CompilerParams(
            dimension_semantics=("arbitrary",), vmem_limit_bytes=VMEM_LIMIT),
        name="sb_decode",
    )(page_table, q, k_new, v_new, kt_cache, vt_cache, tri, g)


def _df_decode_body(pt_ref, q_ref, kn_ref, vn_ref, *rest, dec_t, n_blocks, n_slots, lam_init):
    kc_refs, vc_refs = rest[:n_slots], rest[n_slots:2 * n_slots]
    (bias_ref, bias_new_ref, g_ref, lq1, lk1, lq2, lk2, o_ref,
     qq_ref, m_ref, l_ref, acc_ref) = rest[2 * n_slots:]
    jj = pl.program_id(1)
    rows = 2 * dec_t
    page = bias_new_ref.shape[1]

    def update(k_refs, v_refs, bias):
        own = [pl.ds(h, page, stride=n_blocks) for h in range(n_blocks)]
        gather = lambda refs, h: jnp.concatenate([r[0, own[h], :].astype(BF16) for r in refs], axis=0)
        qq = qq_ref[...]
        s = jnp.concatenate(
            [_dot_tb(qq[h * rows:(h + 1) * rows], gather(k_refs, h)) for h in range(n_blocks)],
            axis=0) + bias[...]
        m_old = m_ref[...]
        m_new = jnp.maximum(m_old, jnp.max(s, axis=-1, keepdims=True))
        alpha = jnp.exp(m_old - m_new)
        p = jnp.exp(s - m_new)
        l_ref[...] = alpha * l_ref[...] + jnp.sum(p, axis=-1, keepdims=True)
        m_ref[...] = m_new
        p = p.astype(BF16)
        pv = jnp.concatenate(
            [_dot(p[h * rows:(h + 1) * rows], gather(v_refs, h)) for h in range(n_blocks)], axis=0)
        acc_ref[...] = alpha * acc_ref[...] + pv

    @pl.when(jj == 0)
    def _():
        qq_ref[...] = _stack_decode_queries(q_ref[0], n_blocks)
        m_ref[...] = jnp.full(m_ref.shape, NEG, F32)
        l_ref[...] = jnp.zeros_like(l_ref)
        acc_ref[...] = jnp.zeros_like(acc_ref)
        update([kn_ref], [vn_ref], bias_new_ref)

    update(kc_refs, vc_refs, bias_ref.at[0])

    @pl.when(jj == pl.num_programs(1) - 1)
    def _():
        lam = _lambda(lq1, lk1, lq2, lk2, lam_init)
        for p in range(n_blocks):
            sl = slice(p * rows, (p + 1) * rows)
            o_ref[0, :, p * LANES:(p + 1) * LANES] = _df_finish(
                acc_ref[sl, :], l_ref[sl, :], dec_t, lam, g_ref[...], lam_init)


def _df_decode(page_table, q, k_new, v_new, k_cache, v_cache, bias_s, bias_new, g, lams, lam_init,
               n_slots):
    b, dec_t, width = q.shape
    n_pages = page_table.shape[1]
    page_rows = k_cache.shape[1]
    n_blocks = width // LANES
    rows = n_blocks * 2 * dec_t
    n_steps = n_pages // n_slots
    per_b = lambda bi, jj, pt: (bi, 0, 0)
    const = lambda bi, jj, pt: (0, 0)
    vec = pl.BlockSpec((1, DH), const)

    def slot_spec(u):
        return pl.BlockSpec((1, page_rows, LANES), lambda bi, jj, pt: (pt[bi, jj * n_slots + u], 0, 0))

    grid_spec = pltpu.PrefetchScalarGridSpec(
        num_scalar_prefetch=1,
        grid=(b, n_steps),
        in_specs=[
            pl.BlockSpec((1, dec_t, width), per_b),
            pl.BlockSpec((1,) + k_new.shape[1:], per_b),
            pl.BlockSpec((1,) + v_new.shape[1:], per_b),
            *[slot_spec(u) for u in range(n_slots)],
            *[slot_spec(u) for u in range(n_slots)],
            pl.BlockSpec((1, rows, bias_s.shape[2]),
                         lambda bi, jj, pt: (jnp.where(jj == n_steps - 1, 1, 0), 0, 0)),
            pl.BlockSpec(bias_new.shape, const),
            pl.BlockSpec((1, LANES), const),
            vec, vec, vec, vec,
        ],
        out_specs=pl.BlockSpec((1, dec_t, width), per_b),
        scratch_shapes=[pltpu.VMEM((rows, LANES), BF16),
                        pltpu.VMEM((rows, 1), F32),
                        pltpu.VMEM((rows, 1), F32),
                        pltpu.VMEM((rows, LANES), F32)],
    )
    return pl.pallas_call(
        functools.partial(_df_decode_body, dec_t=dec_t, n_blocks=n_blocks, n_slots=n_slots,
                          lam_init=lam_init),
        grid_spec=grid_spec,
        out_shape=jax.ShapeDtypeStruct((b, dec_t, width), F32),
        compiler_params=pltpu.CompilerParams(
            dimension_semantics=("arbitrary", "arbitrary"),
            vmem_limit_bytes=VMEM_LIMIT),
        name="df_decode",
    )(page_table, q, k_new, v_new, *([k_cache] * n_slots), *([v_cache] * n_slots),
      bias_s, bias_new, g, *lams)


def _router_gates(logits_t, bias_col, n_experts):
    tm = logits_t.shape[1]
    gsz = n_experts // N_GROUPS
    scores = 1.0 / (1.0 + jnp.exp(-logits_t))
    sel = scores + bias_col
    sub = lax.broadcasted_iota(I32, (gsz, tm), 0)
    group_scores = []
    for g in range(N_GROUPS):
        blk = sel[g * gsz:(g + 1) * gsz]
        m1 = jnp.max(blk, axis=0, keepdims=True)
        first = jnp.min(jnp.where(blk == m1, sub, gsz), axis=0, keepdims=True)
        m2 = jnp.max(jnp.where(sub == first, -jnp.inf, blk), axis=0, keepdims=True)
        group_scores.append(m1 + m2)
    gs = jnp.concatenate(group_scores, axis=0)
    gidx = lax.broadcasted_iota(I32, (N_GROUPS, tm), 0)
    grank = jnp.zeros((N_GROUPS, tm), I32)
    for g in range(N_GROUPS):
        row = gs[g:g + 1]
        ahead = (row > gs) | ((row == gs) & (gidx > g))
        grank = grank + ahead.astype(I32)
    gkeep = grank < TOPK_GROUPS
    masked = jnp.concatenate(
        [jnp.where(jnp.broadcast_to(gkeep[g:g + 1], (gsz, tm)), sel[g * gsz:(g + 1) * gsz], -jnp.inf)
         for g in range(N_GROUPS)], axis=0)
    eidx = lax.broadcasted_iota(I32, (n_experts, tm), 0)
    erank = jnp.zeros((n_experts, tm), I32)
    for e in range(n_experts):
        row = masked[e:e + 1]
        ahead = (row > masked) | ((row == masked) & (eidx > e))
        erank = erank + ahead.astype(I32)
    w = jnp.where(erank < TOP_K, scores, 0.0)
    return w / jnp.sum(w, axis=0, keepdims=True) * ROUTED_SCALE


def _post_body(msb_ref, mdf_ref, x_ref, wo_sb_ref, wo_df_ref, g2_ref, rw_hi_ref, rw_lo_ref,
               rb_ref, x1_ref, h2_ref, gates_ref, *, n_experts):
    att = _dot(msb_ref[...].astype(BF16), wo_sb_ref[...]) + _dot(mdf_ref[...].astype(BF16), wo_df_ref[...])
    x1 = x_ref[...] + att
    x1_ref[...] = x1
    ms = jnp.mean(x1 * x1, axis=-1, keepdims=True)
    h2 = (x1 * lax.rsqrt(ms + EPS)) * g2_ref[...]
    h2_hi, h2_lo = _split_bf16(h2)
    h2_ref[...] = h2_hi
    rw_hi = rw_hi_ref[...]
    logits_t = _dot_tb(rw_hi, h2_hi) + _dot_tb(rw_hi, h2_lo) + _dot_tb(rw_lo_ref[...], h2_hi)
    gates_t = _router_gates(logits_t, rb_ref[...], n_experts)
    tm = gates_t.shape[1]
    pad = gates_ref.shape[1] - n_experts
    gates_t = jnp.concatenate([gates_t, jnp.zeros((pad, tm), F32)], axis=0)
    gates_ref[...] = gates_t.T.astype(gates_ref.dtype)


def _post_attention(msb, mdf, x, wo_sb, wo_df, g2, rw_hi, rw_lo, rb, tm):
    n, d = x.shape
    n_experts = rw_hi.shape[0]
    ge = _round_up(n_experts, LANES)
    row = lambda i: (i, 0)
    const = lambda i: (0, 0)
    return pl.pallas_call(
        functools.partial(_post_body, n_experts=n_experts),
        grid=(n // tm,),
        in_specs=[
            pl.BlockSpec((tm, msb.shape[1]), row),
            pl.BlockSpec((tm, mdf.shape[1]), row),
            pl.BlockSpec((tm, d), row),
            pl.BlockSpec(wo_sb.shape, const),
            pl.BlockSpec(wo_df.shape, const),
            pl.BlockSpec((1, d), const),
            pl.BlockSpec(rw_hi.shape, const),
            pl.BlockSpec(rw_lo.shape, const),
            pl.BlockSpec((n_experts, 1), const),
        ],
        out_specs=[pl.BlockSpec((tm, d), row), pl.BlockSpec((tm, d), row), pl.BlockSpec((tm, ge), row)],
        out_shape=[jax.ShapeDtypeStruct((n, d), F32), jax.ShapeDtypeStruct((n, d), BF16),
                   jax.ShapeDtypeStruct((n, ge), F32)],
        compiler_params=pltpu.CompilerParams(
            dimension_semantics=("arbitrary",), vmem_limit_bytes=VMEM_LIMIT),
        name="post_attention",
    )(msb, mdf, x, wo_sb, wo_df, g2, rw_hi, rw_lo, rb)


def _silu(x):
    return x / (1.0 + jnp.exp(-x))


def _moe_body(h2_ref, x1_ref, gates_ref, wg_ref, wu_ref, wd_ref, sg_ref, su_ref, sd_ref,
              o_ref, acc_ref, *, group):
    eg = pl.program_id(1)
    h2 = h2_ref[...]

    @pl.when(eg == 0)
    def _():
        a = _silu(_dot(h2, sg_ref[...])) * _dot(h2, su_ref[...])
        acc_ref[...] = _dot(a.astype(BF16), sd_ref[...])

    f = wg_ref.shape[2]
    gates = gates_ref[...]
    lane = lax.broadcasted_iota(I32, gates.shape, 1)
    acts = []
    for j in range(group):
        gate = jnp.sum(jnp.where(lane == eg * group + j, gates, 0.0), axis=-1, keepdims=True)
        acts.append((_silu(_dot(h2, wg_ref[j])) * _dot(h2, wu_ref[j]) * gate).astype(BF16))
    a = jnp.concatenate(acts, axis=1)
    wd = wd_ref[...].reshape(group * f, wd_ref.shape[2])
    acc_ref[...] += _dot(a, wd)

    @pl.when(eg == pl.num_programs(1) - 1)
    def _():
        o_ref[...] = x1_ref[...] + acc_ref[...]


def _moe(h2, x1, gates, w_gate, w_up, w_down, sg, su, sd, tm, group):
    n, d = x1.shape
    n_experts, _, f = w_gate.shape
    row = lambda i, e: (i, 0)
    const = lambda i, e: (0, 0)
    return pl.pallas_call(
        functools.partial(_moe_body, group=group),
        grid=(n // tm, n_experts // group),
        in_specs=[
            pl.BlockSpec((tm, d), row),
            pl.BlockSpec((tm, d), row),
            pl.BlockSpec((tm, gates.shape[1]), row),
            pl.BlockSpec((group, d, f), lambda i, e: (e, 0, 0)),
            pl.BlockSpec((group, d, f), lambda i, e: (e, 0, 0)),
            pl.BlockSpec((group, f, d), lambda i, e: (e, 0, 0)),
            pl.BlockSpec(sg.shape, const),
            pl.BlockSpec(su.shape, const),
            pl.BlockSpec(sd.shape, const),
        ],
        out_specs=pl.BlockSpec((tm, d), row),
        out_shape=jax.ShapeDtypeStruct((n, d), F32),
        scratch_shapes=[pltpu.VMEM((tm, d), F32)],
        compiler_params=pltpu.CompilerParams(
            dimension_semantics=("arbitrary", "arbitrary"), vmem_limit_bytes=VMEM_LIMIT),
        name="moe",
    )(h2, x1, gates, w_gate, w_up, w_down, sg, su, sd)


def _largest_tile(n, cap):
    t = cap
    while n % t:
        t //= 2
    return t


def kernel(x_prompt, x_sample, cache_k_sb, cache_v_sb, cache_k_diff, cache_v_diff, page_table,
           meta_tokens, rel_bias, norm1_g, w_in, q_norm_g, k_norm_g, lambda_q1, lambda_k1,
           lambda_q2, lambda_k2, sb_out_g, diff_subln_g, w_out, norm2_g, router_w, router_bias,
           w_gate, w_up, w_down, w_shared_gate, w_shared_up, w_shared_down):
    assert w_in.shape[0] == 1, "single-layer step"
    b, seq, d = x_prompt.shape
    dec_b, dec_t, _ = x_sample.shape
    n_meta = meta_tokens.shape[0]
    width = sb_out_g.shape[1]
    n_pool, page = cache_k_sb.shape[1], cache_k_sb.shape[2]
    n_experts = router_w.shape[2]
    t = seq + n_meta
    blk = ATT_BLOCK
    tp = _round_up(t, blk)
    lam_init = 0.8 - 0.6 * math.exp(-0.3 * 0)

    w_in_b = w_in[0].astype(BF16)
    g1 = norm1_g
    reps = width // DH
    qg = jnp.tile(q_norm_g, (1, reps))
    kg = jnp.tile(k_norm_g, (1, reps))
    lane = jnp.arange(width)
    gm = ((lane[:, None] // DH) == (lane[None, :] // DH)).astype(BF16) * (1.0 / DH)
    tri_p = (jnp.arange(blk)[:, None] >= jnp.arange(blk)[None, :]).astype(BF16)
    tri_d = (jnp.arange(page)[:, None] >= jnp.arange(page)[None, :]).astype(BF16)
    lams = (lambda_q1, lambda_k1, lambda_q2, lambda_k2)
    wo_sb = w_out[0, :width].astype(BF16)
    wo_df = w_out[0, width:].astype(BF16)
    rw_t = router_w[0].T
    rw_hi = rw_t.astype(BF16)
    rw_lo = (rw_t - rw_hi.astype(F32)).astype(BF16)
    rb = router_bias[0][:, None]
    sg = w_shared_gate[0].astype(BF16)
    su = w_shared_up[0].astype(BF16)
    sd = w_shared_down[0].astype(BF16)

    n_h_sb = width // DH
    n_h_df = width // LANES
    n_slots = DECODE_PAGES_PER_STEP
    assert blk >= MAX_DISTANCE and page >= MAX_DISTANCE
    assert page_table.shape[1] % n_slots == 0 and dec_t * n_h_df <= page
    bias_p, bias_s, bias_n = _bias_tiles(rel_bias, blk, dec_t, page, n_slots)

    meta = jnp.broadcast_to(meta_tokens[None], (b, n_meta, d))
    xp = jnp.concatenate([meta, x_prompt, jnp.zeros((b, tp - t, d), F32)], axis=1)
    (k_sb_p, v_sb_p, k_df_p, v_df_p,
     qsb_b, ksb_b, vsb_b, qdf_b, kdf_b, vdf_b) = _project(xp, t, g1, w_in_b, qg, kg, gm, blk)
    msb_p = _sb_prompt(qsb_b, ksb_b, vsb_b, tri_p, sb_out_g, blk)
    mdf_p = _df_prompt(rel_bias, qdf_b, kdf_b, vdf_b, bias_p, diff_subln_g, lams, blk, lam_init)

    n_s = dec_b * dec_t
    xs = x_sample.reshape(1, n_s, d)
    (k_sb_s, v_sb_s, k_df_s, v_df_s,
     qsb_s, _, _, qdf_s, _, _) = _project(xs, n_s, g1, w_in_b, qg, kg, gm, n_s)

    def new_page(rows, n_rows, n_pad):
        r = rows.reshape(dec_b, n_rows, -1)
        return jnp.pad(r, ((0, 0), (0, n_pad - n_rows), (0, 0)))

    transposed_pages = lambda c: jnp.transpose(c[0], (0, 2, 3, 1)).reshape(n_pool, width, page)
    interleaved_rows = lambda c: c.reshape(n_pool, page * n_h_df, LANES)
    msb_s = _sb_decode(page_table, qsb_s.reshape(dec_b, dec_t, width),
                       new_page(k_sb_s, dec_t, page), new_page(v_sb_s, dec_t, page),
                       transposed_pages(cache_k_sb), transposed_pages(cache_v_sb), tri_d, sb_out_g)
    mdf_s = _df_decode(page_table, qdf_s.reshape(dec_b, dec_t, width),
                       new_page(k_df_s, dec_t * n_h_df, page * n_h_df),
                       new_page(v_df_s, dec_t * n_h_df, page * n_h_df),
                       interleaved_rows(cache_k_diff), interleaved_rows(cache_v_diff),
                       bias_s, bias_n, diff_subln_g, lams, lam_init, n_slots)

    wg_b, wu_b, wd_b = w_gate[0].astype(BF16), w_up[0].astype(BF16), w_down[0].astype(BF16)
    n_p = b * tp
    x1_p, h2_p, gates_p = _post_attention(
        msb_p.reshape(n_p, width), mdf_p.reshape(n_p, width), xp.reshape(n_p, d),
        wo_sb, wo_df, norm2_g, rw_hi, rw_lo, rb, _largest_tile(n_p, 512))
    y_p = _moe(h2_p, x1_p, gates_p, wg_b, wu_b, wd_b, sg, su, sd,
               _largest_tile(n_p, MOE_TOKEN_TILE), MOE_EXPERT_GROUP)
    y_prompt = y_p.reshape(b, tp, d)[:, n_meta:t]
    x1_s, h2_s, gates_s = _post_attention(
        msb_s.reshape(n_s, width), mdf_s.reshape(n_s, width), x_sample.reshape(n_s, d),
        wo_sb, wo_df, norm2_g, rw_hi, rw_lo, rb, n_s)
    y_s = _moe(h2_s, x1_s, gates_s, wg_b, wu_b, wd_b, sg, su, sd, n_s, MOE_EXPERT_GROUP)
    y_sample = y_s.reshape(dec_b, dec_t, d)

    sb_rows = lambda r, bb, tt: r.reshape(1, bb, tt, n_h_sb, DH)
    df_rows = lambda r, bb, tt: r.reshape(1, bb, tt, n_h_df, LANES)
    return (y_prompt, y_sample,
            sb_rows(k_sb_p, b, t), sb_rows(v_sb_p, b, t), df_rows(k_df_p, b, t), df_rows(v_df_p, b, t),
            sb_rows(k_sb_s, dec_b, dec_t), sb_rows(v_sb_s, dec_b, dec_t),
            df_rows(k_df_s, dec_b, dec_t), df_rows(v_df_s, dec_b, dec_t))
```

```python
import functools
import math

import jax
import jax.numpy as jnp
from jax import lax
from jax.experimental import pallas as pl
from jax.experimental.pallas import tpu as pltpu

F32 = jnp.float32
BF16 = jnp.bfloat16
I32 = jnp.int32

EPS = 1e-6
DH = 64
LANES = 128
MAX_DISTANCE = 128
TOP_K = 8
N_GROUPS = 8
TOPK_GROUPS = 4
ROUTED_SCALE = 2.5
NEG = -1e30
SB_CUTOFF = -104.0
ATT_BLOCK = 256
DF_HEADS_PER_STEP = 4
SB_PAIRS_PER_STEP = 4
DECODE_PAGES_PER_STEP = 8
MOE_TOKEN_TILE = 1024
MOE_EXPERT_GROUP = 4
VMEM_LIMIT = 48 * 1024 * 1024

_TRANS_B = (((1,), (1,)), ((), ()))


def _dot(a, b):
    return jnp.dot(a, b, preferred_element_type=F32)


def _dot_tb(a, b):
    return lax.dot_general(a, b, _TRANS_B, preferred_element_type=F32)


def _split_bf16(x):
    hi = x.astype(BF16)
    lo = (x - hi.astype(F32)).astype(BF16)
    return hi, lo


def _round_up(n, m):
    return (n + m - 1) // m * m


def _proj_body(x_ref, g1_ref, w_ref, qg_ref, kg_ref, gm_ref,
               ksb_o, vsb_o, kdf_o, vdf_o,
               qsb_b, ksb_b, vsb_b, qdf_b, kdf_b, vdf_b, *, width):
    x = x_ref[0]
    ms = jnp.mean(x * x, axis=-1, keepdims=True)
    h = (x * lax.rsqrt(ms + EPS)) * g1_ref[...]
    proj = _dot(h.astype(BF16), w_ref[...])
    w = width
    q_sb, k_sb, v_sb = proj[:, 0:w], proj[:, w:2 * w], proj[:, 2 * w:3 * w]
    q_df, k_df, v_df = proj[:, 3 * w:4 * w], proj[:, 4 * w:5 * w], proj[:, 5 * w:6 * w]

    def map_norm(t, g_ref):
        msq = _dot((t * t).astype(BF16), gm_ref[...])
        return (t * lax.rsqrt(msq + EPS)) * g_ref[...]

    q_df = map_norm(q_df, qg_ref)
    k_df = map_norm(k_df, kg_ref)
    ksb_o[0] = k_sb
    vsb_o[0] = v_sb
    tm = x.shape[0]
    n_df = w // LANES
    for hd in range(n_df):
        rows = pl.ds(hd, tm, stride=n_df)
        kdf_o[0, rows, :] = k_df[:, hd * LANES:(hd + 1) * LANES]
        vdf_o[0, rows, :] = v_df[:, hd * LANES:(hd + 1) * LANES]
    scale = DH ** -0.5
    qsb_b[0] = (q_sb * scale).astype(BF16)
    ksb_b[0] = k_sb.astype(BF16)
    vsb_b[0] = v_sb.astype(BF16)
    qdf_b[0] = (q_df * scale).astype(BF16)
    kdf_b[0] = k_df.astype(BF16)
    vdf_b[0] = v_df.astype(BF16)


def _project(x_pad, t_valid, g1, w_in_b, qg, kg, gm, tm):
    b, tp, d = x_pad.shape
    width = w_in_b.shape[1] // 6
    grid = (b, tp // tm)
    row = lambda bi, i: (bi, i, 0)
    const = lambda bi, i: (0, 0)
    n_df = width // LANES
    f32_out = jax.ShapeDtypeStruct((b, t_valid, width), F32)
    df_out = jax.ShapeDtypeStruct((b, t_valid * n_df, LANES), F32)
    bf_out = jax.ShapeDtypeStruct((b, tp, width), BF16)
    out_spec = pl.BlockSpec((1, tm, width), row)
    df_spec = pl.BlockSpec((1, tm * n_df, LANES), row)
    return pl.pallas_call(
        functools.partial(_proj_body, width=width),
        grid=grid,
        in_specs=[
            pl.BlockSpec((1, tm, d), row),
            pl.BlockSpec((1, d), const),
            pl.BlockSpec(w_in_b.shape, const),
            pl.BlockSpec((1, width), const),
            pl.BlockSpec((1, width), const),
            pl.BlockSpec((width, width), const),
        ],
        out_specs=[out_spec] * 2 + [df_spec] * 2 + [out_spec] * 6,
        out_shape=[f32_out] * 2 + [df_out] * 2 + [bf_out] * 6,
        compiler_params=pltpu.CompilerParams(
            dimension_semantics=("arbitrary", "arbitrary"),
            vmem_limit_bytes=VMEM_LIMIT),
        name="proj",
    )(x_pad, g1, w_in_b, qg, kg, gm)


def _stack_halves(q):
    lane = lax.broadcasted_iota(I32, (1, LANES), 1)
    zero = jnp.zeros_like(q)
    return jnp.concatenate(
        [jnp.where(lane < DH, q, zero), jnp.where(lane >= DH, q, zero)], axis=0)


def _softplus(z):
    return jnp.maximum(z, 0.0) + jnp.log(1.0 + jnp.exp(-jnp.abs(z)))


def _sb_blocks(qqs, ks, vs, tri, state, vis):
    n = range(len(qqs))
    zs = [_dot_tb(qqs[p], ks[p]) for p in n]
    drops = [_softplus(z) for z in zs]
    if vis is not None:
        drops = [jnp.where(vis, d, 0.0) for d in drops]
    csums = [state[p][1] - _dot(drops[p].astype(BF16), tri) for p in n]
    ws = [jnp.exp(zs[p] + csums[p]) for p in n]
    if vis is not None:
        ws = [jnp.where(vis, w, 0.0) for w in ws]
    pvs = [_dot(ws[p].astype(BF16), vs[p]) for p in n]
    return [(state[p][0] + pvs[p], csums[p][:, 0:1]) for p in n]


def _sb_finish(acc, t, g):
    lane = lax.broadcasted_iota(I32, (1, LANES), 1)
    lo_half = lane < DH
    o = jnp.where(lo_half, acc[:t], acc[t:])
    o2 = o * o
    s_lo = jnp.sum(jnp.where(lo_half, o2, 0.0), axis=-1, keepdims=True)
    s_hi = jnp.sum(jnp.where(lo_half, 0.0, o2), axis=-1, keepdims=True)
    ms = jnp.where(lo_half, s_lo, s_hi) * (1.0 / DH)
    return (o * lax.rsqrt(ms + EPS)) * g


def _df_block(qq, k, v, bias, m, l, acc):
    s = _dot_tb(qq, k) + bias
    m_new = jnp.maximum(m, jnp.max(s, axis=-1, keepdims=True))
    alpha = jnp.exp(m - m_new)
    p = jnp.exp(s - m_new)
    l = alpha * l + jnp.sum(p, axis=-1, keepdims=True)
    acc = alpha * acc + _dot(p.astype(BF16), v)
    return m_new, l, acc


def _lambda(lq1, lk1, lq2, lk2, lam_init):
    s1 = jnp.sum(lq1[...] * lk1[...], axis=-1, keepdims=True)
    s2 = jnp.sum(lq2[...] * lk2[...], axis=-1, keepdims=True)
    return jnp.exp(s1) - jnp.exp(s2) + lam_init


def _df_finish(acc, l, t, lam, g, lam_init):
    o = acc[:t] / l[:t] - lam * (acc[t:] / l[t:])
    ms = jnp.mean(o * o, axis=-1, keepdims=True)
    return (o * lax.rsqrt(ms + EPS)) * (g * (1.0 - lam_init))


def _bias_tile(tab_ref, h, rel, n_buckets):
    max_exact = n_buckets // 2
    n = jnp.maximum(rel, 0)
    nf = jnp.maximum(n, 1).astype(F32)
    large = max_exact + (jnp.log(nf / max_exact) / math.log(MAX_DISTANCE / max_exact)
                         * (n_buckets - max_exact)).astype(I32)
    large = jnp.minimum(large, n_buckets - 1)
    bucket = jnp.where(n < max_exact, n, large)
    bias = jnp.zeros(rel.shape, F32)
    for b in range(n_buckets):
        bias = jnp.where(bucket == b, tab_ref[b, h], bias)
    return jnp.where(rel >= 0, bias, NEG)


def _bias_body(tab_ref, bp_ref, bs_ref, bn_ref, *, n_buckets, n_heads, blk, dec_t, page, n_slots):
    r = lax.broadcasted_iota(I32, (blk, 2 * blk), 0)
    c = lax.broadcasted_iota(I32, (blk, 2 * blk), 1)
    rs = lax.broadcasted_iota(I32, (dec_t, page), 0)
    cs = lax.broadcasted_iota(I32, (dec_t, page), 1)
    for h in range(n_heads):
        bp_ref[h] = _bias_tile(tab_ref, h, blk + r - c, n_buckets)
        far = jnp.full((dec_t, page), tab_ref[n_buckets - 1, h], F32)
        last = _bias_tile(tab_ref, h, page + rs - cs, n_buckets)
        new = jnp.where(cs < dec_t, _bias_tile(tab_ref, h, rs - cs, n_buckets), NEG)
        for mp in range(2):
            rows = pl.ds((2 * h + mp) * dec_t, dec_t)
            bn_ref[rows, :] = new
            for u in range(n_slots):
                bs_ref[0, rows, u * page:(u + 1) * page] = far
                bs_ref[1, rows, u * page:(u + 1) * page] = last if u == n_slots - 1 else far


def _bias_tiles(rel_bias, blk, dec_t, page, n_slots):
    n_buckets, n_heads = rel_bias.shape
    rows = 2 * n_heads * dec_t
    return pl.pallas_call(
        functools.partial(_bias_body, n_buckets=n_buckets, n_heads=n_heads, blk=blk,
                          dec_t=dec_t, page=page, n_slots=n_slots),
        in_specs=[pl.BlockSpec(memory_space=pltpu.SMEM)],
        out_shape=[jax.ShapeDtypeStruct((n_heads, blk, 2 * blk), F32),
                   jax.ShapeDtypeStruct((2, rows, n_slots * page), F32),
                   jax.ShapeDtypeStruct((rows, page), F32)],
        name="bias_tiles",
    )(rel_bias)


def _sb_prompt_body(q_ref, k_ref, v_ref, tri_ref, g_ref, o_ref, *, blk):
    i = pl.program_id(2)
    pairs = range(SB_PAIRS_PER_STEP)
    lanes = lambda p: slice(p * LANES, (p + 1) * LANES)
    qq = [_stack_halves(q_ref[0, :, lanes(p)]) for p in pairs]
    tri = tri_ref[...]
    r = lax.broadcasted_iota(I32, (2 * blk, blk), 0)
    r = jnp.where(r >= blk, r - blk, r)
    c = lax.broadcasted_iota(I32, (2 * blk, blk), 1)
    vis = c < r

    def block(j, state, vis):
        keys = pl.ds(pl.multiple_of(j * blk, blk), blk)
        return _sb_blocks(qq, [k_ref[0, keys, lanes(p)] for p in pairs],
                          [v_ref[0, keys, lanes(p)] for p in pairs], tri, state, vis)

    zero = (jnp.zeros((2 * blk, LANES), F32), jnp.zeros((2 * blk, 1), F32))
    state = block(i, [zero for _ in pairs], vis)

    def more(loop):
        jj, state = loop
        top = functools.reduce(jnp.maximum, [jnp.max(carry) for _, carry in state])
        return (jj < i) & (top > SB_CUTOFF)

    def body(loop):
        jj, state = loop
        return jj + 1, block(i - 1 - jj, state, None)

    _, state = lax.while_loop(more, body, (jnp.int32(0), state))
    for p in pairs:
        o_ref[0, :, lanes(p)] = _sb_finish(state[p][0], blk, g_ref[:, lanes(p)]).astype(o_ref.dtype)


def _sb_prompt(q_b, k_b, v_b, tri, g, blk):
    b, tp, width = q_b.shape
    w = SB_PAIRS_PER_STEP * LANES
    grid = (b, width // w, tp // blk)
    return pl.pallas_call(
        functools.partial(_sb_prompt_body, blk=blk),
        grid=grid,
        in_specs=[
            pl.BlockSpec((1, blk, w), lambda bi, p, i: (bi, i, p)),
            pl.BlockSpec((1, tp, w), lambda bi, p, i: (bi, 0, p)),
            pl.BlockSpec((1, tp, w), lambda bi, p, i: (bi, 0, p)),
            pl.BlockSpec((blk, blk), lambda bi, p, i: (0, 0)),
            pl.BlockSpec((1, w), lambda bi, p, i: (0, p)),
        ],
        out_specs=pl.BlockSpec((1, blk, w), lambda bi, p, i: (bi, i, p)),
        out_shape=jax.ShapeDtypeStruct((b, tp, width), BF16),
        compiler_params=pltpu.CompilerParams(
            dimension_semantics=("arbitrary", "arbitrary", "arbitrary"),
            vmem_limit_bytes=VMEM_LIMIT),
        name="sb_prompt",
    )(q_b, k_b, v_b, tri, g)


def _df_prompt_body(tab_ref, q_ref, k_ref, v_ref, bias_ref, g_ref, lq1, lk1, lq2, lk2,
                    o_ref, m_ref, l_ref, acc_ref, *, blk, lam_init, n_buckets):
    hg = pl.program_id(1)
    i = pl.program_id(2)
    heads = range(DF_HEADS_PER_STEP)
    lanes = lambda hh: slice(hh * LANES, (hh + 1) * LANES)
    qq = [_stack_halves(q_ref[0, :, lanes(hh)]) for hh in heads]
    far_bias = [tab_ref[n_buckets - 1, hg * DF_HEADS_PER_STEP + hh] for hh in heads]

    def scores(hh, j, n, near):
        keys = pl.ds(pl.multiple_of(j * blk, blk), n * blk)
        s = _dot_tb(qq[hh], k_ref[0, keys, lanes(hh)])
        if near is None:
            return s + far_bias[hh], keys
        t = bias_ref[hh, :, near]
        return s + jnp.concatenate([t, t], axis=0), keys

    def sweep(visit):
        @pl.when(i == 0)
        def _():
            visit(0, 1, slice(blk, 2 * blk))

        @pl.when(i >= 1)
        def _():
            visit(i - 1, 2, slice(0, 2 * blk))

        n_far = jnp.maximum(i - 1, 0)
        n4 = lax.shift_right_logical(n_far, 2)

        def wide(g, carry):
            visit(g * 4, 4, None)
            return carry

        lax.fori_loop(0, n4, wide, 0)

        @pl.when((n_far & 2) != 0)
        def _():
            visit(n4 * 4, 2, None)

        @pl.when((n_far & 1) != 0)
        def _():
            visit(n_far - 1, 1, None)

    def lane_chunks(x):
        return [x[:, c * LANES:(c + 1) * LANES] for c in range(x.shape[1] // LANES)]

    m_ref[...] = jnp.full(m_ref.shape, NEG, F32)

    def visit_max(j, n, near):
        for hh in heads:
            s, _ = scores(hh, j, n, near)
            m_ref[hh] = functools.reduce(jnp.maximum, lane_chunks(s), m_ref[hh])

    sweep(visit_max)
    m = [jnp.max(m_ref[hh], axis=-1, keepdims=True) for hh in heads]

    l_ref[...] = jnp.zeros_like(l_ref)
    acc_ref[...] = jnp.zeros_like(acc_ref)

    def visit_sum(j, n, near):
        scored = [scores(hh, j, n, near) for hh in heads]
        ps = [jnp.exp(s - m[hh]) for hh, (s, _) in zip(heads, scored)]
        for hh in heads:
            l_ref[hh] = functools.reduce(jnp.add, lane_chunks(ps[hh]), l_ref[hh])
        pv = [_dot(ps[hh].astype(BF16), v_ref[0, scored[hh][1], lanes(hh)]) for hh in heads]
        for hh in heads:
            acc_ref[hh] += pv[hh]

    sweep(visit_sum)
    lam = _lambda(lq1, lk1, lq2, lk2, lam_init)
    for hh in heads:
        l = jnp.sum(l_ref[hh], axis=-1, keepdims=True)
        o_ref[0, :, lanes(hh)] = _df_finish(acc_ref[hh], l, blk, lam, g_ref[...], lam_init).astype(o_ref.dtype)


def _df_prompt(rel_bias, q_b, k_b, v_b, bias_p, g, lams, blk, lam_init):
    b, tp, width = q_b.shape
    hps = DF_HEADS_PER_STEP
    w = hps * LANES
    grid = (b, width // w, tp // blk)
    vec = pl.BlockSpec((1, DH), lambda bi, h, i: (0, 0))
    return pl.pallas_call(
        functools.partial(_df_prompt_body, blk=blk, lam_init=lam_init,
                          n_buckets=rel_bias.shape[0]),
        grid=grid,
        in_specs=[
            pl.BlockSpec(memory_space=pltpu.SMEM),
            pl.BlockSpec((1, blk, w), lambda bi, h, i: (bi, i, h)),
            pl.BlockSpec((1, tp, w), lambda bi, h, i: (bi, 0, h)),
            pl.BlockSpec((1, tp, w), lambda bi, h, i: (bi, 0, h)),
            pl.BlockSpec((hps, blk, 2 * blk), lambda bi, h, i: (h, 0, 0)),
            pl.BlockSpec((1, LANES), lambda bi, h, i: (0, 0)),
            vec, vec, vec, vec,
        ],
        out_specs=pl.BlockSpec((1, blk, w), lambda bi, h, i: (bi, i, h)),
        out_shape=jax.ShapeDtypeStruct((b, tp, width), BF16),
        scratch_shapes=[pltpu.VMEM((hps, 2 * blk, LANES), F32)] * 3,
        compiler_params=pltpu.CompilerParams(
            dimension_semantics=("arbitrary", "arbitrary", "arbitrary"),
            vmem_limit_bytes=VMEM_LIMIT),
        name="df_prompt",
    )(rel_bias, q_b, k_b, v_b, bias_p, g, *lams)


def _stack_decode_queries(q, n_blocks):
    q = q.astype(F32)
    return jnp.concatenate(
        [_stack_halves(q[:, p * LANES:(p + 1) * LANES]) for p in range(n_blocks)], axis=0).astype(BF16)


def _sb_decode_body(pt_ref, q_ref, kn_ref, vn_ref, kc_hbm, vc_hbm, tri_ref, g_ref, o_ref,
                    kbuf, vbuf, sem, *, dec_t, n_blocks, page, n_pages):
    bi = pl.program_id(0)
    rows = 2 * dec_t
    qq = _stack_decode_queries(q_ref[0], n_blocks)
    tri = tri_ref[...]

    def page_copies(slot, jj):
        pg = pt_ref[bi, n_pages - 1 - jj]
        return (pltpu.make_async_copy(kc_hbm.at[pg], kbuf.at[slot], sem.at[0, slot]),
                pltpu.make_async_copy(vc_hbm.at[pg], vbuf.at[slot], sem.at[1, slot]))

    def start(slot, jj):
        for cp in page_copies(slot, jj):
            cp.start()

    def wait(slot, jj):
        for cp in page_copies(slot, jj):
            cp.wait()

    start(0, 0)

    def weights(z, carry, vis):
        drop = _softplus(z)
        if vis is not None:
            drop = jnp.where(vis, drop, 0.0)
        csum = carry - _dot(drop.astype(BF16), tri)
        w = jnp.exp(z + csum)
        if vis is not None:
            w = jnp.where(vis, w, 0.0)
        return w.astype(BF16), csum[:, 0:1]

    r = lax.broadcasted_iota(I32, (dec_t, page), 0)
    c = lax.broadcasted_iota(I32, (dec_t, page), 1)
    vis = jnp.concatenate([c < r] * (2 * n_blocks), axis=0)
    z = jnp.concatenate(
        [_dot_tb(qq[p * rows:(p + 1) * rows], kn_ref[0, :, p * LANES:(p + 1) * LANES].astype(BF16))
         for p in range(n_blocks)], axis=0)
    w, carry = weights(z, jnp.zeros((n_blocks * rows, 1), F32), vis)
    acc = jnp.concatenate(
        [_dot(w[p * rows:(p + 1) * rows], vn_ref[0, :, p * LANES:(p + 1) * LANES].astype(BF16))
         for p in range(n_blocks)], axis=0)

    def more(state):
        jj, _, carry = state
        return (jj < n_pages) & (jnp.max(carry) > SB_CUTOFF)

    def body(state):
        jj, acc, carry = state
        slot = jj & 1
        wait(slot, jj)

        @pl.when(jj + 1 < n_pages)
        def _():
            start(1 - slot, jj + 1)

        kt = kbuf[slot]
        vt = vbuf[slot]
        z = jnp.concatenate(
            [_dot(qq[p * rows:(p + 1) * rows], kt[p * LANES:(p + 1) * LANES, :].astype(BF16))
             for p in range(n_blocks)], axis=0)
        w, carry = weights(z, carry, None)
        acc = acc + jnp.concatenate(
            [_dot_tb(w[p * rows:(p + 1) * rows], vt[p * LANES:(p + 1) * LANES, :].astype(BF16))
             for p in range(n_blocks)], axis=0)
        return jj + 1, acc, carry

    jj, acc, _ = lax.while_loop(more, body, (jnp.int32(0), acc, carry))

    @pl.when(jj < n_pages)
    def _():
        wait(jj & 1, jj)

    for p in range(n_blocks):
        o_ref[0, :, p * LANES:(p + 1) * LANES] = _sb_finish(
            acc[p * rows:(p + 1) * rows, :], dec_t, g_ref[:, p * LANES:(p + 1) * LANES])


def _sb_decode(page_table, q, k_new, v_new, kt_cache, vt_cache, tri, g):
    b, dec_t, width = q.shape
    n_pages = page_table.shape[1]
    page = kt_cache.shape[2]
    n_blocks = width // LANES
    per_b = lambda bi, pt: (bi, 0, 0)
    const = lambda bi, pt: (0, 0)
    grid_spec = pltpu.PrefetchScalarGridSpec(
        num_scalar_prefetch=1,
        grid=(b,),
        in_specs=[
            pl.BlockSpec((1, dec_t, width), per_b),
            pl.BlockSpec((1, page, width), per_b),
            pl.BlockSpec((1, page, width), per_b),
            pl.BlockSpec(memory_space=pl.ANY),
            pl.BlockSpec(memory_space=pl.ANY),
            pl.BlockSpec((page, page), const),
            pl.BlockSpec((1, width), const),
        ],
        out_specs=pl.BlockSpec((1, dec_t, width), per_b),
        scratch_shapes=[pltpu.VMEM((2, width, page), F32),
                        pltpu.VMEM((2, width, page), F32),
                        pltpu.SemaphoreType.DMA((2, 2))],
    )
    return pl.pallas_call(
        functools.partial(_sb_decode_body, dec_t=dec_t, n_blocks=n_blocks, page=page, n_pages=n_pages),
        grid_spec=grid_spec,
        out_shape=jax.ShapeDtypeStruct((b, dec_t, width), F32),
        compiler_params=pltpu.CompilerParams(
            dimension_semantics=("arbitrary",), vmem_limit_bytes=VMEM_LIMIT),
        name="sb_decode",
    )(page_table, q, k_new, v_new, kt_cache, vt_cache, tri, g)


def _df_decode_body(pt_ref, q_ref, kn_ref, vn_ref, *rest, dec_t, n_blocks, n_slots, lam_init):
    kc_refs, vc_refs = rest[:n_slots], rest[n_slots:2 * n_slots]
    (bias_ref, bias_new_ref, g_ref, lq1, lk1, lq2, lk2, o_ref,
     qq_ref, m_ref, l_ref, acc_ref) = rest[2 * n_slots:]
    jj = pl.program_id(1)
    rows = 2 * dec_t
    page = bias_new_ref.shape[1]

    def update(k_refs, v_refs, bias):
        own = [pl.ds(h, page, stride=n_blocks) for h in range(n_blocks)]
        gather = lambda refs, h: jnp.concatenate([r[0, own[h], :].astype(BF16) for r in refs], axis=0)
        qq = qq_ref[...]
        s = jnp.concatenate(
            [_dot_tb(qq[h * rows:(h + 1) * rows], gather(k_refs, h)) for h in range(n_blocks)],
            axis=0) + bias[...]
        m_old = m_ref[...]
        m_new = jnp.maximum(m_old, jnp.max(s, axis=-1, keepdims=True))
        alpha = jnp.exp(m_old - m_new)
        p = jnp.exp(s - m_new)
        l_ref[...] = alpha * l_ref[...] + jnp.sum(p, axis=-1, keepdims=True)
        m_ref[...] = m_new
        p = p.astype(BF16)
        pv = jnp.concatenate(
            [_dot(p[h * rows:(h + 1) * rows], gather(v_refs, h)) for h in range(n_blocks)], axis=0)
        acc_ref[...] = alpha * acc_ref[...] + pv

    @pl.when(jj == 0)
    def _():
        qq_ref[...] = _stack_decode_queries(q_ref[0], n_blocks)
        m_ref[...] = jnp.full(m_ref.shape, NEG, F32)
        l_ref[...] = jnp.zeros_like(l_ref)
        acc_ref[...] = jnp.zeros_like(acc_ref)
        update([kn_ref], [vn_ref], bias_new_ref)

    update(kc_refs, vc_refs, bias_ref.at[0])

    @pl.when(jj == pl.num_programs(1) - 1)
    def _():
        lam = _lambda(lq1, lk1, lq2, lk2, lam_init)
        for p in range(n_blocks):
            sl = slice(p * rows, (p + 1) * rows)
            o_ref[0, :, p * LANES:(p + 1) * LANES] = _df_finish(
                acc_ref[sl, :], l_ref[sl, :], dec_t, lam, g_ref[...], lam_init)


def _df_decode(page_table, q, k_new, v_new, k_cache, v_cache, bias_s, bias_new, g, lams, lam_init,
               n_slots):
    b, dec_t, width = q.shape
    n_pages = page_table.shape[1]
    page_rows = k_cache.shape[1]
    n_blocks = width // LANES
    rows = n_blocks * 2 * dec_t
    n_steps = n_pages // n_slots
    per_b = lambda bi, jj, pt: (bi, 0, 0)
    const = lambda bi, jj, pt: (0, 0)
    vec = pl.BlockSpec((1, DH), const)

    def slot_spec(u):
        return pl.BlockSpec((1, page_rows, LANES), lambda bi, jj, pt: (pt[bi, jj * n_slots + u], 0, 0))

    grid_spec = pltpu.PrefetchScalarGridSpec(
        num_scalar_prefetch=1,
        grid=(b, n_steps),
        in_specs=[
            pl.BlockSpec((1, dec_t, width), per_b),
            pl.BlockSpec((1,) + k_new.shape[1:], per_b),
            pl.BlockSpec((1,) + v_new.shape[1:], per_b),
            *[slot_spec(u) for u in range(n_slots)],
            *[slot_spec(u) for u in range(n_slots)],
            pl.BlockSpec((1, rows, bias_s.shape[2]),
                         lambda bi, jj, pt: (jnp.where(jj == n_steps - 1, 1, 0), 0, 0)),
            pl.BlockSpec(bias_new.shape, const),
            pl.BlockSpec((1, LANES), const),
            vec, vec, vec, vec,
        ],
        out_specs=pl.BlockSpec((1, dec_t, width), per_b),
        scratch_shapes=[pltpu.VMEM((rows, LANES), BF16),
                        pltpu.VMEM((rows, 1), F32),
                        pltpu.VMEM((rows, 1), F32),
                        pltpu.VMEM((rows, LANES), F32)],
    )
    return pl.pallas_call(
        functools.partial(_df_decode_body, dec_t=dec_t, n_blocks=n_blocks, n_slots=n_slots,
                          lam_init=lam_init),
        grid_spec=grid_spec,
        out_shape=jax.ShapeDtypeStruct((b, dec_t, width), F32),
        compiler_params=pltpu.CompilerParams(
            dimension_semantics=("arbitrary", "arbitrary"),
            vmem_limit_bytes=VMEM_LIMIT),
        name="df_decode",
    )(page_table, q, k_new, v_new, *([k_cache] * n_slots), *([v_cache] * n_slots),
      bias_s, bias_new, g, *lams)


def _router_gates(logits_t, bias_col, n_experts):
    tm = logits_t.shape[1]
    gsz = n_experts // N_GROUPS
    scores = 1.0 / (1.0 + jnp.exp(-logits_t))
    sel = scores + bias_col
    sub = lax.broadcasted_iota(I32, (gsz, tm), 0)
    group_scores = []
    for g in range(N_GROUPS):
        blk = sel[g * gsz:(g + 1) * gsz]
        m1 = jnp.max(blk, axis=0, keepdims=True)
        first = jnp.min(jnp.where(blk == m1, sub, gsz), axis=0, keepdims=True)
        m2 = jnp.max(jnp.where(sub == first, -jnp.inf, blk), axis=0, keepdims=True)
        group_scores.append(m1 + m2)
    gs = jnp.concatenate(group_scores, axis=0)
    gidx = lax.broadcasted_iota(I32, (N_GROUPS, tm), 0)
    grank = jnp.zeros((N_GROUPS, tm), I32)
    for g in range(N_GROUPS):
        row = gs[g:g + 1]
        ahead = (row > gs) | ((row == gs) & (gidx > g))
        grank = grank + ahead.astype(I32)
    gkeep = grank < TOPK_GROUPS
    masked = jnp.concatenate(
        [jnp.where(jnp.broadcast_to(gkeep[g:g + 1], (gsz, tm)), sel[g * gsz:(g + 1) * gsz], -jnp.inf)
         for g in range(N_GROUPS)], axis=0)
    eidx = lax.broadcasted_iota(I32, (n_experts, tm), 0)
    erank = jnp.zeros((n_experts, tm), I32)
    for e in range(n_experts):
        row = masked[e:e + 1]
        ahead = (row > masked) | ((row == masked) & (eidx > e))
        erank = erank + ahead.astype(I32)
    w = jnp.where(erank < TOP_K, scores, 0.0)
    return w / jnp.sum(w, axis=0, keepdims=True) * ROUTED_SCALE


def _post_body(msb_ref, mdf_ref, x_ref, wo_sb_ref, wo_df_ref, g2_ref, rw_hi_ref, rw_lo_ref,
               rb_ref, x1_ref, h2_ref, gates_ref, *, n_experts):
    att = _dot(msb_ref[...].astype(BF16), wo_sb_ref[...]) + _dot(mdf_ref[...].astype(BF16), wo_df_ref[...])
    x1 = x_ref[...] + att
    x1_ref[...] = x1
    ms = jnp.mean(x1 * x1, axis=-1, keepdims=True)
    h2 = (x1 * lax.rsqrt(ms + EPS)) * g2_ref[...]
    h2_hi, h2_lo = _split_bf16(h2)
    h2_ref[...] = h2_hi
    rw_hi = rw_hi_ref[...]
    logits_t = _dot_tb(rw_hi, h2_hi) + _dot_tb(rw_hi, h2_lo) + _dot_tb(rw_lo_ref[...], h2_hi)
    gates_t = _router_gates(logits_t, rb_ref[...], n_experts)
    tm = gates_t.shape[1]
    pad = gates_ref.shape[1] - n_experts
    gates_t = jnp.concatenate([gates_t, jnp.zeros((pad, tm), F32)], axis=0)
    gates_ref[...] = gates_t.T.astype(gates_ref.dtype)


def _post_attention(msb, mdf, x, wo_sb, wo_df, g2, rw_hi, rw_lo, rb, tm):
    n, d = x.shape
    n_experts = rw_hi.shape[0]
    ge = _round_up(n_experts, LANES)
    row = lambda i: (i, 0)
    const = lambda i: (0, 0)
    return pl.pallas_call(
        functools.partial(_post_body, n_experts=n_experts),
        grid=(n // tm,),
        in_specs=[
            pl.BlockSpec((tm, msb.shape[1]), row),
            pl.BlockSpec((tm, mdf.shape[1]), row),
            pl.BlockSpec((tm, d), row),
            pl.BlockSpec(wo_sb.shape, const),
            pl.BlockSpec(wo_df.shape, const),
            pl.BlockSpec((1, d), const),
            pl.BlockSpec(rw_hi.shape, const),
            pl.BlockSpec(rw_lo.shape, const),
            pl.BlockSpec((n_experts, 1), const),
        ],
        out_specs=[pl.BlockSpec((tm, d), row), pl.BlockSpec((tm, d), row), pl.BlockSpec((tm, ge), row)],
        out_shape=[jax.ShapeDtypeStruct((n, d), F32), jax.ShapeDtypeStruct((n, d), BF16),
                   jax.ShapeDtypeStruct((n, ge), F32)],
        compiler_params=pltpu.CompilerParams(
            dimension_semantics=("arbitrary",), vmem_limit_bytes=VMEM_LIMIT),
        name="post_attention",
    )(msb, mdf, x, wo_sb, wo_df, g2, rw_hi, rw_lo, rb)


def _silu(x):
    return x / (1.0 + jnp.exp(-x))


def _moe_body(h2_ref, x1_ref, gates_ref, wg_ref, wu_ref, wd_ref, sg_ref, su_ref, sd_ref,
              o_ref, acc_ref, *, group):
    eg = pl.program_id(1)
    h2 = h2_ref[...]

    @pl.when(eg == 0)
    def _():
        a = _silu(_dot(h2, sg_ref[...])) * _dot(h2, su_ref[...])
        acc_ref[...] = _dot(a.astype(BF16), sd_ref[...])

    f = wg_ref.shape[2]
    gates = gates_ref[...]
    lane = lax.broadcasted_iota(I32, gates.shape, 1)
    acts = []
    for j in range(group):
        gate = jnp.sum(jnp.where(lane == eg * group + j, gates, 0.0), axis=-1, keepdims=True)
        acts.append((_silu(_dot(h2, wg_ref[j])) * _dot(h2, wu_ref[j]) * gate).astype(BF16))
    a = jnp.concatenate(acts, axis=1)
    wd = wd_ref[...].reshape(group * f, wd_ref.shape[2])
    acc_ref[...] += _dot(a, wd)

    @pl.when(eg == pl.num_programs(1) - 1)
    def _():
        o_ref[...] = x1_ref[...] + acc_ref[...]


def _moe(h2, x1, gates, w_gate, w_up, w_down, sg, su, sd, tm, group):
    n, d = x1.shape
    n_experts, _, f = w_gate.shape
    row = lambda i, e: (i, 0)
    const = lambda i, e: (0, 0)
    return pl.pallas_call(
        functools.partial(_moe_body, group=group),
        grid=(n // tm, n_experts // group),
        in_specs=[
            pl.BlockSpec((tm, d), row),
            pl.BlockSpec((tm, d), row),
            pl.BlockSpec((tm, gates.shape[1]), row),
            pl.BlockSpec((group, d, f), lambda i, e: (e, 0, 0)),
            pl.BlockSpec((group, d, f), lambda i, e: (e, 0, 0)),
            pl.BlockSpec((group, f, d), lambda i, e: (e, 0, 0)),
            pl.BlockSpec(sg.shape, const),
            pl.BlockSpec(su.shape, const),
            pl.BlockSpec(sd.shape, const),
        ],
        out_specs=pl.BlockSpec((tm, d), row),
        out_shape=jax.ShapeDtypeStruct((n, d), F32),
        scratch_shapes=[pltpu.VMEM((tm, d), F32)],
        compiler_params=pltpu.CompilerParams(
            dimension_semantics=("arbitrary", "arbitrary"), vmem_limit_bytes=VMEM_LIMIT),
        name="moe",
    )(h2, x1, gates, w_gate, w_up, w_down, sg, su, sd)


def _largest_tile(n, cap):
    t = cap
    while n % t:
        t //= 2
    return t


def kernel(x_prompt, x_sample, cache_k_sb, cache_v_sb, cache_k_diff, cache_v_diff, page_table,
           meta_tokens, rel_bias, norm1_g, w_in, q_norm_g, k_norm_g, lambda_q1, lambda_k1,
           lambda_q2, lambda_k2, sb_out_g, diff_subln_g, w_out, norm2_g, router_w, router_bias,
           w_gate, w_up, w_down, w_shared_gate, w_shared_up, w_shared_down):
    assert w_in.shape[0] == 1, "single-layer step"
    b, seq, d = x_prompt.shape
    dec_b, dec_t, _ = x_sample.shape
    n_meta = meta_tokens.shape[0]
    width = sb_out_g.shape[1]
    n_pool, page = cache_k_sb.shape[1], cache_k_sb.shape[2]
    n_experts = router_w.shape[2]
    t = seq + n_meta
    blk = ATT_BLOCK
    tp = _round_up(t, blk)
    lam_init = 0.8 - 0.6 * math.exp(-0.3 * 0)

    w_in_b = w_in[0].astype(BF16)
    g1 = norm1_g
    reps = width // DH
    qg = jnp.tile(q_norm_g, (1, reps))
    kg = jnp.tile(k_norm_g, (1, reps))
    lane = jnp.arange(width)
    gm = ((lane[:, None] // DH) == (lane[None, :] // DH)).astype(BF16) * (1.0 / DH)
    tri_p = (jnp.arange(blk)[:, None] >= jnp.arange(blk)[None, :]).astype(BF16)
    tri_d = (jnp.arange(page)[:, None] >= jnp.arange(page)[None, :]).astype(BF16)
    lams = (lambda_q1, lambda_k1, lambda_q2, lambda_k2)
    wo_sb = w_out[0, :width].astype(BF16)
    wo_df = w_out[0, width:].astype(BF16)
    rw_t = router_w[0].T
    rw_hi = rw_t.astype(BF16)
    rw_lo = (rw_t - rw_hi.astype(F32)).astype(BF16)
    rb = router_bias[0][:, None]
    sg = w_shared_gate[0].astype(BF16)
    su = w_shared_up[0].astype(BF16)
    sd = w_shared_down[0].astype(BF16)

    n_h_sb = width // DH
    n_h_df = width // LANES
    n_slots = DECODE_PAGES_PER_STEP
    assert blk >= MAX_DISTANCE and page >= MAX_DISTANCE
    assert page_table.shape[1] % n_slots == 0 and dec_t * n_h_df <= page
    bias_p, bias_s, bias_n = _bias_tiles(rel_bias, blk, dec_t, page, n_slots)

    meta = jnp.broadcast_to(meta_tokens[None], (b, n_meta, d))
    xp = jnp.concatenate([meta, x_prompt, jnp.zeros((b, tp - t, d), F32)], axis=1)
    (k_sb_p, v_sb_p, k_df_p, v_df_p,
     qsb_b, ksb_b, vsb_b, qdf_b, kdf_b, vdf_b) = _project(xp, t, g1, w_in_b, qg, kg, gm, blk)
    msb_p = _sb_prompt(qsb_b, ksb_b, vsb_b, tri_p, sb_out_g, blk)
    mdf_p = _df_prompt(rel_bias, qdf_b, kdf_b, vdf_b, bias_p, diff_subln_g, lams, blk, lam_init)

    n_s = dec_b * dec_t
    xs = x_sample.reshape(1, n_s, d)
    (k_sb_s, v_sb_s, k_df_s, v_df_s,
     qsb_s, _, _, qdf_s, _, _) = _project(xs, n_s, g1, w_in_b, qg, kg, gm, n_s)

    def new_page(rows, n_rows, n_pad):
        r = rows.reshape(dec_b, n_rows, -1)
        return jnp.pad(r, ((0, 0), (0, n_pad - n_rows), (0, 0)))

    transposed_pages = lambda c: jnp.transpose(c[0], (0, 2, 3, 1)).reshape(n_pool, width, page)
    interleaved_rows = lambda c: c.reshape(n_pool, page * n_h_df, LANES)
    msb_s = _sb_decode(page_table, qsb_s.reshape(dec_b, dec_t, width),
                       new_page(k_sb_s, dec_t, page), new_page(v_sb_s, dec_t, page),
                       transposed_pages(cache_k_sb), transposed_pages(cache_v_sb), tri_d, sb_out_g)
    mdf_s = _df_decode(page_table, qdf_s.reshape(dec_b, dec_t, width),
                       new_page(k_df_s, dec_t * n_h_df, page * n_h_df),
                       new_page(v_df_s, dec_t * n_h_df, page * n_h_df),
                       interleaved_rows(cache_k_diff), interleaved_rows(cache_v_diff),
                       bias_s, bias_n, diff_subln_g, lams, lam_init, n_slots)

    wg_b, wu_b, wd_b = w_gate[0].astype(BF16), w_up[0].astype(BF16), w_down[0].astype(BF16)
    n_p = b * tp
    x1_p, h2_p, gates_p = _post_attention(
        msb_p.reshape(n_p, width), mdf_p.reshape(n_p, width), xp.reshape(n_p, d),
        wo_sb, wo_df, norm2_g, rw_hi, rw_lo, rb, _largest_tile(n_p, 512))
    y_p = _moe(h2_p, x1_p, gates_p, wg_b, wu_b, wd_b, sg, su, sd,
               _largest_tile(n_p, MOE_TOKEN_TILE), MOE_EXPERT_GROUP)
    y_prompt = y_p.reshape(b, tp, d)[:, n_meta:t]
    x1_s, h2_s, gates_s = _post_attention(
        msb_s.reshape(n_s, width), mdf_s.reshape(n_s, width), x_sample.reshape(n_s, d),
        wo_sb, wo_df, norm2_g, rw_hi, rw_lo, rb, n_s)
    y_s = _moe(h2_s, x1_s, gates_s, wg_b, wu_b, wd_b, sg, su, sd, n_s, MOE_EXPERT_GROUP)
    y_sample = y_s.reshape(dec_b, dec_t, d)

    sb_rows = lambda r, bb, tt: r.reshape(1, bb, tt, n_h_sb, DH)
    df_rows = lambda r, bb, tt: r.reshape(1, bb, tt, n_h_df, LANES)
    return (y_prompt, y_sample,
            sb_rows(k_sb_p, b, t), sb_rows(v_sb_p, b, t), df_rows(k_df_p, b, t), df_rows(v_df_p, b, t),
            sb_rows(k_sb_s, dec_b, dec_t), sb_rows(v_sb_s, dec_b, dec_t),
            df_rows(k_df_s, dec_b, dec_t), df_rows(v_df_s, dec_b, dec_t))
```

```python
import functools
import math

import jax
import jax.numpy as jnp
from jax import lax
from jax.experimental import pallas as pl
from jax.experimental.pallas import tpu as pltpu

F32 = jnp.float32
BF16 = jnp.bfloat16
I32 = jnp.int32

EPS = 1e-6
DH = 64
LANES = 128
MAX_DISTANCE = 128
TOP_K = 8
N_GROUPS = 8
TOPK_GROUPS = 4
ROUTED_SCALE = 2.5
NEG = -1e30
SB_CUTOFF = -104.0
ATT_BLOCK = 256
DF_HEADS_PER_STEP = 4
SB_PAIRS_PER_STEP = 4
DECODE_PAGES_PER_STEP = 8
MOE_TOKEN_TILE = 1024
MOE_EXPERT_GROUP = 4
VMEM_LIMIT = 48 * 1024 * 1024

_TRANS_B = (((1,), (1,)), ((), ()))


def _dot(a, b):
    return jnp.dot(a, b, preferred_element_type=F32)


def _dot_tb(a, b):
    return lax.dot_general(a, b, _TRANS_B, preferred_element_type=F32)


def _split_bf16(x):
    hi = x.astype(BF16)
    lo = (x - hi.astype(F32)).astype(BF16)
    return hi, lo


def _round_up(n, m):
    return (n + m - 1) // m * m


def _proj_rows(x, g1_ref, w_ref, qg_ref, kg_ref, gm_ref, ksb_o, vsb_o, kdf_o, vdf_o,
               qsb_b, ksb_b, vsb_b, qdf_b, kdf_b, vdf_b, width):
    ms = jnp.mean(x * x, axis=-1, keepdims=True)
    h = (x * lax.rsqrt(ms + EPS)) * g1_ref[...]
    proj = _dot(h.astype(BF16), w_ref[...])
    w = width
    q_sb, k_sb, v_sb = proj[:, 0:w], proj[:, w:2 * w], proj[:, 2 * w:3 * w]
    q_df, k_df, v_df = proj[:, 3 * w:4 * w], proj[:, 4 * w:5 * w], proj[:, 5 * w:6 * w]

    def map_norm(t, g_ref):
        msq = _dot((t * t).astype(BF16), gm_ref[...])
        return (t * lax.rsqrt(msq + EPS)) * g_ref[...]

    q_df = map_norm(q_df, qg_ref)
    k_df = map_norm(k_df, kg_ref)
    ksb_o[...] = k_sb
    vsb_o[...] = v_sb
    tm = x.shape[0]
    n_df = w // LANES
    for hd in range(n_df):
        rows = pl.ds(hd, tm, stride=n_df)
        kdf_o[rows, :] = k_df[:, hd * LANES:(hd + 1) * LANES]
        vdf_o[rows, :] = v_df[:, hd * LANES:(hd + 1) * LANES]
    scale = DH ** -0.5
    qsb_b[0] = (q_sb * scale).astype(BF16)
    ksb_b[0] = k_sb.astype(BF16)
    vsb_b[0] = v_sb.astype(BF16)
    qdf_b[0] = (q_df * scale).astype(BF16)
    kdf_b[0] = k_df.astype(BF16)
    vdf_b[0] = v_df.astype(BF16)


def _proj_body(x_ref, g1_ref, w_ref, qg_ref, kg_ref, gm_ref,
               ksb_o, vsb_o, kdf_o, vdf_o, *bf_outs, width):
    _proj_rows(x_ref[0], g1_ref, w_ref, qg_ref, kg_ref, gm_ref,
               ksb_o.at[0], vsb_o.at[0], kdf_o.at[0], vdf_o.at[0], *bf_outs, width)


def _proj_prompt_body(head_ref, x_ref, g1_ref, w_ref, qg_ref, kg_ref, gm_ref,
                      ksb_hbm, vsb_hbm, kdf_hbm, vdf_hbm, *rest, width, lead):
    bf_outs, (st_sb, st_df, sem) = rest[:6], rest[6:]
    b, i = pl.program_id(0), pl.program_id(1)
    n_i = pl.num_programs(1)
    step = b * n_i + i
    slot = step & 1
    tm = x_ref.shape[1]
    n_df = width // LANES

    def copies(slot, bb, ii, head):
        if head:
            src, dst, n = tm - lead, 0, lead
        else:
            src, dst, n = 0, lead + (ii - 1) * tm, tm
        out = []
        for kv, hbm in enumerate((ksb_hbm, vsb_hbm)):
            out.append(pltpu.make_async_copy(st_sb.at[slot, kv, pl.ds(src, n), :],
                                             hbm.at[bb, pl.ds(dst, n), :], sem.at[slot]))
        for kv, hbm in enumerate((kdf_hbm, vdf_hbm)):
            out.append(pltpu.make_async_copy(st_df.at[slot, kv, pl.ds(src * n_df, n * n_df), :],
                                             hbm.at[bb, pl.ds(dst * n_df, n * n_df), :], sem.at[slot]))
        return out

    def wait_step(slot, was_head):
        @pl.when(was_head)
        def _():
            for cp in copies(slot, 0, 1, True):
                cp.wait()

        @pl.when(jnp.logical_not(was_head))
        def _():
            for cp in copies(slot, 0, 1, False):
                cp.wait()

    @pl.when(step >= 2)
    def _():
        wait_step(slot, i == 2)

    x = jnp.where(i == 0, head_ref[...], x_ref[0])
    _proj_rows(x, g1_ref, w_ref, qg_ref, kg_ref, gm_ref,
               st_sb.at[slot, 0], st_sb.at[slot, 1], st_df.at[slot, 0], st_df.at[slot, 1],
               *bf_outs, width)

    @pl.when(i == 0)
    def _():
        for cp in copies(slot, b, i, True):
            cp.start()

    @pl.when(i > 0)
    def _():
        for cp in copies(slot, b, i, False):
            cp.start()

    @pl.when(step == pl.num_programs(0) * n_i - 1)
    def _():
        wait_step(1 - slot, i == 1)
        wait_step(slot, i == 0)


def _project(x_pad, t_valid, g1, w_in_b, qg, kg, gm, tm):
    b, tp, d = x_pad.shape
    width = w_in_b.shape[1] // 6
    grid = (b, tp // tm)
    row = lambda bi, i: (bi, i, 0)
    const = lambda bi, i: (0, 0)
    n_df = width // LANES
    f32_out = jax.ShapeDtypeStruct((b, t_valid, width), F32)
    df_out = jax.ShapeDtypeStruct((b, t_valid * n_df, LANES), F32)
    bf_out = jax.ShapeDtypeStruct((b, tp, width), BF16)
    out_spec = pl.BlockSpec((1, tm, width), row)
    df_spec = pl.BlockSpec((1, tm * n_df, LANES), row)
    return pl.pallas_call(
        functools.partial(_proj_body, width=width),
        grid=grid,
        in_specs=[
            pl.BlockSpec((1, tm, d), row),
            pl.BlockSpec((1, d), const),
            pl.BlockSpec(w_in_b.shape, const),
            pl.BlockSpec((1, width), const),
            pl.BlockSpec((1, width), const),
            pl.BlockSpec((width, width), const),
        ],
        out_specs=[out_spec] * 2 + [df_spec] * 2 + [out_spec] * 6,
        out_shape=[f32_out] * 2 + [df_out] * 2 + [bf_out] * 6,
        compiler_params=pltpu.CompilerParams(
            dimension_semantics=("arbitrary", "arbitrary"),
            vmem_limit_bytes=VMEM_LIMIT),
        name="proj",
    )(x_pad, g1, w_in_b, qg, kg, gm)


def _project_prompt(head, x, lead, g1, w_in_b, qg, kg, gm):
    b, s, d = x.shape
    tm = head.shape[0]
    width = w_in_b.shape[1] // 6
    n_df = width // LANES
    n_i = 1 + s // tm
    assert s % tm == 0 and n_i >= 3 and lead % 8 == 0 and 0 < lead <= tm
    row = lambda bi, i: (bi, i, 0)
    const = lambda bi, i: (0, 0)
    anywhere = pl.BlockSpec(memory_space=pl.ANY)
    bf_out = jax.ShapeDtypeStruct((b, tm + s, width), BF16)
    return pl.pallas_call(
        functools.partial(_proj_prompt_body, width=width, lead=lead),
        grid=(b, n_i),
        in_specs=[
            pl.BlockSpec((tm, d), const),
            pl.BlockSpec((1, tm, d), lambda bi, i: (bi, jnp.maximum(i - 1, 0), 0)),
            pl.BlockSpec((1, d), const),
            pl.BlockSpec(w_in_b.shape, const),
            pl.BlockSpec((1, width), const),
            pl.BlockSpec((1, width), const),
            pl.BlockSpec((width, width), const),
        ],
        out_specs=[anywhere] * 4 + [pl.BlockSpec((1, tm, width), row)] * 6,
        out_shape=[jax.ShapeDtypeStruct((b, lead + s, width), F32)] * 2
        + [jax.ShapeDtypeStruct((b, (lead + s) * n_df, LANES), F32)] * 2 + [bf_out] * 6,
        scratch_shapes=[pltpu.VMEM((2, 2, tm, width), F32),
                        pltpu.VMEM((2, 2, tm * n_df, LANES), F32),
                        pltpu.SemaphoreType.DMA((2,))],
        compiler_params=pltpu.CompilerParams(
            dimension_semantics=("arbitrary", "arbitrary"),
            vmem_limit_bytes=VMEM_LIMIT),
        name="proj_prompt",
    )(head, x, g1, w_in_b, qg, kg, gm)


def _stack_halves(q):
    lane = lax.broadcasted_iota(I32, (1, LANES), 1)
    zero = jnp.zeros_like(q)
    return jnp.concatenate(
        [jnp.where(lane < DH, q, zero), jnp.where(lane >= DH, q, zero)], axis=0)


def _softplus(z):
    return jnp.maximum(z, 0.0) + jnp.log(1.0 + jnp.exp(-jnp.abs(z)))


def _sb_blocks(qqs, ks, vs, tri, state, vis):
    n = range(len(qqs))
    zs = [_dot_tb(qqs[p], ks[p]) for p in n]
    drops = [_softplus(z) for z in zs]
    if vis is not None:
        drops = [jnp.where(vis, d, 0.0) for d in drops]
    csums = [state[p][1] - _dot(drops[p].astype(BF16), tri) for p in n]
    ws = [jnp.exp(zs[p] + csums[p]) for p in n]
    if vis is not None:
        ws = [jnp.where(vis, w, 0.0) for w in ws]
    pvs = [_dot(ws[p].astype(BF16), vs[p]) for p in n]
    return [(state[p][0] + pvs[p], csums[p][:, 0:1]) for p in n]


def _sb_finish(acc, t, g):
    lane = lax.broadcasted_iota(I32, (1, LANES), 1)
    lo_half = lane < DH
    o = jnp.where(lo_half, acc[:t], acc[t:])
    o2 = o * o
    s_lo = jnp.sum(jnp.where(lo_half, o2, 0.0), axis=-1, keepdims=True)
    s_hi = jnp.sum(jnp.where(lo_half, 0.0, o2), axis=-1, keepdims=True)
    ms = jnp.where(lo_half, s_lo, s_hi) * (1.0 / DH)
    return (o * lax.rsqrt(ms + EPS)) * g


def _df_block(qq, k, v, bias, m, l, acc):
    s = _dot_tb(qq, k) + bias
    m_new = jnp.maximum(m, jnp.max(s, axis=-1, keepdims=True))
    alpha = jnp.exp(m - m_new)
    p = jnp.exp(s - m_new)
    l = alpha * l + jnp.sum(p, axis=-1, keepdims=True)
    acc = alpha * acc + _dot(p.astype(BF16), v)
    return m_new, l, acc


def _lambda(lq1, lk1, lq2, lk2, lam_init):
    s1 = jnp.sum(lq1[...] * lk1[...], axis=-1, keepdims=True)
    s2 = jnp.sum(lq2[...] * lk2[...], axis=-1, keepdims=True)
    return jnp.exp(s1) - jnp.exp(s2) + lam_init


def _df_finish(acc, l, t, lam, g, lam_init):
    o = acc[:t] / l[:t] - lam * (acc[t:] / l[t:])
    ms = jnp.mean(o * o, axis=-1, keepdims=True)
    return (o * lax.rsqrt(ms + EPS)) * (g * (1.0 - lam_init))


def _bias_tile(tab_ref, h, rel, n_buckets):
    max_exact = n_buckets // 2
    n = jnp.maximum(rel, 0)
    nf = jnp.maximum(n, 1).astype(F32)
    large = max_exact + (jnp.log(nf / max_exact) / math.log(MAX_DISTANCE / max_exact)
                         * (n_buckets - max_exact)).astype(I32)
    large = jnp.minimum(large, n_buckets - 1)
    bucket = jnp.where(n < max_exact, n, large)
    bias = jnp.zeros(rel.shape, F32)
    for b in range(n_buckets):
        bias = jnp.where(bucket == b, tab_ref[b, h], bias)
    return jnp.where(rel >= 0, bias, NEG)


def _bias_body(tab_ref, bp_ref, bs_ref, bn_ref, *, n_buckets, n_heads, blk, pre, dec_t, page, n_slots):
    r = lax.broadcasted_iota(I32, (blk, 2 * blk), 0)
    c = lax.broadcasted_iota(I32, (blk, 2 * blk), 1)
    rs = lax.broadcasted_iota(I32, (dec_t, page), 0)
    cs = lax.broadcasted_iota(I32, (dec_t, page), 1)
    for h in range(n_heads):
        tile = _bias_tile(tab_ref, h, blk + r - c, n_buckets)
        bp_ref[h, 0] = tile
        bp_ref[h, 1] = jnp.where(c < pre, NEG, tile)
        far = jnp.full((dec_t, page), tab_ref[n_buckets - 1, h], F32)
        last = _bias_tile(tab_ref, h, page + rs - cs, n_buckets)
        new = jnp.where(cs < dec_t, _bias_tile(tab_ref, h, rs - cs, n_buckets), NEG)
        for mp in range(2):
            rows = pl.ds((2 * h + mp) * dec_t, dec_t)
            bn_ref[rows, :] = new
            for u in range(n_slots):
                bs_ref[0, rows, u * page:(u + 1) * page] = far
                bs_ref[1, rows, u * page:(u + 1) * page] = last if u == n_slots - 1 else far


def _bias_tiles(rel_bias, blk, pre, dec_t, page, n_slots):
    n_buckets, n_heads = rel_bias.shape
    rows = 2 * n_heads * dec_t
    return pl.pallas_call(
        functools.partial(_bias_body, n_buckets=n_buckets, n_heads=n_heads, blk=blk, pre=pre,
                          dec_t=dec_t, page=page, n_slots=n_slots),
        in_specs=[pl.BlockSpec(memory_space=pltpu.SMEM)],
        out_shape=[jax.ShapeDtypeStruct((n_heads, 2, blk, 2 * blk), F32),
                   jax.ShapeDtypeStruct((2, rows, n_slots * page), F32),
                   jax.ShapeDtypeStruct((rows, page), F32)],
        name="bias_tiles",
    )(rel_bias)


def _sb_prompt_body(q_ref, k_ref, v_ref, tri_ref, g_ref, o_ref, *, blk, pre):
    i = pl.program_id(2) + 1
    pairs = range(SB_PAIRS_PER_STEP)
    lanes = lambda p: slice(p * LANES, (p + 1) * LANES)
    qq = [_stack_halves(q_ref[0, :, lanes(p)]) for p in pairs]
    tri = tri_ref[...]
    r = lax.broadcasted_iota(I32, (2 * blk, blk), 0)
    r = jnp.where(r >= blk, r - blk, r)
    c = lax.broadcasted_iota(I32, (2 * blk, blk), 1)
    vis = c < r

    def block(j, state, vis):
        keys = pl.ds(pl.multiple_of(j * blk, blk), blk)
        return _sb_blocks(qq, [k_ref[0, keys, lanes(p)] for p in pairs],
                          [v_ref[0, keys, lanes(p)] for p in pairs], tri, state, vis)

    zero = (jnp.zeros((2 * blk, LANES), F32), jnp.zeros((2 * blk, 1), F32))
    state = block(i, [zero for _ in pairs], vis)

    def alive(state):
        return functools.reduce(jnp.maximum, [jnp.max(carry) for _, carry in state]) > SB_CUTOFF

    def more(loop):
        jj, state = loop
        return (jj < i - 1) & alive(state)

    def body(loop):
        jj, state = loop
        return jj + 1, block(i - 1 - jj, state, None)

    jj, state = lax.while_loop(more, body, (jnp.int32(0), state))
    state = lax.cond((jj == i - 1) & alive(state),
                     lambda s: block(0, s, c >= pre), lambda s: s, state)
    for p in pairs:
        o_ref[0, :, lanes(p)] = _sb_finish(state[p][0], blk, g_ref[:, lanes(p)]).astype(o_ref.dtype)


def _sb_prompt(q_b, k_b, v_b, tri, g, blk, pre):
    b, tp, width = q_b.shape
    w = SB_PAIRS_PER_STEP * LANES
    grid = (b, width // w, tp // blk - 1)
    return pl.pallas_call(
        functools.partial(_sb_prompt_body, blk=blk, pre=pre),
        grid=grid,
        in_specs=[
            pl.BlockSpec((1, blk, w), lambda bi, p, i: (bi, i + 1, p)),
            pl.BlockSpec((1, tp, w), lambda bi, p, i: (bi, 0, p)),
            pl.BlockSpec((1, tp, w), lambda bi, p, i: (bi, 0, p)),
            pl.BlockSpec((blk, blk), lambda bi, p, i: (0, 0)),
            pl.BlockSpec((1, w), lambda bi, p, i: (0, p)),
        ],
        out_specs=pl.BlockSpec((1, blk, w), lambda bi, p, i: (bi, i, p)),
        out_shape=jax.ShapeDtypeStruct((b, tp - blk, width), BF16),
        compiler_params=pltpu.CompilerParams(
            dimension_semantics=("arbitrary", "arbitrary", "arbitrary"),
            vmem_limit_bytes=VMEM_LIMIT),
        name="sb_prompt",
    )(q_b, k_b, v_b, tri, g)


def _df_prompt_body(tab_ref, q_ref, k_ref, v_ref, bias_ref, g_ref, lq1, lk1, lq2, lk2,
                    o_ref, m_ref, l_ref, acc_ref, *, blk, pre, lam_init, n_buckets):
    hg = pl.program_id(1)
    i = pl.program_id(2) + 1
    heads = range(DF_HEADS_PER_STEP)
    lanes = lambda hh: slice(hh * LANES, (hh + 1) * LANES)
    qq = [_stack_halves(q_ref[0, :, lanes(hh)]) for hh in heads]
    far_bias = [tab_ref[n_buckets - 1, hg * DF_HEADS_PER_STEP + hh] for hh in heads]

    def scores(hh, j, n, near):
        keys = pl.ds(pl.multiple_of(j * blk, blk), n * blk)
        s = _dot_tb(qq[hh], k_ref[0, keys, lanes(hh)])
        if near:
            t = bias_ref[hh, 0]
            return s + jnp.concatenate([t, t], axis=0), keys
        col = lax.broadcasted_iota(I32, (1, n * blk), 1)
        return s + jnp.where((j == 0) & (col < pre), NEG, far_bias[hh]), keys

    def sweep(visit):
        visit(i - 1, 2, True)
        n_far = i - 1
        n4 = lax.shift_right_logical(n_far, 2)

        def wide(g, carry):
            visit(g * 4, 4, False)
            return carry

        lax.fori_loop(0, n4, wide, 0)

        @pl.when((n_far & 2) != 0)
        def _():
            visit(n4 * 4, 2, False)

        @pl.when((n_far & 1) != 0)
        def _():
            visit(n_far - 1, 1, False)

    def lane_chunks(x):
        return [x[:, c * LANES:(c + 1) * LANES] for c in range(x.shape[1] // LANES)]

    m_ref[...] = jnp.full(m_ref.shape, NEG, F32)

    def visit_max(j, n, near):
        for hh in heads:
            s, _ = scores(hh, j, n, near)
            m_ref[hh] = functools.reduce(jnp.maximum, lane_chunks(s), m_ref[hh])

    sweep(visit_max)
    m = [jnp.max(m_ref[hh], axis=-1, keepdims=True) for hh in heads]

    l_ref[...] = jnp.zeros_like(l_ref)
    acc_ref[...] = jnp.zeros_like(acc_ref)

    def visit_sum(j, n, near):
        scored = [scores(hh, j, n, near) for hh in heads]
        ps = [jnp.exp(s - m[hh]) for hh, (s, _) in zip(heads, scored)]
        for hh in heads:
            l_ref[hh] = functools.reduce(jnp.add, lane_chunks(ps[hh]), l_ref[hh])
        pv = [_dot(ps[hh].astype(BF16), v_ref[0, scored[hh][1], lanes(hh)]) for hh in heads]
        for hh in heads:
            acc_ref[hh] += pv[hh]

    sweep(visit_sum)
    lam = _lambda(lq1, lk1, lq2, lk2, lam_init)
    for hh in heads:
        l = jnp.sum(l_ref[hh], axis=-1, keepdims=True)
        o_ref[0, :, lanes(hh)] = _df_finish(acc_ref[hh], l, blk, lam, g_ref[...], lam_init).astype(o_ref.dtype)


def _df_prompt(rel_bias, q_b, k_b, v_b, bias_p, g, lams, blk, pre, lam_init):
    b, tp, width = q_b.shape
    hps = DF_HEADS_PER_STEP
    w = hps * LANES
    grid = (b, width // w, tp // blk - 1)
    vec = pl.BlockSpec((1, DH), lambda bi, h, i: (0, 0))
    tile = lambda bi, h, i: (h, jnp.where(i == 0, 1, 0), 0, 0)
    return pl.pallas_call(
        functools.partial(_df_prompt_body, blk=blk, pre=pre, lam_init=lam_init,
                          n_buckets=rel_bias.shape[0]),
        grid=grid,
        in_specs=[
            pl.BlockSpec(memory_space=pltpu.SMEM),
            pl.BlockSpec((1, blk, w), lambda bi, h, i: (bi, i + 1, h)),
            pl.BlockSpec((1, tp, w), lambda bi, h, i: (bi, 0, h)),
            pl.BlockSpec((1, tp, w), lambda bi, h, i: (bi, 0, h)),
            pl.BlockSpec((hps, 1, blk, 2 * blk), tile),
            pl.BlockSpec((1, LANES), lambda bi, h, i: (0, 0)),
            vec, vec, vec, vec,
        ],
        out_specs=pl.BlockSpec((1, blk, w), lambda bi, h, i: (bi, i, h)),
        out_shape=jax.ShapeDtypeStruct((b, tp - blk, width), BF16),
        scratch_shapes=[pltpu.VMEM((hps, 2 * blk, LANES), F32)] * 3,
        compiler_params=pltpu.CompilerParams(
            dimension_semantics=("arbitrary", "arbitrary", "arbitrary"),
            vmem_limit_bytes=VMEM_LIMIT),
        name="df_prompt",
    )(rel_bias, q_b, k_b, v_b, bias_p, g, *lams)


def _stack_decode_queries(q, n_blocks):
    q = q.astype(F32)
    return jnp.concatenate(
        [_stack_halves(q[:, p * LANES:(p + 1) * LANES]) for p in range(n_blocks)], axis=0).astype(BF16)


def _sb_decode_body(pt_ref, q_ref, kn_ref, vn_ref, kc_hbm, vc_hbm, tri_ref, g_ref, o_ref,
                    kbuf, vbuf, sem, *, dec_t, n_blocks, page, n_pages):
    bi = pl.program_id(0)
    rows = 2 * dec_t
    qq = _stack_decode_queries(q_ref[0], n_blocks)
    tri = tri_ref[...]

    def page_copies(slot, jj):
        pg = pt_ref[bi, n_pages - 1 - jj]
        return (pltpu.make_async_copy(kc_hbm.at[pg], kbuf.at[slot], sem.at[0, slot]),
                pltpu.make_async_copy(vc_hbm.at[pg], vbuf.at[slot], sem.at[1, slot]))

    def start(slot, jj):
        for cp in page_copies(slot, jj):
            cp.start()

    def wait(slot, jj):
        for cp in page_copies(slot, jj):
            cp.wait()

    start(0, 0)

    def weights(z, carry, vis):
        drop = _softplus(z)
        if vis is not None:
            drop = jnp.where(vis, drop, 0.0)
        csum = carry - _dot(drop.astype(BF16), tri)
        w = jnp.exp(z + csum)
        if vis is not None:
            w = jnp.where(vis, w, 0.0)
        return w.astype(BF16), csum[:, 0:1]

    r = lax.broadcasted_iota(I32, (dec_t, page), 0)
    c = lax.broadcasted_iota(I32, (dec_t, page), 1)
    vis = jnp.concatenate([c < r] * (2 * n_blocks), axis=0)
    z = jnp.concatenate(
        [_dot_tb(qq[p * rows:(p + 1) * rows], kn_ref[0, :, p * LANES:(p + 1) * LANES].astype(BF16))
         for p in range(n_blocks)], axis=0)
    w, carry = weights(z, jnp.zeros((n_blocks * rows, 1), F32), vis)
    acc = jnp.concatenate(
        [_dot(w[p * rows:(p + 1) * rows], vn_ref[0, :, p * LANES:(p + 1) * LANES].astype(BF16))
         for p in range(n_blocks)], axis=0)

    def more(state):
        jj, _, carry = state
        return (jj < n_pages) & (jnp.max(carry) > SB_CUTOFF)

    def body(state):
        jj, acc, carry = state
        slot = jj & 1
        wait(slot, jj)

        @pl.when(jj + 1 < n_pages)
        def _():
            start(1 - slot, jj + 1)

        kt = kbuf[slot]
        vt = vbuf[slot]
        z = jnp.concatenate(
            [_dot(qq[p * rows:(p + 1) * rows], kt[p * LANES:(p + 1) * LANES, :].astype(BF16))
             for p in range(n_blocks)], axis=0)
        w, carry = weights(z, carry, None)
        acc = acc + jnp.concatenate(
            [_dot_tb(w[p * rows:(p + 1) * rows], vt[p * LANES:(p + 1) * LANES, :].astype(BF16))
             for p in range(n_blocks)], axis=0)
        return jj + 1, acc, carry

    jj, acc, _ = lax.while_loop(more, body, (jnp.int32(0), acc, carry))

    @pl.when(jj < n_pages)
    def _():
        wait(jj & 1, jj)

    for p in range(n_blocks):
        o_ref[0, :, p * LANES:(p + 1) * LANES] = _sb_finish(
            acc[p * rows:(p + 1) * rows, :], dec_t, g_ref[:, p * LANES:(p + 1) * LANES])


def _sb_decode(page_table, q, k_new, v_new, kt_cache, vt_cache, tri, g):
    b, dec_t, width = q.shape
    n_pages = page_table.shape[1]
    page = kt_cache.shape[2]
    n_blocks = width // LANES
    per_b = lambda bi, pt: (bi, 0, 0)
    const = lambda bi, pt: (0, 0)
    grid_spec = pltpu.PrefetchScalarGridSpec(
        num_scalar_prefetch=1,
        grid=(b,),
        in_specs=[
            pl.BlockSpec((1, dec_t, width), per_b),
            pl.BlockSpec((1, page, width), per_b),
            pl.BlockSpec((1, page, width), per_b),
            pl.BlockSpec(memory_space=pl.ANY),
            pl.BlockSpec(memory_space=pl.ANY),
            pl.BlockSpec((page, page), const),
            pl.BlockSpec((1, width), const),
        ],
        out_specs=pl.BlockSpec((1, dec_t, width), per_b),
        scratch_shapes=[pltpu.VMEM((2, width, page), F32),
                        pltpu.VMEM((2, width, page), F32),
                        pltpu.SemaphoreType.DMA((2, 2))],
    )
    return pl.pallas_call(
        functools.partial(_sb_decode_body, dec_t=dec_t, n_blocks=n_blocks, page=page, n_pages=n_pages),
        grid_spec=grid_spec,
        out_shape=jax.ShapeDtypeStruct((b, dec_t, width), F32),
        compiler_params=pltpu.CompilerParams(
            dimension_semantics=("arbitrary",), vmem_limit_bytes=VMEM_LIMIT),
        name="sb_decode",
    )(page_table, q, k_new, v_new, kt_cache, vt_cache, tri, g)


def _df_decode_body(pt_ref, q_ref, kn_ref, vn_ref, *rest, dec_t, n_blocks, n_slots, lam_init):
    kc_refs, vc_refs = rest[:n_slots], rest[n_slots:2 * n_slots]
    (bias_ref, bias_new_ref, g_ref, lq1, lk1, lq2, lk2, o_ref,
     qq_ref, m_ref, l_ref, acc_ref) = rest[2 * n_slots:]
    jj = pl.program_id(1)
    rows = 2 * dec_t
    page = bias_new_ref.shape[1]

    def update(k_refs, v_refs, bias):
        own = [pl.ds(h, page, stride=n_blocks) for h in range(n_blocks)]
        gather = lambda refs, h: jnp.concatenate([r[0, own[h], :].astype(BF16) for r in refs], axis=0)
        qq = qq_ref[...]
        s = jnp.concatenate(
            [_dot_tb(qq[h * rows:(h + 1) * rows], gather(k_refs, h)) for h in range(n_blocks)],
            axis=0) + bias[...]
        m_old = m_ref[...]
        m_new = jnp.maximum(m_old, jnp.max(s, axis=-1, keepdims=True))
        alpha = jnp.exp(m_old - m_new)
        p = jnp.exp(s - m_new)
        l_ref[...] = alpha * l_ref[...] + jnp.sum(p, axis=-1, keepdims=True)
        m_ref[...] = m_new
        p = p.astype(BF16)
        pv = jnp.concatenate(
            [_dot(p[h * rows:(h + 1) * rows], gather(v_refs, h)) for h in range(n_blocks)], axis=0)
        acc_ref[...] = alpha * acc_ref[...] + pv

    @pl.when(jj == 0)
    def _():
        qq_ref[...] = _stack_decode_queries(q_ref[0], n_blocks)
        m_ref[...] = jnp.full(m_ref.shape, NEG, F32)
        l_ref[...] = jnp.zeros_like(l_ref)
        acc_ref[...] = jnp.zeros_like(acc_ref)
        update([kn_ref], [vn_ref], bias_new_ref)

    update(kc_refs, vc_refs, bias_ref.at[0])

    @pl.when(jj == pl.num_programs(1) - 1)
    def _():
        lam = _lambda(lq1, lk1, lq2, lk2, lam_init)
        for p in range(n_blocks):
            sl = slice(p * rows, (p + 1) * rows)
            o_ref[0, :, p * LANES:(p + 1) * LANES] = _df_finish(
                acc_ref[sl, :], l_ref[sl, :], dec_t, lam, g_ref[...], lam_init)


def _df_decode(page_table, q, k_new, v_new, k_cache, v_cache, bias_s, bias_new, g, lams, lam_init,
               n_slots):
    b, dec_t, width = q.shape
    n_pages = page_table.shape[1]
    page_rows = k_cache.shape[1]
    n_blocks = width // LANES
    rows = n_blocks * 2 * dec_t
    n_steps = n_pages // n_slots
    per_b = lambda bi, jj, pt: (bi, 0, 0)
    const = lambda bi, jj, pt: (0, 0)
    vec = pl.BlockSpec((1, DH), const)

    def slot_spec(u):
        return pl.BlockSpec((1, page_rows, LANES), lambda bi, jj, pt: (pt[bi, jj * n_slots + u], 0, 0))

    grid_spec = pltpu.PrefetchScalarGridSpec(
        num_scalar_prefetch=1,
        grid=(b, n_steps),
        in_specs=[
            pl.BlockSpec((1, dec_t, width), per_b),
            pl.BlockSpec((1,) + k_new.shape[1:], per_b),
            pl.BlockSpec((1,) + v_new.shape[1:], per_b),
            *[slot_spec(u) for u in range(n_slots)],
            *[slot_spec(u) for u in range(n_slots)],
            pl.BlockSpec((1, rows, bias_s.shape[2]),
                         lambda bi, jj, pt: (jnp.where(jj == n_steps - 1, 1, 0), 0, 0)),
            pl.BlockSpec(bias_new.shape, const),
            pl.BlockSpec((1, LANES), const),
            vec, vec, vec, vec,
        ],
        out_specs=pl.BlockSpec((1, dec_t, width), per_b),
        scratch_shapes=[pltpu.VMEM((rows, LANES), BF16),
                        pltpu.VMEM((rows, 1), F32),
                        pltpu.VMEM((rows, 1), F32),
                        pltpu.VMEM((rows, LANES), F32)],
    )
    return pl.pallas_call(
        functools.partial(_df_decode_body, dec_t=dec_t, n_blocks=n_blocks, n_slots=n_slots,
                          lam_init=lam_init),
        grid_spec=grid_spec,
        out_shape=jax.ShapeDtypeStruct((b, dec_t, width), F32),
        compiler_params=pltpu.CompilerParams(
            dimension_semantics=("arbitrary", "arbitrary"),
            vmem_limit_bytes=VMEM_LIMIT),
        name="df_decode",
    )(page_table, q, k_new, v_new, *([k_cache] * n_slots), *([v_cache] * n_slots),
      bias_s, bias_new, g, *lams)


def _router_gates(logits_t, bias_col, n_experts):
    tm = logits_t.shape[1]
    gsz = n_experts // N_GROUPS
    scores = 1.0 / (1.0 + jnp.exp(-logits_t))
    sel = scores + bias_col
    sub = lax.broadcasted_iota(I32, (gsz, tm), 0)
    group_scores = []
    for g in range(N_GROUPS):
        blk = sel[g * gsz:(g + 1) * gsz]
        m1 = jnp.max(blk, axis=0, keepdims=True)
        first = jnp.min(jnp.where(blk == m1, sub, gsz), axis=0, keepdims=True)
        m2 = jnp.max(jnp.where(sub == first, -jnp.inf, blk), axis=0, keepdims=True)
        group_scores.append(m1 + m2)
    gs = jnp.concatenate(group_scores, axis=0)
    gidx = lax.broadcasted_iota(I32, (N_GROUPS, tm), 0)
    grank = jnp.zeros((N_GROUPS, tm), I32)
    for g in range(N_GROUPS):
        row = gs[g:g + 1]
        ahead = (row > gs) | ((row == gs) & (gidx > g))
        grank = grank + ahead.astype(I32)
    gkeep = grank < TOPK_GROUPS
    masked = jnp.concatenate(
        [jnp.where(jnp.broadcast_to(gkeep[g:g + 1], (gsz, tm)), sel[g * gsz:(g + 1) * gsz], -jnp.inf)
         for g in range(N_GROUPS)], axis=0)
    eidx = lax.broadcasted_iota(I32, (n_experts, tm), 0)
    erank = jnp.zeros((n_experts, tm), I32)
    for e in range(n_experts):
        row = masked[e:e + 1]
        ahead = (row > masked) | ((row == masked) & (eidx > e))
        erank = erank + ahead.astype(I32)
    w = jnp.where(erank < TOP_K, scores, 0.0)
    return w / jnp.sum(w, axis=0, keepdims=True) * ROUTED_SCALE


def _post_body(msb_ref, mdf_ref, x_ref, wo_sb_ref, wo_df_ref, g2_ref, rw_hi_ref, rw_lo_ref,
               rb_ref, x1_ref, h2_ref, gates_ref, *, n_experts):
    att = _dot(msb_ref[0].astype(BF16), wo_sb_ref[...]) + _dot(mdf_ref[0].astype(BF16), wo_df_ref[...])
    x1 = x_ref[0] + att
    x1_ref[...] = x1
    ms = jnp.mean(x1 * x1, axis=-1, keepdims=True)
    h2 = (x1 * lax.rsqrt(ms + EPS)) * g2_ref[...]
    h2_hi, h2_lo = _split_bf16(h2)
    h2_ref[...] = h2_hi
    rw_hi = rw_hi_ref[...]
    logits_t = _dot_tb(rw_hi, h2_hi) + _dot_tb(rw_hi, h2_lo) + _dot_tb(rw_lo_ref[...], h2_hi)
    gates_t = _router_gates(logits_t, rb_ref[...], n_experts)
    tm = gates_t.shape[1]
    pad = gates_ref.shape[1] - n_experts
    gates_t = jnp.concatenate([gates_t, jnp.zeros((pad, tm), F32)], axis=0)
    gates_ref[...] = gates_t.T.astype(gates_ref.dtype)


def _post_attention(msb, mdf, x, wo_sb, wo_df, g2, rw_hi, rw_lo, rb, tm):
    b, s, d = x.shape
    n = b * s
    per_b = s // tm
    n_experts = rw_hi.shape[0]
    ge = _round_up(n_experts, LANES)
    row = lambda bi, i: (bi * per_b + i, 0)
    const = lambda bi, i: (0, 0)
    return pl.pallas_call(
        functools.partial(_post_body, n_experts=n_experts),
        grid=(b, per_b),
        in_specs=[
            pl.BlockSpec((1, tm, msb.shape[2]), lambda bi, i: (bi, i, 0)),
            pl.BlockSpec((1, tm, mdf.shape[2]), lambda bi, i: (bi, i, 0)),
            pl.BlockSpec((1, tm, d), lambda bi, i: (bi, i, 0)),
            pl.BlockSpec(wo_sb.shape, const),
            pl.BlockSpec(wo_df.shape, const),
            pl.BlockSpec((1, d), const),
            pl.BlockSpec(rw_hi.shape, const),
            pl.BlockSpec(rw_lo.shape, const),
            pl.BlockSpec((n_experts, 1), const),
        ],
        out_specs=[pl.BlockSpec((tm, d), row), pl.BlockSpec((tm, d), row), pl.BlockSpec((tm, ge), row)],
        out_shape=[jax.ShapeDtypeStruct((n, d), F32), jax.ShapeDtypeStruct((n, d), BF16),
                   jax.ShapeDtypeStruct((n, ge), F32)],
        compiler_params=pltpu.CompilerParams(
            dimension_semantics=("arbitrary", "arbitrary"), vmem_limit_bytes=VMEM_LIMIT),
        name="post_attention",
    )(msb, mdf, x, wo_sb, wo_df, g2, rw_hi, rw_lo, rb)


def _silu(x):
    return x / (1.0 + jnp.exp(-x))


def _moe_body(h2_ref, x1_ref, gates_ref, wg_ref, wu_ref, wd_ref, sg_ref, su_ref, sd_ref,
              o_ref, acc_ref, *, group):
    eg = pl.program_id(1)
    h2 = h2_ref[...]

    @pl.when(eg == 0)
    def _():
        a = _silu(_dot(h2, sg_ref[...])) * _dot(h2, su_ref[...])
        acc_ref[...] = _dot(a.astype(BF16), sd_ref[...])

    f = wg_ref.shape[2]
    gates = gates_ref[...]
    lane = lax.broadcasted_iota(I32, gates.shape, 1)
    acts = []
    for j in range(group):
        gate = jnp.sum(jnp.where(lane == eg * group + j, gates, 0.0), axis=-1, keepdims=True)
        acts.append((_silu(_dot(h2, wg_ref[j])) * _dot(h2, wu_ref[j]) * gate).astype(BF16))
    a = jnp.concatenate(acts, axis=1)
    wd = wd_ref[...].reshape(group * f, wd_ref.shape[2])
    acc_ref[...] += _dot(a, wd)

    @pl.when(eg == pl.num_programs(1) - 1)
    def _():
        o_ref[...] = x1_ref[...] + acc_ref[...]


def _moe(h2, x1, gates, w_gate, w_up, w_down, sg, su, sd, tm, group):
    n, d = x1.shape
    n_experts, _, f = w_gate.shape
    row = lambda i, e: (i, 0)
    const = lambda i, e: (0, 0)
    return pl.pallas_call(
        functools.partial(_moe_body, group=group),
        grid=(n // tm, n_experts // group),
        in_specs=[
            pl.BlockSpec((tm, d), row),
            pl.BlockSpec((tm, d), row),
            pl.BlockSpec((tm, gates.shape[1]), row),
            pl.BlockSpec((group, d, f), lambda i, e: (e, 0, 0)),
            pl.BlockSpec((group, d, f), lambda i, e: (e, 0, 0)),
            pl.BlockSpec((group, f, d), lambda i, e: (e, 0, 0)),
            pl.BlockSpec(sg.shape, const),
            pl.BlockSpec(su.shape, const),
            pl.BlockSpec(sd.shape, const),
        ],
        out_specs=pl.BlockSpec((tm, d), row),
        out_shape=jax.ShapeDtypeStruct((n, d), F32),
        scratch_shapes=[pltpu.VMEM((tm, d), F32)],
        compiler_params=pltpu.CompilerParams(
            dimension_semantics=("arbitrary", "arbitrary"), vmem_limit_bytes=VMEM_LIMIT),
        name="moe",
    )(h2, x1, gates, w_gate, w_up, w_down, sg, su, sd)


def _largest_tile(n, cap):
    t = cap
    while n % t:
        t //= 2
    return t


def kernel(x_prompt, x_sample, cache_k_sb, cache_v_sb, cache_k_diff, cache_v_diff, page_table,
           meta_tokens, rel_bias, norm1_g, w_in, q_norm_g, k_norm_g, lambda_q1, lambda_k1,
           lambda_q2, lambda_k2, sb_out_g, diff_subln_g, w_out, norm2_g, router_w, router_bias,
           w_gate, w_up, w_down, w_shared_gate, w_shared_up, w_shared_down):
    assert w_in.shape[0] == 1, "single-layer step"
    b, seq, d = x_prompt.shape
    dec_b, dec_t, _ = x_sample.shape
    n_meta = meta_tokens.shape[0]
    width = sb_out_g.shape[1]
    n_pool, page = cache_k_sb.shape[1], cache_k_sb.shape[2]
    n_experts = router_w.shape[2]
    t = seq + n_meta
    blk = ATT_BLOCK
    lam_init = 0.8 - 0.6 * math.exp(-0.3 * 0)

    w_in_b = w_in[0].astype(BF16)
    g1 = norm1_g
    reps = width // DH
    qg = jnp.tile(q_norm_g, (1, reps))
    kg = jnp.tile(k_norm_g, (1, reps))
    lane = jnp.arange(width)
    gm = ((lane[:, None] // DH) == (lane[None, :] // DH)).astype(BF16) * (1.0 / DH)
    tri_p = (jnp.arange(blk)[:, None] >= jnp.arange(blk)[None, :]).astype(BF16)
    tri_d = (jnp.arange(page)[:, None] >= jnp.arange(page)[None, :]).astype(BF16)
    lams = (lambda_q1, lambda_k1, lambda_q2, lambda_k2)
    wo_sb = w_out[0, :width].astype(BF16)
    wo_df = w_out[0, width:].astype(BF16)
    rw_t = router_w[0].T
    rw_hi = rw_t.astype(BF16)
    rw_lo = (rw_t - rw_hi.astype(F32)).astype(BF16)
    rb = router_bias[0][:, None]
    sg = w_shared_gate[0].astype(BF16)
    su = w_shared_up[0].astype(BF16)
    sd = w_shared_down[0].astype(BF16)

    n_h_sb = width // DH
    n_h_df = width // LANES
    n_slots = DECODE_PAGES_PER_STEP
    assert blk >= MAX_DISTANCE and page >= MAX_DISTANCE
    assert page_table.shape[1] % n_slots == 0 and dec_t * n_h_df <= page
    pre = blk - n_meta
    assert 0 <= pre < blk and seq % blk == 0
    bias_p, bias_s, bias_n = _bias_tiles(rel_bias, blk, pre, dec_t, page, n_slots)
    head = jnp.concatenate([jnp.zeros((pre, d), F32), meta_tokens], axis=0)
    (k_sb_p, v_sb_p, k_df_p, v_df_p,
     qsb_b, ksb_b, vsb_b, qdf_b, kdf_b, vdf_b) = _project_prompt(head, x_prompt, n_meta, g1, w_in_b, qg, kg, gm)
    msb_p = _sb_prompt(qsb_b, ksb_b, vsb_b, tri_p, sb_out_g, blk, pre)
    mdf_p = _df_prompt(rel_bias, qdf_b, kdf_b, vdf_b, bias_p, diff_subln_g, lams, blk, pre, lam_init)

    n_s = dec_b * dec_t
    xs = x_sample.reshape(1, n_s, d)
    (k_sb_s, v_sb_s, k_df_s, v_df_s,
     qsb_s, _, _, qdf_s, _, _) = _project(xs, n_s, g1, w_in_b, qg, kg, gm, n_s)

    def new_page(rows, n_rows, n_pad):
        r = rows.reshape(dec_b, n_rows, -1)
        return jnp.pad(r, ((0, 0), (0, n_pad - n_rows), (0, 0)))

    transposed_pages = lambda c: jnp.transpose(c[0], (0, 2, 3, 1)).reshape(n_pool, width, page)
    interleaved_rows = lambda c: c.reshape(n_pool, page * n_h_df, LANES)
    msb_s = _sb_decode(page_table, qsb_s.reshape(dec_b, dec_t, width),
                       new_page(k_sb_s, dec_t, page), new_page(v_sb_s, dec_t, page),
                       transposed_pages(cache_k_sb), transposed_pages(cache_v_sb), tri_d, sb_out_g)
    mdf_s = _df_decode(page_table, qdf_s.reshape(dec_b, dec_t, width),
                       new_page(k_df_s, dec_t * n_h_df, page * n_h_df),
                       new_page(v_df_s, dec_t * n_h_df, page * n_h_df),
                       interleaved_rows(cache_k_diff), interleaved_rows(cache_v_diff),
                       bias_s, bias_n, diff_subln_g, lams, lam_init, n_slots)

    wg_b, wu_b, wd_b = w_gate[0].astype(BF16), w_up[0].astype(BF16), w_down[0].astype(BF16)
    x1_p, h2_p, gates_p = _post_attention(msb_p, mdf_p, x_prompt, wo_sb, wo_df, norm2_g,
                                          rw_hi, rw_lo, rb, blk)
    y_p = _moe(h2_p, x1_p, gates_p, wg_b, wu_b, wd_b, sg, su, sd,
               _largest_tile(b * seq, MOE_TOKEN_TILE), MOE_EXPERT_GROUP)
    y_prompt = y_p.reshape(b, seq, d)
    x1_s, h2_s, gates_s = _post_attention(
        msb_s.reshape(1, n_s, width), mdf_s.reshape(1, n_s, width), x_sample.reshape(1, n_s, d),
        wo_sb, wo_df, norm2_g, rw_hi, rw_lo, rb, n_s)
    y_s = _moe(h2_s, x1_s, gates_s, wg_b, wu_b, wd_b, sg, su, sd, n_s, MOE_EXPERT_GROUP)
    y_sample = y_s.reshape(dec_b, dec_t, d)

    sb_rows = lambda r, bb, tt: r.reshape(1, bb, tt, n_h_sb, DH)
    df_rows = lambda r, bb, tt: r.reshape(1, bb, tt, n_h_df, LANES)
    return (y_prompt, y_sample,
            sb_rows(k_sb_p, b, t), sb_rows(v_sb_p, b, t), df_rows(k_df_p, b, t), df_rows(v_df_p, b, t),
            sb_rows(k_sb_s, dec_b, dec_t), sb_rows(v_sb_s, dec_b, dec_t),
            df_rows(k_df_s, dec_b, dec_t), df_rows(v_df_s, dec_b, dec_t))
```

```python
import functools
import math

import jax
import jax.numpy as jnp
from jax import lax
from jax.experimental import pallas as pl
from jax.experimental.pallas import tpu as pltpu

F32 = jnp.float32
BF16 = jnp.bfloat16
I32 = jnp.int32

EPS = 1e-6
DH = 64
LANES = 128
MAX_DISTANCE = 128
TOP_K = 8
N_GROUPS = 8
TOPK_GROUPS = 4
ROUTED_SCALE = 2.5
NEG = -1e30
SB_CUTOFF = -104.0
ATT_BLOCK = 256
DF_HEADS_PER_STEP = 4
SB_PAIRS_PER_STEP = 4
DECODE_PAGES_PER_STEP = 16
MOE_TOKEN_TILE = 1024
MOE_EXPERT_GROUP = 8
VMEM_LIMIT = 60 * 1024 * 1024

_TRANS_B = (((1,), (1,)), ((), ()))


def _dot(a, b):
    return jnp.dot(a, b, preferred_element_type=F32)


def _dot_tb(a, b):
    return lax.dot_general(a, b, _TRANS_B, preferred_element_type=F32)


def _split_bf16(x):
    hi = x.astype(BF16)
    lo = (x - hi.astype(F32)).astype(BF16)
    return hi, lo


def _round_up(n, m):
    return (n + m - 1) // m * m


def _proj_rows(x, g1_ref, w_ref, qg_ref, kg_ref, gm_ref, ksb_o, vsb_o, kdf_o, vdf_o,
               qsb_b, ksb_b, vsb_b, qdf_b, kdf_b, vdf_b, width):
    ms = jnp.mean(x * x, axis=-1, keepdims=True)
    h = (x * lax.rsqrt(ms + EPS)) * g1_ref[...]
    proj = _dot(h.astype(BF16), w_ref[...])
    w = width
    q_sb, k_sb, v_sb = proj[:, 0:w], proj[:, w:2 * w], proj[:, 2 * w:3 * w]
    q_df, k_df, v_df = proj[:, 3 * w:4 * w], proj[:, 4 * w:5 * w], proj[:, 5 * w:6 * w]

    def map_norm(t, g_ref):
        msq = _dot((t * t).astype(BF16), gm_ref[...])
        return (t * lax.rsqrt(msq + EPS)) * g_ref[...]

    q_df = map_norm(q_df, qg_ref)
    k_df = map_norm(k_df, kg_ref)
    ksb_o[...] = k_sb
    vsb_o[...] = v_sb
    tm = x.shape[0]
    n_df = w // LANES
    for hd in range(n_df):
        rows = pl.ds(hd, tm, stride=n_df)
        kdf_o[rows, :] = k_df[:, hd * LANES:(hd + 1) * LANES]
        vdf_o[rows, :] = v_df[:, hd * LANES:(hd + 1) * LANES]
    scale = DH ** -0.5
    qsb_b[0] = (q_sb * scale).astype(BF16)
    ksb_b[0] = k_sb.astype(BF16)
    vsb_b[0] = v_sb.astype(BF16)
    qdf_b[0] = (q_df * scale).astype(BF16)
    kdf_b[0] = k_df.astype(BF16)
    vdf_b[0] = v_df.astype(BF16)


def _proj_body(x_ref, g1_ref, w_ref, qg_ref, kg_ref, gm_ref,
               ksb_o, vsb_o, kdf_o, vdf_o, *bf_outs, width):
    _proj_rows(x_ref[0], g1_ref, w_ref, qg_ref, kg_ref, gm_ref,
               ksb_o.at[0], vsb_o.at[0], kdf_o.at[0], vdf_o.at[0], *bf_outs, width)


def _proj_prompt_body(head_ref, x_ref, g1_ref, w_ref, qg_ref, kg_ref, gm_ref,
                      ksb_hbm, vsb_hbm, kdf_hbm, vdf_hbm, *rest, width, lead):
    bf_outs, (st_sb, st_df, sem) = rest[:6], rest[6:]
    b, i = pl.program_id(0), pl.program_id(1)
    n_i = pl.num_programs(1)
    step = b * n_i + i
    slot = step & 1
    tm = x_ref.shape[1]
    n_df = width // LANES

    def copies(slot, bb, ii, head):
        if head:
            src, dst, n = tm - lead, 0, lead
        else:
            src, dst, n = 0, lead + (ii - 1) * tm, tm
        out = []
        for kv, hbm in enumerate((ksb_hbm, vsb_hbm)):
            out.append(pltpu.make_async_copy(st_sb.at[slot, kv, pl.ds(src, n), :],
                                             hbm.at[bb, pl.ds(dst, n), :], sem.at[slot]))
        for kv, hbm in enumerate((kdf_hbm, vdf_hbm)):
            out.append(pltpu.make_async_copy(st_df.at[slot, kv, pl.ds(src * n_df, n * n_df), :],
                                             hbm.at[bb, pl.ds(dst * n_df, n * n_df), :], sem.at[slot]))
        return out

    def wait_step(slot, was_head):
        @pl.when(was_head)
        def _():
            for cp in copies(slot, 0, 1, True):
                cp.wait()

        @pl.when(jnp.logical_not(was_head))
        def _():
            for cp in copies(slot, 0, 1, False):
                cp.wait()

    @pl.when(step >= 2)
    def _():
        wait_step(slot, i == 2)

    x = jnp.where(i == 0, head_ref[...], x_ref[0])
    _proj_rows(x, g1_ref, w_ref, qg_ref, kg_ref, gm_ref,
               st_sb.at[slot, 0], st_sb.at[slot, 1], st_df.at[slot, 0], st_df.at[slot, 1],
               *bf_outs, width)

    @pl.when(i == 0)
    def _():
        for cp in copies(slot, b, i, True):
            cp.start()

    @pl.when(i > 0)
    def _():
        for cp in copies(slot, b, i, False):
            cp.start()

    @pl.when(step == pl.num_programs(0) * n_i - 1)
    def _():
        wait_step(1 - slot, i == 1)
        wait_step(slot, i == 0)


def _project(x_pad, t_valid, g1, w_in_b, qg, kg, gm, tm):
    b, tp, d = x_pad.shape
    width = w_in_b.shape[1] // 6
    grid = (b, tp // tm)
    row = lambda bi, i: (bi, i, 0)
    const = lambda bi, i: (0, 0)
    n_df = width // LANES
    f32_out = jax.ShapeDtypeStruct((b, t_valid, width), F32)
    df_out = jax.ShapeDtypeStruct((b, t_valid * n_df, LANES), F32)
    bf_out = jax.ShapeDtypeStruct((b, tp, width), BF16)
    out_spec = pl.BlockSpec((1, tm, width), row)
    df_spec = pl.BlockSpec((1, tm * n_df, LANES), row)
    return pl.pallas_call(
        functools.partial(_proj_body, width=width),
        grid=grid,
        in_specs=[
            pl.BlockSpec((1, tm, d), row),
            pl.BlockSpec((1, d), const),
            pl.BlockSpec(w_in_b.shape, const),
            pl.BlockSpec((1, width), const),
            pl.BlockSpec((1, width), const),
            pl.BlockSpec((width, width), const),
        ],
        out_specs=[out_spec] * 2 + [df_spec] * 2 + [out_spec] * 6,
        out_shape=[f32_out] * 2 + [df_out] * 2 + [bf_out] * 6,
        compiler_params=pltpu.CompilerParams(
            dimension_semantics=("arbitrary", "arbitrary"),
            vmem_limit_bytes=VMEM_LIMIT),
        name="proj",
    )(x_pad, g1, w_in_b, qg, kg, gm)


def _project_prompt(head, x, lead, g1, w_in_b, qg, kg, gm):
    b, s, d = x.shape
    tm = head.shape[0]
    width = w_in_b.shape[1] // 6
    n_df = width // LANES
    n_i = 1 + s // tm
    assert s % tm == 0 and n_i >= 3 and lead % 8 == 0 and 0 < lead <= tm
    row = lambda bi, i: (bi, i, 0)
    const = lambda bi, i: (0, 0)
    anywhere = pl.BlockSpec(memory_space=pl.ANY)
    bf_out = jax.ShapeDtypeStruct((b, tm + s, width), BF16)
    return pl.pallas_call(
        functools.partial(_proj_prompt_body, width=width, lead=lead),
        grid=(b, n_i),
        in_specs=[
            pl.BlockSpec((tm, d), const),
            pl.BlockSpec((1, tm, d), lambda bi, i: (bi, jnp.maximum(i - 1, 0), 0)),
            pl.BlockSpec((1, d), const),
            pl.BlockSpec(w_in_b.shape, const),
            pl.BlockSpec((1, width), const),
            pl.BlockSpec((1, width), const),
            pl.BlockSpec((width, width), const),
        ],
        out_specs=[anywhere] * 4 + [pl.BlockSpec((1, tm, width), row)] * 6,
        out_shape=[jax.ShapeDtypeStruct((b, lead + s, width), F32)] * 2
        + [jax.ShapeDtypeStruct((b, (lead + s) * n_df, LANES), F32)] * 2 + [bf_out] * 6,
        scratch_shapes=[pltpu.VMEM((2, 2, tm, width), F32),
                        pltpu.VMEM((2, 2, tm * n_df, LANES), F32),
                        pltpu.SemaphoreType.DMA((2,))],
        compiler_params=pltpu.CompilerParams(
            dimension_semantics=("arbitrary", "arbitrary"),
            vmem_limit_bytes=VMEM_LIMIT),
        name="proj_prompt",
    )(head, x, g1, w_in_b, qg, kg, gm)


def _stack_halves(q):
    lane = lax.broadcasted_iota(I32, (1, LANES), 1)
    zero = jnp.zeros_like(q)
    return jnp.concatenate(
        [jnp.where(lane < DH, q, zero), jnp.where(lane >= DH, q, zero)], axis=0)


def _softplus(z):
    return jnp.maximum(z, 0.0) + jnp.log(1.0 + jnp.exp(-jnp.abs(z)))


def _sb_blocks(qqs, ks, vs, tri, state, vis):
    n = range(len(qqs))
    zs = [_dot_tb(qqs[p], ks[p]) for p in n]
    drops = [_softplus(z) for z in zs]
    if vis is not None:
        drops = [jnp.where(vis, d, 0.0) for d in drops]
    csums = [state[p][1] - _dot(drops[p].astype(BF16), tri) for p in n]
    ws = [jnp.exp(zs[p] + csums[p]) for p in n]
    if vis is not None:
        ws = [jnp.where(vis, w, 0.0) for w in ws]
    pvs = [_dot(ws[p].astype(BF16), vs[p]) for p in n]
    return [(state[p][0] + pvs[p], csums[p][:, 0:1]) for p in n]


def _sb_finish(acc, t, g):
    lane = lax.broadcasted_iota(I32, (1, LANES), 1)
    lo_half = lane < DH
    o = jnp.where(lo_half, acc[:t], acc[t:])
    o2 = o * o
    s_lo = jnp.sum(jnp.where(lo_half, o2, 0.0), axis=-1, keepdims=True)
    s_hi = jnp.sum(jnp.where(lo_half, 0.0, o2), axis=-1, keepdims=True)
    ms = jnp.where(lo_half, s_lo, s_hi) * (1.0 / DH)
    return (o * lax.rsqrt(ms + EPS)) * g


def _df_block(qq, k, v, bias, m, l, acc):
    s = _dot_tb(qq, k) + bias
    m_new = jnp.maximum(m, jnp.max(s, axis=-1, keepdims=True))
    alpha = jnp.exp(m - m_new)
    p = jnp.exp(s - m_new)
    l = alpha * l + jnp.sum(p, axis=-1, keepdims=True)
    acc = alpha * acc + _dot(p.astype(BF16), v)
    return m_new, l, acc


def _lambda(lq1, lk1, lq2, lk2, lam_init):
    s1 = jnp.sum(lq1[...] * lk1[...], axis=-1, keepdims=True)
    s2 = jnp.sum(lq2[...] * lk2[...], axis=-1, keepdims=True)
    return jnp.exp(s1) - jnp.exp(s2) + lam_init


def _df_finish(acc, l, t, lam, g, lam_init):
    o = acc[:t] / l[:t] - lam * (acc[t:] / l[t:])
    ms = jnp.mean(o * o, axis=-1, keepdims=True)
    return (o * lax.rsqrt(ms + EPS)) * (g * (1.0 - lam_init))


def _bias_tile(tab_ref, h, rel, n_buckets):
    max_exact = n_buckets // 2
    n = jnp.maximum(rel, 0)
    nf = jnp.maximum(n, 1).astype(F32)
    large = max_exact + (jnp.log(nf / max_exact) / math.log(MAX_DISTANCE / max_exact)
                         * (n_buckets - max_exact)).astype(I32)
    large = jnp.minimum(large, n_buckets - 1)
    bucket = jnp.where(n < max_exact, n, large)
    bias = jnp.zeros(rel.shape, F32)
    for b in range(n_buckets):
        bias = jnp.where(bucket == b, tab_ref[b, h], bias)
    return jnp.where(rel >= 0, bias, NEG)


def _bias_body(tab_ref, bp_ref, bs_ref, bn_ref, *, n_buckets, n_heads, blk, pre, dec_t, page, n_slots):
    r = lax.broadcasted_iota(I32, (blk, 2 * blk), 0)
    c = lax.broadcasted_iota(I32, (blk, 2 * blk), 1)
    rs = lax.broadcasted_iota(I32, (dec_t, page), 0)
    cs = lax.broadcasted_iota(I32, (dec_t, page), 1)
    for h in range(n_heads):
        tile = _bias_tile(tab_ref, h, blk + r - c, n_buckets)
        bp_ref[h, 0] = tile
        bp_ref[h, 1] = jnp.where(c < pre, NEG, tile)
        far = jnp.full((dec_t, page), tab_ref[n_buckets - 1, h], F32)
        last = _bias_tile(tab_ref, h, page + rs - cs, n_buckets)
        new = jnp.where(cs < dec_t, _bias_tile(tab_ref, h, rs - cs, n_buckets), NEG)
        for mp in range(2):
            rows = pl.ds((2 * h + mp) * dec_t, dec_t)
            bn_ref[rows, :] = new
            for u in range(n_slots):
                bs_ref[0, rows, u * page:(u + 1) * page] = far
                bs_ref[1, rows, u * page:(u + 1) * page] = last if u == n_slots - 1 else far


def _bias_tiles(rel_bias, blk, pre, dec_t, page, n_slots):
    n_buckets, n_heads = rel_bias.shape
    rows = 2 * n_heads * dec_t
    return pl.pallas_call(
        functools.partial(_bias_body, n_buckets=n_buckets, n_heads=n_heads, blk=blk, pre=pre,
                          dec_t=dec_t, page=page, n_slots=n_slots),
        in_specs=[pl.BlockSpec(memory_space=pltpu.SMEM)],
        out_shape=[jax.ShapeDtypeStruct((n_heads, 2, blk, 2 * blk), F32),
                   jax.ShapeDtypeStruct((2, rows, n_slots * page), F32),
                   jax.ShapeDtypeStruct((rows, page), F32)],
        name="bias_tiles",
    )(rel_bias)


def _sb_prompt_body(q_ref, k_ref, v_ref, tri_ref, g_ref, o_ref, *, blk, pre):
    i = pl.program_id(2) + 1
    pairs = range(SB_PAIRS_PER_STEP)
    lanes = lambda p: slice(p * LANES, (p + 1) * LANES)
    qq = [_stack_halves(q_ref[0, :, lanes(p)]) for p in pairs]
    tri = tri_ref[...]
    r = lax.broadcasted_iota(I32, (2 * blk, blk), 0)
    r = jnp.where(r >= blk, r - blk, r)
    c = lax.broadcasted_iota(I32, (2 * blk, blk), 1)
    vis = c < r

    def block(j, state, vis):
        keys = pl.ds(pl.multiple_of(j * blk, blk), blk)
        return _sb_blocks(qq, [k_ref[0, keys, lanes(p)] for p in pairs],
                          [v_ref[0, keys, lanes(p)] for p in pairs], tri, state, vis)

    zero = (jnp.zeros((2 * blk, LANES), F32), jnp.zeros((2 * blk, 1), F32))
    state = block(i, [zero for _ in pairs], vis)

    def alive(state):
        return functools.reduce(jnp.maximum, [jnp.max(carry) for _, carry in state]) > SB_CUTOFF

    def more(loop):
        jj, state = loop
        return (jj < i - 1) & alive(state)

    def body(loop):
        jj, state = loop
        return jj + 1, block(i - 1 - jj, state, None)

    jj, state = lax.while_loop(more, body, (jnp.int32(0), state))
    state = lax.cond((jj == i - 1) & alive(state),
                     lambda s: block(0, s, c >= pre), lambda s: s, state)
    for p in pairs:
        o_ref[0, :, lanes(p)] = _sb_finish(state[p][0], blk, g_ref[:, lanes(p)]).astype(o_ref.dtype)


def _sb_prompt(q_b, k_b, v_b, tri, g, blk, pre):
    b, tp, width = q_b.shape
    w = SB_PAIRS_PER_STEP * LANES
    grid = (b, width // w, tp // blk - 1)
    return pl.pallas_call(
        functools.partial(_sb_prompt_body, blk=blk, pre=pre),
        grid=grid,
        in_specs=[
            pl.BlockSpec((1, blk, w), lambda bi, p, i: (bi, i + 1, p)),
            pl.BlockSpec((1, tp, w), lambda bi, p, i: (bi, 0, p)),
            pl.BlockSpec((1, tp, w), lambda bi, p, i: (bi, 0, p)),
            pl.BlockSpec((blk, blk), lambda bi, p, i: (0, 0)),
            pl.BlockSpec((1, w), lambda bi, p, i: (0, p)),
        ],
        out_specs=pl.BlockSpec((1, blk, w), lambda bi, p, i: (bi, i, p)),
        out_shape=jax.ShapeDtypeStruct((b, tp - blk, width), BF16),
        compiler_params=pltpu.CompilerParams(
            dimension_semantics=("arbitrary", "arbitrary", "arbitrary"),
            vmem_limit_bytes=VMEM_LIMIT),
        name="sb_prompt",
    )(q_b, k_b, v_b, tri, g)


def _df_prompt_body(tab_ref, q_ref, k_ref, v_ref, bias_ref, g_ref, lq1, lk1, lq2, lk2,
                    o_ref, m_ref, l_ref, acc_ref, *, blk, pre, lam_init, n_buckets):
    hg = pl.program_id(1)
    i = pl.program_id(2) + 1
    heads = range(DF_HEADS_PER_STEP)
    lanes = lambda hh: slice(hh * LANES, (hh + 1) * LANES)
    qq = [_stack_halves(q_ref[0, :, lanes(hh)]) for hh in heads]
    far_bias = [tab_ref[n_buckets - 1, hg * DF_HEADS_PER_STEP + hh] for hh in heads]

    def scores(hh, j, n, near):
        keys = pl.ds(pl.multiple_of(j * blk, blk), n * blk)
        s = _dot_tb(qq[hh], k_ref[0, keys, lanes(hh)])
        if near:
            t = bias_ref[hh, 0]
            return s + jnp.concatenate([t, t], axis=0), keys
        col = lax.broadcasted_iota(I32, (1, n * blk), 1)
        return s + jnp.where((j == 0) & (col < pre), NEG, far_bias[hh]), keys

    def sweep(visit):
        visit(i - 1, 2, True)
        n_far = i - 1
        n4 = lax.shift_right_logical(n_far, 2)

        def wide(g, carry):
            visit(g * 4, 4, False)
            return carry

        lax.fori_loop(0, n4, wide, 0)

        @pl.when((n_far & 2) != 0)
        def _():
            visit(n4 * 4, 2, False)

        @pl.when((n_far & 1) != 0)
        def _():
            visit(n_far - 1, 1, False)

    def lane_chunks(x):
        return [x[:, c * LANES:(c + 1) * LANES] for c in range(x.shape[1] // LANES)]

    m_ref[...] = jnp.full(m_ref.shape, NEG, F32)

    def visit_max(j, n, near):
        for hh in heads:
            s, _ = scores(hh, j, n, near)
            m_ref[hh] = functools.reduce(jnp.maximum, lane_chunks(s), m_ref[hh])

    sweep(visit_max)
    m = [jnp.max(m_ref[hh], axis=-1, keepdims=True) for hh in heads]

    l_ref[...] = jnp.zeros_like(l_ref)
    acc_ref[...] = jnp.zeros_like(acc_ref)

    def visit_sum(j, n, near):
        scored = [scores(hh, j, n, near) for hh in heads]
        ps = [jnp.exp(s - m[hh]) for hh, (s, _) in zip(heads, scored)]
        for hh in heads:
            l_ref[hh] = functools.reduce(jnp.add, lane_chunks(ps[hh]), l_ref[hh])
        pv = [_dot(ps[hh].astype(BF16), v_ref[0, scored[hh][1], lanes(hh)]) for hh in heads]
        for hh in heads:
            acc_ref[hh] += pv[hh]

    sweep(visit_sum)
    lam = _lambda(lq1, lk1, lq2, lk2, lam_init)
    for hh in heads:
        l = jnp.sum(l_ref[hh], axis=-1, keepdims=True)
        o_ref[0, :, lanes(hh)] = _df_finish(acc_ref[hh], l, blk, lam, g_ref[...], lam_init).astype(o_ref.dtype)


def _df_prompt(rel_bias, q_b, k_b, v_b, bias_p, g, lams, blk, pre, lam_init):
    b, tp, width = q_b.shape
    hps = DF_HEADS_PER_STEP
    w = hps * LANES
    grid = (b, width // w, tp // blk - 1)
    vec = pl.BlockSpec((1, DH), lambda bi, h, i: (0, 0))
    tile = lambda bi, h, i: (h, jnp.where(i == 0, 1, 0), 0, 0)
    return pl.pallas_call(
        functools.partial(_df_prompt_body, blk=blk, pre=pre, lam_init=lam_init,
                          n_buckets=rel_bias.shape[0]),
        grid=grid,
        in_specs=[
            pl.BlockSpec(memory_space=pltpu.SMEM),
            pl.BlockSpec((1, blk, w), lambda bi, h, i: (bi, i + 1, h)),
            pl.BlockSpec((1, tp, w), lambda bi, h, i: (bi, 0, h)),
            pl.BlockSpec((1, tp, w), lambda bi, h, i: (bi, 0, h)),
            pl.BlockSpec((hps, 1, blk, 2 * blk), tile),
            pl.BlockSpec((1, LANES), lambda bi, h, i: (0, 0)),
            vec, vec, vec, vec,
        ],
        out_specs=pl.BlockSpec((1, blk, w), lambda bi, h, i: (bi, i, h)),
        out_shape=jax.ShapeDtypeStruct((b, tp - blk, width), BF16),
        scratch_shapes=[pltpu.VMEM((hps, 2 * blk, LANES), F32)] * 3,
        compiler_params=pltpu.CompilerParams(
            dimension_semantics=("arbitrary", "arbitrary", "arbitrary"),
            vmem_limit_bytes=VMEM_LIMIT),
        name="df_prompt",
    )(rel_bias, q_b, k_b, v_b, bias_p, g, *lams)


def _stack_decode_queries(q, n_blocks):
    q = q.astype(F32)
    return jnp.concatenate(
        [_stack_halves(q[:, p * LANES:(p + 1) * LANES]) for p in range(n_blocks)], axis=0).astype(BF16)


def _sb_decode_body(pt_ref, q_ref, kn_ref, vn_ref, kc_hbm, vc_hbm, tri_ref, g_ref, o_ref,
                    kbuf, vbuf, sem, *, dec_t, n_blocks, page, n_pages):
    bi = pl.program_id(0)
    rows = 2 * dec_t
    qq = _stack_decode_queries(q_ref[0], n_blocks)
    tri = tri_ref[...]

    def page_copies(slot, jj):
        pg = pt_ref[bi, n_pages - 1 - jj]
        return (pltpu.make_async_copy(kc_hbm.at[pg], kbuf.at[slot], sem.at[0, slot]),
                pltpu.make_async_copy(vc_hbm.at[pg], vbuf.at[slot], sem.at[1, slot]))

    def start(slot, jj):
        for cp in page_copies(slot, jj):
            cp.start()

    def wait(slot, jj):
        for cp in page_copies(slot, jj):
            cp.wait()

    start(0, 0)

    def weights(z, carry, vis):
        drop = _softplus(z)
        if vis is not None:
            drop = jnp.where(vis, drop, 0.0)
        csum = carry - _dot(drop.astype(BF16), tri)
        w = jnp.exp(z + csum)
        if vis is not None:
            w = jnp.where(vis, w, 0.0)
        return w.astype(BF16), csum[:, 0:1]

    r = lax.broadcasted_iota(I32, (dec_t, page), 0)
    c = lax.broadcasted_iota(I32, (dec_t, page), 1)
    vis = jnp.concatenate([c < r] * (2 * n_blocks), axis=0)
    z = jnp.concatenate(
        [_dot_tb(qq[p * rows:(p + 1) * rows], kn_ref[0, :, p * LANES:(p + 1) * LANES].astype(BF16))
         for p in range(n_blocks)], axis=0)
    w, carry = weights(z, jnp.zeros((n_blocks * rows, 1), F32), vis)
    acc = jnp.concatenate(
        [_dot(w[p * rows:(p + 1) * rows], vn_ref[0, :, p * LANES:(p + 1) * LANES].astype(BF16))
         for p in range(n_blocks)], axis=0)

    def more(state):
        jj, _, carry = state
        return (jj < n_pages) & (jnp.max(carry) > SB_CUTOFF)

    def body(state):
        jj, acc, carry = state
        slot = jj & 1
        wait(slot, jj)

        @pl.when(jj + 1 < n_pages)
        def _():
            start(1 - slot, jj + 1)

        kt = kbuf[slot]
        vt = vbuf[slot]
        z = jnp.concatenate(
            [_dot(qq[p * rows:(p + 1) * rows], kt[p * LANES:(p + 1) * LANES, :].astype(BF16))
             for p in range(n_blocks)], axis=0)
        w, carry = weights(z, carry, None)
        acc = acc + jnp.concatenate(
            [_dot_tb(w[p * rows:(p + 1) * rows], vt[p * LANES:(p + 1) * LANES, :].astype(BF16))
             for p in range(n_blocks)], axis=0)
        return jj + 1, acc, carry

    jj, acc, _ = lax.while_loop(more, body, (jnp.int32(0), acc, carry))

    @pl.when(jj < n_pages)
    def _():
        wait(jj & 1, jj)

    for p in range(n_blocks):
        o_ref[0, :, p * LANES:(p + 1) * LANES] = _sb_finish(
            acc[p * rows:(p + 1) * rows, :], dec_t, g_ref[:, p * LANES:(p + 1) * LANES])


def _sb_decode(page_table, q, k_new, v_new, kt_cache, vt_cache, tri, g):
    b, dec_t, width = q.shape
    n_pages = page_table.shape[1]
    page = kt_cache.shape[2]
    n_blocks = width // LANES
    per_b = lambda bi, pt: (bi, 0, 0)
    const = lambda bi, pt: (0, 0)
    grid_spec = pltpu.PrefetchScalarGridSpec(
        num_scalar_prefetch=1,
        grid=(b,),
        in_specs=[
            pl.BlockSpec((1, dec_t, width), per_b),
            pl.BlockSpec((1, page, width), per_b),
            pl.BlockSpec((1, page, width), per_b),
            pl.BlockSpec(memory_space=pl.ANY),
            pl.BlockSpec(memory_space=pl.ANY),
            pl.BlockSpec((page, page), const),
            pl.BlockSpec((1, width), const),
        ],
        out_specs=pl.BlockSpec((1, dec_t, width), per_b),
        scratch_shapes=[pltpu.VMEM((2, width, page), F32),
                        pltpu.VMEM((2, width, page), F32),
                        pltpu.SemaphoreType.DMA((2, 2))],
    )
    return pl.pallas_call(
        functools.partial(_sb_decode_body, dec_t=dec_t, n_blocks=n_blocks, page=page, n_pages=n_pages),
        grid_spec=grid_spec,
        out_shape=jax.ShapeDtypeStruct((b, dec_t, width), F32),
        compiler_params=pltpu.CompilerParams(
            dimension_semantics=("arbitrary",), vmem_limit_bytes=VMEM_LIMIT),
        name="sb_decode",
    )(page_table, q, k_new, v_new, kt_cache, vt_cache, tri, g)


def _df_decode_body(pt_ref, q_ref, kn_ref, vn_ref, *rest, dec_t, n_blocks, n_slots, lam_init):
    kc_refs, vc_refs = rest[:n_slots], rest[n_slots:2 * n_slots]
    (bias_ref, bias_new_ref, g_ref, lq1, lk1, lq2, lk2, o_ref,
     qq_ref, m_ref, l_ref, acc_ref) = rest[2 * n_slots:]
    jj = pl.program_id(1)
    rows = 2 * dec_t
    page = bias_new_ref.shape[1]

    def update(k_refs, v_refs, bias):
        own = [pl.ds(h, page, stride=n_blocks) for h in range(n_blocks)]
        gather = lambda refs, h: jnp.concatenate([r[0, own[h], :].astype(BF16) for r in refs], axis=0)
        qq = qq_ref[...]
        s = jnp.concatenate(
            [_dot_tb(qq[h * rows:(h + 1) * rows], gather(k_refs, h)) for h in range(n_blocks)],
            axis=0) + bias[...]
        m_old = m_ref[...]
        m_new = jnp.maximum(m_old, jnp.max(s, axis=-1, keepdims=True))
        alpha = jnp.exp(m_old - m_new)
        p = jnp.exp(s - m_new)
        l_ref[...] = alpha * l_ref[...] + jnp.sum(p, axis=-1, keepdims=True)
        m_ref[...] = m_new
        p = p.astype(BF16)
        pv = jnp.concatenate(
            [_dot(p[h * rows:(h + 1) * rows], gather(v_refs, h)) for h in range(n_blocks)], axis=0)
        acc_ref[...] = alpha * acc_ref[...] + pv

    @pl.when(jj == 0)
    def _():
        qq_ref[...] = _stack_decode_queries(q_ref[0], n_blocks)
        m_ref[...] = jnp.full(m_ref.shape, NEG, F32)
        l_ref[...] = jnp.zeros_like(l_ref)
        acc_ref[...] = jnp.zeros_like(acc_ref)
        update([kn_ref], [vn_ref], bias_new_ref)

    update(kc_refs, vc_refs, bias_ref.at[0])

    @pl.when(jj == pl.num_programs(1) - 1)
    def _():
        lam = _lambda(lq1, lk1, lq2, lk2, lam_init)
        for p in range(n_blocks):
            sl = slice(p * rows, (p + 1) * rows)
            o_ref[0, :, p * LANES:(p + 1) * LANES] = _df_finish(
                acc_ref[sl, :], l_ref[sl, :], dec_t, lam, g_ref[...], lam_init)


def _df_decode(page_table, q, k_new, v_new, k_cache, v_cache, bias_s, bias_new, g, lams, lam_init,
               n_slots):
    b, dec_t, width = q.shape
    n_pages = page_table.shape[1]
    page_rows = k_cache.shape[1]
    n_blocks = width // LANES
    rows = n_blocks * 2 * dec_t
    n_steps = n_pages // n_slots
    per_b = lambda bi, jj, pt: (bi, 0, 0)
    const = lambda bi, jj, pt: (0, 0)
    vec = pl.BlockSpec((1, DH), const)

    def slot_spec(u):
        return pl.BlockSpec((1, page_rows, LANES), lambda bi, jj, pt: (pt[bi, jj * n_slots + u], 0, 0))

    grid_spec = pltpu.PrefetchScalarGridSpec(
        num_scalar_prefetch=1,
        grid=(b, n_steps),
        in_specs=[
            pl.BlockSpec((1, dec_t, width), per_b),
            pl.BlockSpec((1,) + k_new.shape[1:], per_b),
            pl.BlockSpec((1,) + v_new.shape[1:], per_b),
            *[slot_spec(u) for u in range(n_slots)],
            *[slot_spec(u) for u in range(n_slots)],
            pl.BlockSpec((1, rows, bias_s.shape[2]),
                         lambda bi, jj, pt: (jnp.where(jj == n_steps - 1, 1, 0), 0, 0)),
            pl.BlockSpec(bias_new.shape, const),
            pl.BlockSpec((1, LANES), const),
            vec, vec, vec, vec,
        ],
        out_specs=pl.BlockSpec((1, dec_t, width), per_b),
        scratch_shapes=[pltpu.VMEM((rows, LANES), BF16),
                        pltpu.VMEM((rows, 1), F32),
                        pltpu.VMEM((rows, 1), F32),
                        pltpu.VMEM((rows, LANES), F32)],
    )
    return pl.pallas_call(
        functools.partial(_df_decode_body, dec_t=dec_t, n_blocks=n_blocks, n_slots=n_slots,
                          lam_init=lam_init),
        grid_spec=grid_spec,
        out_shape=jax.ShapeDtypeStruct((b, dec_t, width), F32),
        compiler_params=pltpu.CompilerParams(
            dimension_semantics=("arbitrary", "arbitrary"),
            vmem_limit_bytes=VMEM_LIMIT),
        name="df_decode",
    )(page_table, q, k_new, v_new, *([k_cache] * n_slots), *([v_cache] * n_slots),
      bias_s, bias_new, g, *lams)


def _router_gates(logits_t, bias_col, n_experts):
    tm = logits_t.shape[1]
    gsz = n_experts // N_GROUPS
    scores = 1.0 / (1.0 + jnp.exp(-logits_t))
    sel = scores + bias_col
    sub = lax.broadcasted_iota(I32, (gsz, tm), 0)
    group_scores = []
    for g in range(N_GROUPS):
        blk = sel[g * gsz:(g + 1) * gsz]
        m1 = jnp.max(blk, axis=0, keepdims=True)
        first = jnp.min(jnp.where(blk == m1, sub, gsz), axis=0, keepdims=True)
        m2 = jnp.max(jnp.where(sub == first, -jnp.inf, blk), axis=0, keepdims=True)
        group_scores.append(m1 + m2)
    gs = jnp.concatenate(group_scores, axis=0)
    gidx = lax.broadcasted_iota(I32, (N_GROUPS, tm), 0)
    grank = jnp.zeros((N_GROUPS, tm), I32)
    for g in range(N_GROUPS):
        row = gs[g:g + 1]
        ahead = (row > gs) | ((row == gs) & (gidx > g))
        grank = grank + ahead.astype(I32)
    gkeep = grank < TOPK_GROUPS
    masked = jnp.concatenate(
        [jnp.where(jnp.broadcast_to(gkeep[g:g + 1], (gsz, tm)), sel[g * gsz:(g + 1) * gsz], -jnp.inf)
         for g in range(N_GROUPS)], axis=0)
    eidx = lax.broadcasted_iota(I32, (n_experts, tm), 0)
    erank = jnp.zeros((n_experts, tm), I32)
    for e in range(n_experts):
        row = masked[e:e + 1]
        ahead = (row > masked) | ((row == masked) & (eidx > e))
        erank = erank + ahead.astype(I32)
    w = jnp.where(erank < TOP_K, scores, 0.0)
    return w / jnp.sum(w, axis=0, keepdims=True) * ROUTED_SCALE


def _post_body(msb_ref, mdf_ref, x_ref, wo_sb_ref, wo_df_ref, g2_ref, rw_hi_ref, rw_lo_ref,
               rb_ref, x1_ref, h2_ref, gates_ref, *, n_experts):
    att = _dot(msb_ref[0].astype(BF16), wo_sb_ref[...]) + _dot(mdf_ref[0].astype(BF16), wo_df_ref[...])
    x1 = x_ref[0] + att
    x1_ref[...] = x1
    ms = jnp.mean(x1 * x1, axis=-1, keepdims=True)
    h2 = (x1 * lax.rsqrt(ms + EPS)) * g2_ref[...]
    h2_hi, h2_lo = _split_bf16(h2)
    h2_ref[...] = h2_hi
    rw_hi = rw_hi_ref[...]
    logits_t = _dot_tb(rw_hi, h2_hi) + _dot_tb(rw_hi, h2_lo) + _dot_tb(rw_lo_ref[...], h2_hi)
    gates_t = _router_gates(logits_t, rb_ref[...], n_experts)
    tm = gates_t.shape[1]
    pad = gates_ref.shape[1] - n_experts
    gates_t = jnp.concatenate([gates_t, jnp.zeros((pad, tm), F32)], axis=0)
    gates_ref[...] = gates_t.T.astype(gates_ref.dtype)


def _post_attention(msb, mdf, x, wo_sb, wo_df, g2, rw_hi, rw_lo, rb, tm):
    b, s, d = x.shape
    n = b * s
    per_b = s // tm
    n_experts = rw_hi.shape[0]
    ge = _round_up(n_experts, LANES)
    row = lambda bi, i: (bi * per_b + i, 0)
    const = lambda bi, i: (0, 0)
    return pl.pallas_call(
        functools.partial(_post_body, n_experts=n_experts),
        grid=(b, per_b),
        in_specs=[
            pl.BlockSpec((1, tm, msb.shape[2]), lambda bi, i: (bi, i, 0)),
            pl.BlockSpec((1, tm, mdf.shape[2]), lambda bi, i: (bi, i, 0)),
            pl.BlockSpec((1, tm, d), lambda bi, i: (bi, i, 0)),
            pl.BlockSpec(wo_sb.shape, const),
            pl.BlockSpec(wo_df.shape, const),
            pl.BlockSpec((1, d), const),
            pl.BlockSpec(rw_hi.shape, const),
            pl.BlockSpec(rw_lo.shape, const),
            pl.BlockSpec((n_experts, 1), const),
        ],
        out_specs=[pl.BlockSpec((tm, d), row), pl.BlockSpec((tm, d), row), pl.BlockSpec((tm, ge), row)],
        out_shape=[jax.ShapeDtypeStruct((n, d), F32), jax.ShapeDtypeStruct((n, d), BF16),
                   jax.ShapeDtypeStruct((n, ge), F32)],
        compiler_params=pltpu.CompilerParams(
            dimension_semantics=("arbitrary", "arbitrary"), vmem_limit_bytes=VMEM_LIMIT),
        name="post_attention",
    )(msb, mdf, x, wo_sb, wo_df, g2, rw_hi, rw_lo, rb)


def _silu(x):
    return x / (1.0 + jnp.exp(-x))


def _moe_body(h2_ref, x1_ref, gates_ref, wg_ref, wu_ref, wd_ref, sg_ref, su_ref, sd_ref,
              o_ref, acc_ref, *, group):
    eg = pl.program_id(1)
    h2 = h2_ref[...]

    @pl.when(eg == 0)
    def _():
        a = _silu(_dot(h2, sg_ref[...])) * _dot(h2, su_ref[...])
        acc_ref[...] = _dot(a.astype(BF16), sd_ref[...])

    f = wg_ref.shape[2]
    gates = gates_ref[...]
    lane = lax.broadcasted_iota(I32, gates.shape, 1)
    acts = []
    for j in range(group):
        gate = jnp.sum(jnp.where(lane == eg * group + j, gates, 0.0), axis=-1, keepdims=True)
        acts.append((_silu(_dot(h2, wg_ref[j])) * _dot(h2, wu_ref[j]) * gate).astype(BF16))
    a = jnp.concatenate(acts, axis=1)
    wd = wd_ref[...].reshape(group * f, wd_ref.shape[2])
    acc_ref[...] += _dot(a, wd)

    @pl.when(eg == pl.num_programs(1) - 1)
    def _():
        o_ref[...] = x1_ref[...] + acc_ref[...]


def _moe(h2, x1, gates, w_gate, w_up, w_down, sg, su, sd, tm, group):
    n, d = x1.shape
    n_experts, _, f = w_gate.shape
    row = lambda i, e: (i, 0)
    const = lambda i, e: (0, 0)
    return pl.pallas_call(
        functools.partial(_moe_body, group=group),
        grid=(n // tm, n_experts // group),
        in_specs=[
            pl.BlockSpec((tm, d), row),
            pl.BlockSpec((tm, d), row),
            pl.BlockSpec((tm, gates.shape[1]), row),
            pl.BlockSpec((group, d, f), lambda i, e: (e, 0, 0)),
            pl.BlockSpec((group, d, f), lambda i, e: (e, 0, 0)),
            pl.BlockSpec((group, f, d), lambda i, e: (e, 0, 0)),
            pl.BlockSpec(sg.shape, const),
            pl.BlockSpec(su.shape, const),
            pl.BlockSpec(sd.shape, const),
        ],
        out_specs=pl.BlockSpec((tm, d), row),
        out_shape=jax.ShapeDtypeStruct((n, d), F32),
        scratch_shapes=[pltpu.VMEM((tm, d), F32)],
        compiler_params=pltpu.CompilerParams(
            dimension_semantics=("arbitrary", "arbitrary"), vmem_limit_bytes=VMEM_LIMIT),
        name="moe",
    )(h2, x1, gates, w_gate, w_up, w_down, sg, su, sd)


def _largest_tile(n, cap):
    t = cap
    while n % t:
        t //= 2
    return t


def kernel(x_prompt, x_sample, cache_k_sb, cache_v_sb, cache_k_diff, cache_v_diff, page_table,
           meta_tokens, rel_bias, norm1_g, w_in, q_norm_g, k_norm_g, lambda_q1, lambda_k1,
           lambda_q2, lambda_k2, sb_out_g, diff_subln_g, w_out, norm2_g, router_w, router_bias,
           w_gate, w_up, w_down, w_shared_gate, w_shared_up, w_shared_down):
    assert w_in.shape[0] == 1, "single-layer step"
    b, seq, d = x_prompt.shape
    dec_b, dec_t, _ = x_sample.shape
    n_meta = meta_tokens.shape[0]
    width = sb_out_g.shape[1]
    n_pool, page = cache_k_sb.shape[1], cache_k_sb.shape[2]
    n_experts = router_w.shape[2]
    t = seq + n_meta
    blk = ATT_BLOCK
    lam_init = 0.8 - 0.6 * math.exp(-0.3 * 0)

    w_in_b = w_in[0].astype(BF16)
    g1 = norm1_g
    reps = width // DH
    qg = jnp.tile(q_norm_g, (1, reps))
    kg = jnp.tile(k_norm_g, (1, reps))
    lane = jnp.arange(width)
    gm = ((lane[:, None] // DH) == (lane[None, :] // DH)).astype(BF16) * (1.0 / DH)
    tri_p = (jnp.arange(blk)[:, None] >= jnp.arange(blk)[None, :]).astype(BF16)
    tri_d = (jnp.arange(page)[:, None] >= jnp.arange(page)[None, :]).astype(BF16)
    lams = (lambda_q1, lambda_k1, lambda_q2, lambda_k2)
    wo_sb = w_out[0, :width].astype(BF16)
    wo_df = w_out[0, width:].astype(BF16)
    rw_t = router_w[0].T
    rw_hi = rw_t.astype(BF16)
    rw_lo = (rw_t - rw_hi.astype(F32)).astype(BF16)
    rb = router_bias[0][:, None]
    sg = w_shared_gate[0].astype(BF16)
    su = w_shared_up[0].astype(BF16)
    sd = w_shared_down[0].astype(BF16)

    n_h_sb = width // DH
    n_h_df = width // LANES
    n_slots = DECODE_PAGES_PER_STEP
    assert blk >= MAX_DISTANCE and page >= MAX_DISTANCE
    assert page_table.shape[1] % n_slots == 0 and dec_t * n_h_df <= page
    pre = blk - n_meta
    assert 0 <= pre < blk and seq % blk == 0
    bias_p, bias_s, bias_n = _bias_tiles(rel_bias, blk, pre, dec_t, page, n_slots)
    head = jnp.concatenate([jnp.zeros((pre, d), F32), meta_tokens], axis=0)
    (k_sb_p, v_sb_p, k_df_p, v_df_p,
     qsb_b, ksb_b, vsb_b, qdf_b, kdf_b, vdf_b) = _project_prompt(head, x_prompt, n_meta, g1, w_in_b, qg, kg, gm)
    msb_p = _sb_prompt(qsb_b, ksb_b, vsb_b, tri_p, sb_out_g, blk, pre)
    mdf_p = _df_prompt(rel_bias, qdf_b, kdf_b, vdf_b, bias_p, diff_subln_g, lams, blk, pre, lam_init)

    n_s = dec_b * dec_t
    xs = x_sample.reshape(1, n_s, d)
    (k_sb_s, v_sb_s, k_df_s, v_df_s,
     qsb_s, _, _, qdf_s, _, _) = _project(xs, n_s, g1, w_in_b, qg, kg, gm, n_s)

    def new_page(rows, n_rows, n_pad):
        r = rows.reshape(dec_b, n_rows, -1)
        return jnp.pad(r, ((0, 0), (0, n_pad - n_rows), (0, 0)))

    transposed_pages = lambda c: jnp.transpose(c[0], (0, 2, 3, 1)).reshape(n_pool, width, page)
    interleaved_rows = lambda c: c.reshape(n_pool, page * n_h_df, LANES)
    msb_s = _sb_decode(page_table, qsb_s.reshape(dec_b, dec_t, width),
                       new_page(k_sb_s, dec_t, page), new_page(v_sb_s, dec_t, page),
                       transposed_pages(cache_k_sb), transposed_pages(cache_v_sb), tri_d, sb_out_g)
    mdf_s = _df_decode(page_table, qdf_s.reshape(dec_b, dec_t, width),
                       new_page(k_df_s, dec_t * n_h_df, page * n_h_df),
                       new_page(v_df_s, dec_t * n_h_df, page * n_h_df),
                       interleaved_rows(cache_k_diff), interleaved_rows(cache_v_diff),
                       bias_s, bias_n, diff_subln_g, lams, lam_init, n_slots)

    wg_b, wu_b, wd_b = w_gate[0].astype(BF16), w_up[0].astype(BF16), w_down[0].astype(BF16)
    x1_p, h2_p, gates_p = _post_attention(msb_p, mdf_p, x_prompt, wo_sb, wo_df, norm2_g,
                                          rw_hi, rw_lo, rb, _largest_tile(seq, 512))
    y_p = _moe(h2_p, x1_p, gates_p, wg_b, wu_b, wd_b, sg, su, sd,
               _largest_tile(b * seq, MOE_TOKEN_TILE), MOE_EXPERT_GROUP)
    y_prompt = y_p.reshape(b, seq, d)
    x1_s, h2_s, gates_s = _post_attention(
        msb_s.reshape(1, n_s, width), mdf_s.reshape(1, n_s, width), x_sample.reshape(1, n_s, d),
        wo_sb, wo_df, norm2_g, rw_hi, rw_lo, rb, n_s)
    y_s = _moe(h2_s, x1_s, gates_s, wg_b, wu_b, wd_b, sg, su, sd, n_s, MOE_EXPERT_GROUP)
    y_sample = y_s.reshape(dec_b, dec_t, d)

    sb_rows = lambda r, bb, tt: r.reshape(1, bb, tt, n_h_sb, DH)
    df_rows = lambda r, bb, tt: r.reshape(1, bb, tt, n_h_df, LANES)
    return (y_prompt, y_sample,
            sb_rows(k_sb_p, b, t), sb_rows(v_sb_p, b, t), df_rows(k_df_p, b, t), df_rows(v_df_p, b, t),
            sb_rows(k_sb_s, dec_b, dec_t), sb_rows(v_sb_s, dec_b, dec_t),
            df_rows(k_df_s, dec_b, dec_t), df_rows(v_df_s, dec_b, dec_t))
```

```python
import functools
import math

import jax
import jax.numpy as jnp
from jax import lax
from jax.experimental import pallas as pl
from jax.experimental.pallas import tpu as pltpu

F32 = jnp.float32
BF16 = jnp.bfloat16
I32 = jnp.int32

EPS = 1e-6
DH = 64
LANES = 128
MAX_DISTANCE = 128
TOP_K = 8
N_GROUPS = 8
TOPK_GROUPS = 4
ROUTED_SCALE = 2.5
NEG = -1e30
SB_CUTOFF = -104.0
ATT_BLOCK = 256
DF_HEADS_PER_STEP = 4
SB_PAIRS_PER_STEP = 4
DECODE_PAGES_PER_STEP = 32
MOE_TOKEN_TILE = 1024
MOE_EXPERT_GROUP = 8
VMEM_LIMIT = 60 * 1024 * 1024

_TRANS_B = (((1,), (1,)), ((), ()))


def _dot(a, b):
    return jnp.dot(a, b, preferred_element_type=F32)


def _dot_tb(a, b):
    return lax.dot_general(a, b, _TRANS_B, preferred_element_type=F32)


def _split_bf16(x):
    hi = x.astype(BF16)
    lo = (x - hi.astype(F32)).astype(BF16)
    return hi, lo


def _round_up(n, m):
    return (n + m - 1) // m * m


def _proj_rows(x, g1_ref, w_ref, qg_ref, kg_ref, gm_ref, ksb_o, vsb_o, kdf_o, vdf_o,
               qsb_b, ksb_b, vsb_b, qdf_b, kdf_b, vdf_b, width):
    ms = jnp.mean(x * x, axis=-1, keepdims=True)
    h = (x * lax.rsqrt(ms + EPS)) * g1_ref[...]
    proj = _dot(h.astype(BF16), w_ref[...])
    w = width
    q_sb, k_sb, v_sb = proj[:, 0:w], proj[:, w:2 * w], proj[:, 2 * w:3 * w]
    q_df, k_df, v_df = proj[:, 3 * w:4 * w], proj[:, 4 * w:5 * w], proj[:, 5 * w:6 * w]

    def map_norm(t, g_ref):
        msq = _dot((t * t).astype(BF16), gm_ref[...])
        return (t * lax.rsqrt(msq + EPS)) * g_ref[...]

    q_df = map_norm(q_df, qg_ref)
    k_df = map_norm(k_df, kg_ref)
    ksb_o[...] = k_sb
    vsb_o[...] = v_sb
    tm = x.shape[0]
    n_df = w // LANES
    for hd in range(n_df):
        rows = pl.ds(hd, tm, stride=n_df)
        kdf_o[rows, :] = k_df[:, hd * LANES:(hd + 1) * LANES]
        vdf_o[rows, :] = v_df[:, hd * LANES:(hd + 1) * LANES]
    scale = DH ** -0.5
    qsb_b[0] = (q_sb * scale).astype(BF16)
    ksb_b[0] = k_sb.astype(BF16)
    vsb_b[0] = v_sb.astype(BF16)
    qdf_b[0] = (q_df * scale).astype(BF16)
    kdf_b[0] = k_df.astype(BF16)
    vdf_b[0] = v_df.astype(BF16)


def _proj_body(x_ref, g1_ref, w_ref, qg_ref, kg_ref, gm_ref,
               ksb_o, vsb_o, kdf_o, vdf_o, *bf_outs, width):
    _proj_rows(x_ref[0], g1_ref, w_ref, qg_ref, kg_ref, gm_ref,
               ksb_o.at[0], vsb_o.at[0], kdf_o.at[0], vdf_o.at[0], *bf_outs, width)


def _proj_prompt_body(head_ref, x_ref, g1_ref, w_ref, qg_ref, kg_ref, gm_ref,
                      ksb_hbm, vsb_hbm, kdf_hbm, vdf_hbm, *rest, width, lead):
    bf_outs, (st_sb, st_df, sem) = rest[:6], rest[6:]
    b, i = pl.program_id(0), pl.program_id(1)
    n_i = pl.num_programs(1)
    step = b * n_i + i
    slot = step & 1
    tm = x_ref.shape[1]
    n_df = width // LANES

    def copies(slot, bb, ii, head):
        if head:
            src, dst, n = tm - lead, 0, lead
        else:
            src, dst, n = 0, lead + (ii - 1) * tm, tm
        out = []
        for kv, hbm in enumerate((ksb_hbm, vsb_hbm)):
            out.append(pltpu.make_async_copy(st_sb.at[slot, kv, pl.ds(src, n), :],
                                             hbm.at[bb, pl.ds(dst, n), :], sem.at[slot]))
        for kv, hbm in enumerate((kdf_hbm, vdf_hbm)):
            out.append(pltpu.make_async_copy(st_df.at[slot, kv, pl.ds(src * n_df, n * n_df), :],
                                             hbm.at[bb, pl.ds(dst * n_df, n * n_df), :], sem.at[slot]))
        return out

    def wait_step(slot, was_head):
        @pl.when(was_head)
        def _():
            for cp in copies(slot, 0, 1, True):
                cp.wait()

        @pl.when(jnp.logical_not(was_head))
        def _():
            for cp in copies(slot, 0, 1, False):
                cp.wait()

    @pl.when(step >= 2)
    def _():
        wait_step(slot, i == 2)

    x = jnp.where(i == 0, head_ref[...], x_ref[0])
    _proj_rows(x, g1_ref, w_ref, qg_ref, kg_ref, gm_ref,
               st_sb.at[slot, 0], st_sb.at[slot, 1], st_df.at[slot, 0], st_df.at[slot, 1],
               *bf_outs, width)

    @pl.when(i == 0)
    def _():
        for cp in copies(slot, b, i, True):
            cp.start()

    @pl.when(i > 0)
    def _():
        for cp in copies(slot, b, i, False):
            cp.start()

    @pl.when(step == pl.num_programs(0) * n_i - 1)
    def _():
        wait_step(1 - slot, i == 1)
        wait_step(slot, i == 0)


def _project(x_pad, t_valid, g1, w_in_b, qg, kg, gm, tm):
    b, tp, d = x_pad.shape
    width = w_in_b.shape[1] // 6
    grid = (b, tp // tm)
    row = lambda bi, i: (bi, i, 0)
    const = lambda bi, i: (0, 0)
    n_df = width // LANES
    f32_out = jax.ShapeDtypeStruct((b, t_valid, width), F32)
    df_out = jax.ShapeDtypeStruct((b, t_valid * n_df, LANES), F32)
    bf_out = jax.ShapeDtypeStruct((b, tp, width), BF16)
    out_spec = pl.BlockSpec((1, tm, width), row)
    df_spec = pl.BlockSpec((1, tm * n_df, LANES), row)
    return pl.pallas_call(
        functools.partial(_proj_body, width=width),
        grid=grid,
        in_specs=[
            pl.BlockSpec((1, tm, d), row),
            pl.BlockSpec((1, d), const),
            pl.BlockSpec(w_in_b.shape, const),
            pl.BlockSpec((1, width), const),
            pl.BlockSpec((1, width), const),
            pl.BlockSpec((width, width), const),
        ],
        out_specs=[out_spec] * 2 + [df_spec] * 2 + [out_spec] * 6,
        out_shape=[f32_out] * 2 + [df_out] * 2 + [bf_out] * 6,
        compiler_params=pltpu.CompilerParams(
            dimension_semantics=("arbitrary", "arbitrary"),
            vmem_limit_bytes=VMEM_LIMIT),
        name="proj",
    )(x_pad, g1, w_in_b, qg, kg, gm)


def _project_prompt(head, x, lead, g1, w_in_b, qg, kg, gm):
    b, s, d = x.shape
    tm = head.shape[0]
    width = w_in_b.shape[1] // 6
    n_df = width // LANES
    n_i = 1 + s // tm
    assert s % tm == 0 and n_i >= 3 and lead % 8 == 0 and 0 < lead <= tm
    row = lambda bi, i: (bi, i, 0)
    const = lambda bi, i: (0, 0)
    anywhere = pl.BlockSpec(memory_space=pl.ANY)
    bf_out = jax.ShapeDtypeStruct((b, tm + s, width), BF16)
    return pl.pallas_call(
        functools.partial(_proj_prompt_body, width=width, lead=lead),
        grid=(b, n_i),
        in_specs=[
            pl.BlockSpec((tm, d), const),
            pl.BlockSpec((1, tm, d), lambda bi, i: (bi, jnp.maximum(i - 1, 0), 0)),
            pl.BlockSpec((1, d), const),
            pl.BlockSpec(w_in_b.shape, const),
            pl.BlockSpec((1, width), const),
            pl.BlockSpec((1, width), const),
            pl.BlockSpec((width, width), const),
        ],
        out_specs=[anywhere] * 4 + [pl.BlockSpec((1, tm, width), row)] * 6,
        out_shape=[jax.ShapeDtypeStruct((b, lead + s, width), F32)] * 2
        + [jax.ShapeDtypeStruct((b, (lead + s) * n_df, LANES), F32)] * 2 + [bf_out] * 6,
        scratch_shapes=[pltpu.VMEM((2, 2, tm, width), F32),
                        pltpu.VMEM((2, 2, tm * n_df, LANES), F32),
                        pltpu.SemaphoreType.DMA((2,))],
        compiler_params=pltpu.CompilerParams(
            dimension_semantics=("arbitrary", "arbitrary"),
            vmem_limit_bytes=VMEM_LIMIT),
        name="proj_prompt",
    )(head, x, g1, w_in_b, qg, kg, gm)


def _stack_halves(q):
    lane = lax.broadcasted_iota(I32, (1, LANES), 1)
    zero = jnp.zeros_like(q)
    return jnp.concatenate(
        [jnp.where(lane < DH, q, zero), jnp.where(lane >= DH, q, zero)], axis=0)


def _softplus(z):
    return jnp.maximum(z, 0.0) + jnp.log(1.0 + jnp.exp(-jnp.abs(z)))


def _sb_blocks(qqs, ks, vs, tri, state, vis):
    n = range(len(qqs))
    zs = [_dot_tb(qqs[p], ks[p]) for p in n]
    drops = [_softplus(z) for z in zs]
    if vis is not None:
        drops = [jnp.where(vis, d, 0.0) for d in drops]
    csums = [state[p][1] - _dot(drops[p].astype(BF16), tri) for p in n]
    ws = [jnp.exp(zs[p] + csums[p]) for p in n]
    if vis is not None:
        ws = [jnp.where(vis, w, 0.0) for w in ws]
    pvs = [_dot(ws[p].astype(BF16), vs[p]) for p in n]
    return [(state[p][0] + pvs[p], csums[p][:, 0:1]) for p in n]


def _sb_finish(acc, t, g):
    lane = lax.broadcasted_iota(I32, (1, LANES), 1)
    lo_half = lane < DH
    o = jnp.where(lo_half, acc[:t], acc[t:])
    o2 = o * o
    s_lo = jnp.sum(jnp.where(lo_half, o2, 0.0), axis=-1, keepdims=True)
    s_hi = jnp.sum(jnp.where(lo_half, 0.0, o2), axis=-1, keepdims=True)
    ms = jnp.where(lo_half, s_lo, s_hi) * (1.0 / DH)
    return (o * lax.rsqrt(ms + EPS)) * g


def _df_block(qq, k, v, bias, m, l, acc):
    s = _dot_tb(qq, k) + bias
    m_new = jnp.maximum(m, jnp.max(s, axis=-1, keepdims=True))
    alpha = jnp.exp(m - m_new)
    p = jnp.exp(s - m_new)
    l = alpha * l + jnp.sum(p, axis=-1, keepdims=True)
    acc = alpha * acc + _dot(p.astype(BF16), v)
    return m_new, l, acc


def _lambda(lq1, lk1, lq2, lk2, lam_init):
    s1 = jnp.sum(lq1[...] * lk1[...], axis=-1, keepdims=True)
    s2 = jnp.sum(lq2[...] * lk2[...], axis=-1, keepdims=True)
    return jnp.exp(s1) - jnp.exp(s2) + lam_init


def _df_finish(acc, l, t, lam, g, lam_init):
    o = acc[:t] / l[:t] - lam * (acc[t:] / l[t:])
    ms = jnp.mean(o * o, axis=-1, keepdims=True)
    return (o * lax.rsqrt(ms + EPS)) * (g * (1.0 - lam_init))


def _bias_tile(tab_ref, h, rel, n_buckets):
    max_exact = n_buckets // 2
    n = jnp.maximum(rel, 0)
    nf = jnp.maximum(n, 1).astype(F32)
    large = max_exact + (jnp.log(nf / max_exact) / math.log(MAX_DISTANCE / max_exact)
                         * (n_buckets - max_exact)).astype(I32)
    large = jnp.minimum(large, n_buckets - 1)
    bucket = jnp.where(n < max_exact, n, large)
    bias = jnp.zeros(rel.shape, F32)
    for b in range(n_buckets):
        bias = jnp.where(bucket == b, tab_ref[b, h], bias)
    return jnp.where(rel >= 0, bias, NEG)


def _bias_body(tab_ref, bp_ref, bs_ref, bn_ref, *, n_buckets, n_heads, blk, pre, dec_t, page, n_slots):
    r = lax.broadcasted_iota(I32, (blk, 2 * blk), 0)
    c = lax.broadcasted_iota(I32, (blk, 2 * blk), 1)
    rs = lax.broadcasted_iota(I32, (dec_t, page), 0)
    cs = lax.broadcasted_iota(I32, (dec_t, page), 1)
    for h in range(n_heads):
        tile = _bias_tile(tab_ref, h, blk + r - c, n_buckets)
        bp_ref[h, 0] = tile
        bp_ref[h, 1] = jnp.where(c < pre, NEG, tile)
        far = jnp.full((dec_t, page), tab_ref[n_buckets - 1, h], F32)
        last = _bias_tile(tab_ref, h, page + rs - cs, n_buckets)
        new = jnp.where(cs < dec_t, _bias_tile(tab_ref, h, rs - cs, n_buckets), NEG)
        for mp in range(2):
            rows = pl.ds((2 * h + mp) * dec_t, dec_t)
            bn_ref[rows, :] = new
            for u in range(n_slots):
                bs_ref[0, rows, u * page:(u + 1) * page] = far
                bs_ref[1, rows, u * page:(u + 1) * page] = last if u == n_slots - 1 else far


def _bias_tiles(rel_bias, blk, pre, dec_t, page, n_slots):
    n_buckets, n_heads = rel_bias.shape
    rows = 2 * n_heads * dec_t
    return pl.pallas_call(
        functools.partial(_bias_body, n_buckets=n_buckets, n_heads=n_heads, blk=blk, pre=pre,
                          dec_t=dec_t, page=page, n_slots=n_slots),
        in_specs=[pl.BlockSpec(memory_space=pltpu.SMEM)],
        out_shape=[jax.ShapeDtypeStruct((n_heads, 2, blk, 2 * blk), F32),
                   jax.ShapeDtypeStruct((2, rows, n_slots * page), F32),
                   jax.ShapeDtypeStruct((rows, page), F32)],
        name="bias_tiles",
    )(rel_bias)


def _sb_prompt_body(q_ref, k_ref, v_ref, tri_ref, g_ref, o_ref, *, blk, pre):
    i = pl.program_id(2) + 1
    pairs = range(SB_PAIRS_PER_STEP)
    lanes = lambda p: slice(p * LANES, (p + 1) * LANES)
    qq = [_stack_halves(q_ref[0, :, lanes(p)]) for p in pairs]
    tri = tri_ref[...]
    r = lax.broadcasted_iota(I32, (2 * blk, blk), 0)
    r = jnp.where(r >= blk, r - blk, r)
    c = lax.broadcasted_iota(I32, (2 * blk, blk), 1)
    vis = c < r

    def block(j, state, vis):
        keys = pl.ds(pl.multiple_of(j * blk, blk), blk)
        return _sb_blocks(qq, [k_ref[0, keys, lanes(p)] for p in pairs],
                          [v_ref[0, keys, lanes(p)] for p in pairs], tri, state, vis)

    zero = (jnp.zeros((2 * blk, LANES), F32), jnp.zeros((2 * blk, 1), F32))
    state = block(i, [zero for _ in pairs], vis)

    def alive(state):
        return functools.reduce(jnp.maximum, [jnp.max(carry) for _, carry in state]) > SB_CUTOFF

    def more(loop):
        jj, state = loop
        return (jj < i - 1) & alive(state)

    def body(loop):
        jj, state = loop
        return jj + 1, block(i - 1 - jj, state, None)

    jj, state = lax.while_loop(more, body, (jnp.int32(0), state))
    state = lax.cond((jj == i - 1) & alive(state),
                     lambda s: block(0, s, c >= pre), lambda s: s, state)
    for p in pairs:
        o_ref[0, :, lanes(p)] = _sb_finish(state[p][0], blk, g_ref[:, lanes(p)]).astype(o_ref.dtype)


def _sb_prompt(q_b, k_b, v_b, tri, g, blk, pre):
    b, tp, width = q_b.shape
    w = SB_PAIRS_PER_STEP * LANES
    grid = (b, width // w, tp // blk - 1)
    return pl.pallas_call(
        functools.partial(_sb_prompt_body, blk=blk, pre=pre),
        grid=grid,
        in_specs=[
            pl.BlockSpec((1, blk, w), lambda bi, p, i: (bi, i + 1, p)),
            pl.BlockSpec((1, tp, w), lambda bi, p, i: (bi, 0, p)),
            pl.BlockSpec((1, tp, w), lambda bi, p, i: (bi, 0, p)),
            pl.BlockSpec((blk, blk), lambda bi, p, i: (0, 0)),
            pl.BlockSpec((1, w), lambda bi, p, i: (0, p)),
        ],
        out_specs=pl.BlockSpec((1, blk, w), lambda bi, p, i: (bi, i, p)),
        out_shape=jax.ShapeDtypeStruct((b, tp - blk, width), BF16),
        compiler_params=pltpu.CompilerParams(
            dimension_semantics=("arbitrary", "arbitrary", "arbitrary"),
            vmem_limit_bytes=VMEM_LIMIT),
        name="sb_prompt",
    )(q_b, k_b, v_b, tri, g)


def _df_prompt_body(tab_ref, q_ref, k_ref, v_ref, bias_ref, g_ref, lq1, lk1, lq2, lk2,
                    o_ref, m_ref, l_ref, acc_ref, *, blk, pre, lam_init, n_buckets):
    hg = pl.program_id(1)
    i = pl.program_id(2) + 1
    heads = range(DF_HEADS_PER_STEP)
    lanes = lambda hh: slice(hh * LANES, (hh + 1) * LANES)
    qq = [_stack_halves(q_ref[0, :, lanes(hh)]) for hh in heads]
    far_bias = [tab_ref[n_buckets - 1, hg * DF_HEADS_PER_STEP + hh] for hh in heads]

    def scores(hh, j, n, near):
        keys = pl.ds(pl.multiple_of(j * blk, blk), n * blk)
        s = _dot_tb(qq[hh], k_ref[0, keys, lanes(hh)])
        if near:
            t = bias_ref[hh, 0]
            return s + jnp.concatenate([t, t], axis=0), keys
        col = lax.broadcasted_iota(I32, (1, n * blk), 1)
        return s + jnp.where((j == 0) & (col < pre), NEG, far_bias[hh]), keys

    def sweep(visit):
        visit(i - 1, 2, True)
        n_far = i - 1
        n4 = lax.shift_right_logical(n_far, 2)

        def wide(g, carry):
            visit(g * 4, 4, False)
            return carry

        lax.fori_loop(0, n4, wide, 0)

        @pl.when((n_far & 2) != 0)
        def _():
            visit(n4 * 4, 2, False)

        @pl.when((n_far & 1) != 0)
        def _():
            visit(n_far - 1, 1, False)

    def lane_chunks(x):
        return [x[:, c * LANES:(c + 1) * LANES] for c in range(x.shape[1] // LANES)]

    m_ref[...] = jnp.full(m_ref.shape, NEG, F32)

    def visit_max(j, n, near):
        for hh in heads:
            s, _ = scores(hh, j, n, near)
            m_ref[hh] = functools.reduce(jnp.maximum, lane_chunks(s), m_ref[hh])

    sweep(visit_max)
    m = [jnp.max(m_ref[hh], axis=-1, keepdims=True) for hh in heads]

    l_ref[...] = jnp.zeros_like(l_ref)
    acc_ref[...] = jnp.zeros_like(acc_ref)

    def visit_sum(j, n, near):
        scored = [scores(hh, j, n, near) for hh in heads]
        ps = [jnp.exp(s - m[hh]) for hh, (s, _) in zip(heads, scored)]
        for hh in heads:
            l_ref[hh] = functools.reduce(jnp.add, lane_chunks(ps[hh]), l_ref[hh])
        pv = [_dot(ps[hh].astype(BF16), v_ref[0, scored[hh][1], lanes(hh)]) for hh in heads]
        for hh in heads:
            acc_ref[hh] += pv[hh]

    sweep(visit_sum)
    lam = _lambda(lq1, lk1, lq2, lk2, lam_init)
    for hh in heads:
        l = jnp.sum(l_ref[hh], axis=-1, keepdims=True)
        o_ref[0, :, lanes(hh)] = _df_finish(acc_ref[hh], l, blk, lam, g_ref[...], lam_init).astype(o_ref.dtype)


def _df_prompt(rel_bias, q_b, k_b, v_b, bias_p, g, lams, blk, pre, lam_init):
    b, tp, width = q_b.shape
    hps = DF_HEADS_PER_STEP
    w = hps * LANES
    grid = (b, width // w, tp // blk - 1)
    vec = pl.BlockSpec((1, DH), lambda bi, h, i: (0, 0))
    tile = lambda bi, h, i: (h, jnp.where(i == 0, 1, 0), 0, 0)
    return pl.pallas_call(
        functools.partial(_df_prompt_body, blk=blk, pre=pre, lam_init=lam_init,
                          n_buckets=rel_bias.shape[0]),
        grid=grid,
        in_specs=[
            pl.BlockSpec(memory_space=pltpu.SMEM),
            pl.BlockSpec((1, blk, w), lambda bi, h, i: (bi, i + 1, h)),
            pl.BlockSpec((1, tp, w), lambda bi, h, i: (bi, 0, h)),
            pl.BlockSpec((1, tp, w), lambda bi, h, i: (bi, 0, h)),
            pl.BlockSpec((hps, 1, blk, 2 * blk), tile),
            pl.BlockSpec((1, LANES), lambda bi, h, i: (0, 0)),
            vec, vec, vec, vec,
        ],
        out_specs=pl.BlockSpec((1, blk, w), lambda bi, h, i: (bi, i, h)),
        out_shape=jax.ShapeDtypeStruct((b, tp - blk, width), BF16),
        scratch_shapes=[pltpu.VMEM((hps, 2 * blk, LANES), F32)] * 3,
        compiler_params=pltpu.CompilerParams(
            dimension_semantics=("arbitrary", "arbitrary", "arbitrary"),
            vmem_limit_bytes=VMEM_LIMIT),
        name="df_prompt",
    )(rel_bias, q_b, k_b, v_b, bias_p, g, *lams)


def _stack_decode_queries(q, n_blocks):
    q = q.astype(F32)
    return jnp.concatenate(
        [_stack_halves(q[:, p * LANES:(p + 1) * LANES]) for p in range(n_blocks)], axis=0).astype(BF16)


def _sb_decode_body(pt_ref, q_ref, kn_ref, vn_ref, kc_hbm, vc_hbm, tri_ref, g_ref, o_ref,
                    kbuf, vbuf, sem, *, dec_t, n_blocks, page, n_pages):
    bi = pl.program_id(0)
    rows = 2 * dec_t
    qq = _stack_decode_queries(q_ref[0], n_blocks)
    tri = tri_ref[...]

    def page_copies(slot, jj):
        pg = pt_ref[bi, n_pages - 1 - jj]
        return (pltpu.make_async_copy(kc_hbm.at[pg], kbuf.at[slot], sem.at[0, slot]),
                pltpu.make_async_copy(vc_hbm.at[pg], vbuf.at[slot], sem.at[1, slot]))

    def start(slot, jj):
        for cp in page_copies(slot, jj):
            cp.start()

    def wait(slot, jj):
        for cp in page_copies(slot, jj):
            cp.wait()

    start(0, 0)

    def weights(z, carry, vis):
        drop = _softplus(z)
        if vis is not None:
            drop = jnp.where(vis, drop, 0.0)
        csum = carry - _dot(drop.astype(BF16), tri)
        w = jnp.exp(z + csum)
        if vis is not None:
            w = jnp.where(vis, w, 0.0)
        return w.astype(BF16), csum[:, 0:1]

    r = lax.broadcasted_iota(I32, (dec_t, page), 0)
    c = lax.broadcasted_iota(I32, (dec_t, page), 1)
    vis = jnp.concatenate([c < r] * (2 * n_blocks), axis=0)
    z = jnp.concatenate(
        [_dot_tb(qq[p * rows:(p + 1) * rows], kn_ref[0, :, p * LANES:(p + 1) * LANES].astype(BF16))
         for p in range(n_blocks)], axis=0)
    w, carry = weights(z, jnp.zeros((n_blocks * rows, 1), F32), vis)
    acc = jnp.concatenate(
        [_dot(w[p * rows:(p + 1) * rows], vn_ref[0, :, p * LANES:(p + 1) * LANES].astype(BF16))
         for p in range(n_blocks)], axis=0)

    def more(state):
        jj, _, carry = state
        return (jj < n_pages) & (jnp.max(carry) > SB_CUTOFF)

    def body(state):
        jj, acc, carry = state
        slot = jj & 1
        wait(slot, jj)

        @pl.when(jj + 1 < n_pages)
        def _():
            start(1 - slot, jj + 1)

        kt = kbuf[slot]
        vt = vbuf[slot]
        z = jnp.concatenate(
            [_dot(qq[p * rows:(p + 1) * rows], kt[p * LANES:(p + 1) * LANES, :].astype(BF16))
             for p in range(n_blocks)], axis=0)
        w, carry = weights(z, carry, None)
        acc = acc + jnp.concatenate(
            [_dot_tb(w[p * rows:(p + 1) * rows], vt[p * LANES:(p + 1) * LANES, :].astype(BF16))
             for p in range(n_blocks)], axis=0)
        return jj + 1, acc, carry

    jj, acc, _ = lax.while_loop(more, body, (jnp.int32(0), acc, carry))

    @pl.when(jj < n_pages)
    def _():
        wait(jj & 1, jj)

    for p in range(n_blocks):
        o_ref[0, :, p * LANES:(p + 1) * LANES] = _sb_finish(
            acc[p * rows:(p + 1) * rows, :], dec_t, g_ref[:, p * LANES:(p + 1) * LANES])


def _sb_decode(page_table, q, k_new, v_new, kt_cache, vt_cache, tri, g):
    b, dec_t, width = q.shape
    n_pages = page_table.shape[1]
    page = kt_cache.shape[2]
    n_blocks = width // LANES
    per_b = lambda bi, pt: (bi, 0, 0)
    const = lambda bi, pt: (0, 0)
    grid_spec = pltpu.PrefetchScalarGridSpec(
        num_scalar_prefetch=1,
        grid=(b,),
        in_specs=[
            pl.BlockSpec((1, dec_t, width), per_b),
            pl.BlockSpec((1, page, width), per_b),
            pl.BlockSpec((1, page, width), per_b),
            pl.BlockSpec(memory_space=pl.ANY),
            pl.BlockSpec(memory_space=pl.ANY),
            pl.BlockSpec((page, page), const),
            pl.BlockSpec((1, width), const),
        ],
        out_specs=pl.BlockSpec((1, dec_t, width), per_b),
        scratch_shapes=[pltpu.VMEM((2, width, page), F32),
                        pltpu.VMEM((2, width, page), F32),
                        pltpu.SemaphoreType.DMA((2, 2))],
    )
    return pl.pallas_call(
        functools.partial(_sb_decode_body, dec_t=dec_t, n_blocks=n_blocks, page=page, n_pages=n_pages),
        grid_spec=grid_spec,
        out_shape=jax.ShapeDtypeStruct((b, dec_t, width), F32),
        compiler_params=pltpu.CompilerParams(
            dimension_semantics=("arbitrary",), vmem_limit_bytes=VMEM_LIMIT),
        name="sb_decode",
    )(page_table, q, k_new, v_new, kt_cache, vt_cache, tri, g)


def _df_decode_body(pt_ref, q_ref, kn_ref, vn_ref, *rest, dec_t, n_blocks, n_slots, lam_init):
    kc_refs, vc_refs = rest[:n_slots], rest[n_slots:2 * n_slots]
    (bias_ref, bias_new_ref, g_ref, lq1, lk1, lq2, lk2, o_ref,
     qq_ref, m_ref, l_ref, acc_ref) = rest[2 * n_slots:]
    jj = pl.program_id(1)
    rows = 2 * dec_t
    page = bias_new_ref.shape[1]

    def update(k_refs, v_refs, bias):
        own = [pl.ds(h, page, stride=n_blocks) for h in range(n_blocks)]
        gather = lambda refs, h: jnp.concatenate([r[0, own[h], :].astype(BF16) for r in refs], axis=0)
        qq = qq_ref[...]
        s = jnp.concatenate(
            [_dot_tb(qq[h * rows:(h + 1) * rows], gather(k_refs, h)) for h in range(n_blocks)],
            axis=0) + bias[...]
        m_old = m_ref[...]
        m_new = jnp.maximum(m_old, jnp.max(s, axis=-1, keepdims=True))
        alpha = jnp.exp(m_old - m_new)
        p = jnp.exp(s - m_new)
        l_ref[...] = alpha * l_ref[...] + jnp.sum(p, axis=-1, keepdims=True)
        m_ref[...] = m_new
        p = p.astype(BF16)
        pv = jnp.concatenate(
            [_dot(p[h * rows:(h + 1) * rows], gather(v_refs, h)) for h in range(n_blocks)], axis=0)
        acc_ref[...] = alpha * acc_ref[...] + pv

    @pl.when(jj == 0)
    def _():
        qq_ref[...] = _stack_decode_queries(q_ref[0], n_blocks)
        m_ref[...] = jnp.full(m_ref.shape, NEG, F32)
        l_ref[...] = jnp.zeros_like(l_ref)
        acc_ref[...] = jnp.zeros_like(acc_ref)
        update([kn_ref], [vn_ref], bias_new_ref)

    update(kc_refs, vc_refs, bias_ref.at[0])

    @pl.when(jj == pl.num_programs(1) - 1)
    def _():
        lam = _lambda(lq1, lk1, lq2, lk2, lam_init)
        for p in range(n_blocks):
            sl = slice(p * rows, (p + 1) * rows)
            o_ref[0, :, p * LANES:(p + 1) * LANES] = _df_finish(
                acc_ref[sl, :], l_ref[sl, :], dec_t, lam, g_ref[...], lam_init)


def _df_decode(page_table, q, k_new, v_new, k_cache, v_cache, bias_s, bias_new, g, lams, lam_init,
               n_slots):
    b, dec_t, width = q.shape
    n_pages = page_table.shape[1]
    page_rows = k_cache.shape[1]
    n_blocks = width // LANES
    rows = n_blocks * 2 * dec_t
    n_steps = n_pages // n_slots
    per_b = lambda bi, jj, pt: (bi, 0, 0)
    const = lambda bi, jj, pt: (0, 0)
    vec = pl.BlockSpec((1, DH), const)

    def slot_spec(u):
        return pl.BlockSpec((1, page_rows, LANES), lambda bi, jj, pt: (pt[bi, jj * n_slots + u], 0, 0))

    grid_spec = pltpu.PrefetchScalarGridSpec(
        num_scalar_prefetch=1,
        grid=(b, n_steps),
        in_specs=[
            pl.BlockSpec((1, dec_t, width), per_b),
            pl.BlockSpec((1,) + k_new.shape[1:], per_b),
            pl.BlockSpec((1,) + v_new.shape[1:], per_b),
            *[slot_spec(u) for u in range(n_slots)],
            *[slot_spec(u) for u in range(n_slots)],
            pl.BlockSpec((1, rows, bias_s.shape[2]),
                         lambda bi, jj, pt: (jnp.where(jj == n_steps - 1, 1, 0), 0, 0)),
            pl.BlockSpec(bias_new.shape, const),
            pl.BlockSpec((1, LANES), const),
            vec, vec, vec, vec,
        ],
        out_specs=pl.BlockSpec((1, dec_t, width), per_b),
        scratch_shapes=[pltpu.VMEM((rows, LANES), BF16),
                        pltpu.VMEM((rows, 1), F32),
                        pltpu.VMEM((rows, 1), F32),
                        pltpu.VMEM((rows, LANES), F32)],
    )
    return pl.pallas_call(
        functools.partial(_df_decode_body, dec_t=dec_t, n_blocks=n_blocks, n_slots=n_slots,
                          lam_init=lam_init),
        grid_spec=grid_spec,
        out_shape=jax.ShapeDtypeStruct((b, dec_t, width), F32),
        compiler_params=pltpu.CompilerParams(
            dimension_semantics=("arbitrary", "arbitrary"),
            vmem_limit_bytes=VMEM_LIMIT),
        name="df_decode",
    )(page_table, q, k_new, v_new, *([k_cache] * n_slots), *([v_cache] * n_slots),
      bias_s, bias_new, g, *lams)


def _router_gates(logits_t, bias_col, n_experts):
    tm = logits_t.shape[1]
    gsz = n_experts // N_GROUPS
    scores = 1.0 / (1.0 + jnp.exp(-logits_t))
    sel = scores + bias_col
    sub = lax.broadcasted_iota(I32, (gsz, tm), 0)
    group_scores = []
    for g in range(N_GROUPS):
        blk = sel[g * gsz:(g + 1) * gsz]
        m1 = jnp.max(blk, axis=0, keepdims=True)
        first = jnp.min(jnp.where(blk == m1, sub, gsz), axis=0, keepdims=True)
        m2 = jnp.max(jnp.where(sub == first, -jnp.inf, blk), axis=0, keepdims=True)
        group_scores.append(m1 + m2)
    gs = jnp.concatenate(group_scores, axis=0)
    gidx = lax.broadcasted_iota(I32, (N_GROUPS, tm), 0)
    grank = jnp.zeros((N_GROUPS, tm), I32)
    for g in range(N_GROUPS):
        row = gs[g:g + 1]
        ahead = (row > gs) | ((row == gs) & (gidx > g))
        grank = grank + ahead.astype(I32)
    gkeep = grank < TOPK_GROUPS
    masked = jnp.concatenate(
        [jnp.where(jnp.broadcast_to(gkeep[g:g + 1], (gsz, tm)), sel[g * gsz:(g + 1) * gsz], -jnp.inf)
         for g in range(N_GROUPS)], axis=0)
    eidx = lax.broadcasted_iota(I32, (n_experts, tm), 0)
    erank = jnp.zeros((n_experts, tm), I32)
    for e in range(n_experts):
        row = masked[e:e + 1]
        ahead = (row > masked) | ((row == masked) & (eidx > e))
        erank = erank + ahead.astype(I32)
    w = jnp.where(erank < TOP_K, scores, 0.0)
    return w / jnp.sum(w, axis=0, keepdims=True) * ROUTED_SCALE


def _post_body(msb_ref, mdf_ref, x_ref, wo_sb_ref, wo_df_ref, g2_ref, rw_hi_ref, rw_lo_ref,
               rb_ref, x1_ref, h2_ref, gates_ref, *, n_experts):
    att = _dot(msb_ref[0].astype(BF16), wo_sb_ref[...]) + _dot(mdf_ref[0].astype(BF16), wo_df_ref[...])
    x1 = x_ref[0] + att
    x1_ref[...] = x1
    ms = jnp.mean(x1 * x1, axis=-1, keepdims=True)
    h2 = (x1 * lax.rsqrt(ms + EPS)) * g2_ref[...]
    h2_hi, h2_lo = _split_bf16(h2)
    h2_ref[...] = h2_hi
    rw_hi = rw_hi_ref[...]
    logits_t = _dot_tb(rw_hi, h2_hi) + _dot_tb(rw_hi, h2_lo) + _dot_tb(rw_lo_ref[...], h2_hi)
    gates_t = _router_gates(logits_t, rb_ref[...], n_experts)
    tm = gates_t.shape[1]
    pad = gates_ref.shape[1] - n_experts
    gates_t = jnp.concatenate([gates_t, jnp.zeros((pad, tm), F32)], axis=0)
    gates_ref[...] = gates_t.T.astype(gates_ref.dtype)


def _post_attention(msb, mdf, x, wo_sb, wo_df, g2, rw_hi, rw_lo, rb, tm):
    b, s, d = x.shape
    n = b * s
    per_b = s // tm
    n_experts = rw_hi.shape[0]
    ge = _round_up(n_experts, LANES)
    row = lambda bi, i: (bi * per_b + i, 0)
    const = lambda bi, i: (0, 0)
    return pl.pallas_call(
        functools.partial(_post_body, n_experts=n_experts),
        grid=(b, per_b),
        in_specs=[
            pl.BlockSpec((1, tm, msb.shape[2]), lambda bi, i: (bi, i, 0)),
            pl.BlockSpec((1, tm, mdf.shape[2]), lambda bi, i: (bi, i, 0)),
            pl.BlockSpec((1, tm, d), lambda bi, i: (bi, i, 0)),
            pl.BlockSpec(wo_sb.shape, const),
            pl.BlockSpec(wo_df.shape, const),
            pl.BlockSpec((1, d), const),
            pl.BlockSpec(rw_hi.shape, const),
            pl.BlockSpec(rw_lo.shape, const),
            pl.BlockSpec((n_experts, 1), const),
        ],
        out_specs=[pl.BlockSpec((tm, d), row), pl.BlockSpec((tm, d), row), pl.BlockSpec((tm, ge), row)],
        out_shape=[jax.ShapeDtypeStruct((n, d), F32), jax.ShapeDtypeStruct((n, d), BF16),
                   jax.ShapeDtypeStruct((n, ge), F32)],
        compiler_params=pltpu.CompilerParams(
            dimension_semantics=("arbitrary", "arbitrary"), vmem_limit_bytes=VMEM_LIMIT),
        name="post_attention",
    )(msb, mdf, x, wo_sb, wo_df, g2, rw_hi, rw_lo, rb)


def _silu(x):
    return x / (1.0 + jnp.exp(-x))


def _moe_body(h2_ref, x1_ref, gates_ref, wg_ref, wu_ref, wd_ref, sg_ref, su_ref, sd_ref,
              o_ref, acc_ref, *, group):
    eg = pl.program_id(1)
    h2 = h2_ref[...]

    @pl.when(eg == 0)
    def _():
        a = _silu(_dot(h2, sg_ref[...])) * _dot(h2, su_ref[...])
        acc_ref[...] = _dot(a.astype(BF16), sd_ref[...])

    f = wg_ref.shape[2]
    gates = gates_ref[...]
    lane = lax.broadcasted_iota(I32, gates.shape, 1)
    acts = []
    for j in range(group):
        gate = jnp.sum(jnp.where(lane == eg * group + j, gates, 0.0), axis=-1, keepdims=True)
        acts.append((_silu(_dot(h2, wg_ref[j])) * _dot(h2, wu_ref[j]) * gate).astype(BF16))
    a = jnp.concatenate(acts, axis=1)
    wd = wd_ref[...].reshape(group * f, wd_ref.shape[2])
    acc_ref[...] += _dot(a, wd)

    @pl.when(eg == pl.num_programs(1) - 1)
    def _():
        o_ref[...] = x1_ref[...] + acc_ref[...]


def _moe(h2, x1, gates, w_gate, w_up, w_down, sg, su, sd, tm, group):
    n, d = x1.shape
    n_experts, _, f = w_gate.shape
    row = lambda i, e: (i, 0)
    const = lambda i, e: (0, 0)
    return pl.pallas_call(
        functools.partial(_moe_body, group=group),
        grid=(n // tm, n_experts // group),
        in_specs=[
            pl.BlockSpec((tm, d), row),
            pl.BlockSpec((tm, d), row),
            pl.BlockSpec((tm, gates.shape[1]), row),
            pl.BlockSpec((group, d, f), lambda i, e: (e, 0, 0)),
            pl.BlockSpec((group, d, f), lambda i, e: (e, 0, 0)),
            pl.BlockSpec((group, f, d), lambda i, e: (e, 0, 0)),
            pl.BlockSpec(sg.shape, const),
            pl.BlockSpec(su.shape, const),
            pl.BlockSpec(sd.shape, const),
        ],
        out_specs=pl.BlockSpec((tm, d), row),
        out_shape=jax.ShapeDtypeStruct((n, d), F32),
        scratch_shapes=[pltpu.VMEM((tm, d), F32)],
        compiler_params=pltpu.CompilerParams(
            dimension_semantics=("arbitrary", "arbitrary"), vmem_limit_bytes=VMEM_LIMIT),
        name="moe",
    )(h2, x1, gates, w_gate, w_up, w_down, sg, su, sd)


def _largest_tile(n, cap):
    t = cap
    while n % t:
        t //= 2
    return t


def kernel(x_prompt, x_sample, cache_k_sb, cache_v_sb, cache_k_diff, cache_v_diff, page_table,
           meta_tokens, rel_bias, norm1_g, w_in, q_norm_g, k_norm_g, lambda_q1, lambda_k1,
           lambda_q2, lambda_k2, sb_out_g, diff_subln_g, w_out, norm2_g, router_w, router_bias,
           w_gate, w_up, w_down, w_shared_gate, w_shared_up, w_shared_down):
    assert w_in.shape[0] == 1, "single-layer step"
    b, seq, d = x_prompt.shape
    dec_b, dec_t, _ = x_sample.shape
    n_meta = meta_tokens.shape[0]
    width = sb_out_g.shape[1]
    n_pool, page = cache_k_sb.shape[1], cache_k_sb.shape[2]
    n_experts = router_w.shape[2]
    t = seq + n_meta
    blk = ATT_BLOCK
    lam_init = 0.8 - 0.6 * math.exp(-0.3 * 0)

    w_in_b = w_in[0].astype(BF16)
    g1 = norm1_g
    reps = width // DH
    qg = jnp.tile(q_norm_g, (1, reps))
    kg = jnp.tile(k_norm_g, (1, reps))
    lane = jnp.arange(width)
    gm = ((lane[:, None] // DH) == (lane[None, :] // DH)).astype(BF16) * (1.0 / DH)
    tri_p = (jnp.arange(blk)[:, None] >= jnp.arange(blk)[None, :]).astype(BF16)
    tri_d = (jnp.arange(page)[:, None] >= jnp.arange(page)[None, :]).astype(BF16)
    lams = (lambda_q1, lambda_k1, lambda_q2, lambda_k2)
    wo_sb = w_out[0, :width].astype(BF16)
    wo_df = w_out[0, width:].astype(BF16)
    rw_t = router_w[0].T
    rw_hi = rw_t.astype(BF16)
    rw_lo = (rw_t - rw_hi.astype(F32)).astype(BF16)
    rb = router_bias[0][:, None]
    sg = w_shared_gate[0].astype(BF16)
    su = w_shared_up[0].astype(BF16)
    sd = w_shared_down[0].astype(BF16)

    n_h_sb = width // DH
    n_h_df = width // LANES
    n_slots = DECODE_PAGES_PER_STEP
    assert blk >= MAX_DISTANCE and page >= MAX_DISTANCE
    assert page_table.shape[1] % n_slots == 0 and dec_t * n_h_df <= page
    pre = blk - n_meta
    assert 0 <= pre < blk and seq % blk == 0
    bias_p, bias_s, bias_n = _bias_tiles(rel_bias, blk, pre, dec_t, page, n_slots)
    head = jnp.concatenate([jnp.zeros((pre, d), F32), meta_tokens], axis=0)
    (k_sb_p, v_sb_p, k_df_p, v_df_p,
     qsb_b, ksb_b, vsb_b, qdf_b, kdf_b, vdf_b) = _project_prompt(head, x_prompt, n_meta, g1, w_in_b, qg, kg, gm)
    msb_p = _sb_prompt(qsb_b, ksb_b, vsb_b, tri_p, sb_out_g, blk, pre)
    mdf_p = _df_prompt(rel_bias, qdf_b, kdf_b, vdf_b, bias_p, diff_subln_g, lams, blk, pre, lam_init)

    n_s = dec_b * dec_t
    xs = x_sample.reshape(1, n_s, d)
    (k_sb_s, v_sb_s, k_df_s, v_df_s,
     qsb_s, _, _, qdf_s, _, _) = _project(xs, n_s, g1, w_in_b, qg, kg, gm, n_s)

    def new_page(rows, n_rows, n_pad):
        r = rows.reshape(dec_b, n_rows, -1)
        return jnp.pad(r, ((0, 0), (0, n_pad - n_rows), (0, 0)))

    transposed_pages = lambda c: jnp.transpose(c[0], (0, 2, 3, 1)).reshape(n_pool, width, page)
    interleaved_rows = lambda c: c.reshape(n_pool, page * n_h_df, LANES)
    msb_s = _sb_decode(page_table, qsb_s.reshape(dec_b, dec_t, width),
                       new_page(k_sb_s, dec_t, page), new_page(v_sb_s, dec_t, page),
                       transposed_pages(cache_k_sb), transposed_pages(cache_v_sb), tri_d, sb_out_g)
    mdf_s = _df_decode(page_table, qdf_s.reshape(dec_b, dec_t, width),
                       new_page(k_df_s, dec_t * n_h_df, page * n_h_df),
                       new_page(v_df_s, dec_t * n_h_df, page * n_h_df),
                       interleaved_rows(cache_k_diff), interleaved_rows(cache_v_diff),
                       bias_s, bias_n, diff_subln_g, lams, lam_init, n_slots)

    wg_b, wu_b, wd_b = w_gate[0].astype(BF16), w_up[0].astype(BF16), w_down[0].astype(BF16)
    x1_p, h2_p, gates_p = _post_attention(msb_p, mdf_p, x_prompt, wo_sb, wo_df, norm2_g,
                                          rw_hi, rw_lo, rb, _largest_tile(seq, 512))
    y_p = _moe(h2_p, x1_p, gates_p, wg_b, wu_b, wd_b, sg, su, sd,
               _largest_tile(b * seq, MOE_TOKEN_TILE), MOE_EXPERT_GROUP)
    y_prompt = y_p.reshape(b, seq, d)
    x1_s, h2_s, gates_s = _post_attention(
        msb_s.reshape(1, n_s, width), mdf_s.reshape(1, n_s, width), x_sample.reshape(1, n_s, d),
        wo_sb, wo_df, norm2_g, rw_hi, rw_lo, rb, n_s)
    y_s = _moe(h2_s, x1_s, gates_s, wg_b, wu_b, wd_b, sg, su, sd, n_s, MOE_EXPERT_GROUP)
    y_sample = y_s.reshape(dec_b, dec_t, d)

    sb_rows = lambda r, bb, tt: r.reshape(1, bb, tt, n_h_sb, DH)
    df_rows = lambda r, bb, tt: r.reshape(1, bb, tt, n_h_df, LANES)
    return (y_prompt, y_sample,
            sb_rows(k_sb_p, b, t), sb_rows(v_sb_p, b, t), df_rows(k_df_p, b, t), df_rows(v_df_p, b, t),
            sb_rows(k_sb_s, dec_b, dec_t), sb_rows(v_sb_s, dec_b, dec_t),
            df_rows(k_df_s, dec_b, dec_t), df_rows(v_df_s, dec_b, dec_t))
```

```python
import functools
import math

import jax
import jax.numpy as jnp
from jax import lax
from jax.experimental import pallas as pl
from jax.experimental.pallas import tpu as pltpu

F32 = jnp.float32
BF16 = jnp.bfloat16
I32 = jnp.int32

EPS = 1e-6
DH = 64
LANES = 128
MAX_DISTANCE = 128
TOP_K = 8
N_GROUPS = 8
TOPK_GROUPS = 4
ROUTED_SCALE = 2.5
NEG = -1e30
SB_CUTOFF = -104.0
ATT_BLOCK = 256
DF_HEADS_PER_STEP = 4
SB_PAIRS_PER_STEP = 4
DECODE_PAGES_PER_STEP = 32
MOE_TOKEN_TILE = 1024
MOE_EXPERT_GROUP = 8
VMEM_LIMIT = 60 * 1024 * 1024

_TRANS_B = (((1,), (1,)), ((), ()))


def _dot(a, b):
    return jnp.dot(a, b, preferred_element_type=F32)


def _dot_tb(a, b):
    return lax.dot_general(a, b, _TRANS_B, preferred_element_type=F32)


def _split_bf16(x):
    hi = x.astype(BF16)
    lo = (x - hi.astype(F32)).astype(BF16)
    return hi, lo


def _round_up(n, m):
    return (n + m - 1) // m * m


def _proj_rows(x, g1_ref, w_ref, qg_ref, kg_ref, gm_ref, ksb_o, vsb_o, kdf_o, vdf_o,
               qsb_b, ksb_b, vsb_b, qdf_b, kdf_b, vdf_b, width):
    ms = jnp.mean(x * x, axis=-1, keepdims=True)
    h = (x * lax.rsqrt(ms + EPS)) * g1_ref[...]
    proj = _dot(h.astype(BF16), w_ref[...])
    w = width
    q_sb, k_sb, v_sb = proj[:, 0:w], proj[:, w:2 * w], proj[:, 2 * w:3 * w]
    q_df, k_df, v_df = proj[:, 3 * w:4 * w], proj[:, 4 * w:5 * w], proj[:, 5 * w:6 * w]

    def map_norm(t, g_ref):
        msq = _dot((t * t).astype(BF16), gm_ref[...])
        return (t * lax.rsqrt(msq + EPS)) * g_ref[...]

    q_df = map_norm(q_df, qg_ref)
    k_df = map_norm(k_df, kg_ref)
    ksb_o[...] = k_sb
    vsb_o[...] = v_sb
    tm = x.shape[0]
    n_df = w // LANES
    for hd in range(n_df):
        rows = pl.ds(hd, tm, stride=n_df)
        kdf_o[rows, :] = k_df[:, hd * LANES:(hd + 1) * LANES]
        vdf_o[rows, :] = v_df[:, hd * LANES:(hd + 1) * LANES]
    scale = DH ** -0.5
    qsb_b[0] = (q_sb * scale).astype(BF16)
    ksb_b[0] = k_sb.astype(BF16)
    vsb_b[0] = v_sb.astype(BF16)
    qdf_b[0] = (q_df * scale).astype(BF16)
    kdf_b[0] = k_df.astype(BF16)
    vdf_b[0] = v_df.astype(BF16)


def _proj_body(x_ref, g1_ref, w_ref, qg_ref, kg_ref, gm_ref,
               ksb_o, vsb_o, kdf_o, vdf_o, *bf_outs, width):
    _proj_rows(x_ref[0], g1_ref, w_ref, qg_ref, kg_ref, gm_ref,
               ksb_o.at[0], vsb_o.at[0], kdf_o.at[0], vdf_o.at[0], *bf_outs, width)


def _proj_prompt_body(head_ref, x_ref, g1_ref, w_ref, qg_ref, kg_ref, gm_ref,
                      ksb_hbm, vsb_hbm, kdf_hbm, vdf_hbm, *rest, width, lead):
    bf_outs, (st_sb, st_df, sem) = rest[:6], rest[6:]
    b, i = pl.program_id(0), pl.program_id(1)
    n_i = pl.num_programs(1)
    step = b * n_i + i
    slot = step & 1
    tm = x_ref.shape[1]
    n_df = width // LANES

    def copies(slot, bb, ii, head):
        if head:
            src, dst, n = tm - lead, 0, lead
        else:
            src, dst, n = 0, lead + (ii - 1) * tm, tm
        out = []
        for kv, hbm in enumerate((ksb_hbm, vsb_hbm)):
            out.append(pltpu.make_async_copy(st_sb.at[slot, kv, pl.ds(src, n), :],
                                             hbm.at[bb, pl.ds(dst, n), :], sem.at[slot]))
        for kv, hbm in enumerate((kdf_hbm, vdf_hbm)):
            out.append(pltpu.make_async_copy(st_df.at[slot, kv, pl.ds(src * n_df, n * n_df), :],
                                             hbm.at[bb, pl.ds(dst * n_df, n * n_df), :], sem.at[slot]))
        return out

    def wait_step(slot, was_head):
        @pl.when(was_head)
        def _():
            for cp in copies(slot, 0, 1, True):
                cp.wait()

        @pl.when(jnp.logical_not(was_head))
        def _():
            for cp in copies(slot, 0, 1, False):
                cp.wait()

    @pl.when(step >= 2)
    def _():
        wait_step(slot, i == 2)

    x = jnp.where(i == 0, head_ref[...], x_ref[0])
    _proj_rows(x, g1_ref, w_ref, qg_ref, kg_ref, gm_ref,
               st_sb.at[slot, 0], st_sb.at[slot, 1], st_df.at[slot, 0], st_df.at[slot, 1],
               *bf_outs, width)

    @pl.when(i == 0)
    def _():
        for cp in copies(slot, b, i, True):
            cp.start()

    @pl.when(i > 0)
    def _():
        for cp in copies(slot, b, i, False):
            cp.start()

    @pl.when(step == pl.num_programs(0) * n_i - 1)
    def _():
        wait_step(1 - slot, i == 1)
        wait_step(slot, i == 0)


def _project(x_pad, t_valid, g1, w_in_b, qg, kg, gm, tm):
    b, tp, d = x_pad.shape
    width = w_in_b.shape[1] // 6
    grid = (b, tp // tm)
    row = lambda bi, i: (bi, i, 0)
    const = lambda bi, i: (0, 0)
    n_df = width // LANES
    f32_out = jax.ShapeDtypeStruct((b, t_valid, width), F32)
    df_out = jax.ShapeDtypeStruct((b, t_valid * n_df, LANES), F32)
    bf_out = jax.ShapeDtypeStruct((b, tp, width), BF16)
    out_spec = pl.BlockSpec((1, tm, width), row)
    df_spec = pl.BlockSpec((1, tm * n_df, LANES), row)
    return pl.pallas_call(
        functools.partial(_proj_body, width=width),
        grid=grid,
        in_specs=[
            pl.BlockSpec((1, tm, d), row),
            pl.BlockSpec((1, d), const),
            pl.BlockSpec(w_in_b.shape, const),
            pl.BlockSpec((1, width), const),
            pl.BlockSpec((1, width), const),
            pl.BlockSpec((width, width), const),
        ],
        out_specs=[out_spec] * 2 + [df_spec] * 2 + [out_spec] * 6,
        out_shape=[f32_out] * 2 + [df_out] * 2 + [bf_out] * 6,
        compiler_params=pltpu.CompilerParams(
            dimension_semantics=("arbitrary", "arbitrary"),
            vmem_limit_bytes=VMEM_LIMIT),
        name="proj",
    )(x_pad, g1, w_in_b, qg, kg, gm)


def _project_prompt(head, x, lead, g1, w_in_b, qg, kg, gm):
    b, s, d = x.shape
    tm = head.shape[0]
    width = w_in_b.shape[1] // 6
    n_df = width // LANES
    n_i = 1 + s // tm
    assert s % tm == 0 and n_i >= 3 and lead % 8 == 0 and 0 < lead <= tm
    row = lambda bi, i: (bi, i, 0)
    const = lambda bi, i: (0, 0)
    anywhere = pl.BlockSpec(memory_space=pl.ANY)
    bf_out = jax.ShapeDtypeStruct((b, tm + s, width), BF16)
    return pl.pallas_call(
        functools.partial(_proj_prompt_body, width=width, lead=lead),
        grid=(b, n_i),
        in_specs=[
            pl.BlockSpec((tm, d), const),
            pl.BlockSpec((1, tm, d), lambda bi, i: (bi, jnp.maximum(i - 1, 0), 0)),
            pl.BlockSpec((1, d), const),
            pl.BlockSpec(w_in_b.shape, const),
            pl.BlockSpec((1, width), const),
            pl.BlockSpec((1, width), const),
            pl.BlockSpec((width, width), const),
        ],
        out_specs=[anywhere] * 4 + [pl.BlockSpec((1, tm, width), row)] * 6,
        out_shape=[jax.ShapeDtypeStruct((b, lead + s, width), F32)] * 2
        + [jax.ShapeDtypeStruct((b, (lead + s) * n_df, LANES), F32)] * 2 + [bf_out] * 6,
        scratch_shapes=[pltpu.VMEM((2, 2, tm, width), F32),
                        pltpu.VMEM((2, 2, tm * n_df, LANES), F32),
                        pltpu.SemaphoreType.DMA((2,))],
        compiler_params=pltpu.CompilerParams(
            dimension_semantics=("arbitrary", "arbitrary"),
            vmem_limit_bytes=VMEM_LIMIT),
        name="proj_prompt",
    )(head, x, g1, w_in_b, qg, kg, gm)


def _stack_halves(q):
    lane = lax.broadcasted_iota(I32, (1, LANES), 1)
    zero = jnp.zeros_like(q)
    return jnp.concatenate(
        [jnp.where(lane < DH, q, zero), jnp.where(lane >= DH, q, zero)], axis=0)


def _softplus(z):
    return jnp.maximum(z, 0.0) + jnp.log(1.0 + jnp.exp(-jnp.abs(z)))


def _sb_blocks(qqs, ks, vs, tri, state, vis):
    n = range(len(qqs))
    zs = [_dot_tb(qqs[p], ks[p]) for p in n]
    drops = [_softplus(z) for z in zs]
    if vis is not None:
        drops = [jnp.where(vis, d, 0.0) for d in drops]
    csums = [state[p][1] - _dot(drops[p].astype(BF16), tri) for p in n]
    ws = [jnp.exp(zs[p] + csums[p]) for p in n]
    if vis is not None:
        ws = [jnp.where(vis, w, 0.0) for w in ws]
    pvs = [_dot(ws[p].astype(BF16), vs[p]) for p in n]
    return [(state[p][0] + pvs[p], csums[p][:, 0:1]) for p in n]


def _sb_finish(acc, t, g):
    lane = lax.broadcasted_iota(I32, (1, LANES), 1)
    lo_half = lane < DH
    o = jnp.where(lo_half, acc[:t], acc[t:])
    o2 = o * o
    s_lo = jnp.sum(jnp.where(lo_half, o2, 0.0), axis=-1, keepdims=True)
    s_hi = jnp.sum(jnp.where(lo_half, 0.0, o2), axis=-1, keepdims=True)
    ms = jnp.where(lo_half, s_lo, s_hi) * (1.0 / DH)
    return (o * lax.rsqrt(ms + EPS)) * g


def _df_block(qq, k, v, bias, m, l, acc):
    s = _dot_tb(qq, k) + bias
    m_new = jnp.maximum(m, jnp.max(s, axis=-1, keepdims=True))
    alpha = jnp.exp(m - m_new)
    p = jnp.exp(s - m_new)
    l = alpha * l + jnp.sum(p, axis=-1, keepdims=True)
    acc = alpha * acc + _dot(p.astype(BF16), v)
    return m_new, l, acc


def _lambda(lq1, lk1, lq2, lk2, lam_init):
    s1 = jnp.sum(lq1[...] * lk1[...], axis=-1, keepdims=True)
    s2 = jnp.sum(lq2[...] * lk2[...], axis=-1, keepdims=True)
    return jnp.exp(s1) - jnp.exp(s2) + lam_init


def _df_finish(acc, l, t, lam, g, lam_init):
    o = acc[:t] / l[:t] - lam * (acc[t:] / l[t:])
    ms = jnp.mean(o * o, axis=-1, keepdims=True)
    return (o * lax.rsqrt(ms + EPS)) * (g * (1.0 - lam_init))


def _bias_tile(tab_ref, h, rel, n_buckets):
    max_exact = n_buckets // 2
    n = jnp.maximum(rel, 0)
    nf = jnp.maximum(n, 1).astype(F32)
    large = max_exact + (jnp.log(nf / max_exact) / math.log(MAX_DISTANCE / max_exact)
                         * (n_buckets - max_exact)).astype(I32)
    large = jnp.minimum(large, n_buckets - 1)
    bucket = jnp.where(n < max_exact, n, large)
    bias = jnp.zeros(rel.shape, F32)
    for b in range(n_buckets):
        bias = jnp.where(bucket == b, tab_ref[b, h], bias)
    return jnp.where(rel >= 0, bias, NEG)


def _bias_body(tab_ref, bp_ref, bs_ref, bn_ref, *, n_buckets, n_heads, blk, pre, dec_t, page, n_slots):
    r = lax.broadcasted_iota(I32, (blk, 2 * blk), 0)
    c = lax.broadcasted_iota(I32, (blk, 2 * blk), 1)
    rs = lax.broadcasted_iota(I32, (dec_t, page), 0)
    cs = lax.broadcasted_iota(I32, (dec_t, page), 1)
    for h in range(n_heads):
        tile = _bias_tile(tab_ref, h, blk + r - c, n_buckets)
        bp_ref[h, 0] = tile
        bp_ref[h, 1] = jnp.where(c < pre, NEG, tile)
        far = jnp.full((dec_t, page), tab_ref[n_buckets - 1, h], F32)
        last = _bias_tile(tab_ref, h, page + rs - cs, n_buckets)
        new = jnp.where(cs < dec_t, _bias_tile(tab_ref, h, rs - cs, n_buckets), NEG)
        for mp in range(2):
            rows = pl.ds((2 * h + mp) * dec_t, dec_t)
            bn_ref[rows, :] = new
            for u in range(n_slots):
                bs_ref[0, rows, u * page:(u + 1) * page] = far
                bs_ref[1, rows, u * page:(u + 1) * page] = last if u == n_slots - 1 else far


def _bias_tiles(rel_bias, blk, pre, dec_t, page, n_slots):
    n_buckets, n_heads = rel_bias.shape
    rows = 2 * n_heads * dec_t
    return pl.pallas_call(
        functools.partial(_bias_body, n_buckets=n_buckets, n_heads=n_heads, blk=blk, pre=pre,
                          dec_t=dec_t, page=page, n_slots=n_slots),
        in_specs=[pl.BlockSpec(memory_space=pltpu.SMEM)],
        out_shape=[jax.ShapeDtypeStruct((n_heads, 2, blk, 2 * blk), F32),
                   jax.ShapeDtypeStruct((2, rows, n_slots * page), F32),
                   jax.ShapeDtypeStruct((rows, page), F32)],
        name="bias_tiles",
    )(rel_bias)


def _sb_prompt_body(q_ref, k_ref, v_ref, tri_ref, g_ref, o_ref, *, blk, pre):
    i = pl.program_id(2) + 1
    pairs = range(SB_PAIRS_PER_STEP)
    lanes = lambda p: slice(p * LANES, (p + 1) * LANES)
    qq = [_stack_halves(q_ref[0, :, lanes(p)]) for p in pairs]
    tri = tri_ref[...]
    r = lax.broadcasted_iota(I32, (2 * blk, blk), 0)
    r = jnp.where(r >= blk, r - blk, r)
    c = lax.broadcasted_iota(I32, (2 * blk, blk), 1)
    vis = c < r

    def block(j, state, vis):
        keys = pl.ds(pl.multiple_of(j * blk, blk), blk)
        return _sb_blocks(qq, [k_ref[0, keys, lanes(p)] for p in pairs],
                          [v_ref[0, keys, lanes(p)] for p in pairs], tri, state, vis)

    zero = (jnp.zeros((2 * blk, LANES), F32), jnp.zeros((2 * blk, 1), F32))
    state = block(i, [zero for _ in pairs], vis)

    def alive(state):
        return functools.reduce(jnp.maximum, [jnp.max(carry) for _, carry in state]) > SB_CUTOFF

    def more(loop):
        jj, state = loop
        return (jj < i - 1) & alive(state)

    def body(loop):
        jj, state = loop
        return jj + 1, block(i - 1 - jj, state, None)

    jj, state = lax.while_loop(more, body, (jnp.int32(0), state))
    state = lax.cond((jj == i - 1) & alive(state),
                     lambda s: block(0, s, c >= pre), lambda s: s, state)
    for p in pairs:
        o_ref[0, :, lanes(p)] = _sb_finish(state[p][0], blk, g_ref[:, lanes(p)]).astype(o_ref.dtype)


def _sb_prompt(q_b, k_b, v_b, tri, g, blk, pre):
    b, tp, width = q_b.shape
    w = SB_PAIRS_PER_STEP * LANES
    grid = (b, width // w, tp // blk - 1)
    return pl.pallas_call(
        functools.partial(_sb_prompt_body, blk=blk, pre=pre),
        grid=grid,
        in_specs=[
            pl.BlockSpec((1, blk, w), lambda bi, p, i: (bi, i + 1, p)),
            pl.BlockSpec((1, tp, w), lambda bi, p, i: (bi, 0, p)),
            pl.BlockSpec((1, tp, w), lambda bi, p, i: (bi, 0, p)),
            pl.BlockSpec((blk, blk), lambda bi, p, i: (0, 0)),
            pl.BlockSpec((1, w), lambda bi, p, i: (0, p)),
        ],
        out_specs=pl.BlockSpec((1, blk, w), lambda bi, p, i: (bi, i, p)),
        out_shape=jax.ShapeDtypeStruct((b, tp - blk, width), BF16),
        compiler_params=pltpu.CompilerParams(
            dimension_semantics=("arbitrary", "arbitrary", "arbitrary"),
            vmem_limit_bytes=VMEM_LIMIT),
        name="sb_prompt",
    )(q_b, k_b, v_b, tri, g)


def _df_prompt_body(tab_ref, q_ref, k_ref, v_ref, bias_ref, g_ref, lq1, lk1, lq2, lk2,
                    o_ref, m_ref, l_ref, acc_ref, *, blk, pre, lam_init, n_buckets):
    hg = pl.program_id(1)
    i = pl.program_id(2) + 1
    heads = range(DF_HEADS_PER_STEP)
    lanes = lambda hh: slice(hh * LANES, (hh + 1) * LANES)
    qq = [_stack_halves(q_ref[0, :, lanes(hh)]) for hh in heads]
    far_bias = [tab_ref[n_buckets - 1, hg * DF_HEADS_PER_STEP + hh] for hh in heads]

    def scores(hh, j, n, near):
        keys = pl.ds(pl.multiple_of(j * blk, blk), n * blk)
        s = _dot_tb(qq[hh], k_ref[0, keys, lanes(hh)])
        if near:
            t = bias_ref[hh, 0]
            return s + jnp.concatenate([t, t], axis=0), keys
        col = lax.broadcasted_iota(I32, (1, n * blk), 1)
        return s + jnp.where((j == 0) & (col < pre), NEG, far_bias[hh]), keys

    def sweep(visit):
        visit(i - 1, 2, True)
        n_far = i - 1
        n4 = lax.shift_right_logical(n_far, 2)

        def wide(g, carry):
            visit(g * 4, 4, False)
            return carry

        lax.fori_loop(0, n4, wide, 0)

        @pl.when((n_far & 2) != 0)
        def _():
            visit(n4 * 4, 2, False)

        @pl.when((n_far & 1) != 0)
        def _():
            visit(n_far - 1, 1, False)

    def lane_chunks(x):
        return [x[:, c * LANES:(c + 1) * LANES] for c in range(x.shape[1] // LANES)]

    m_ref[...] = jnp.full(m_ref.shape, NEG, F32)

    def visit_max(j, n, near):
        for hh in heads:
            s, _ = scores(hh, j, n, near)
            m_ref[hh] = functools.reduce(jnp.maximum, lane_chunks(s), m_ref[hh])

    sweep(visit_max)
    m = [jnp.max(m_ref[hh], axis=-1, keepdims=True) for hh in heads]

    l_ref[...] = jnp.zeros_like(l_ref)
    acc_ref[...] = jnp.zeros_like(acc_ref)

    def visit_sum(j, n, near):
        scored = [scores(hh, j, n, near) for hh in heads]
        ps = [jnp.exp(s - m[hh]) for hh, (s, _) in zip(heads, scored)]
        for hh in heads:
            l_ref[hh] = functools.reduce(jnp.add, lane_chunks(ps[hh]), l_ref[hh])
        pv = [_dot(ps[hh].astype(BF16), v_ref[0, scored[hh][1], lanes(hh)]) for hh in heads]
        for hh in heads:
            acc_ref[hh] += pv[hh]

    sweep(visit_sum)
    lam = _lambda(lq1, lk1, lq2, lk2, lam_init)
    for hh in heads:
        l = jnp.sum(l_ref[hh], axis=-1, keepdims=True)
        o_ref[0, :, lanes(hh)] = _df_finish(acc_ref[hh], l, blk, lam, g_ref[...], lam_init).astype(o_ref.dtype)


def _df_prompt(rel_bias, q_b, k_b, v_b, bias_p, g, lams, blk, pre, lam_init):
    b, tp, width = q_b.shape
    hps = DF_HEADS_PER_STEP
    w = hps * LANES
    grid = (b, width // w, tp // blk - 1)
    vec = pl.BlockSpec((1, DH), lambda bi, h, i: (0, 0))
    tile = lambda bi, h, i: (h, jnp.where(i == 0, 1, 0), 0, 0)
    return pl.pallas_call(
        functools.partial(_df_prompt_body, blk=blk, pre=pre, lam_init=lam_init,
                          n_buckets=rel_bias.shape[0]),
        grid=grid,
        in_specs=[
            pl.BlockSpec(memory_space=pltpu.SMEM),
            pl.BlockSpec((1, blk, w), lambda bi, h, i: (bi, i + 1, h)),
            pl.BlockSpec((1, tp, w), lambda bi, h, i: (bi, 0, h)),
            pl.BlockSpec((1, tp, w), lambda bi, h, i: (bi, 0, h)),
            pl.BlockSpec((hps, 1, blk, 2 * blk), tile),
            pl.BlockSpec((1, LANES), lambda bi, h, i: (0, 0)),
            vec, vec, vec, vec,
        ],
        out_specs=pl.BlockSpec((1, blk, w), lambda bi, h, i: (bi, i, h)),
        out_shape=jax.ShapeDtypeStruct((b, tp - blk, width), BF16),
        scratch_shapes=[pltpu.VMEM((hps, 2 * blk, LANES), F32)] * 3,
        compiler_params=pltpu.CompilerParams(
            dimension_semantics=("arbitrary", "arbitrary", "arbitrary"),
            vmem_limit_bytes=VMEM_LIMIT),
        name="df_prompt",
    )(rel_bias, q_b, k_b, v_b, bias_p, g, *lams)


def _stack_decode_queries(q, n_blocks):
    q = q.astype(F32)
    return jnp.concatenate(
        [_stack_halves(q[:, p * LANES:(p + 1) * LANES]) for p in range(n_blocks)], axis=0).astype(BF16)


def _sb_decode_body(pt_ref, q_ref, kn_ref, vn_ref, kc_hbm, vc_hbm, tri_ref, g_ref, o_ref,
                    kbuf, vbuf, sem, *, dec_t, n_blocks, page, n_pages):
    bi = pl.program_id(0)
    rows = 2 * dec_t
    qq = _stack_decode_queries(q_ref[0], n_blocks)
    tri = tri_ref[...]

    def page_copies(slot, jj):
        pg = pt_ref[bi, n_pages - 1 - jj]
        return (pltpu.make_async_copy(kc_hbm.at[pg], kbuf.at[slot], sem.at[0, slot]),
                pltpu.make_async_copy(vc_hbm.at[pg], vbuf.at[slot], sem.at[1, slot]))

    def start(slot, jj):
        for cp in page_copies(slot, jj):
            cp.start()

    def wait(slot, jj):
        for cp in page_copies(slot, jj):
            cp.wait()

    start(0, 0)

    def weights(z, carry, vis):
        drop = _softplus(z)
        if vis is not None:
            drop = jnp.where(vis, drop, 0.0)
        csum = carry - _dot(drop.astype(BF16), tri)
        w = jnp.exp(z + csum)
        if vis is not None:
            w = jnp.where(vis, w, 0.0)
        return w.astype(BF16), csum[:, 0:1]

    r = lax.broadcasted_iota(I32, (dec_t, page), 0)
    c = lax.broadcasted_iota(I32, (dec_t, page), 1)
    vis = jnp.concatenate([c < r] * (2 * n_blocks), axis=0)
    z = jnp.concatenate(
        [_dot_tb(qq[p * rows:(p + 1) * rows], kn_ref[0, :, p * LANES:(p + 1) * LANES].astype(BF16))
         for p in range(n_blocks)], axis=0)
    w, carry = weights(z, jnp.zeros((n_blocks * rows, 1), F32), vis)
    acc = jnp.concatenate(
        [_dot(w[p * rows:(p + 1) * rows], vn_ref[0, :, p * LANES:(p + 1) * LANES].astype(BF16))
         for p in range(n_blocks)], axis=0)

    def more(state):
        jj, _, carry = state
        return (jj < n_pages) & (jnp.max(carry) > SB_CUTOFF)

    def body(state):
        jj, acc, carry = state
        slot = jj & 1
        wait(slot, jj)

        @pl.when(jj + 1 < n_pages)
        def _():
            start(1 - slot, jj + 1)

        kt = kbuf[slot]
        vt = vbuf[slot]
        z = jnp.concatenate(
            [_dot(qq[p * rows:(p + 1) * rows], kt[p * LANES:(p + 1) * LANES, :].astype(BF16))
             for p in range(n_blocks)], axis=0)
        w, carry = weights(z, carry, None)
        acc = acc + jnp.concatenate(
            [_dot_tb(w[p * rows:(p + 1) * rows], vt[p * LANES:(p + 1) * LANES, :].astype(BF16))
             for p in range(n_blocks)], axis=0)
        return jj + 1, acc, carry

    jj, acc, _ = lax.while_loop(more, body, (jnp.int32(0), acc, carry))

    @pl.when(jj < n_pages)
    def _():
        wait(jj & 1, jj)

    for p in range(n_blocks):
        o_ref[0, :, p * LANES:(p + 1) * LANES] = _sb_finish(
            acc[p * rows:(p + 1) * rows, :], dec_t, g_ref[:, p * LANES:(p + 1) * LANES])


def _sb_decode(page_table, q, k_new, v_new, kt_cache, vt_cache, tri, g):
    b, dec_t, width = q.shape
    n_pages = page_table.shape[1]
    page = kt_cache.shape[2]
    n_blocks = width // LANES
    per_b = lambda bi, pt: (bi, 0, 0)
    const = lambda bi, pt: (0, 0)
    grid_spec = pltpu.PrefetchScalarGridSpec(
        num_scalar_prefetch=1,
        grid=(b,),
        in_specs=[
            pl.BlockSpec((1, dec_t, width), per_b),
            pl.BlockSpec((1, page, width), per_b),
            pl.BlockSpec((1, page, width), per_b),
            pl.BlockSpec(memory_space=pl.ANY),
            pl.BlockSpec(memory_space=pl.ANY),
            pl.BlockSpec((page, page), const),
            pl.BlockSpec((1, width), const),
        ],
        out_specs=pl.BlockSpec((1, dec_t, width), per_b),
        scratch_shapes=[pltpu.VMEM((2, width, page), F32),
                        pltpu.VMEM((2, width, page), F32),
                        pltpu.SemaphoreType.DMA((2, 2))],
    )
    return pl.pallas_call(
        functools.partial(_sb_decode_body, dec_t=dec_t, n_blocks=n_blocks, page=page, n_pages=n_pages),
        grid_spec=grid_spec,
        out_shape=jax.ShapeDtypeStruct((b, dec_t, width), F32),
        compiler_params=pltpu.CompilerParams(
            dimension_semantics=("arbitrary",), vmem_limit_bytes=VMEM_LIMIT),
        name="sb_decode",
    )(page_table, q, k_new, v_new, kt_cache, vt_cache, tri, g)


def _df_decode_body(pt_ref, q_ref, kn_ref, vn_ref, *rest, dec_t, n_blocks, n_slots, lam_init):
    kc_refs, vc_refs = rest[:n_slots], rest[n_slots:2 * n_slots]
    (bias_ref, bias_new_ref, g_ref, lq1, lk1, lq2, lk2, o_ref,
     qq_ref, m_ref, l_ref, acc_ref) = rest[2 * n_slots:]
    jj = pl.program_id(1)
    rows = 2 * dec_t
    page = bias_new_ref.shape[1]

    def update(k_refs, v_refs, bias):
        own = [pl.ds(h, page, stride=n_blocks) for h in range(n_blocks)]
        gather = lambda refs, h: jnp.concatenate([r[0, own[h], :].astype(BF16) for r in refs], axis=0)
        qq = qq_ref[...]
        s = jnp.concatenate(
            [_dot_tb(qq[h * rows:(h + 1) * rows], gather(k_refs, h)) for h in range(n_blocks)],
            axis=0) + bias[...]
        m_old = m_ref[...]
        m_new = jnp.maximum(m_old, jnp.max(s, axis=-1, keepdims=True))
        alpha = jnp.exp(m_old - m_new)
        p = jnp.exp(s - m_new)
        l_ref[...] = alpha * l_ref[...] + jnp.sum(p, axis=-1, keepdims=True)
        m_ref[...] = m_new
        p = p.astype(BF16)
        pv = jnp.concatenate(
            [_dot(p[h * rows:(h + 1) * rows], gather(v_refs, h)) for h in range(n_blocks)], axis=0)
        acc_ref[...] = alpha * acc_ref[...] + pv

    @pl.when(jj == 0)
    def _():
        qq_ref[...] = _stack_decode_queries(q_ref[0], n_blocks)
        m_ref[...] = jnp.full(m_ref.shape, NEG, F32)
        l_ref[...] = jnp.zeros_like(l_ref)
        acc_ref[...] = jnp.zeros_like(acc_ref)
        update([kn_ref], [vn_ref], bias_new_ref)

    update(kc_refs, vc_refs, bias_ref.at[0])

    @pl.when(jj == pl.num_programs(1) - 1)
    def _():
        lam = _lambda(lq1, lk1, lq2, lk2, lam_init)
        for p in range(n_blocks):
            sl = slice(p * rows, (p + 1) * rows)
            o_ref[0, :, p * LANES:(p + 1) * LANES] = _df_finish(
                acc_ref[sl, :], l_ref[sl, :], dec_t, lam, g_ref[...], lam_init)


def _df_decode(page_table, q, k_new, v_new, k_cache, v_cache, bias_s, bias_new, g, lams, lam_init,
               n_slots):
    b, dec_t, width = q.shape
    n_pages = page_table.shape[1]
    page_rows = k_cache.shape[1]
    n_blocks = width // LANES
    rows = n_blocks * 2 * dec_t
    n_steps = n_pages // n_slots
    per_b = lambda bi, jj, pt: (bi, 0, 0)
    const = lambda bi, jj, pt: (0, 0)
    vec = pl.BlockSpec((1, DH), const)

    def slot_spec(u):
        return pl.BlockSpec((1, page_rows, LANES), lambda bi, jj, pt: (pt[bi, jj * n_slots + u], 0, 0))

    grid_spec = pltpu.PrefetchScalarGridSpec(
        num_scalar_prefetch=1,
        grid=(b, n_steps),
        in_specs=[
            pl.BlockSpec((1, dec_t, width), per_b),
            pl.BlockSpec((1,) + k_new.shape[1:], per_b),
            pl.BlockSpec((1,) + v_new.shape[1:], per_b),
            *[slot_spec(u) for u in range(n_slots)],
            *[slot_spec(u) for u in range(n_slots)],
            pl.BlockSpec((1, rows, bias_s.shape[2]),
                         lambda bi, jj, pt: (jnp.where(jj == n_steps - 1, 1, 0), 0, 0)),
            pl.BlockSpec(bias_new.shape, const),
            pl.BlockSpec((1, LANES), const),
            vec, vec, vec, vec,
        ],
        out_specs=pl.BlockSpec((1, dec_t, width), per_b),
        scratch_shapes=[pltpu.VMEM((rows, LANES), BF16),
                        pltpu.VMEM((rows, 1), F32),
                        pltpu.VMEM((rows, 1), F32),
                        pltpu.VMEM((rows, LANES), F32)],
    )
    return pl.pallas_call(
        functools.partial(_df_decode_body, dec_t=dec_t, n_blocks=n_blocks, n_slots=n_slots,
                          lam_init=lam_init),
        grid_spec=grid_spec,
        out_shape=jax.ShapeDtypeStruct((b, dec_t, width), F32),
        compiler_params=pltpu.CompilerParams(
            dimension_semantics=("arbitrary", "arbitrary"),
            vmem_limit_bytes=VMEM_LIMIT),
        name="df_decode",
    )(page_table, q, k_new, v_new, *([k_cache] * n_slots), *([v_cache] * n_slots),
      bias_s, bias_new, g, *lams)


def _router_gates(logits_t, bias_col, n_experts):
    tm = logits_t.shape[1]
    gsz = n_experts // N_GROUPS
    scores = 1.0 / (1.0 + jnp.exp(-logits_t))
    sel = scores + bias_col
    sub = lax.broadcasted_iota(I32, (gsz, tm), 0)
    group_scores = []
    for g in range(N_GROUPS):
        blk = sel[g * gsz:(g + 1) * gsz]
        m1 = jnp.max(blk, axis=0, keepdims=True)
        first = jnp.min(jnp.where(blk == m1, sub, gsz), axis=0, keepdims=True)
        m2 = jnp.max(jnp.where(sub == first, -jnp.inf, blk), axis=0, keepdims=True)
        group_scores.append(m1 + m2)
    gs = jnp.concatenate(group_scores, axis=0)
    gidx = lax.broadcasted_iota(I32, (N_GROUPS, tm), 0)
    grank = jnp.zeros((N_GROUPS, tm), I32)
    for g in range(N_GROUPS):
        row = gs[g:g + 1]
        ahead = (row > gs) | ((row == gs) & (gidx > g))
        grank = grank + ahead.astype(I32)
    gkeep = grank < TOPK_GROUPS
    masked = jnp.concatenate(
        [jnp.where(jnp.broadcast_to(gkeep[g:g + 1], (gsz, tm)), sel[g * gsz:(g + 1) * gsz], -jnp.inf)
         for g in range(N_GROUPS)], axis=0)
    sub8 = 8
    tiles = [masked[s * sub8:(s + 1) * sub8] for s in range(n_experts // sub8)]
    eidx = lax.broadcasted_iota(I32, (sub8, tm), 0)
    eranks = [jnp.zeros((sub8, tm), I32) for _ in tiles]
    for e in range(n_experts):
        row = masked[e:e + 1]
        own = e // sub8
        for s, tile in enumerate(tiles):
            if s < own:
                ahead = row > tile
            elif s > own:
                ahead = row >= tile
            else:
                ahead = (row > tile) | ((row == tile) & (eidx > e % sub8))
            eranks[s] = jnp.where(ahead, eranks[s] + 1, eranks[s])
    erank = jnp.concatenate(eranks, axis=0)
    w = jnp.where(erank < TOP_K, scores, 0.0)
    return w / jnp.sum(w, axis=0, keepdims=True) * ROUTED_SCALE


def _post_body(msb_ref, mdf_ref, x_ref, wo_sb_ref, wo_df_ref, g2_ref, rw_hi_ref, rw_lo_ref,
               rb_ref, x1_ref, h2_ref, gates_ref, *, n_experts):
    att = _dot(msb_ref[0].astype(BF16), wo_sb_ref[...]) + _dot(mdf_ref[0].astype(BF16), wo_df_ref[...])
    x1 = x_ref[0] + att
    x1_ref[...] = x1
    ms = jnp.mean(x1 * x1, axis=-1, keepdims=True)
    h2 = (x1 * lax.rsqrt(ms + EPS)) * g2_ref[...]
    h2_hi, h2_lo = _split_bf16(h2)
    h2_ref[...] = h2_hi
    rw_hi = rw_hi_ref[...]
    logits_t = _dot_tb(rw_hi, h2_hi) + _dot_tb(rw_hi, h2_lo) + _dot_tb(rw_lo_ref[...], h2_hi)
    gates_t = _router_gates(logits_t, rb_ref[...], n_experts)
    tm = gates_t.shape[1]
    pad = gates_ref.shape[1] - n_experts
    gates_t = jnp.concatenate([gates_t, jnp.zeros((pad, tm), F32)], axis=0)
    gates_ref[...] = gates_t.T.astype(gates_ref.dtype)


def _post_attention(msb, mdf, x, wo_sb, wo_df, g2, rw_hi, rw_lo, rb, tm):
    b, s, d = x.shape
    n = b * s
    per_b = s // tm
    n_experts = rw_hi.shape[0]
    ge = _round_up(n_experts, LANES)
    row = lambda bi, i: (bi * per_b + i, 0)
    const = lambda bi, i: (0, 0)
    return pl.pallas_call(
        functools.partial(_post_body, n_experts=n_experts),
        grid=(b, per_b),
        in_specs=[
            pl.BlockSpec((1, tm, msb.shape[2]), lambda bi, i: (bi, i, 0)),
            pl.BlockSpec((1, tm, mdf.shape[2]), lambda bi, i: (bi, i, 0)),
            pl.BlockSpec((1, tm, d), lambda bi, i: (bi, i, 0)),
            pl.BlockSpec(wo_sb.shape, const),
            pl.BlockSpec(wo_df.shape, const),
            pl.BlockSpec((1, d), const),
            pl.BlockSpec(rw_hi.shape, const),
            pl.BlockSpec(rw_lo.shape, const),
            pl.BlockSpec((n_experts, 1), const),
        ],
        out_specs=[pl.BlockSpec((tm, d), row), pl.BlockSpec((tm, d), row), pl.BlockSpec((tm, ge), row)],
        out_shape=[jax.ShapeDtypeStruct((n, d), F32), jax.ShapeDtypeStruct((n, d), BF16),
                   jax.ShapeDtypeStruct((n, ge), F32)],
        compiler_params=pltpu.CompilerParams(
            dimension_semantics=("arbitrary", "arbitrary"), vmem_limit_bytes=VMEM_LIMIT),
        name="post_attention",
    )(msb, mdf, x, wo_sb, wo_df, g2, rw_hi, rw_lo, rb)


def _silu(x):
    return x / (1.0 + jnp.exp(-x))


def _moe_body(h2_ref, x1_ref, gates_ref, wg_ref, wu_ref, wd_ref, sg_ref, su_ref, sd_ref,
              o_ref, acc_ref, *, group):
    eg = pl.program_id(1)
    h2 = h2_ref[...]

    @pl.when(eg == 0)
    def _():
        a = _silu(_dot(h2, sg_ref[...])) * _dot(h2, su_ref[...])
        acc_ref[...] = _dot(a.astype(BF16), sd_ref[...])

    f = wg_ref.shape[2]
    gates = gates_ref[...]
    lane = lax.broadcasted_iota(I32, gates.shape, 1)
    acts = []
    for j in range(group):
        gate = jnp.sum(jnp.where(lane == eg * group + j, gates, 0.0), axis=-1, keepdims=True)
        acts.append((_silu(_dot(h2, wg_ref[j])) * _dot(h2, wu_ref[j]) * gate).astype(BF16))
    a = jnp.concatenate(acts, axis=1)
    wd = wd_ref[...].reshape(group * f, wd_ref.shape[2])
    acc_ref[...] += _dot(a, wd)

    @pl.when(eg == pl.num_programs(1) - 1)
    def _():
        o_ref[...] = x1_ref[...] + acc_ref[...]


def _moe(h2, x1, gates, w_gate, w_up, w_down, sg, su, sd, tm, group):
    n, d = x1.shape
    n_experts, _, f = w_gate.shape
    row = lambda i, e: (i, 0)
    const = lambda i, e: (0, 0)
    return pl.pallas_call(
        functools.partial(_moe_body, group=group),
        grid=(n // tm, n_experts // group),
        in_specs=[
            pl.BlockSpec((tm, d), row),
            pl.BlockSpec((tm, d), row),
            pl.BlockSpec((tm, gates.shape[1]), row),
            pl.BlockSpec((group, d, f), lambda i, e: (e, 0, 0)),
            pl.BlockSpec((group, d, f), lambda i, e: (e, 0, 0)),
            pl.BlockSpec((group, f, d), lambda i, e: (e, 0, 0)),
            pl.BlockSpec(sg.shape, const),
            pl.BlockSpec(su.shape, const),
            pl.BlockSpec(sd.shape, const),
        ],
        out_specs=pl.BlockSpec((tm, d), row),
        out_shape=jax.ShapeDtypeStruct((n, d), F32),
        scratch_shapes=[pltpu.VMEM((tm, d), F32)],
        compiler_params=pltpu.CompilerParams(
            dimension_semantics=("arbitrary", "arbitrary"), vmem_limit_bytes=VMEM_LIMIT),
        name="moe",
    )(h2, x1, gates, w_gate, w_up, w_down, sg, su, sd)


def _largest_tile(n, cap):
    t = cap
    while n % t:
        t //= 2
    return t


def kernel(x_prompt, x_sample, cache_k_sb, cache_v_sb, cache_k_diff, cache_v_diff, page_table,
           meta_tokens, rel_bias, norm1_g, w_in, q_norm_g, k_norm_g, lambda_q1, lambda_k1,
           lambda_q2, lambda_k2, sb_out_g, diff_subln_g, w_out, norm2_g, router_w, router_bias,
           w_gate, w_up, w_down, w_shared_gate, w_shared_up, w_shared_down):
    assert w_in.shape[0] == 1, "single-layer step"
    b, seq, d = x_prompt.shape
    dec_b, dec_t, _ = x_sample.shape
    n_meta = meta_tokens.shape[0]
    width = sb_out_g.shape[1]
    n_pool, page = cache_k_sb.shape[1], cache_k_sb.shape[2]
    n_experts = router_w.shape[2]
    t = seq + n_meta
    blk = ATT_BLOCK
    lam_init = 0.8 - 0.6 * math.exp(-0.3 * 0)

    w_in_b = w_in[0].astype(BF16)
    g1 = norm1_g
    reps = width // DH
    qg = jnp.tile(q_norm_g, (1, reps))
    kg = jnp.tile(k_norm_g, (1, reps))
    lane = jnp.arange(width)
    gm = ((lane[:, None] // DH) == (lane[None, :] // DH)).astype(BF16) * (1.0 / DH)
    tri_p = (jnp.arange(blk)[:, None] >= jnp.arange(blk)[None, :]).astype(BF16)
    tri_d = (jnp.arange(page)[:, None] >= jnp.arange(page)[None, :]).astype(BF16)
    lams = (lambda_q1, lambda_k1, lambda_q2, lambda_k2)
    wo_sb = w_out[0, :width].astype(BF16)
    wo_df = w_out[0, width:].astype(BF16)
    rw_t = router_w[0].T
    rw_hi = rw_t.astype(BF16)
    rw_lo = (rw_t - rw_hi.astype(F32)).astype(BF16)
    rb = router_bias[0][:, None]
    sg = w_shared_gate[0].astype(BF16)
    su = w_shared_up[0].astype(BF16)
    sd = w_shared_down[0].astype(BF16)

    n_h_sb = width // DH
    n_h_df = width // LANES
    n_slots = DECODE_PAGES_PER_STEP
    assert blk >= MAX_DISTANCE and page >= MAX_DISTANCE
    assert page_table.shape[1] % n_slots == 0 and dec_t * n_h_df <= page
    pre = blk - n_meta
    assert 0 <= pre < blk and seq % blk == 0
    bias_p, bias_s, bias_n = _bias_tiles(rel_bias, blk, pre, dec_t, page, n_slots)
    head = jnp.concatenate([jnp.zeros((pre, d), F32), meta_tokens], axis=0)
    (k_sb_p, v_sb_p, k_df_p, v_df_p,
     qsb_b, ksb_b, vsb_b, qdf_b, kdf_b, vdf_b) = _project_prompt(head, x_prompt, n_meta, g1, w_in_b, qg, kg, gm)
    msb_p = _sb_prompt(qsb_b, ksb_b, vsb_b, tri_p, sb_out_g, blk, pre)
    mdf_p = _df_prompt(rel_bias, qdf_b, kdf_b, vdf_b, bias_p, diff_subln_g, lams, blk, pre, lam_init)

    n_s = dec_b * dec_t
    xs = x_sample.reshape(1, n_s, d)
    (k_sb_s, v_sb_s, k_df_s, v_df_s,
     qsb_s, _, _, qdf_s, _, _) = _project(xs, n_s, g1, w_in_b, qg, kg, gm, n_s)

    def new_page(rows, n_rows, n_pad):
        r = rows.reshape(dec_b, n_rows, -1)
        return jnp.pad(r, ((0, 0), (0, n_pad - n_rows), (0, 0)))

    transposed_pages = lambda c: jnp.transpose(c[0], (0, 2, 3, 1)).reshape(n_pool, width, page)
    interleaved_rows = lambda c: c.reshape(n_pool, page * n_h_df, LANES)
    msb_s = _sb_decode(page_table, qsb_s.reshape(dec_b, dec_t, width),
                       new_page(k_sb_s, dec_t, page), new_page(v_sb_s, dec_t, page),
                       transposed_pages(cache_k_sb), transposed_pages(cache_v_sb), tri_d, sb_out_g)
    mdf_s = _df_decode(page_table, qdf_s.reshape(dec_b, dec_t, width),
                       new_page(k_df_s, dec_t * n_h_df, page * n_h_df),
                       new_page(v_df_s, dec_t * n_h_df, page * n_h_df),
                       interleaved_rows(cache_k_diff), interleaved_rows(cache_v_diff),
                       bias_s, bias_n, diff_subln_g, lams, lam_init, n_slots)

    wg_b, wu_b, wd_b = w_gate[0].astype(BF16), w_up[0].astype(BF16), w_down[0].astype(BF16)
    x1_p, h2_p, gates_p = _post_attention(msb_p, mdf_p, x_prompt, wo_sb, wo_df, norm2_g,
                                          rw_hi, rw_lo, rb, _largest_tile(seq, 512))
    y_p = _moe(h2_p, x1_p, gates_p, wg_b, wu_b, wd_b, sg, su, sd,
               _largest_tile(b * seq, MOE_TOKEN_TILE), MOE_EXPERT_GROUP)
    y_prompt = y_p.reshape(b, seq, d)
    x1_s, h2_s, gates_s = _post_attention(
        msb_s.reshape(1, n_s, width), mdf_s.reshape(1, n_s, width), x_sample.reshape(1, n_s, d),
        wo_sb, wo_df, norm2_g, rw_hi, rw_lo, rb, n_s)
    y_s = _moe(h2_s, x1_s, gates_s, wg_b, wu_b, wd_b, sg, su, sd, n_s, MOE_EXPERT_GROUP)
    y_sample = y_s.reshape(dec_b, dec_t, d)

    sb_rows = lambda r, bb, tt: r.reshape(1, bb, tt, n_h_sb, DH)
    df_rows = lambda r, bb, tt: r.reshape(1, bb, tt, n_h_df, LANES)
    return (y_prompt, y_sample,
            sb_rows(k_sb_p, b, t), sb_rows(v_sb_p, b, t), df_rows(k_df_p, b, t), df_rows(v_df_p, b, t),
            sb_rows(k_sb_s, dec_b, dec_t), sb_rows(v_sb_s, dec_b, dec_t),
            df_rows(k_df_s, dec_b, dec_t), df_rows(v_df_s, dec_b, dec_t))
```

```python
import functools
import math

import jax
import jax.numpy as jnp
from jax import lax
from jax.experimental import pallas as pl
from jax.experimental.pallas import tpu as pltpu

F32 = jnp.float32
BF16 = jnp.bfloat16
I32 = jnp.int32

EPS = 1e-6
DH = 64
LANES = 128
MAX_DISTANCE = 128
TOP_K = 8
N_GROUPS = 8
TOPK_GROUPS = 4
ROUTED_SCALE = 2.5
NEG = -1e30
SB_CUTOFF = -104.0
ATT_BLOCK = 256
DF_HEADS_PER_STEP = 4
SB_PAIRS_PER_STEP = 4
DECODE_PAGES_PER_STEP = 32
MOE_TOKEN_TILE = 1024
MOE_EXPERT_GROUP = 8
VMEM_LIMIT = 60 * 1024 * 1024

_TRANS_B = (((1,), (1,)), ((), ()))


def _dot(a, b):
    return jnp.dot(a, b, preferred_element_type=F32)


def _dot_tb(a, b):
    return lax.dot_general(a, b, _TRANS_B, preferred_element_type=F32)


def _split_bf16(x):
    hi = x.astype(BF16)
    lo = (x - hi.astype(F32)).astype(BF16)
    return hi, lo


def _round_up(n, m):
    return (n + m - 1) // m * m


def _proj_rows(x, g1_ref, w_ref, qg_ref, kg_ref, gm_ref, ksb_o, vsb_o, kdf_o, vdf_o,
               qsb_b, ksb_b, vsb_b, qdf_b, kdf_b, vdf_b, width):
    ms = jnp.mean(x * x, axis=-1, keepdims=True)
    h = (x * lax.rsqrt(ms + EPS)) * g1_ref[...]
    proj = _dot(h.astype(BF16), w_ref[...])
    w = width
    q_sb, k_sb, v_sb = proj[:, 0:w], proj[:, w:2 * w], proj[:, 2 * w:3 * w]
    q_df, k_df, v_df = proj[:, 3 * w:4 * w], proj[:, 4 * w:5 * w], proj[:, 5 * w:6 * w]

    def map_norm(t, g_ref):
        msq = _dot((t * t).astype(BF16), gm_ref[...])
        return (t * lax.rsqrt(msq + EPS)) * g_ref[...]

    q_df = map_norm(q_df, qg_ref)
    k_df = map_norm(k_df, kg_ref)
    ksb_o[...] = k_sb
    vsb_o[...] = v_sb
    tm = x.shape[0]
    n_df = w // LANES
    for hd in range(n_df):
        rows = pl.ds(hd, tm, stride=n_df)
        kdf_o[rows, :] = k_df[:, hd * LANES:(hd + 1) * LANES]
        vdf_o[rows, :] = v_df[:, hd * LANES:(hd + 1) * LANES]
    scale = DH ** -0.5
    qsb_b[0] = (q_sb * scale).astype(BF16)
    ksb_b[0] = k_sb.astype(BF16)
    vsb_b[0] = v_sb.astype(BF16)
    qdf_b[0] = (q_df * scale).astype(BF16)
    kdf_b[0] = k_df.astype(BF16)
    vdf_b[0] = v_df.astype(BF16)


def _proj_body(x_ref, g1_ref, w_ref, qg_ref, kg_ref, gm_ref,
               ksb_o, vsb_o, kdf_o, vdf_o, *bf_outs, width):
    _proj_rows(x_ref[0], g1_ref, w_ref, qg_ref, kg_ref, gm_ref,
               ksb_o.at[0], vsb_o.at[0], kdf_o.at[0], vdf_o.at[0], *bf_outs, width)


def _proj_prompt_body(head_ref, x_ref, g1_ref, w_ref, qg_ref, kg_ref, gm_ref,
                      ksb_hbm, vsb_hbm, kdf_hbm, vdf_hbm, *rest, width, lead):
    bf_outs, (st_sb, st_df, sem) = rest[:6], rest[6:]
    b, i = pl.program_id(0), pl.program_id(1)
    n_i = pl.num_programs(1)
    step = b * n_i + i
    slot = step & 1
    tm = x_ref.shape[1]
    n_df = width // LANES

    def copies(slot, bb, ii, head):
        if head:
            src, dst, n = tm - lead, 0, lead
        else:
            src, dst, n = 0, lead + (ii - 1) * tm, tm
        out = []
        for kv, hbm in enumerate((ksb_hbm, vsb_hbm)):
            out.append(pltpu.make_async_copy(st_sb.at[slot, kv, pl.ds(src, n), :],
                                             hbm.at[bb, pl.ds(dst, n), :], sem.at[slot]))
        for kv, hbm in enumerate((kdf_hbm, vdf_hbm)):
            out.append(pltpu.make_async_copy(st_df.at[slot, kv, pl.ds(src * n_df, n * n_df), :],
                                             hbm.at[bb, pl.ds(dst * n_df, n * n_df), :], sem.at[slot]))
        return out

    def wait_step(slot, was_head):
        @pl.when(was_head)
        def _():
            for cp in copies(slot, 0, 1, True):
                cp.wait()

        @pl.when(jnp.logical_not(was_head))
        def _():
            for cp in copies(slot, 0, 1, False):
                cp.wait()

    @pl.when(step >= 2)
    def _():
        wait_step(slot, i == 2)

    x = jnp.where(i == 0, head_ref[...], x_ref[0])
    _proj_rows(x, g1_ref, w_ref, qg_ref, kg_ref, gm_ref,
               st_sb.at[slot, 0], st_sb.at[slot, 1], st_df.at[slot, 0], st_df.at[slot, 1],
               *bf_outs, width)

    @pl.when(i == 0)
    def _():
        for cp in copies(slot, b, i, True):
            cp.start()

    @pl.when(i > 0)
    def _():
        for cp in copies(slot, b, i, False):
            cp.start()

    @pl.when(step == pl.num_programs(0) * n_i - 1)
    def _():
        wait_step(1 - slot, i == 1)
        wait_step(slot, i == 0)


def _project(x_pad, t_valid, g1, w_in_b, qg, kg, gm, tm):
    b, tp, d = x_pad.shape
    width = w_in_b.shape[1] // 6
    grid = (b, tp // tm)
    row = lambda bi, i: (bi, i, 0)
    const = lambda bi, i: (0, 0)
    n_df = width // LANES
    f32_out = jax.ShapeDtypeStruct((b, t_valid, width), F32)
    df_out = jax.ShapeDtypeStruct((b, t_valid * n_df, LANES), F32)
    bf_out = jax.ShapeDtypeStruct((b, tp, width), BF16)
    out_spec = pl.BlockSpec((1, tm, width), row)
    df_spec = pl.BlockSpec((1, tm * n_df, LANES), row)
    return pl.pallas_call(
        functools.partial(_proj_body, width=width),
        grid=grid,
        in_specs=[
            pl.BlockSpec((1, tm, d), row),
            pl.BlockSpec((1, d), const),
            pl.BlockSpec(w_in_b.shape, const),
            pl.BlockSpec((1, width), const),
            pl.BlockSpec((1, width), const),
            pl.BlockSpec((width, width), const),
        ],
        out_specs=[out_spec] * 2 + [df_spec] * 2 + [out_spec] * 6,
        out_shape=[f32_out] * 2 + [df_out] * 2 + [bf_out] * 6,
        compiler_params=pltpu.CompilerParams(
            dimension_semantics=("arbitrary", "arbitrary"),
            vmem_limit_bytes=VMEM_LIMIT),
        name="proj",
    )(x_pad, g1, w_in_b, qg, kg, gm)


def _project_prompt(head, x, lead, g1, w_in_b, qg, kg, gm):
    b, s, d = x.shape
    tm = head.shape[0]
    width = w_in_b.shape[1] // 6
    n_df = width // LANES
    n_i = 1 + s // tm
    assert s % tm == 0 and n_i >= 3 and lead % 8 == 0 and 0 < lead <= tm
    row = lambda bi, i: (bi, i, 0)
    const = lambda bi, i: (0, 0)
    anywhere = pl.BlockSpec(memory_space=pl.ANY)
    bf_out = jax.ShapeDtypeStruct((b, tm + s, width), BF16)
    return pl.pallas_call(
        functools.partial(_proj_prompt_body, width=width, lead=lead),
        grid=(b, n_i),
        in_specs=[
            pl.BlockSpec((tm, d), const),
            pl.BlockSpec((1, tm, d), lambda bi, i: (bi, jnp.maximum(i - 1, 0), 0)),
            pl.BlockSpec((1, d), const),
            pl.BlockSpec(w_in_b.shape, const),
            pl.BlockSpec((1, width), const),
            pl.BlockSpec((1, width), const),
            pl.BlockSpec((width, width), const),
        ],
        out_specs=[anywhere] * 4 + [pl.BlockSpec((1, tm, width), row)] * 6,
        out_shape=[jax.ShapeDtypeStruct((b, lead + s, width), F32)] * 2
        + [jax.ShapeDtypeStruct((b, (lead + s) * n_df, LANES), F32)] * 2 + [bf_out] * 6,
        scratch_shapes=[pltpu.VMEM((2, 2, tm, width), F32),
                        pltpu.VMEM((2, 2, tm * n_df, LANES), F32),
                        pltpu.SemaphoreType.DMA((2,))],
        compiler_params=pltpu.CompilerParams(
            dimension_semantics=("arbitrary", "arbitrary"),
            vmem_limit_bytes=VMEM_LIMIT),
        name="proj_prompt",
    )(head, x, g1, w_in_b, qg, kg, gm)


def _stack_halves(q):
    lane = lax.broadcasted_iota(I32, (1, LANES), 1)
    zero = jnp.zeros_like(q)
    return jnp.concatenate(
        [jnp.where(lane < DH, q, zero), jnp.where(lane >= DH, q, zero)], axis=0)


def _softplus(z):
    return jnp.maximum(z, 0.0) + jnp.log(1.0 + jnp.exp(-jnp.abs(z)))


def _sb_blocks(qqs, ks, vs, tri, state, vis):
    n = range(len(qqs))
    zs = [_dot_tb(qqs[p], ks[p]) for p in n]
    drops = [_softplus(z) for z in zs]
    if vis is not None:
        drops = [jnp.where(vis, d, 0.0) for d in drops]
    csums = [state[p][1] - _dot(drops[p].astype(BF16), tri) for p in n]
    ws = [jnp.exp(zs[p] + csums[p]) for p in n]
    if vis is not None:
        ws = [jnp.where(vis, w, 0.0) for w in ws]
    pvs = [_dot(ws[p].astype(BF16), vs[p]) for p in n]
    return [(state[p][0] + pvs[p], csums[p][:, 0:1]) for p in n]


def _sb_finish(acc, t, g):
    lane = lax.broadcasted_iota(I32, (1, LANES), 1)
    lo_half = lane < DH
    o = jnp.where(lo_half, acc[:t], acc[t:])
    o2 = o * o
    s_lo = jnp.sum(jnp.where(lo_half, o2, 0.0), axis=-1, keepdims=True)
    s_hi = jnp.sum(jnp.where(lo_half, 0.0, o2), axis=-1, keepdims=True)
    ms = jnp.where(lo_half, s_lo, s_hi) * (1.0 / DH)
    return (o * lax.rsqrt(ms + EPS)) * g


def _df_block(qq, k, v, bias, m, l, acc):
    s = _dot_tb(qq, k) + bias
    m_new = jnp.maximum(m, jnp.max(s, axis=-1, keepdims=True))
    alpha = jnp.exp(m - m_new)
    p = jnp.exp(s - m_new)
    l = alpha * l + jnp.sum(p, axis=-1, keepdims=True)
    acc = alpha * acc + _dot(p.astype(BF16), v)
    return m_new, l, acc


def _lambda(lq1, lk1, lq2, lk2, lam_init):
    s1 = jnp.sum(lq1[...] * lk1[...], axis=-1, keepdims=True)
    s2 = jnp.sum(lq2[...] * lk2[...], axis=-1, keepdims=True)
    return jnp.exp(s1) - jnp.exp(s2) + lam_init


def _df_finish(acc, l, t, lam, g, lam_init):
    o = acc[:t] / l[:t] - lam * (acc[t:] / l[t:])
    ms = jnp.mean(o * o, axis=-1, keepdims=True)
    return (o * lax.rsqrt(ms + EPS)) * (g * (1.0 - lam_init))


def _bias_tile(tab_ref, h, rel, n_buckets):
    max_exact = n_buckets // 2
    n = jnp.maximum(rel, 0)
    nf = jnp.maximum(n, 1).astype(F32)
    large = max_exact + (jnp.log(nf / max_exact) / math.log(MAX_DISTANCE / max_exact)
                         * (n_buckets - max_exact)).astype(I32)
    large = jnp.minimum(large, n_buckets - 1)
    bucket = jnp.where(n < max_exact, n, large)
    bias = jnp.zeros(rel.shape, F32)
    for b in range(n_buckets):
        bias = jnp.where(bucket == b, tab_ref[b, h], bias)
    return jnp.where(rel >= 0, bias, NEG)


def _bias_body(tab_ref, bp_ref, bs_ref, bn_ref, *, n_buckets, n_heads, blk, pre, dec_t, page, n_slots):
    r = lax.broadcasted_iota(I32, (blk, 2 * blk), 0)
    c = lax.broadcasted_iota(I32, (blk, 2 * blk), 1)
    rs = lax.broadcasted_iota(I32, (dec_t, page), 0)
    cs = lax.broadcasted_iota(I32, (dec_t, page), 1)
    for h in range(n_heads):
        tile = _bias_tile(tab_ref, h, blk + r - c, n_buckets)
        bp_ref[h, 0] = tile
        bp_ref[h, 1] = jnp.where(c < pre, NEG, tile)
        far = jnp.full((dec_t, page), tab_ref[n_buckets - 1, h], F32)
        last = _bias_tile(tab_ref, h, page + rs - cs, n_buckets)
        new = jnp.where(cs < dec_t, _bias_tile(tab_ref, h, rs - cs, n_buckets), NEG)
        for mp in range(2):
            rows = pl.ds((2 * h + mp) * dec_t, dec_t)
            bn_ref[rows, :] = new
            for u in range(n_slots):
                bs_ref[0, rows, u * page:(u + 1) * page] = far
                bs_ref[1, rows, u * page:(u + 1) * page] = last if u == n_slots - 1 else far


def _bias_tiles(rel_bias, blk, pre, dec_t, page, n_slots):
    n_buckets, n_heads = rel_bias.shape
    rows = 2 * n_heads * dec_t
    return pl.pallas_call(
        functools.partial(_bias_body, n_buckets=n_buckets, n_heads=n_heads, blk=blk, pre=pre,
                          dec_t=dec_t, page=page, n_slots=n_slots),
        in_specs=[pl.BlockSpec(memory_space=pltpu.SMEM)],
        out_shape=[jax.ShapeDtypeStruct((n_heads, 2, blk, 2 * blk), F32),
                   jax.ShapeDtypeStruct((2, rows, n_slots * page), F32),
                   jax.ShapeDtypeStruct((rows, page), F32)],
        name="bias_tiles",
    )(rel_bias)


def _sb_prompt_body(q_ref, k_ref, v_ref, tri_ref, g_ref, o_ref, *, blk, pre):
    i = pl.program_id(2) + 1
    pairs = range(SB_PAIRS_PER_STEP)
    lanes = lambda p: slice(p * LANES, (p + 1) * LANES)
    qq = [_stack_halves(q_ref[0, :, lanes(p)]) for p in pairs]
    tri = tri_ref[...]
    r = lax.broadcasted_iota(I32, (2 * blk, blk), 0)
    r = jnp.where(r >= blk, r - blk, r)
    c = lax.broadcasted_iota(I32, (2 * blk, blk), 1)
    vis = c < r

    def block(j, state, vis):
        keys = pl.ds(pl.multiple_of(j * blk, blk), blk)
        return _sb_blocks(qq, [k_ref[0, keys, lanes(p)] for p in pairs],
                          [v_ref[0, keys, lanes(p)] for p in pairs], tri, state, vis)

    zero = (jnp.zeros((2 * blk, LANES), F32), jnp.zeros((2 * blk, 1), F32))
    state = block(i, [zero for _ in pairs], vis)

    def alive(state):
        return functools.reduce(jnp.maximum, [jnp.max(carry) for _, carry in state]) > SB_CUTOFF

    def more(loop):
        jj, state = loop
        return (jj < i - 1) & alive(state)

    def body(loop):
        jj, state = loop
        return jj + 1, block(i - 1 - jj, state, None)

    jj, state = lax.while_loop(more, body, (jnp.int32(0), state))
    state = lax.cond((jj == i - 1) & alive(state),
                     lambda s: block(0, s, c >= pre), lambda s: s, state)
    for p in pairs:
        o_ref[0, :, lanes(p)] = _sb_finish(state[p][0], blk, g_ref[:, lanes(p)]).astype(o_ref.dtype)


def _sb_prompt(q_b, k_b, v_b, tri, g, blk, pre):
    b, tp, width = q_b.shape
    w = SB_PAIRS_PER_STEP * LANES
    grid = (b, width // w, tp // blk - 1)
    return pl.pallas_call(
        functools.partial(_sb_prompt_body, blk=blk, pre=pre),
        grid=grid,
        in_specs=[
            pl.BlockSpec((1, blk, w), lambda bi, p, i: (bi, i + 1, p)),
            pl.BlockSpec((1, tp, w), lambda bi, p, i: (bi, 0, p)),
            pl.BlockSpec((1, tp, w), lambda bi, p, i: (bi, 0, p)),
            pl.BlockSpec((blk, blk), lambda bi, p, i: (0, 0)),
            pl.BlockSpec((1, w), lambda bi, p, i: (0, p)),
        ],
        out_specs=pl.BlockSpec((1, blk, w), lambda bi, p, i: (bi, i, p)),
        out_shape=jax.ShapeDtypeStruct((b, tp - blk, width), BF16),
        compiler_params=pltpu.CompilerParams(
            dimension_semantics=("arbitrary", "arbitrary", "arbitrary"),
            vmem_limit_bytes=VMEM_LIMIT),
        name="sb_prompt",
    )(q_b, k_b, v_b, tri, g)


def _df_prompt_body(tab_ref, q_ref, k_ref, v_ref, bias_ref, g_ref, lq1, lk1, lq2, lk2,
                    o_ref, m_ref, l_ref, acc_ref, *, blk, pre, lam_init, n_buckets):
    hg = pl.program_id(1)
    i = pl.program_id(2) + 1
    heads = range(DF_HEADS_PER_STEP)
    lanes = lambda hh: slice(hh * LANES, (hh + 1) * LANES)
    qq = [_stack_halves(q_ref[0, :, lanes(hh)]) for hh in heads]
    far_bias = [tab_ref[n_buckets - 1, hg * DF_HEADS_PER_STEP + hh] for hh in heads]

    def scores(hh, j, n, near):
        keys = pl.ds(pl.multiple_of(j * blk, blk), n * blk)
        s = _dot_tb(qq[hh], k_ref[0, keys, lanes(hh)])
        if near:
            t = bias_ref[hh, 0]
            return s + jnp.concatenate([t, t], axis=0), keys
        col = lax.broadcasted_iota(I32, (1, n * blk), 1)
        return s + jnp.where((j == 0) & (col < pre), NEG, far_bias[hh]), keys

    def sweep(visit):
        visit(i - 1, 2, True)
        n_far = i - 1
        n8 = lax.shift_right_logical(n_far, 3)

        def wide(g, carry):
            visit(g * 8, 8, False)
            return carry

        lax.fori_loop(0, n8, wide, 0)

        @pl.when((n_far & 4) != 0)
        def _():
            visit(n8 * 8, 4, False)

        n4 = lax.shift_right_logical(n_far, 2)

        @pl.when((n_far & 2) != 0)
        def _():
            visit(n4 * 4, 2, False)

        @pl.when((n_far & 1) != 0)
        def _():
            visit(n_far - 1, 1, False)

    def lane_chunks(x):
        return [x[:, c * LANES:(c + 1) * LANES] for c in range(x.shape[1] // LANES)]

    m_ref[...] = jnp.full(m_ref.shape, NEG, F32)

    def visit_max(j, n, near):
        for hh in heads:
            s, _ = scores(hh, j, n, near)
            m_ref[hh] = functools.reduce(jnp.maximum, lane_chunks(s), m_ref[hh])

    sweep(visit_max)
    m = [jnp.max(m_ref[hh], axis=-1, keepdims=True) for hh in heads]

    l_ref[...] = jnp.zeros_like(l_ref)
    acc_ref[...] = jnp.zeros_like(acc_ref)

    def visit_sum(j, n, near):
        scored = [scores(hh, j, n, near) for hh in heads]
        ps = [jnp.exp(s - m[hh]) for hh, (s, _) in zip(heads, scored)]
        for hh in heads:
            l_ref[hh] = functools.reduce(jnp.add, lane_chunks(ps[hh]), l_ref[hh])
        pv = [_dot(ps[hh].astype(BF16), v_ref[0, scored[hh][1], lanes(hh)]) for hh in heads]
        for hh in heads:
            acc_ref[hh] += pv[hh]

    sweep(visit_sum)
    lam = _lambda(lq1, lk1, lq2, lk2, lam_init)
    for hh in heads:
        l = jnp.sum(l_ref[hh], axis=-1, keepdims=True)
        o_ref[0, :, lanes(hh)] = _df_finish(acc_ref[hh], l, blk, lam, g_ref[...], lam_init).astype(o_ref.dtype)


def _df_prompt(rel_bias, q_b, k_b, v_b, bias_p, g, lams, blk, pre, lam_init):
    b, tp, width = q_b.shape
    hps = DF_HEADS_PER_STEP
    w = hps * LANES
    grid = (b, width // w, tp // blk - 1)
    vec = pl.BlockSpec((1, DH), lambda bi, h, i: (0, 0))
    tile = lambda bi, h, i: (h, jnp.where(i == 0, 1, 0), 0, 0)
    return pl.pallas_call(
        functools.partial(_df_prompt_body, blk=blk, pre=pre, lam_init=lam_init,
                          n_buckets=rel_bias.shape[0]),
        grid=grid,
        in_specs=[
            pl.BlockSpec(memory_space=pltpu.SMEM),
            pl.BlockSpec((1, blk, w), lambda bi, h, i: (bi, i + 1, h)),
            pl.BlockSpec((1, tp, w), lambda bi, h, i: (bi, 0, h)),
            pl.BlockSpec((1, tp, w), lambda bi, h, i: (bi, 0, h)),
            pl.BlockSpec((hps, 1, blk, 2 * blk), tile),
            pl.BlockSpec((1, LANES), lambda bi, h, i: (0, 0)),
            vec, vec, vec, vec,
        ],
        out_specs=pl.BlockSpec((1, blk, w), lambda bi, h, i: (bi, i, h)),
        out_shape=jax.ShapeDtypeStruct((b, tp - blk, width), BF16),
        scratch_shapes=[pltpu.VMEM((hps, 2 * blk, LANES), F32)] * 3,
        compiler_params=pltpu.CompilerParams(
            dimension_semantics=("arbitrary", "arbitrary", "arbitrary"),
            vmem_limit_bytes=VMEM_LIMIT),
        name="df_prompt",
    )(rel_bias, q_b, k_b, v_b, bias_p, g, *lams)


def _stack_decode_queries(q, n_blocks):
    q = q.astype(F32)
    return jnp.concatenate(
        [_stack_halves(q[:, p * LANES:(p + 1) * LANES]) for p in range(n_blocks)], axis=0).astype(BF16)


def _sb_decode_body(pt_ref, q_ref, kn_ref, vn_ref, kc_hbm, vc_hbm, tri_ref, g_ref, o_ref,
                    kbuf, vbuf, sem, *, dec_t, n_blocks, page, n_pages):
    bi = pl.program_id(0)
    rows = 2 * dec_t
    qq = _stack_decode_queries(q_ref[0], n_blocks)
    tri = tri_ref[...]

    def page_copies(slot, jj):
        pg = pt_ref[bi, n_pages - 1 - jj]
        return (pltpu.make_async_copy(kc_hbm.at[pg], kbuf.at[slot], sem.at[0, slot]),
                pltpu.make_async_copy(vc_hbm.at[pg], vbuf.at[slot], sem.at[1, slot]))

    def start(slot, jj):
        for cp in page_copies(slot, jj):
            cp.start()

    def wait(slot, jj):
        for cp in page_copies(slot, jj):
            cp.wait()

    start(0, 0)

    def weights(z, carry, vis):
        drop = _softplus(z)
        if vis is not None:
            drop = jnp.where(vis, drop, 0.0)
        csum = carry - _dot(drop.astype(BF16), tri)
        w = jnp.exp(z + csum)
        if vis is not None:
            w = jnp.where(vis, w, 0.0)
        return w.astype(BF16), csum[:, 0:1]

    r = lax.broadcasted_iota(I32, (dec_t, page), 0)
    c = lax.broadcasted_iota(I32, (dec_t, page), 1)
    vis = jnp.concatenate([c < r] * (2 * n_blocks), axis=0)
    z = jnp.concatenate(
        [_dot_tb(qq[p * rows:(p + 1) * rows], kn_ref[0, :, p * LANES:(p + 1) * LANES].astype(BF16))
         for p in range(n_blocks)], axis=0)
    w, carry = weights(z, jnp.zeros((n_blocks * rows, 1), F32), vis)
    acc = jnp.concatenate(
        [_dot(w[p * rows:(p + 1) * rows], vn_ref[0, :, p * LANES:(p + 1) * LANES].astype(BF16))
         for p in range(n_blocks)], axis=0)

    def more(state):
        jj, _, carry = state
        return (jj < n_pages) & (jnp.max(carry) > SB_CUTOFF)

    def body(state):
        jj, acc, carry = state
        slot = jj & 1
        wait(slot, jj)

        @pl.when(jj + 1 < n_pages)
        def _():
            start(1 - slot, jj + 1)

        kt = kbuf[slot]
        vt = vbuf[slot]
        z = jnp.concatenate(
            [_dot(qq[p * rows:(p + 1) * rows], kt[p * LANES:(p + 1) * LANES, :].astype(BF16))
             for p in range(n_blocks)], axis=0)
        w, carry = weights(z, carry, None)
        acc = acc + jnp.concatenate(
            [_dot_tb(w[p * rows:(p + 1) * rows], vt[p * LANES:(p + 1) * LANES, :].astype(BF16))
             for p in range(n_blocks)], axis=0)
        return jj + 1, acc, carry

    jj, acc, _ = lax.while_loop(more, body, (jnp.int32(0), acc, carry))

    @pl.when(jj < n_pages)
    def _():
        wait(jj & 1, jj)

    for p in range(n_blocks):
        o_ref[0, :, p * LANES:(p + 1) * LANES] = _sb_finish(
            acc[p * rows:(p + 1) * rows, :], dec_t, g_ref[:, p * LANES:(p + 1) * LANES])


def _sb_decode(page_table, q, k_new, v_new, kt_cache, vt_cache, tri, g):
    b, dec_t, width = q.shape
    n_pages = page_table.shape[1]
    page = kt_cache.shape[2]
    n_blocks = width // LANES
    per_b = lambda bi, pt: (bi, 0, 0)
    const = lambda bi, pt: (0, 0)
    grid_spec = pltpu.PrefetchScalarGridSpec(
        num_scalar_prefetch=1,
        grid=(b,),
        in_specs=[
            pl.BlockSpec((1, dec_t, width), per_b),
            pl.BlockSpec((1, page, width), per_b),
            pl.BlockSpec((1, page, width), per_b),
            pl.BlockSpec(memory_space=pl.ANY),
            pl.BlockSpec(memory_space=pl.ANY),
            pl.BlockSpec((page, page), const),
            pl.BlockSpec((1, width), const),
        ],
        out_specs=pl.BlockSpec((1, dec_t, width), per_b),
        scratch_shapes=[pltpu.VMEM((2, width, page), F32),
                        pltpu.VMEM((2, width, page), F32),
                        pltpu.SemaphoreType.DMA((2, 2))],
    )
    return pl.pallas_call(
        functools.partial(_sb_decode_body, dec_t=dec_t, n_blocks=n_blocks, page=page, n_pages=n_pages),
        grid_spec=grid_spec,
        out_shape=jax.ShapeDtypeStruct((b, dec_t, width), F32),
        compiler_params=pltpu.CompilerParams(
            dimension_semantics=("arbitrary",), vmem_limit_bytes=VMEM_LIMIT),
        name="sb_decode",
    )(page_table, q, k_new, v_new, kt_cache, vt_cache, tri, g)


def _df_decode_body(pt_ref, q_ref, kn_ref, vn_ref, *rest, dec_t, n_blocks, n_slots, lam_init):
    kc_refs, vc_refs = rest[:n_slots], rest[n_slots:2 * n_slots]
    (bias_ref, bias_new_ref, g_ref, lq1, lk1, lq2, lk2, o_ref,
     qq_ref, m_ref, l_ref, acc_ref) = rest[2 * n_slots:]
    jj = pl.program_id(1)
    rows = 2 * dec_t
    page = bias_new_ref.shape[1]

    def update(k_refs, v_refs, bias):
        own = [pl.ds(h, page, stride=n_blocks) for h in range(n_blocks)]
        gather = lambda refs, h: jnp.concatenate([r[0, own[h], :].astype(BF16) for r in refs], axis=0)
        qq = qq_ref[...]
        s = jnp.concatenate(
            [_dot_tb(qq[h * rows:(h + 1) * rows], gather(k_refs, h)) for h in range(n_blocks)],
            axis=0) + bias[...]
        m_old = m_ref[...]
        m_new = jnp.maximum(m_old, jnp.max(s, axis=-1, keepdims=True))
        alpha = jnp.exp(m_old - m_new)
        p = jnp.exp(s - m_new)
        l_ref[...] = alpha * l_ref[...] + jnp.sum(p, axis=-1, keepdims=True)
        m_ref[...] = m_new
        p = p.astype(BF16)
        pv = jnp.concatenate(
            [_dot(p[h * rows:(h + 1) * rows], gather(v_refs, h)) for h in range(n_blocks)], axis=0)
        acc_ref[...] = alpha * acc_ref[...] + pv

    @pl.when(jj == 0)
    def _():
        qq_ref[...] = _stack_decode_queries(q_ref[0], n_blocks)
        m_ref[...] = jnp.full(m_ref.shape, NEG, F32)
        l_ref[...] = jnp.zeros_like(l_ref)
        acc_ref[...] = jnp.zeros_like(acc_ref)
        update([kn_ref], [vn_ref], bias_new_ref)

    update(kc_refs, vc_refs, bias_ref.at[0])

    @pl.when(jj == pl.num_programs(1) - 1)
    def _():
        lam = _lambda(lq1, lk1, lq2, lk2, lam_init)
        for p in range(n_blocks):
            sl = slice(p * rows, (p + 1) * rows)
            o_ref[0, :, p * LANES:(p + 1) * LANES] = _df_finish(
                acc_ref[sl, :], l_ref[sl, :], dec_t, lam, g_ref[...], lam_init)


def _df_decode(page_table, q, k_new, v_new, k_cache, v_cache, bias_s, bias_new, g, lams, lam_init,
               n_slots):
    b, dec_t, width = q.shape
    n_pages = page_table.shape[1]
    page_rows = k_cache.shape[1]
    n_blocks = width // LANES
    rows = n_blocks * 2 * dec_t
    n_steps = n_pages // n_slots
    per_b = lambda bi, jj, pt: (bi, 0, 0)
    const = lambda bi, jj, pt: (0, 0)
    vec = pl.BlockSpec((1, DH), const)

    def slot_spec(u):
        return pl.BlockSpec((1, page_rows, LANES), lambda bi, jj, pt: (pt[bi, jj * n_slots + u], 0, 0))

    grid_spec = pltpu.PrefetchScalarGridSpec(
        num_scalar_prefetch=1,
        grid=(b, n_steps),
        in_specs=[
            pl.BlockSpec((1, dec_t, width), per_b),
            pl.BlockSpec((1,) + k_new.shape[1:], per_b),
            pl.BlockSpec((1,) + v_new.shape[1:], per_b),
            *[slot_spec(u) for u in range(n_slots)],
            *[slot_spec(u) for u in range(n_slots)],
            pl.BlockSpec((1, rows, bias_s.shape[2]),
                         lambda bi, jj, pt: (jnp.where(jj == n_steps - 1, 1, 0), 0, 0)),
            pl.BlockSpec(bias_new.shape, const),
            pl.BlockSpec((1, LANES), const),
            vec, vec, vec, vec,
        ],
        out_specs=pl.BlockSpec((1, dec_t, width), per_b),
        scratch_shapes=[pltpu.VMEM((rows, LANES), BF16),
                        pltpu.VMEM((rows, 1), F32),
                        pltpu.VMEM((rows, 1), F32),
                        pltpu.VMEM((rows, LANES), F32)],
    )
    return pl.pallas_call(
        functools.partial(_df_decode_body, dec_t=dec_t, n_blocks=n_blocks, n_slots=n_slots,
                          lam_init=lam_init),
        grid_spec=grid_spec,
        out_shape=jax.ShapeDtypeStruct((b, dec_t, width), F32),
        compiler_params=pltpu.CompilerParams(
            dimension_semantics=("arbitrary", "arbitrary"),
            vmem_limit_bytes=VMEM_LIMIT),
        name="df_decode",
    )(page_table, q, k_new, v_new, *([k_cache] * n_slots), *([v_cache] * n_slots),
      bias_s, bias_new, g, *lams)


def _router_gates(logits_t, bias_col, n_experts):
    tm = logits_t.shape[1]
    gsz = n_experts // N_GROUPS
    scores = 1.0 / (1.0 + jnp.exp(-logits_t))
    sel = scores + bias_col
    sub = lax.broadcasted_iota(I32, (gsz, tm), 0)
    group_scores = []
    for g in range(N_GROUPS):
        blk = sel[g * gsz:(g + 1) * gsz]
        m1 = jnp.max(blk, axis=0, keepdims=True)
        first = jnp.min(jnp.where(blk == m1, sub, gsz), axis=0, keepdims=True)
        m2 = jnp.max(jnp.where(sub == first, -jnp.inf, blk), axis=0, keepdims=True)
        group_scores.append(m1 + m2)
    gs = jnp.concatenate(group_scores, axis=0)
    gidx = lax.broadcasted_iota(I32, (N_GROUPS, tm), 0)
    grank = jnp.zeros((N_GROUPS, tm), I32)
    for g in range(N_GROUPS):
        row = gs[g:g + 1]
        ahead = (row > gs) | ((row == gs) & (gidx > g))
        grank = grank + ahead.astype(I32)
    gkeep = grank < TOPK_GROUPS
    masked = jnp.concatenate(
        [jnp.where(jnp.broadcast_to(gkeep[g:g + 1], (gsz, tm)), sel[g * gsz:(g + 1) * gsz], -jnp.inf)
         for g in range(N_GROUPS)], axis=0)
    eidx = lax.broadcasted_iota(I32, (n_experts, tm), 0)
    erank = jnp.zeros((n_experts, tm), I32)
    for e in range(n_experts):
        row = masked[e:e + 1]
        ahead = (row > masked) | ((row == masked) & (eidx > e))
        erank = erank + ahead.astype(I32)
    w = jnp.where(erank < TOP_K, scores, 0.0)
    return w / jnp.sum(w, axis=0, keepdims=True) * ROUTED_SCALE


def _post_body(msb_ref, mdf_ref, x_ref, wo_sb_ref, wo_df_ref, g2_ref, rw_hi_ref, rw_lo_ref,
               rb_ref, x1_ref, h2_ref, gates_ref, *, n_experts):
    att = _dot(msb_ref[0].astype(BF16), wo_sb_ref[...]) + _dot(mdf_ref[0].astype(BF16), wo_df_ref[...])
    x1 = x_ref[0] + att
    x1_ref[...] = x1
    ms = jnp.mean(x1 * x1, axis=-1, keepdims=True)
    h2 = (x1 * lax.rsqrt(ms + EPS)) * g2_ref[...]
    h2_hi, h2_lo = _split_bf16(h2)
    h2_ref[...] = h2_hi
    rw_hi = rw_hi_ref[...]
    logits_t = _dot_tb(rw_hi, h2_hi) + _dot_tb(rw_hi, h2_lo) + _dot_tb(rw_lo_ref[...], h2_hi)
    gates_t = _router_gates(logits_t, rb_ref[...], n_experts)
    tm = gates_t.shape[1]
    pad = gates_ref.shape[1] - n_experts
    gates_t = jnp.concatenate([gates_t, jnp.zeros((pad, tm), F32)], axis=0)
    gates_ref[...] = gates_t.T.astype(gates_ref.dtype)


def _post_attention(msb, mdf, x, wo_sb, wo_df, g2, rw_hi, rw_lo, rb, tm):
    b, s, d = x.shape
    n = b * s
    per_b = s // tm
    n_experts = rw_hi.shape[0]
    ge = _round_up(n_experts, LANES)
    row = lambda bi, i: (bi * per_b + i, 0)
    const = lambda bi, i: (0, 0)
    return pl.pallas_call(
        functools.partial(_post_body, n_experts=n_experts),
        grid=(b, per_b),
        in_specs=[
            pl.BlockSpec((1, tm, msb.shape[2]), lambda bi, i: (bi, i, 0)),
            pl.BlockSpec((1, tm, mdf.shape[2]), lambda bi, i: (bi, i, 0)),
            pl.BlockSpec((1, tm, d), lambda bi, i: (bi, i, 0)),
            pl.BlockSpec(wo_sb.shape, const),
            pl.BlockSpec(wo_df.shape, const),
            pl.BlockSpec((1, d), const),
            pl.BlockSpec(rw_hi.shape, const),
            pl.BlockSpec(rw_lo.shape, const),
            pl.BlockSpec((n_experts, 1), const),
        ],
        out_specs=[pl.BlockSpec((tm, d), row), pl.BlockSpec((tm, d), row), pl.BlockSpec((tm, ge), row)],
        out_shape=[jax.ShapeDtypeStruct((n, d), F32), jax.ShapeDtypeStruct((n, d), BF16),
                   jax.ShapeDtypeStruct((n, ge), F32)],
        compiler_params=pltpu.CompilerParams(
            dimension_semantics=("arbitrary", "arbitrary"), vmem_limit_bytes=VMEM_LIMIT),
        name="post_attention",
    )(msb, mdf, x, wo_sb, wo_df, g2, rw_hi, rw_lo, rb)


def _silu(x):
    return x / (1.0 + jnp.exp(-x))


def _moe_body(h2_ref, x1_ref, gates_ref, wg_ref, wu_ref, wd_ref, sg_ref, su_ref, sd_ref,
              o_ref, acc_ref, *, group):
    eg = pl.program_id(1)
    h2 = h2_ref[...]

    @pl.when(eg == 0)
    def _():
        a = _silu(_dot(h2, sg_ref[...])) * _dot(h2, su_ref[...])
        acc_ref[...] = _dot(a.astype(BF16), sd_ref[...])

    f = wg_ref.shape[2]
    gates = gates_ref[...]
    lane = lax.broadcasted_iota(I32, gates.shape, 1)
    acts = []
    for j in range(group):
        gate = jnp.sum(jnp.where(lane == eg * group + j, gates, 0.0), axis=-1, keepdims=True)
        acts.append((_silu(_dot(h2, wg_ref[j])) * _dot(h2, wu_ref[j]) * gate).astype(BF16))
    a = jnp.concatenate(acts, axis=1)
    wd = wd_ref[...].reshape(group * f, wd_ref.shape[2])
    acc_ref[...] += _dot(a, wd)

    @pl.when(eg == pl.num_programs(1) - 1)
    def _():
        o_ref[...] = x1_ref[...] + acc_ref[...]


def _moe(h2, x1, gates, w_gate, w_up, w_down, sg, su, sd, tm, group):
    n, d = x1.shape
    n_experts, _, f = w_gate.shape
    row = lambda i, e: (i, 0)
    const = lambda i, e: (0, 0)
    return pl.pallas_call(
        functools.partial(_moe_body, group=group),
        grid=(n // tm, n_experts // group),
        in_specs=[
            pl.BlockSpec((tm, d), row),
            pl.BlockSpec((tm, d), row),
            pl.BlockSpec((tm, gates.shape[1]), row),
            pl.BlockSpec((group, d, f), lambda i, e: (e, 0, 0)),
            pl.BlockSpec((group, d, f), lambda i, e: (e, 0, 0)),
            pl.BlockSpec((group, f, d), lambda i, e: (e, 0, 0)),
            pl.BlockSpec(sg.shape, const),
            pl.BlockSpec(su.shape, const),
            pl.BlockSpec(sd.shape, const),
        ],
        out_specs=pl.BlockSpec((tm, d), row),
        out_shape=jax.ShapeDtypeStruct((n, d), F32),
        scratch_shapes=[pltpu.VMEM((tm, d), F32)],
        compiler_params=pltpu.CompilerParams(
            dimension_semantics=("arbitrary", "arbitrary"), vmem_limit_bytes=VMEM_LIMIT),
        name="moe",
    )(h2, x1, gates, w_gate, w_up, w_down, sg, su, sd)


def _largest_tile(n, cap):
    t = cap
    while n % t:
        t //= 2
    return t


def kernel(x_prompt, x_sample, cache_k_sb, cache_v_sb, cache_k_diff, cache_v_diff, page_table,
           meta_tokens, rel_bias, norm1_g, w_in, q_norm_g, k_norm_g, lambda_q1, lambda_k1,
           lambda_q2, lambda_k2, sb_out_g, diff_subln_g, w_out, norm2_g, router_w, router_bias,
           w_gate, w_up, w_down, w_shared_gate, w_shared_up, w_shared_down):
    assert w_in.shape[0] == 1, "single-layer step"
    b, seq, d = x_prompt.shape
    dec_b, dec_t, _ = x_sample.shape
    n_meta = meta_tokens.shape[0]
    width = sb_out_g.shape[1]
    n_pool, page = cache_k_sb.shape[1], cache_k_sb.shape[2]
    n_experts = router_w.shape[2]
    t = seq + n_meta
    blk = ATT_BLOCK
    lam_init = 0.8 - 0.6 * math.exp(-0.3 * 0)

    w_in_b = w_in[0].astype(BF16)
    g1 = norm1_g
    reps = width // DH
    qg = jnp.tile(q_norm_g, (1, reps))
    kg = jnp.tile(k_norm_g, (1, reps))
    lane = jnp.arange(width)
    gm = ((lane[:, None] // DH) == (lane[None, :] // DH)).astype(BF16) * (1.0 / DH)
    tri_p = (jnp.arange(blk)[:, None] >= jnp.arange(blk)[None, :]).astype(BF16)
    tri_d = (jnp.arange(page)[:, None] >= jnp.arange(page)[None, :]).astype(BF16)
    lams = (lambda_q1, lambda_k1, lambda_q2, lambda_k2)
    wo_sb = w_out[0, :width].astype(BF16)
    wo_df = w_out[0, width:].astype(BF16)
    rw_t = router_w[0].T
    rw_hi = rw_t.astype(BF16)
    rw_lo = (rw_t - rw_hi.astype(F32)).astype(BF16)
    rb = router_bias[0][:, None]
    sg = w_shared_gate[0].astype(BF16)
    su = w_shared_up[0].astype(BF16)
    sd = w_shared_down[0].astype(BF16)

    n_h_sb = width // DH
    n_h_df = width // LANES
    n_slots = DECODE_PAGES_PER_STEP
    assert blk >= MAX_DISTANCE and page >= MAX_DISTANCE
    assert page_table.shape[1] % n_slots == 0 and dec_t * n_h_df <= page
    pre = blk - n_meta
    assert 0 <= pre < blk and seq % blk == 0
    bias_p, bias_s, bias_n = _bias_tiles(rel_bias, blk, pre, dec_t, page, n_slots)
    head = jnp.concatenate([jnp.zeros((pre, d), F32), meta_tokens], axis=0)
    (k_sb_p, v_sb_p, k_df_p, v_df_p,
     qsb_b, ksb_b, vsb_b, qdf_b, kdf_b, vdf_b) = _project_prompt(head, x_prompt, n_meta, g1, w_in_b, qg, kg, gm)
    msb_p = _sb_prompt(qsb_b, ksb_b, vsb_b, tri_p, sb_out_g, blk, pre)
    mdf_p = _df_prompt(rel_bias, qdf_b, kdf_b, vdf_b, bias_p, diff_subln_g, lams, blk, pre, lam_init)

    n_s = dec_b * dec_t
    xs = x_sample.reshape(1, n_s, d)
    (k_sb_s, v_sb_s, k_df_s, v_df_s,
     qsb_s, _, _, qdf_s, _, _) = _project(xs, n_s, g1, w_in_b, qg, kg, gm, n_s)

    def new_page(rows, n_rows, n_pad):
        r = rows.reshape(dec_b, n_rows, -1)
        return jnp.pad(r, ((0, 0), (0, n_pad - n_rows), (0, 0)))

    transposed_pages = lambda c: jnp.transpose(c[0], (0, 2, 3, 1)).reshape(n_pool, width, page)
    interleaved_rows = lambda c: c.reshape(n_pool, page * n_h_df, LANES)
    msb_s = _sb_decode(page_table, qsb_s.reshape(dec_b, dec_t, width),
                       new_page(k_sb_s, dec_t, page), new_page(v_sb_s, dec_t, page),
                       transposed_pages(cache_k_sb), transposed_pages(cache_v_sb), tri_d, sb_out_g)
    mdf_s = _df_decode(page_table, qdf_s.reshape(dec_b, dec_t, width),
                       new_page(k_df_s, dec_t * n_h_df, page * n_h_df),
                       new_page(v_df_s, dec_t * n_h_df, page * n_h_df),
                       interleaved_rows(cache_k_diff), interleaved_rows(cache_v_diff),
                       bias_s, bias_n, diff_subln_g, lams, lam_init, n_slots)

    wg_b, wu_b, wd_b = w_gate[0].astype(BF16), w_up[0].astype(BF16), w_down[0].astype(BF16)
    x1_p, h2_p, gates_p = _post_attention(msb_p, mdf_p, x_prompt, wo_sb, wo_df, norm2_g,
                                          rw_hi, rw_lo, rb, _largest_tile(seq, 512))
    y_p = _moe(h2_p, x1_p, gates_p, wg_b, wu_b, wd_b, sg, su, sd,
               _largest_tile(b * seq, MOE_TOKEN_TILE), MOE_EXPERT_GROUP)
    y_prompt = y_p.reshape(b, seq, d)
    x1_s, h2_s, gates_s = _post_attention(
        msb_s.reshape(1, n_s, width), mdf_s.reshape(1, n_s, width), x_sample.reshape(1, n_s, d),
        wo_sb, wo_df, norm2_g, rw_hi, rw_lo, rb, n_s)
    y_s = _moe(h2_s, x1_s, gates_s, wg_b, wu_b, wd_b, sg, su, sd, n_s, MOE_EXPERT_GROUP)
    y_sample = y_s.reshape(dec_b, dec_t, d)

    sb_rows = lambda r, bb, tt: r.reshape(1, bb, tt, n_h_sb, DH)
    df_rows = lambda r, bb, tt: r.reshape(1, bb, tt, n_h_df, LANES)
    return (y_prompt, y_sample,
            sb_rows(k_sb_p, b, t), sb_rows(v_sb_p, b, t), df_rows(k_df_p, b, t), df_rows(v_df_p, b, t),
            sb_rows(k_sb_s, dec_b, dec_t), sb_rows(v_sb_s, dec_b, dec_t),
            df_rows(k_df_s, dec_b, dec_t), df_rows(v_df_s, dec_b, dec_t))
```

```python
import functools
import math

import jax
import jax.numpy as jnp
from jax import lax
from jax.experimental import pallas as pl
from jax.experimental.pallas import tpu as pltpu

F32 = jnp.float32
BF16 = jnp.bfloat16
I32 = jnp.int32

EPS = 1e-6
DH = 64
LANES = 128
MAX_DISTANCE = 128
TOP_K = 8
N_GROUPS = 8
TOPK_GROUPS = 4
ROUTED_SCALE = 2.5
NEG = -1e30
SB_CUTOFF = -104.0
ATT_BLOCK = 256
DF_HEADS_PER_STEP = 4
SB_PAIRS_PER_STEP = 4
DECODE_PAGES_PER_STEP = 32
MOE_TOKEN_TILE = 1024
MOE_EXPERT_GROUP = 8
VMEM_LIMIT = 60 * 1024 * 1024

_TRANS_B = (((1,), (1,)), ((), ()))


def _dot(a, b):
    return jnp.dot(a, b, preferred_element_type=F32)


def _dot_tb(a, b):
    return lax.dot_general(a, b, _TRANS_B, preferred_element_type=F32)


def _split_bf16(x):
    hi = x.astype(BF16)
    lo = (x - hi.astype(F32)).astype(BF16)
    return hi, lo


def _round_up(n, m):
    return (n + m - 1) // m * m


def _proj_rows(x, g1_ref, w_ref, qg_ref, kg_ref, gm_ref, ksb_o, vsb_o, kdf_o, vdf_o,
               qsb_b, ksb_b, vsb_b, qdf_b, kdf_b, vdf_b, width):
    ms = jnp.mean(x * x, axis=-1, keepdims=True)
    h = (x * lax.rsqrt(ms + EPS)) * g1_ref[...]
    proj = _dot(h.astype(BF16), w_ref[...])
    w = width
    q_sb, k_sb, v_sb = proj[:, 0:w], proj[:, w:2 * w], proj[:, 2 * w:3 * w]
    q_df, k_df, v_df = proj[:, 3 * w:4 * w], proj[:, 4 * w:5 * w], proj[:, 5 * w:6 * w]

    def map_norm(t, g_ref):
        msq = _dot((t * t).astype(BF16), gm_ref[...])
        return (t * lax.rsqrt(msq + EPS)) * g_ref[...]

    q_df = map_norm(q_df, qg_ref)
    k_df = map_norm(k_df, kg_ref)
    ksb_o[...] = k_sb
    vsb_o[...] = v_sb
    tm = x.shape[0]
    n_df = w // LANES
    for hd in range(n_df):
        rows = pl.ds(hd, tm, stride=n_df)
        kdf_o[rows, :] = k_df[:, hd * LANES:(hd + 1) * LANES]
        vdf_o[rows, :] = v_df[:, hd * LANES:(hd + 1) * LANES]
    scale = DH ** -0.5
    qsb_b[0] = (q_sb * scale).astype(BF16)
    ksb_b[0] = k_sb.astype(BF16)
    vsb_b[0] = v_sb.astype(BF16)
    qdf_b[0] = (q_df * scale).astype(BF16)
    kdf_b[0] = k_df.astype(BF16)
    vdf_b[0] = v_df.astype(BF16)


def _proj_body(x_ref, g1_ref, w_ref, qg_ref, kg_ref, gm_ref,
               ksb_o, vsb_o, kdf_o, vdf_o, *bf_outs, width):
    _proj_rows(x_ref[0], g1_ref, w_ref, qg_ref, kg_ref, gm_ref,
               ksb_o.at[0], vsb_o.at[0], kdf_o.at[0], vdf_o.at[0], *bf_outs, width)


def _proj_prompt_body(head_ref, x_ref, g1_ref, w_ref, qg_ref, kg_ref, gm_ref,
                      ksb_hbm, vsb_hbm, kdf_hbm, vdf_hbm, *rest, width, lead):
    bf_outs, (st_sb, st_df, sem) = rest[:6], rest[6:]
    b, i = pl.program_id(0), pl.program_id(1)
    n_i = pl.num_programs(1)
    step = b * n_i + i
    slot = step & 1
    tm = x_ref.shape[1]
    n_df = width // LANES

    def copies(slot, bb, ii, head):
        if head:
            src, dst, n = tm - lead, 0, lead
        else:
            src, dst, n = 0, lead + (ii - 1) * tm, tm
        out = []
        for kv, hbm in enumerate((ksb_hbm, vsb_hbm)):
            out.append(pltpu.make_async_copy(st_sb.at[slot, kv, pl.ds(src, n), :],
                                             hbm.at[bb, pl.ds(dst, n), :], sem.at[slot]))
        for kv, hbm in enumerate((kdf_hbm, vdf_hbm)):
            out.append(pltpu.make_async_copy(st_df.at[slot, kv, pl.ds(src * n_df, n * n_df), :],
                                             hbm.at[bb, pl.ds(dst * n_df, n * n_df), :], sem.at[slot]))
        return out

    def wait_step(slot, was_head):
        @pl.when(was_head)
        def _():
            for cp in copies(slot, 0, 1, True):
                cp.wait()

        @pl.when(jnp.logical_not(was_head))
        def _():
            for cp in copies(slot, 0, 1, False):
                cp.wait()

    @pl.when(step >= 2)
    def _():
        wait_step(slot, i == 2)

    x = jnp.where(i == 0, head_ref[...], x_ref[0])
    _proj_rows(x, g1_ref, w_ref, qg_ref, kg_ref, gm_ref,
               st_sb.at[slot, 0], st_sb.at[slot, 1], st_df.at[slot, 0], st_df.at[slot, 1],
               *bf_outs, width)

    @pl.when(i == 0)
    def _():
        for cp in copies(slot, b, i, True):
            cp.start()

    @pl.when(i > 0)
    def _():
        for cp in copies(slot, b, i, False):
            cp.start()

    @pl.when(step == pl.num_programs(0) * n_i - 1)
    def _():
        wait_step(1 - slot, i == 1)
        wait_step(slot, i == 0)


def _project(x_pad, t_valid, g1, w_in_b, qg, kg, gm, tm):
    b, tp, d = x_pad.shape
    width = w_in_b.shape[1] // 6
    grid = (b, tp // tm)
    row = lambda bi, i: (bi, i, 0)
    const = lambda bi, i: (0, 0)
    n_df = width // LANES
    f32_out = jax.ShapeDtypeStruct((b, t_valid, width), F32)
    df_out = jax.ShapeDtypeStruct((b, t_valid * n_df, LANES), F32)
    bf_out = jax.ShapeDtypeStruct((b, tp, width), BF16)
    out_spec = pl.BlockSpec((1, tm, width), row)
    df_spec = pl.BlockSpec((1, tm * n_df, LANES), row)
    return pl.pallas_call(
        functools.partial(_proj_body, width=width),
        grid=grid,
        in_specs=[
            pl.BlockSpec((1, tm, d), row),
            pl.BlockSpec((1, d), const),
            pl.BlockSpec(w_in_b.shape, const),
            pl.BlockSpec((1, width), const),
            pl.BlockSpec((1, width), const),
            pl.BlockSpec((width, width), const),
        ],
        out_specs=[out_spec] * 2 + [df_spec] * 2 + [out_spec] * 6,
        out_shape=[f32_out] * 2 + [df_out] * 2 + [bf_out] * 6,
        compiler_params=pltpu.CompilerParams(
            dimension_semantics=("arbitrary", "arbitrary"),
            vmem_limit_bytes=VMEM_LIMIT),
        name="proj",
    )(x_pad, g1, w_in_b, qg, kg, gm)


def _project_prompt(head, x, lead, g1, w_in_b, qg, kg, gm):
    b, s, d = x.shape
    tm = head.shape[0]
    width = w_in_b.shape[1] // 6
    n_df = width // LANES
    n_i = 1 + s // tm
    assert s % tm == 0 and n_i >= 3 and lead % 8 == 0 and 0 < lead <= tm
    row = lambda bi, i: (bi, i, 0)
    const = lambda bi, i: (0, 0)
    anywhere = pl.BlockSpec(memory_space=pl.ANY)
    bf_out = jax.ShapeDtypeStruct((b, tm + s, width), BF16)
    return pl.pallas_call(
        functools.partial(_proj_prompt_body, width=width, lead=lead),
        grid=(b, n_i),
        in_specs=[
            pl.BlockSpec((tm, d), const),
            pl.BlockSpec((1, tm, d), lambda bi, i: (bi, jnp.maximum(i - 1, 0), 0)),
            pl.BlockSpec((1, d), const),
            pl.BlockSpec(w_in_b.shape, const),
            pl.BlockSpec((1, width), const),
            pl.BlockSpec((1, width), const),
            pl.BlockSpec((width, width), const),
        ],
        out_specs=[anywhere] * 4 + [pl.BlockSpec((1, tm, width), row)] * 6,
        out_shape=[jax.ShapeDtypeStruct((b, lead + s, width), F32)] * 2
        + [jax.ShapeDtypeStruct((b, (lead + s) * n_df, LANES), F32)] * 2 + [bf_out] * 6,
        scratch_shapes=[pltpu.VMEM((2, 2, tm, width), F32),
                        pltpu.VMEM((2, 2, tm * n_df, LANES), F32),
                        pltpu.SemaphoreType.DMA((2,))],
        compiler_params=pltpu.CompilerParams(
            dimension_semantics=("arbitrary", "arbitrary"),
            vmem_limit_bytes=VMEM_LIMIT),
        name="proj_prompt",
    )(head, x, g1, w_in_b, qg, kg, gm)


def _stack_halves(q):
    lane = lax.broadcasted_iota(I32, (1, LANES), 1)
    zero = jnp.zeros_like(q)
    return jnp.concatenate(
        [jnp.where(lane < DH, q, zero), jnp.where(lane >= DH, q, zero)], axis=0)


def _softplus(z):
    return jnp.maximum(z, 0.0) + jnp.log(1.0 + jnp.exp(-jnp.abs(z)))


def _sb_blocks(qqs, ks, vs, tri, state, vis):
    n = range(len(qqs))
    zs = [_dot_tb(qqs[p], ks[p]) for p in n]
    drops = [_softplus(z) for z in zs]
    if vis is not None:
        drops = [jnp.where(vis, d, 0.0) for d in drops]
    csums = [state[p][1] - _dot(drops[p].astype(BF16), tri) for p in n]
    ws = [jnp.exp(zs[p] + csums[p]) for p in n]
    if vis is not None:
        ws = [jnp.where(vis, w, 0.0) for w in ws]
    pvs = [_dot(ws[p].astype(BF16), vs[p]) for p in n]
    return [(state[p][0] + pvs[p], csums[p][:, 0:1]) for p in n]


def _sb_finish(acc, t, g):
    lane = lax.broadcasted_iota(I32, (1, LANES), 1)
    lo_half = lane < DH
    o = jnp.where(lo_half, acc[:t], acc[t:])
    o2 = o * o
    s_lo = jnp.sum(jnp.where(lo_half, o2, 0.0), axis=-1, keepdims=True)
    s_hi = jnp.sum(jnp.where(lo_half, 0.0, o2), axis=-1, keepdims=True)
    ms = jnp.where(lo_half, s_lo, s_hi) * (1.0 / DH)
    return (o * lax.rsqrt(ms + EPS)) * g


def _df_block(qq, k, v, bias, m, l, acc):
    s = _dot_tb(qq, k) + bias
    m_new = jnp.maximum(m, jnp.max(s, axis=-1, keepdims=True))
    alpha = jnp.exp(m - m_new)
    p = jnp.exp(s - m_new)
    l = alpha * l + jnp.sum(p, axis=-1, keepdims=True)
    acc = alpha * acc + _dot(p.astype(BF16), v)
    return m_new, l, acc


def _lambda(lq1, lk1, lq2, lk2, lam_init):
    s1 = jnp.sum(lq1[...] * lk1[...], axis=-1, keepdims=True)
    s2 = jnp.sum(lq2[...] * lk2[...], axis=-1, keepdims=True)
    return jnp.exp(s1) - jnp.exp(s2) + lam_init


def _df_finish(acc, l, t, lam, g, lam_init):
    o = acc[:t] / l[:t] - lam * (acc[t:] / l[t:])
    ms = jnp.mean(o * o, axis=-1, keepdims=True)
    return (o * lax.rsqrt(ms + EPS)) * (g * (1.0 - lam_init))


def _bias_tile(tab_ref, h, rel, n_buckets):
    max_exact = n_buckets // 2
    n = jnp.maximum(rel, 0)
    nf = jnp.maximum(n, 1).astype(F32)
    large = max_exact + (jnp.log(nf / max_exact) / math.log(MAX_DISTANCE / max_exact)
                         * (n_buckets - max_exact)).astype(I32)
    large = jnp.minimum(large, n_buckets - 1)
    bucket = jnp.where(n < max_exact, n, large)
    bias = jnp.zeros(rel.shape, F32)
    for b in range(n_buckets):
        bias = jnp.where(bucket == b, tab_ref[b, h], bias)
    return jnp.where(rel >= 0, bias, NEG)


def _bias_body(tab_ref, bp_ref, bs_ref, bn_ref, *, n_buckets, n_heads, blk, pre, dec_t, page, n_slots):
    r = lax.broadcasted_iota(I32, (blk, 2 * blk), 0)
    c = lax.broadcasted_iota(I32, (blk, 2 * blk), 1)
    rs = lax.broadcasted_iota(I32, (dec_t, page), 0)
    cs = lax.broadcasted_iota(I32, (dec_t, page), 1)
    for h in range(n_heads):
        tile = _bias_tile(tab_ref, h, blk + r - c, n_buckets)
        bp_ref[h, 0] = tile
        bp_ref[h, 1] = jnp.where(c < pre, NEG, tile)
        far = jnp.full((dec_t, page), tab_ref[n_buckets - 1, h], F32)
        last = _bias_tile(tab_ref, h, page + rs - cs, n_buckets)
        new = jnp.where(cs < dec_t, _bias_tile(tab_ref, h, rs - cs, n_buckets), NEG)
        for mp in range(2):
            rows = pl.ds((2 * h + mp) * dec_t, dec_t)
            bn_ref[rows, :] = new
            for u in range(n_slots):
                bs_ref[0, rows, u * page:(u + 1) * page] = far
                bs_ref[1, rows, u * page:(u + 1) * page] = last if u == n_slots - 1 else far


def _bias_tiles(rel_bias, blk, pre, dec_t, page, n_slots):
    n_buckets, n_heads = rel_bias.shape
    rows = 2 * n_heads * dec_t
    return pl.pallas_call(
        functools.partial(_bias_body, n_buckets=n_buckets, n_heads=n_heads, blk=blk, pre=pre,
                          dec_t=dec_t, page=page, n_slots=n_slots),
        in_specs=[pl.BlockSpec(memory_space=pltpu.SMEM)],
        out_shape=[jax.ShapeDtypeStruct((n_heads, 2, blk, 2 * blk), F32),
                   jax.ShapeDtypeStruct((2, rows, n_slots * page), F32),
                   jax.ShapeDtypeStruct((rows, page), F32)],
        name="bias_tiles",
    )(rel_bias)


def _sb_prompt_body(q_ref, k_ref, v_ref, tri_ref, g_ref, o_ref, *, blk, pre):
    i = pl.program_id(2) + 1
    pairs = range(SB_PAIRS_PER_STEP)
    lanes = lambda p: slice(p * LANES, (p + 1) * LANES)
    qq = [_stack_halves(q_ref[0, :, lanes(p)]) for p in pairs]
    tri = tri_ref[...]
    r = lax.broadcasted_iota(I32, (2 * blk, blk), 0)
    r = jnp.where(r >= blk, r - blk, r)
    c = lax.broadcasted_iota(I32, (2 * blk, blk), 1)
    vis = c < r

    def block(j, state, vis):
        keys = pl.ds(pl.multiple_of(j * blk, blk), blk)
        return _sb_blocks(qq, [k_ref[0, keys, lanes(p)] for p in pairs],
                          [v_ref[0, keys, lanes(p)] for p in pairs], tri, state, vis)

    zero = (jnp.zeros((2 * blk, LANES), F32), jnp.zeros((2 * blk, 1), F32))
    state = block(i, [zero for _ in pairs], vis)

    def alive(state):
        return functools.reduce(jnp.maximum, [jnp.max(carry) for _, carry in state]) > SB_CUTOFF

    def more(loop):
        jj, state = loop
        return (jj < i - 1) & alive(state)

    def body(loop):
        jj, state = loop
        return jj + 1, block(i - 1 - jj, state, None)

    jj, state = lax.while_loop(more, body, (jnp.int32(0), state))
    state = lax.cond((jj == i - 1) & alive(state),
                     lambda s: block(0, s, c >= pre), lambda s: s, state)
    for p in pairs:
        o_ref[0, :, lanes(p)] = _sb_finish(state[p][0], blk, g_ref[:, lanes(p)]).astype(o_ref.dtype)


def _sb_prompt(q_b, k_b, v_b, tri, g, blk, pre):
    b, tp, width = q_b.shape
    w = SB_PAIRS_PER_STEP * LANES
    grid = (b, width // w, tp // blk - 1)
    return pl.pallas_call(
        functools.partial(_sb_prompt_body, blk=blk, pre=pre),
        grid=grid,
        in_specs=[
            pl.BlockSpec((1, blk, w), lambda bi, p, i: (bi, i + 1, p)),
            pl.BlockSpec((1, tp, w), lambda bi, p, i: (bi, 0, p)),
            pl.BlockSpec((1, tp, w), lambda bi, p, i: (bi, 0, p)),
            pl.BlockSpec((blk, blk), lambda bi, p, i: (0, 0)),
            pl.BlockSpec((1, w), lambda bi, p, i: (0, p)),
        ],
        out_specs=pl.BlockSpec((1, blk, w), lambda bi, p, i: (bi, i, p)),
        out_shape=jax.ShapeDtypeStruct((b, tp - blk, width), BF16),
        compiler_params=pltpu.CompilerParams(
            dimension_semantics=("arbitrary", "arbitrary", "arbitrary"),
            vmem_limit_bytes=VMEM_LIMIT),
        name="sb_prompt",
    )(q_b, k_b, v_b, tri, g)


def _df_prompt_body(tab_ref, q_ref, k_ref, v_ref, bias_ref, g_ref, lq1, lk1, lq2, lk2,
                    o_ref, m_ref, l_ref, acc_ref, *, blk, pre, lam_init, n_buckets):
    hg = pl.program_id(1)
    i = pl.program_id(2) + 1
    heads = range(DF_HEADS_PER_STEP)
    lanes = lambda hh: slice(hh * LANES, (hh + 1) * LANES)
    qq = [_stack_halves(q_ref[0, :, lanes(hh)]) for hh in heads]
    far_bias = [tab_ref[n_buckets - 1, hg * DF_HEADS_PER_STEP + hh] for hh in heads]

    def scores(hh, j, n, near):
        keys = pl.ds(pl.multiple_of(j * blk, blk), n * blk)
        s = _dot_tb(qq[hh], k_ref[0, keys, lanes(hh)])
        if near:
            t = bias_ref[hh, 0]
            return s + jnp.concatenate([t, t], axis=0), keys
        col = lax.broadcasted_iota(I32, (1, n * blk), 1)
        return s + jnp.where((j == 0) & (col < pre), NEG, far_bias[hh]), keys

    def sweep(visit):
        visit(i - 1, 2, True)
        n_far = i - 1
        n4 = lax.shift_right_logical(n_far, 2)

        def wide(g, carry):
            visit(g * 4, 4, False)
            return carry

        lax.fori_loop(0, n4, wide, 0)

        @pl.when((n_far & 2) != 0)
        def _():
            visit(n4 * 4, 2, False)

        @pl.when((n_far & 1) != 0)
        def _():
            visit(n_far - 1, 1, False)

    def lane_chunks(x):
        return [x[:, c * LANES:(c + 1) * LANES] for c in range(x.shape[1] // LANES)]

    m_ref[...] = jnp.full(m_ref.shape, NEG, F32)

    def visit_max(j, n, near):
        for hh in heads:
            s, _ = scores(hh, j, n, near)
            m_ref[hh] = functools.reduce(jnp.maximum, lane_chunks(s), m_ref[hh])

    sweep(visit_max)
    m = [jnp.max(m_ref[hh], axis=-1, keepdims=True) for hh in heads]

    l_ref[...] = jnp.zeros_like(l_ref)
    acc_ref[...] = jnp.zeros_like(acc_ref)

    def visit_sum(j, n, near):
        scored = [scores(hh, j, n, near) for hh in heads]
        ps = [jnp.exp(s - m[hh]) for hh, (s, _) in zip(heads, scored)]
        for hh in heads:
            l_ref[hh] = functools.reduce(jnp.add, lane_chunks(ps[hh]), l_ref[hh])
        pv = [_dot(ps[hh].astype(BF16), v_ref[0, scored[hh][1], lanes(hh)]) for hh in heads]
        for hh in heads:
            acc_ref[hh] += pv[hh]

    sweep(visit_sum)
    lam = _lambda(lq1, lk1, lq2, lk2, lam_init)
    for hh in heads:
        l = jnp.sum(l_ref[hh], axis=-1, keepdims=True)
        o_ref[0, :, lanes(hh)] = _df_finish(acc_ref[hh], l, blk, lam, g_ref[...], lam_init).astype(o_ref.dtype)


def _df_prompt(rel_bias, q_b, k_b, v_b, bias_p, g, lams, blk, pre, lam_init):
    b, tp, width = q_b.shape
    hps = DF_HEADS_PER_STEP
    w = hps * LANES
    grid = (b, width // w, tp // blk - 1)
    vec = pl.BlockSpec((1, DH), lambda bi, h, i: (0, 0))
    tile = lambda bi, h, i: (h, jnp.where(i == 0, 1, 0), 0, 0)
    return pl.pallas_call(
        functools.partial(_df_prompt_body, blk=blk, pre=pre, lam_init=lam_init,
                          n_buckets=rel_bias.shape[0]),
        grid=grid,
        in_specs=[
            pl.BlockSpec(memory_space=pltpu.SMEM),
            pl.BlockSpec((1, blk, w), lambda bi, h, i: (bi, i + 1, h)),
            pl.BlockSpec((1, tp, w), lambda bi, h, i: (bi, 0, h)),
            pl.BlockSpec((1, tp, w), lambda bi, h, i: (bi, 0, h)),
            pl.BlockSpec((hps, 1, blk, 2 * blk), tile),
            pl.BlockSpec((1, LANES), lambda bi, h, i: (0, 0)),
            vec, vec, vec, vec,
        ],
        out_specs=pl.BlockSpec((1, blk, w), lambda bi, h, i: (bi, i, h)),
        out_shape=jax.ShapeDtypeStruct((b, tp - blk, width), BF16),
        scratch_shapes=[pltpu.VMEM((hps, 2 * blk, LANES), F32)] * 3,
        compiler_params=pltpu.CompilerParams(
            dimension_semantics=("arbitrary", "arbitrary", "arbitrary"),
            vmem_limit_bytes=VMEM_LIMIT),
        name="df_prompt",
    )(rel_bias, q_b, k_b, v_b, bias_p, g, *lams)


def _stack_decode_queries(q, n_blocks):
    q = q.astype(F32)
    return jnp.concatenate(
        [_stack_halves(q[:, p * LANES:(p + 1) * LANES]) for p in range(n_blocks)], axis=0).astype(BF16)


def _sb_decode_body(pt_ref, q_ref, kn_ref, vn_ref, kc_hbm, vc_hbm, tri_ref, g_ref, o_ref,
                    kbuf, vbuf, sem, *, dec_t, n_blocks, page, n_pages):
    bi = pl.program_id(0)
    rows = 2 * dec_t
    qq = _stack_decode_queries(q_ref[0], n_blocks)
    tri = tri_ref[...]

    def page_copies(slot, jj, row=None):
        pg = pt_ref[bi if row is None else row, n_pages - 1 - jj]
        return (pltpu.make_async_copy(kc_hbm.at[pg], kbuf.at[slot], sem.at[0, slot]),
                pltpu.make_async_copy(vc_hbm.at[pg], vbuf.at[slot], sem.at[1, slot]))

    def start(slot, jj, row=None):
        for cp in page_copies(slot, jj, row):
            cp.start()

    def wait(slot, jj):
        for cp in page_copies(slot, jj):
            cp.wait()

    @pl.when(bi == 0)
    def _():
        start(0, 0)

    def weights(z, carry, vis):
        drop = _softplus(z)
        if vis is not None:
            drop = jnp.where(vis, drop, 0.0)
        csum = carry - _dot(drop.astype(BF16), tri)
        w = jnp.exp(z + csum)
        if vis is not None:
            w = jnp.where(vis, w, 0.0)
        return w.astype(BF16), csum[:, 0:1]

    r = lax.broadcasted_iota(I32, (dec_t, page), 0)
    c = lax.broadcasted_iota(I32, (dec_t, page), 1)
    vis = jnp.concatenate([c < r] * (2 * n_blocks), axis=0)
    z = jnp.concatenate(
        [_dot_tb(qq[p * rows:(p + 1) * rows], kn_ref[0, :, p * LANES:(p + 1) * LANES].astype(BF16))
         for p in range(n_blocks)], axis=0)
    w, carry = weights(z, jnp.zeros((n_blocks * rows, 1), F32), vis)
    acc = jnp.concatenate(
        [_dot(w[p * rows:(p + 1) * rows], vn_ref[0, :, p * LANES:(p + 1) * LANES].astype(BF16))
         for p in range(n_blocks)], axis=0)

    def more(state):
        jj, _, carry = state
        return (jj < n_pages) & (jnp.max(carry) > SB_CUTOFF)

    def body(state):
        jj, acc, carry = state
        slot = jj & 1
        wait(slot, jj)

        @pl.when(jj + 1 < n_pages)
        def _():
            start(1 - slot, jj + 1)

        kt = kbuf[slot]
        vt = vbuf[slot]
        z = jnp.concatenate(
            [_dot(qq[p * rows:(p + 1) * rows], kt[p * LANES:(p + 1) * LANES, :].astype(BF16))
             for p in range(n_blocks)], axis=0)
        w, carry = weights(z, carry, None)
        acc = acc + jnp.concatenate(
            [_dot_tb(w[p * rows:(p + 1) * rows], vt[p * LANES:(p + 1) * LANES, :].astype(BF16))
             for p in range(n_blocks)], axis=0)
        return jj + 1, acc, carry

    jj, acc, _ = lax.while_loop(more, body, (jnp.int32(0), acc, carry))

    @pl.when(jj < n_pages)
    def _():
        wait(jj & 1, jj)

    @pl.when(bi + 1 < pl.num_programs(0))
    def _():
        start(0, 0, bi + 1)

    for p in range(n_blocks):
        o_ref[0, :, p * LANES:(p + 1) * LANES] = _sb_finish(
            acc[p * rows:(p + 1) * rows, :], dec_t, g_ref[:, p * LANES:(p + 1) * LANES])


def _sb_decode(page_table, q, k_new, v_new, kt_cache, vt_cache, tri, g):
    b, dec_t, width = q.shape
    n_pages = page_table.shape[1]
    page = kt_cache.shape[2]
    n_blocks = width // LANES
    per_b = lambda bi, pt: (bi, 0, 0)
    const = lambda bi, pt: (0, 0)
    grid_spec = pltpu.PrefetchScalarGridSpec(
        num_scalar_prefetch=1,
        grid=(b,),
        in_specs=[
            pl.BlockSpec((1, dec_t, width), per_b),
            pl.BlockSpec((1, page, width), per_b),
            pl.BlockSpec((1, page, width), per_b),
            pl.BlockSpec(memory_space=pl.ANY),
            pl.BlockSpec(memory_space=pl.ANY),
            pl.BlockSpec((page, page), const),
            pl.BlockSpec((1, width), const),
        ],
        out_specs=pl.BlockSpec((1, dec_t, width), per_b),
        scratch_shapes=[pltpu.VMEM((2, width, page), F32),
                        pltpu.VMEM((2, width, page), F32),
                        pltpu.SemaphoreType.DMA((2, 2))],
    )
    return pl.pallas_call(
        functools.partial(_sb_decode_body, dec_t=dec_t, n_blocks=n_blocks, page=page, n_pages=n_pages),
        grid_spec=grid_spec,
        out_shape=jax.ShapeDtypeStruct((b, dec_t, width), F32),
        compiler_params=pltpu.CompilerParams(
            dimension_semantics=("arbitrary",), vmem_limit_bytes=VMEM_LIMIT),
        name="sb_decode",
    )(page_table, q, k_new, v_new, kt_cache, vt_cache, tri, g)


def _df_decode_body(pt_ref, q_ref, kn_ref, vn_ref, *rest, dec_t, n_blocks, n_slots, lam_init):
    kc_refs, vc_refs = rest[:n_slots], rest[n_slots:2 * n_slots]
    (bias_ref, bias_new_ref, g_ref, lq1, lk1, lq2, lk2, o_ref,
     qq_ref, m_ref, l_ref, acc_ref) = rest[2 * n_slots:]
    jj = pl.program_id(1)
    rows = 2 * dec_t
    page = bias_new_ref.shape[1]

    def update(k_refs, v_refs, bias):
        own = [pl.ds(h, page, stride=n_blocks) for h in range(n_blocks)]
        gather = lambda refs, h: jnp.concatenate([r[0, own[h], :].astype(BF16) for r in refs], axis=0)
        qq = qq_ref[...]
        s = jnp.concatenate(
            [_dot_tb(qq[h * rows:(h + 1) * rows], gather(k_refs, h)) for h in range(n_blocks)],
            axis=0) + bias[...]
        m_old = m_ref[...]
        m_new = jnp.maximum(m_old, jnp.max(s, axis=-1, keepdims=True))
        alpha = jnp.exp(m_old - m_new)
        p = jnp.exp(s - m_new)
        l_ref[...] = alpha * l_ref[...] + jnp.sum(p, axis=-1, keepdims=True)
        m_ref[...] = m_new
        p = p.astype(BF16)
        pv = jnp.concatenate(
            [_dot(p[h * rows:(h + 1) * rows], gather(v_refs, h)) for h in range(n_blocks)], axis=0)
        acc_ref[...] = alpha * acc_ref[...] + pv

    @pl.when(jj == 0)
    def _():
        qq_ref[...] = _stack_decode_queries(q_ref[0], n_blocks)
        m_ref[...] = jnp.full(m_ref.shape, NEG, F32)
        l_ref[...] = jnp.zeros_like(l_ref)
        acc_ref[...] = jnp.zeros_like(acc_ref)
        update([kn_ref], [vn_ref], bias_new_ref)

    update(kc_refs, vc_refs, bias_ref.at[0])

    @pl.when(jj == pl.num_programs(1) - 1)
    def _():
        lam = _lambda(lq1, lk1, lq2, lk2, lam_init)
        for p in range(n_blocks):
            sl = slice(p * rows, (p + 1) * rows)
            o_ref[0, :, p * LANES:(p + 1) * LANES] = _df_finish(
                acc_ref[sl, :], l_ref[sl, :], dec_t, lam, g_ref[...], lam_init)


def _df_decode(page_table, q, k_new, v_new, k_cache, v_cache, bias_s, bias_new, g, lams, lam_init,
               n_slots):
    b, dec_t, width = q.shape
    n_pages = page_table.shape[1]
    page_rows = k_cache.shape[1]
    n_blocks = width // LANES
    rows = n_blocks * 2 * dec_t
    n_steps = n_pages // n_slots
    per_b = lambda bi, jj, pt: (bi, 0, 0)
    const = lambda bi, jj, pt: (0, 0)
    vec = pl.BlockSpec((1, DH), const)

    def slot_spec(u):
        return pl.BlockSpec((1, page_rows, LANES), lambda bi, jj, pt: (pt[bi, jj * n_slots + u], 0, 0))

    grid_spec = pltpu.PrefetchScalarGridSpec(
        num_scalar_prefetch=1,
        grid=(b, n_steps),
        in_specs=[
            pl.BlockSpec((1, dec_t, width), per_b),
            pl.BlockSpec((1,) + k_new.shape[1:], per_b),
            pl.BlockSpec((1,) + v_new.shape[1:], per_b),
            *[slot_spec(u) for u in range(n_slots)],
            *[slot_spec(u) for u in range(n_slots)],
            pl.BlockSpec((1, rows, bias_s.shape[2]),
                         lambda bi, jj, pt: (jnp.where(jj == n_steps - 1, 1, 0), 0, 0)),
            pl.BlockSpec(bias_new.shape, const),
            pl.BlockSpec((1, LANES), const),
            vec, vec, vec, vec,
        ],
        out_specs=pl.BlockSpec((1, dec_t, width), per_b),
        scratch_shapes=[pltpu.VMEM((rows, LANES), BF16),
                        pltpu.VMEM((rows, 1), F32),
                        pltpu.VMEM((rows, 1), F32),
                        pltpu.VMEM((rows, LANES), F32)],
    )
    return pl.pallas_call(
        functools.partial(_df_decode_body, dec_t=dec_t, n_blocks=n_blocks, n_slots=n_slots,
                          lam_init=lam_init),
        grid_spec=grid_spec,
        out_shape=jax.ShapeDtypeStruct((b, dec_t, width), F32),
        compiler_params=pltpu.CompilerParams(
            dimension_semantics=("arbitrary", "arbitrary"),
            vmem_limit_bytes=VMEM_LIMIT),
        name="df_decode",
    )(page_table, q, k_new, v_new, *([k_cache] * n_slots), *([v_cache] * n_slots),
      bias_s, bias_new, g, *lams)


def _router_gates(logits_t, bias_col, n_experts):
    tm = logits_t.shape[1]
    gsz = n_experts // N_GROUPS
    scores = 1.0 / (1.0 + jnp.exp(-logits_t))
    sel = scores + bias_col
    sub = lax.broadcasted_iota(I32, (gsz, tm), 0)
    group_scores = []
    for g in range(N_GROUPS):
        blk = sel[g * gsz:(g + 1) * gsz]
        m1 = jnp.max(blk, axis=0, keepdims=True)
        first = jnp.min(jnp.where(blk == m1, sub, gsz), axis=0, keepdims=True)
        m2 = jnp.max(jnp.where(sub == first, -jnp.inf, blk), axis=0, keepdims=True)
        group_scores.append(m1 + m2)
    gs = jnp.concatenate(group_scores, axis=0)
    gidx = lax.broadcasted_iota(I32, (N_GROUPS, tm), 0)
    grank = jnp.zeros((N_GROUPS, tm), I32)
    for g in range(N_GROUPS):
        row = gs[g:g + 1]
        ahead = (row > gs) | ((row == gs) & (gidx > g))
        grank = grank + ahead.astype(I32)
    gkeep = grank < TOPK_GROUPS
    masked = jnp.concatenate(
        [jnp.where(jnp.broadcast_to(gkeep[g:g + 1], (gsz, tm)), sel[g * gsz:(g + 1) * gsz], -jnp.inf)
         for g in range(N_GROUPS)], axis=0)
    eidx = lax.broadcasted_iota(I32, (n_experts, tm), 0)
    erank = jnp.zeros((n_experts, tm), I32)
    for e in range(n_experts):
        row = masked[e:e + 1]
        ahead = (row > masked) | ((row == masked) & (eidx > e))
        erank = erank + ahead.astype(I32)
    w = jnp.where(erank < TOP_K, scores, 0.0)
    return w / jnp.sum(w, axis=0, keepdims=True) * ROUTED_SCALE


def _post_body(msb_ref, mdf_ref, x_ref, wo_sb_ref, wo_df_ref, g2_ref, rw_hi_ref, rw_lo_ref,
               rb_ref, x1_ref, h2_ref, gates_ref, *, n_experts):
    att = _dot(msb_ref[0].astype(BF16), wo_sb_ref[...]) + _dot(mdf_ref[0].astype(BF16), wo_df_ref[...])
    x1 = x_ref[0] + att
    x1_ref[...] = x1
    ms = jnp.mean(x1 * x1, axis=-1, keepdims=True)
    h2 = (x1 * lax.rsqrt(ms + EPS)) * g2_ref[...]
    h2_hi, h2_lo = _split_bf16(h2)
    h2_ref[...] = h2_hi
    rw_hi = rw_hi_ref[...]
    logits_t = _dot_tb(rw_hi, h2_hi) + _dot_tb(rw_hi, h2_lo) + _dot_tb(rw_lo_ref[...], h2_hi)
    gates_t = _router_gates(logits_t, rb_ref[...], n_experts)
    tm = gates_t.shape[1]
    pad = gates_ref.shape[1] - n_experts
    gates_t = jnp.concatenate([gates_t, jnp.zeros((pad, tm), F32)], axis=0)
    gates_ref[...] = gates_t.T.astype(gates_ref.dtype)


def _post_attention(msb, mdf, x, wo_sb, wo_df, g2, rw_hi, rw_lo, rb, tm):
    b, s, d = x.shape
    n = b * s
    per_b = s // tm
    n_experts = rw_hi.shape[0]
    ge = _round_up(n_experts, LANES)
    row = lambda bi, i: (bi * per_b + i, 0)
    const = lambda bi, i: (0, 0)
    return pl.pallas_call(
        functools.partial(_post_body, n_experts=n_experts),
        grid=(b, per_b),
        in_specs=[
            pl.BlockSpec((1, tm, msb.shape[2]), lambda bi, i: (bi, i, 0)),
            pl.BlockSpec((1, tm, mdf.shape[2]), lambda bi, i: (bi, i, 0)),
            pl.BlockSpec((1, tm, d), lambda bi, i: (bi, i, 0)),
            pl.BlockSpec(wo_sb.shape, const),
            pl.BlockSpec(wo_df.shape, const),
            pl.BlockSpec((1, d), const),
            pl.BlockSpec(rw_hi.shape, const),
            pl.BlockSpec(rw_lo.shape, const),
            pl.BlockSpec((n_experts, 1), const),
        ],
        out_specs=[pl.BlockSpec((tm, d), row), pl.BlockSpec((tm, d), row), pl.BlockSpec((tm, ge), row)],
        out_shape=[jax.ShapeDtypeStruct((n, d), F32), jax.ShapeDtypeStruct((n, d), BF16),
                   jax.ShapeDtypeStruct((n, ge), F32)],
        compiler_params=pltpu.CompilerParams(
            dimension_semantics=("arbitrary", "arbitrary"), vmem_limit_bytes=VMEM_LIMIT),
        name="post_attention",
    )(msb, mdf, x, wo_sb, wo_df, g2, rw_hi, rw_lo, rb)


def _silu(x):
    return x / (1.0 + jnp.exp(-x))


def _moe_body(h2_ref, x1_ref, gates_ref, wg_ref, wu_ref, wd_ref, sg_ref, su_ref, sd_ref,
              o_ref, acc_ref, *, group):
    eg = pl.program_id(1)
    h2 = h2_ref[...]

    @pl.when(eg == 0)
    def _():
        a = _silu(_dot(h2, sg_ref[...])) * _dot(h2, su_ref[...])
        acc_ref[...] = _dot(a.astype(BF16), sd_ref[...])

    f = wg_ref.shape[2]
    gates = gates_ref[...]
    lane = lax.broadcasted_iota(I32, gates.shape, 1)
    acts = []
    for j in range(group):
        gate = jnp.sum(jnp.where(lane == eg * group + j, gates, 0.0), axis=-1, keepdims=True)
        acts.append((_silu(_dot(h2, wg_ref[j])) * _dot(h2, wu_ref[j]) * gate).astype(BF16))
    a = jnp.concatenate(acts, axis=1)
    wd = wd_ref[...].reshape(group * f, wd_ref.shape[2])
    acc_ref[...] += _dot(a, wd)

    @pl.when(eg == pl.num_programs(1) - 1)
    def _():
        o_ref[...] = x1_ref[...] + acc_ref[...]


def _moe(h2, x1, gates, w_gate, w_up, w_down, sg, su, sd, tm, group):
    n, d = x1.shape
    n_experts, _, f = w_gate.shape
    row = lambda i, e: (i, 0)
    const = lambda i, e: (0, 0)
    return pl.pallas_call(
        functools.partial(_moe_body, group=group),
        grid=(n // tm, n_experts // group),
        in_specs=[
            pl.BlockSpec((tm, d), row),
            pl.BlockSpec((tm, d), row),
            pl.BlockSpec((tm, gates.shape[1]), row),
            pl.BlockSpec((group, d, f), lambda i, e: (e, 0, 0)),
            pl.BlockSpec((group, d, f), lambda i, e: (e, 0, 0)),
            pl.BlockSpec((group, f, d), lambda i, e: (e, 0, 0)),
            pl.BlockSpec(sg.shape, const),
            pl.BlockSpec(su.shape, const),
            pl.BlockSpec(sd.shape, const),
        ],
        out_specs=pl.BlockSpec((tm, d), row),
        out_shape=jax.ShapeDtypeStruct((n, d), F32),
        scratch_shapes=[pltpu.VMEM((tm, d), F32)],
        compiler_params=pltpu.CompilerParams(
            dimension_semantics=("arbitrary", "arbitrary"), vmem_limit_bytes=VMEM_LIMIT),
        name="moe",
    )(h2, x1, gates, w_gate, w_up, w_down, sg, su, sd)


def _largest_tile(n, cap):
    t = cap
    while n % t:
        t //= 2
    return t


def kernel(x_prompt, x_sample, cache_k_sb, cache_v_sb, cache_k_diff, cache_v_diff, page_table,
           meta_tokens, rel_bias, norm1_g, w_in, q_norm_g, k_norm_g, lambda_q1, lambda_k1,
           lambda_q2, lambda_k2, sb_out_g, diff_subln_g, w_out, norm2_g, router_w, router_bias,
           w_gate, w_up, w_down, w_shared_gate, w_shared_up, w_shared_down):
    assert w_in.shape[0] == 1, "single-layer step"
    b, seq, d = x_prompt.shape
    dec_b, dec_t, _ = x_sample.shape
    n_meta = meta_tokens.shape[0]
    width = sb_out_g.shape[1]
    n_pool, page = cache_k_sb.shape[1], cache_k_sb.shape[2]
    n_experts = router_w.shape[2]
    t = seq + n_meta
    blk = ATT_BLOCK
    lam_init = 0.8 - 0.6 * math.exp(-0.3 * 0)

    w_in_b = w_in[0].astype(BF16)
    g1 = norm1_g
    reps = width // DH
    qg = jnp.tile(q_norm_g, (1, reps))
    kg = jnp.tile(k_norm_g, (1, reps))
    lane = jnp.arange(width)
    gm = ((lane[:, None] // DH) == (lane[None, :] // DH)).astype(BF16) * (1.0 / DH)
    tri_p = (jnp.arange(blk)[:, None] >= jnp.arange(blk)[None, :]).astype(BF16)
    tri_d = (jnp.arange(page)[:, None] >= jnp.arange(page)[None, :]).astype(BF16)
    lams = (lambda_q1, lambda_k1, lambda_q2, lambda_k2)
    wo_sb = w_out[0, :width].astype(BF16)
    wo_df = w_out[0, width:].astype(BF16)
    rw_t = router_w[0].T
    rw_hi = rw_t.astype(BF16)
    rw_lo = (rw_t - rw_hi.astype(F32)).astype(BF16)
    rb = router_bias[0][:, None]
    sg = w_shared_gate[0].astype(BF16)
    su = w_shared_up[0].astype(BF16)
    sd = w_shared_down[0].astype(BF16)

    n_h_sb = width // DH
    n_h_df = width // LANES
    n_slots = DECODE_PAGES_PER_STEP
    assert blk >= MAX_DISTANCE and page >= MAX_DISTANCE
    assert page_table.shape[1] % n_slots == 0 and dec_t * n_h_df <= page
    pre = blk - n_meta
    assert 0 <= pre < blk and seq % blk == 0
    bias_p, bias_s, bias_n = _bias_tiles(rel_bias, blk, pre, dec_t, page, n_slots)
    head = jnp.concatenate([jnp.zeros((pre, d), F32), meta_tokens], axis=0)
    (k_sb_p, v_sb_p, k_df_p, v_df_p,
     qsb_b, ksb_b, vsb_b, qdf_b, kdf_b, vdf_b) = _project_prompt(head, x_prompt, n_meta, g1, w_in_b, qg, kg, gm)
    msb_p = _sb_prompt(qsb_b, ksb_b, vsb_b, tri_p, sb_out_g, blk, pre)
    mdf_p = _df_prompt(rel_bias, qdf_b, kdf_b, vdf_b, bias_p, diff_subln_g, lams, blk, pre, lam_init)

    n_s = dec_b * dec_t
    xs = x_sample.reshape(1, n_s, d)
    (k_sb_s, v_sb_s, k_df_s, v_df_s,
     qsb_s, _, _, qdf_s, _, _) = _project(xs, n_s, g1, w_in_b, qg, kg, gm, n_s)

    def new_page(rows, n_rows, n_pad):
        r = rows.reshape(dec_b, n_rows, -1)
        return jnp.pad(r, ((0, 0), (0, n_pad - n_rows), (0, 0)))

    transposed_pages = lambda c: jnp.transpose(c[0], (0, 2, 3, 1)).reshape(n_pool, width, page)
    interleaved_rows = lambda c: c.reshape(n_pool, page * n_h_df, LANES)
    msb_s = _sb_decode(page_table, qsb_s.reshape(dec_b, dec_t, width),
                       new_page(k_sb_s, dec_t, page), new_page(v_sb_s, dec_t, page),
                       transposed_pages(cache_k_sb), transposed_pages(cache_v_sb), tri_d, sb_out_g)
    mdf_s = _df_decode(page_table, qdf_s.reshape(dec_b, dec_t, width),
                       new_page(k_df_s, dec_t * n_h_df, page * n_h_df),
                       new_page(v_df_s, dec_t * n_h_df, page * n_h_df),
                       interleaved_rows(cache_k_diff), interleaved_rows(cache_v_diff),
                       bias_s, bias_n, diff_subln_g, lams, lam_init, n_slots)

    wg_b, wu_b, wd_b = w_gate[0].astype(BF16), w_up[0].astype(BF16), w_down[0].astype(BF16)
    x1_p, h2_p, gates_p = _post_attention(msb_p, mdf_p, x_prompt, wo_sb, wo_df, norm2_g,
                                          rw_hi, rw_lo, rb, _largest_tile(seq, 512))
    y_p = _moe(h2_p, x1_p, gates_p, wg_b, wu_b, wd_b, sg, su, sd,
               _largest_tile(b * seq, MOE_TOKEN_TILE), MOE_EXPERT_GROUP)
    y_prompt = y_p.reshape(b, seq, d)
    x1_s, h2_s, gates_s = _post_attention(
        msb_s.reshape(1, n_s, width), mdf_s.reshape(1, n_s, width), x_sample.reshape(1, n_s, d),
        wo_sb, wo_df, norm2_g, rw_hi, rw_lo, rb, n_s)
    y_s = _moe(h2_s, x1_s, gates_s, wg_b, wu_b, wd_b, sg, su, sd, n_s, MOE_EXPERT_GROUP)
    y_sample = y_s.reshape(dec_b, dec_t, d)

    sb_rows = lambda r, bb, tt: r.reshape(1, bb, tt, n_h_sb, DH)
    df_rows = lambda r, bb, tt: r.reshape(1, bb, tt, n_h_df, LANES)
    return (y_prompt, y_sample,
            sb_rows(k_sb_p, b, t), sb_rows(v_sb_p, b, t), df_rows(k_df_p, b, t), df_rows(v_df_p, b, t),
            sb_rows(k_sb_s, dec_b, dec_t), sb_rows(v_sb_s, dec_b, dec_t),
            df_rows(k_df_s, dec_b, dec_t), df_rows(v_df_s, dec_b, dec_t))
```

```python
import functools
import math

import jax
import jax.numpy as jnp
from jax import lax
from jax.experimental import pallas as pl
from jax.experimental.pallas import tpu as pltpu

F32 = jnp.float32
BF16 = jnp.bfloat16
I32 = jnp.int32

EPS = 1e-6
DH = 64
LANES = 128
MAX_DISTANCE = 128
TOP_K = 8
N_GROUPS = 8
TOPK_GROUPS = 4
ROUTED_SCALE = 2.5
NEG = -1e30
SB_CUTOFF = -104.0
ATT_BLOCK = 256
DF_HEADS_PER_STEP = 4
SB_PAIRS_PER_STEP = 4
DECODE_PAGES_PER_STEP = 32
MOE_TOKEN_TILE = 1024
MOE_EXPERT_GROUP = 8
VMEM_LIMIT = 60 * 1024 * 1024

_TRANS_B = (((1,), (1,)), ((), ()))


def _dot(a, b):
    return jnp.dot(a, b, preferred_element_type=F32)


def _dot_tb(a, b):
    return lax.dot_general(a, b, _TRANS_B, preferred_element_type=F32)


def _split_bf16(x):
    hi = x.astype(BF16)
    lo = (x - hi.astype(F32)).astype(BF16)
    return hi, lo


def _round_up(n, m):
    return (n + m - 1) // m * m


def _proj_rows(x, g1_ref, w_ref, qg_ref, kg_ref, gm_ref, ksb_o, vsb_o, kdf_o, vdf_o,
               qsb_b, ksb_b, vsb_b, qdf_b, kdf_b, vdf_b, width):
    ms = jnp.mean(x * x, axis=-1, keepdims=True)
    h = (x * lax.rsqrt(ms + EPS)) * g1_ref[...]
    proj = _dot(h.astype(BF16), w_ref[...])
    w = width
    q_sb, k_sb, v_sb = proj[:, 0:w], proj[:, w:2 * w], proj[:, 2 * w:3 * w]
    q_df, k_df, v_df = proj[:, 3 * w:4 * w], proj[:, 4 * w:5 * w], proj[:, 5 * w:6 * w]

    def map_norm(t, g_ref):
        msq = _dot((t * t).astype(BF16), gm_ref[...])
        return (t * lax.rsqrt(msq + EPS)) * g_ref[...]

    q_df = map_norm(q_df, qg_ref)
    k_df = map_norm(k_df, kg_ref)
    ksb_o[...] = k_sb
    vsb_o[...] = v_sb
    tm = x.shape[0]
    n_df = w // LANES
    for hd in range(n_df):
        rows = pl.ds(hd, tm, stride=n_df)
        kdf_o[rows, :] = k_df[:, hd * LANES:(hd + 1) * LANES]
        vdf_o[rows, :] = v_df[:, hd * LANES:(hd + 1) * LANES]
    scale = DH ** -0.5
    qsb_b[0] = (q_sb * scale).astype(BF16)
    ksb_b[0] = k_sb.astype(BF16)
    vsb_b[0] = v_sb.astype(BF16)
    qdf_b[0] = (q_df * scale).astype(BF16)
    kdf_b[0] = k_df.astype(BF16)
    vdf_b[0] = v_df.astype(BF16)


def _proj_body(x_ref, g1_ref, w_ref, qg_ref, kg_ref, gm_ref,
               ksb_o, vsb_o, kdf_o, vdf_o, *bf_outs, width):
    _proj_rows(x_ref[0], g1_ref, w_ref, qg_ref, kg_ref, gm_ref,
               ksb_o.at[0], vsb_o.at[0], kdf_o.at[0], vdf_o.at[0], *bf_outs, width)


def _proj_prompt_body(head_ref, x_ref, g1_ref, w_ref, qg_ref, kg_ref, gm_ref,
                      ksb_hbm, vsb_hbm, kdf_hbm, vdf_hbm, *rest, width, lead):
    bf_outs, (st_sb, st_df, sem) = rest[:6], rest[6:]
    b, i = pl.program_id(0), pl.program_id(1)
    n_i = pl.num_programs(1)
    step = b * n_i + i
    slot = step & 1
    tm = x_ref.shape[1]
    n_df = width // LANES

    def copies(slot, bb, ii, head):
        if head:
            src, dst, n = tm - lead, 0, lead
        else:
            src, dst, n = 0, lead + (ii - 1) * tm, tm
        out = []
        for kv, hbm in enumerate((ksb_hbm, vsb_hbm)):
            out.append(pltpu.make_async_copy(st_sb.at[slot, kv, pl.ds(src, n), :],
                                             hbm.at[bb, pl.ds(dst, n), :], sem.at[slot]))
        for kv, hbm in enumerate((kdf_hbm, vdf_hbm)):
            out.append(pltpu.make_async_copy(st_df.at[slot, kv, pl.ds(src * n_df, n * n_df), :],
                                             hbm.at[bb, pl.ds(dst * n_df, n * n_df), :], sem.at[slot]))
        return out

    def wait_step(slot, was_head):
        @pl.when(was_head)
        def _():
            for cp in copies(slot, 0, 1, True):
                cp.wait()

        @pl.when(jnp.logical_not(was_head))
        def _():
            for cp in copies(slot, 0, 1, False):
                cp.wait()

    @pl.when(step >= 2)
    def _():
        wait_step(slot, i == 2)

    x = jnp.where(i == 0, head_ref[...], x_ref[0])
    _proj_rows(x, g1_ref, w_ref, qg_ref, kg_ref, gm_ref,
               st_sb.at[slot, 0], st_sb.at[slot, 1], st_df.at[slot, 0], st_df.at[slot, 1],
               *bf_outs, width)

    @pl.when(i == 0)
    def _():
        for cp in copies(slot, b, i, True):
            cp.start()

    @pl.when(i > 0)
    def _():
        for cp in copies(slot, b, i, False):
            cp.start()

    @pl.when(step == pl.num_programs(0) * n_i - 1)
    def _():
        wait_step(1 - slot, i == 1)
        wait_step(slot, i == 0)


def _project(x_pad, t_valid, g1, w_in_b, qg, kg, gm, tm):
    b, tp, d = x_pad.shape
    width = w_in_b.shape[1] // 6
    grid = (b, tp // tm)
    row = lambda bi, i: (bi, i, 0)
    const = lambda bi, i: (0, 0)
    n_df = width // LANES
    f32_out = jax.ShapeDtypeStruct((b, t_valid, width), F32)
    df_out = jax.ShapeDtypeStruct((b, t_valid * n_df, LANES), F32)
    bf_out = jax.ShapeDtypeStruct((b, tp, width), BF16)
    out_spec = pl.BlockSpec((1, tm, width), row)
    df_spec = pl.BlockSpec((1, tm * n_df, LANES), row)
    return pl.pallas_call(
        functools.partial(_proj_body, width=width),
        grid=grid,
        in_specs=[
            pl.BlockSpec((1, tm, d), row),
            pl.BlockSpec((1, d), const),
            pl.BlockSpec(w_in_b.shape, const),
            pl.BlockSpec((1, width), const),
            pl.BlockSpec((1, width), const),
            pl.BlockSpec((width, width), const),
        ],
        out_specs=[out_spec] * 2 + [df_spec] * 2 + [out_spec] * 6,
        out_shape=[f32_out] * 2 + [df_out] * 2 + [bf_out] * 6,
        compiler_params=pltpu.CompilerParams(
            dimension_semantics=("arbitrary", "arbitrary"),
            vmem_limit_bytes=VMEM_LIMIT),
        name="proj",
    )(x_pad, g1, w_in_b, qg, kg, gm)


def _project_prompt(head, x, lead, g1, w_in_b, qg, kg, gm):
    b, s, d = x.shape
    tm = head.shape[0]
    width = w_in_b.shape[1] // 6
    n_df = width // LANES
    n_i = 1 + s // tm
    assert s % tm == 0 and n_i >= 3 and lead % 8 == 0 and 0 < lead <= tm
    row = lambda bi, i: (bi, i, 0)
    const = lambda bi, i: (0, 0)
    anywhere = pl.BlockSpec(memory_space=pl.ANY)
    bf_out = jax.ShapeDtypeStruct((b, tm + s, width), BF16)
    return pl.pallas_call(
        functools.partial(_proj_prompt_body, width=width, lead=lead),
        grid=(b, n_i),
        in_specs=[
            pl.BlockSpec((tm, d), const),
            pl.BlockSpec((1, tm, d), lambda bi, i: (bi, jnp.maximum(i - 1, 0), 0)),
            pl.BlockSpec((1, d), const),
            pl.BlockSpec(w_in_b.shape, const),
            pl.BlockSpec((1, width), const),
            pl.BlockSpec((1, width), const),
            pl.BlockSpec((width, width), const),
        ],
        out_specs=[anywhere] * 4 + [pl.BlockSpec((1, tm, width), row)] * 6,
        out_shape=[jax.ShapeDtypeStruct((b, lead + s, width), F32)] * 2
        + [jax.ShapeDtypeStruct((b, (lead + s) * n_df, LANES), F32)] * 2 + [bf_out] * 6,
        scratch_shapes=[pltpu.VMEM((2, 2, tm, width), F32),
                        pltpu.VMEM((2, 2, tm * n_df, LANES), F32),
                        pltpu.SemaphoreType.DMA((2,))],
        compiler_params=pltpu.CompilerParams(
            dimension_semantics=("arbitrary", "arbitrary"),
            vmem_limit_bytes=VMEM_LIMIT),
        name="proj_prompt",
    )(head, x, g1, w_in_b, qg, kg, gm)


def _stack_halves(q):
    lane = lax.broadcasted_iota(I32, (1, LANES), 1)
    zero = jnp.zeros_like(q)
    return jnp.concatenate(
        [jnp.where(lane < DH, q, zero), jnp.where(lane >= DH, q, zero)], axis=0)


def _softplus(z):
    return jnp.maximum(z, 0.0) + jnp.log(1.0 + jnp.exp(-jnp.abs(z)))


def _sb_blocks(qqs, ks, vs, tri, state, vis):
    n = range(len(qqs))
    zs = [_dot_tb(qqs[p], ks[p]) for p in n]
    drops = [_softplus(z) for z in zs]
    own = [zs[p] - drops[p] for p in n]
    if vis is not None:
        drops = [jnp.where(vis, d, 0.0) for d in drops]
    right = [state[p][1] - _dot(drops[p].astype(BF16), tri) for p in n]
    ws = [jnp.exp(own[p] + right[p]) for p in n]
    if vis is not None:
        ws = [jnp.where(vis, w, 0.0) for w in ws]
    pvs = [_dot(ws[p].astype(BF16), vs[p]) for p in n]
    return [(state[p][0] + pvs[p], right[p][:, 0:1] - drops[p][:, 0:1]) for p in n]


def _sb_finish(acc, t, g):
    lane = lax.broadcasted_iota(I32, (1, LANES), 1)
    lo_half = lane < DH
    o = jnp.where(lo_half, acc[:t], acc[t:])
    o2 = o * o
    s_lo = jnp.sum(jnp.where(lo_half, o2, 0.0), axis=-1, keepdims=True)
    s_hi = jnp.sum(jnp.where(lo_half, 0.0, o2), axis=-1, keepdims=True)
    ms = jnp.where(lo_half, s_lo, s_hi) * (1.0 / DH)
    return (o * lax.rsqrt(ms + EPS)) * g


def _df_block(qq, k, v, bias, m, l, acc):
    s = _dot_tb(qq, k) + bias
    m_new = jnp.maximum(m, jnp.max(s, axis=-1, keepdims=True))
    alpha = jnp.exp(m - m_new)
    p = jnp.exp(s - m_new)
    l = alpha * l + jnp.sum(p, axis=-1, keepdims=True)
    acc = alpha * acc + _dot(p.astype(BF16), v)
    return m_new, l, acc


def _lambda(lq1, lk1, lq2, lk2, lam_init):
    s1 = jnp.sum(lq1[...] * lk1[...], axis=-1, keepdims=True)
    s2 = jnp.sum(lq2[...] * lk2[...], axis=-1, keepdims=True)
    return jnp.exp(s1) - jnp.exp(s2) + lam_init


def _df_finish(acc, l, t, lam, g, lam_init):
    o = acc[:t] / l[:t] - lam * (acc[t:] / l[t:])
    ms = jnp.mean(o * o, axis=-1, keepdims=True)
    return (o * lax.rsqrt(ms + EPS)) * (g * (1.0 - lam_init))


def _bias_tile(tab_ref, h, rel, n_buckets):
    max_exact = n_buckets // 2
    n = jnp.maximum(rel, 0)
    nf = jnp.maximum(n, 1).astype(F32)
    large = max_exact + (jnp.log(nf / max_exact) / math.log(MAX_DISTANCE / max_exact)
                         * (n_buckets - max_exact)).astype(I32)
    large = jnp.minimum(large, n_buckets - 1)
    bucket = jnp.where(n < max_exact, n, large)
    bias = jnp.zeros(rel.shape, F32)
    for b in range(n_buckets):
        bias = jnp.where(bucket == b, tab_ref[b, h], bias)
    return jnp.where(rel >= 0, bias, NEG)


def _bias_body(tab_ref, bp_ref, bs_ref, bn_ref, *, n_buckets, n_heads, blk, pre, dec_t, page, n_slots):
    r = lax.broadcasted_iota(I32, (blk, 2 * blk), 0)
    c = lax.broadcasted_iota(I32, (blk, 2 * blk), 1)
    rs = lax.broadcasted_iota(I32, (dec_t, page), 0)
    cs = lax.broadcasted_iota(I32, (dec_t, page), 1)
    for h in range(n_heads):
        tile = _bias_tile(tab_ref, h, blk + r - c, n_buckets)
        bp_ref[h, 0] = tile
        bp_ref[h, 1] = jnp.where(c < pre, NEG, tile)
        far = jnp.full((dec_t, page), tab_ref[n_buckets - 1, h], F32)
        last = _bias_tile(tab_ref, h, page + rs - cs, n_buckets)
        new = jnp.where(cs < dec_t, _bias_tile(tab_ref, h, rs - cs, n_buckets), NEG)
        for mp in range(2):
            rows = pl.ds((2 * h + mp) * dec_t, dec_t)
            bn_ref[rows, :] = new
            for u in range(n_slots):
                bs_ref[0, rows, u * page:(u + 1) * page] = far
                bs_ref[1, rows, u * page:(u + 1) * page] = last if u == n_slots - 1 else far


def _bias_tiles(rel_bias, blk, pre, dec_t, page, n_slots):
    n_buckets, n_heads = rel_bias.shape
    rows = 2 * n_heads * dec_t
    return pl.pallas_call(
        functools.partial(_bias_body, n_buckets=n_buckets, n_heads=n_heads, blk=blk, pre=pre,
                          dec_t=dec_t, page=page, n_slots=n_slots),
        in_specs=[pl.BlockSpec(memory_space=pltpu.SMEM)],
        out_shape=[jax.ShapeDtypeStruct((n_heads, 2, blk, 2 * blk), F32),
                   jax.ShapeDtypeStruct((2, rows, n_slots * page), F32),
                   jax.ShapeDtypeStruct((rows, page), F32)],
        name="bias_tiles",
    )(rel_bias)


def _sb_prompt_body(q_ref, k_ref, v_ref, tri_ref, g_ref, o_ref, *, blk, pre):
    i = pl.program_id(2) + 1
    pairs = range(SB_PAIRS_PER_STEP)
    lanes = lambda p: slice(p * LANES, (p + 1) * LANES)
    qq = [_stack_halves(q_ref[0, :, lanes(p)]) for p in pairs]
    tri = tri_ref[...]
    r = lax.broadcasted_iota(I32, (2 * blk, blk), 0)
    r = jnp.where(r >= blk, r - blk, r)
    c = lax.broadcasted_iota(I32, (2 * blk, blk), 1)
    vis = c < r

    def block(j, state, vis):
        keys = pl.ds(pl.multiple_of(j * blk, blk), blk)
        return _sb_blocks(qq, [k_ref[0, keys, lanes(p)] for p in pairs],
                          [v_ref[0, keys, lanes(p)] for p in pairs], tri, state, vis)

    zero = (jnp.zeros((2 * blk, LANES), F32), jnp.zeros((2 * blk, 1), F32))
    state = block(i, [zero for _ in pairs], vis)

    def alive(state):
        return functools.reduce(jnp.maximum, [jnp.max(carry) for _, carry in state]) > SB_CUTOFF

    def more(loop):
        jj, state = loop
        return (jj < i - 1) & alive(state)

    def body(loop):
        jj, state = loop
        return jj + 1, block(i - 1 - jj, state, None)

    jj, state = lax.while_loop(more, body, (jnp.int32(0), state))
    state = lax.cond((jj == i - 1) & alive(state),
                     lambda s: block(0, s, c >= pre), lambda s: s, state)
    for p in pairs:
        o_ref[0, :, lanes(p)] = _sb_finish(state[p][0], blk, g_ref[:, lanes(p)]).astype(o_ref.dtype)


def _sb_prompt(q_b, k_b, v_b, tri, g, blk, pre):
    b, tp, width = q_b.shape
    w = SB_PAIRS_PER_STEP * LANES
    grid = (b, width // w, tp // blk - 1)
    return pl.pallas_call(
        functools.partial(_sb_prompt_body, blk=blk, pre=pre),
        grid=grid,
        in_specs=[
            pl.BlockSpec((1, blk, w), lambda bi, p, i: (bi, i + 1, p)),
            pl.BlockSpec((1, tp, w), lambda bi, p, i: (bi, 0, p)),
            pl.BlockSpec((1, tp, w), lambda bi, p, i: (bi, 0, p)),
            pl.BlockSpec((blk, blk), lambda bi, p, i: (0, 0)),
            pl.BlockSpec((1, w), lambda bi, p, i: (0, p)),
        ],
        out_specs=pl.BlockSpec((1, blk, w), lambda bi, p, i: (bi, i, p)),
        out_shape=jax.ShapeDtypeStruct((b, tp - blk, width), BF16),
        compiler_params=pltpu.CompilerParams(
            dimension_semantics=("arbitrary", "arbitrary", "arbitrary"),
            vmem_limit_bytes=VMEM_LIMIT),
        name="sb_prompt",
    )(q_b, k_b, v_b, tri, g)


def _df_prompt_body(tab_ref, q_ref, k_ref, v_ref, bias_ref, g_ref, lq1, lk1, lq2, lk2,
                    o_ref, m_ref, l_ref, acc_ref, *, blk, pre, lam_init, n_buckets):
    hg = pl.program_id(1)
    i = pl.program_id(2) + 1
    heads = range(DF_HEADS_PER_STEP)
    lanes = lambda hh: slice(hh * LANES, (hh + 1) * LANES)
    qq = [_stack_halves(q_ref[0, :, lanes(hh)]) for hh in heads]
    far_bias = [tab_ref[n_buckets - 1, hg * DF_HEADS_PER_STEP + hh] for hh in heads]

    def scores(hh, j, n, near):
        keys = pl.ds(pl.multiple_of(j * blk, blk), n * blk)
        s = _dot_tb(qq[hh], k_ref[0, keys, lanes(hh)])
        if near:
            t = bias_ref[hh, 0]
            return s + jnp.concatenate([t, t], axis=0), keys
        col = lax.broadcasted_iota(I32, (1, n * blk), 1)
        return s + jnp.where((j == 0) & (col < pre), NEG, far_bias[hh]), keys

    def sweep(visit):
        visit(i - 1, 2, True)
        n_far = i - 1
        n4 = lax.shift_right_logical(n_far, 2)

        def wide(g, carry):
            visit(g * 4, 4, False)
            return carry

        lax.fori_loop(0, n4, wide, 0)

        @pl.when((n_far & 2) != 0)
        def _():
            visit(n4 * 4, 2, False)

        @pl.when((n_far & 1) != 0)
        def _():
            visit(n_far - 1, 1, False)

    def lane_chunks(x):
        return [x[:, c * LANES:(c + 1) * LANES] for c in range(x.shape[1] // LANES)]

    m_ref[...] = jnp.full(m_ref.shape, NEG, F32)

    def visit_max(j, n, near):
        for hh in heads:
            s, _ = scores(hh, j, n, near)
            m_ref[hh] = functools.reduce(jnp.maximum, lane_chunks(s), m_ref[hh])

    sweep(visit_max)
    m = [jnp.max(m_ref[hh], axis=-1, keepdims=True) for hh in heads]

    l_ref[...] = jnp.zeros_like(l_ref)
    acc_ref[...] = jnp.zeros_like(acc_ref)

    def visit_sum(j, n, near):
        scored = [scores(hh, j, n, near) for hh in heads]
        ps = [jnp.exp(s - m[hh]) for hh, (s, _) in zip(heads, scored)]
        for hh in heads:
            l_ref[hh] = functools.reduce(jnp.add, lane_chunks(ps[hh]), l_ref[hh])
        pv = [_dot(ps[hh].astype(BF16), v_ref[0, scored[hh][1], lanes(hh)]) for hh in heads]
        for hh in heads:
            acc_ref[hh] += pv[hh]

    sweep(visit_sum)
    lam = _lambda(lq1, lk1, lq2, lk2, lam_init)
    for hh in heads:
        l = jnp.sum(l_ref[hh], axis=-1, keepdims=True)
        o_ref[0, :, lanes(hh)] = _df_finish(acc_ref[hh], l, blk, lam, g_ref[...], lam_init).astype(o_ref.dtype)


def _df_prompt(rel_bias, q_b, k_b, v_b, bias_p, g, lams, blk, pre, lam_init):
    b, tp, width = q_b.shape
    hps = DF_HEADS_PER_STEP
    w = hps * LANES
    grid = (b, width // w, tp // blk - 1)
    vec = pl.BlockSpec((1, DH), lambda bi, h, i: (0, 0))
    tile = lambda bi, h, i: (h, jnp.where(i == 0, 1, 0), 0, 0)
    return pl.pallas_call(
        functools.partial(_df_prompt_body, blk=blk, pre=pre, lam_init=lam_init,
                          n_buckets=rel_bias.shape[0]),
        grid=grid,
        in_specs=[
            pl.BlockSpec(memory_space=pltpu.SMEM),
            pl.BlockSpec((1, blk, w), lambda bi, h, i: (bi, i + 1, h)),
            pl.BlockSpec((1, tp, w), lambda bi, h, i: (bi, 0, h)),
            pl.BlockSpec((1, tp, w), lambda bi, h, i: (bi, 0, h)),
            pl.BlockSpec((hps, 1, blk, 2 * blk), tile),
            pl.BlockSpec((1, LANES), lambda bi, h, i: (0, 0)),
            vec, vec, vec, vec,
        ],
        out_specs=pl.BlockSpec((1, blk, w), lambda bi, h, i: (bi, i, h)),
        out_shape=jax.ShapeDtypeStruct((b, tp - blk, width), BF16),
        scratch_shapes=[pltpu.VMEM((hps, 2 * blk, LANES), F32)] * 3,
        compiler_params=pltpu.CompilerParams(
            dimension_semantics=("arbitrary", "arbitrary", "arbitrary"),
            vmem_limit_bytes=VMEM_LIMIT),
        name="df_prompt",
    )(rel_bias, q_b, k_b, v_b, bias_p, g, *lams)


def _stack_decode_queries(q, n_blocks):
    q = q.astype(F32)
    return jnp.concatenate(
        [_stack_halves(q[:, p * LANES:(p + 1) * LANES]) for p in range(n_blocks)], axis=0).astype(BF16)


def _sb_decode_body(pt_ref, q_ref, kn_ref, vn_ref, kc_hbm, vc_hbm, tri_ref, g_ref, o_ref,
                    kbuf, vbuf, sem, *, dec_t, n_blocks, page, n_pages):
    bi = pl.program_id(0)
    rows = 2 * dec_t
    qq = _stack_decode_queries(q_ref[0], n_blocks)
    tri = tri_ref[...]

    def page_copies(slot, jj):
        pg = pt_ref[bi, n_pages - 1 - jj]
        return (pltpu.make_async_copy(kc_hbm.at[pg], kbuf.at[slot], sem.at[0, slot]),
                pltpu.make_async_copy(vc_hbm.at[pg], vbuf.at[slot], sem.at[1, slot]))

    def start(slot, jj):
        for cp in page_copies(slot, jj):
            cp.start()

    def wait(slot, jj):
        for cp in page_copies(slot, jj):
            cp.wait()

    start(0, 0)

    def weights(z, carry, vis):
        drop = _softplus(z)
        own = z - drop
        if vis is not None:
            drop = jnp.where(vis, drop, 0.0)
        right = carry - _dot(drop.astype(BF16), tri)
        w = jnp.exp(own + right)
        if vis is not None:
            w = jnp.where(vis, w, 0.0)
        return w.astype(BF16), right[:, 0:1] - drop[:, 0:1]

    r = lax.broadcasted_iota(I32, (dec_t, page), 0)
    c = lax.broadcasted_iota(I32, (dec_t, page), 1)
    vis = jnp.concatenate([c < r] * (2 * n_blocks), axis=0)
    z = jnp.concatenate(
        [_dot_tb(qq[p * rows:(p + 1) * rows], kn_ref[0, :, p * LANES:(p + 1) * LANES].astype(BF16))
         for p in range(n_blocks)], axis=0)
    w, carry = weights(z, jnp.zeros((n_blocks * rows, 1), F32), vis)
    acc = jnp.concatenate(
        [_dot(w[p * rows:(p + 1) * rows], vn_ref[0, :, p * LANES:(p + 1) * LANES].astype(BF16))
         for p in range(n_blocks)], axis=0)

    def more(state):
        jj, _, carry = state
        return (jj < n_pages) & (jnp.max(carry) > SB_CUTOFF)

    def body(state):
        jj, acc, carry = state
        slot = jj & 1
        wait(slot, jj)

        @pl.when(jj + 1 < n_pages)
        def _():
            start(1 - slot, jj + 1)

        kt = kbuf[slot]
        vt = vbuf[slot]
        z = jnp.concatenate(
            [_dot(qq[p * rows:(p + 1) * rows], kt[p * LANES:(p + 1) * LANES, :].astype(BF16))
             for p in range(n_blocks)], axis=0)
        w, carry = weights(z, carry, None)
        acc = acc + jnp.concatenate(
            [_dot_tb(w[p * rows:(p + 1) * rows], vt[p * LANES:(p + 1) * LANES, :].astype(BF16))
             for p in range(n_blocks)], axis=0)
        return jj + 1, acc, carry

    jj, acc, _ = lax.while_loop(more, body, (jnp.int32(0), acc, carry))

    @pl.when(jj < n_pages)
    def _():
        wait(jj & 1, jj)

    for p in range(n_blocks):
        o_ref[0, :, p * LANES:(p + 1) * LANES] = _sb_finish(
            acc[p * rows:(p + 1) * rows, :], dec_t, g_ref[:, p * LANES:(p + 1) * LANES])


def _sb_decode(page_table, q, k_new, v_new, kt_cache, vt_cache, tri, g):
    b, dec_t, width = q.shape
    n_pages = page_table.shape[1]
    page = kt_cache.shape[2]
    n_blocks = width // LANES
    per_b = lambda bi, pt: (bi, 0, 0)
    const = lambda bi, pt: (0, 0)
    grid_spec = pltpu.PrefetchScalarGridSpec(
        num_scalar_prefetch=1,
        grid=(b,),
        in_specs=[
            pl.BlockSpec((1, dec_t, width), per_b),
            pl.BlockSpec((1, page, width), per_b),
            pl.BlockSpec((1, page, width), per_b),
            pl.BlockSpec(memory_space=pl.ANY),
            pl.BlockSpec(memory_space=pl.ANY),
            pl.BlockSpec((page, page), const),
            pl.BlockSpec((1, width), const),
        ],
        out_specs=pl.BlockSpec((1, dec_t, width), per_b),
        scratch_shapes=[pltpu.VMEM((2, width, page), F32),
                        pltpu.VMEM((2, width, page), F32),
                        pltpu.SemaphoreType.DMA((2, 2))],
    )
    return pl.pallas_call(
        functools.partial(_sb_decode_body, dec_t=dec_t, n_blocks=n_blocks, page=page, n_pages=n_pages),
        grid_spec=grid_spec,
        out_shape=jax.ShapeDtypeStruct((b, dec_t, width), F32),
        compiler_params=pltpu.CompilerParams(
            dimension_semantics=("arbitrary",), vmem_limit_bytes=VMEM_LIMIT),
        name="sb_decode",
    )(page_table, q, k_new, v_new, kt_cache, vt_cache, tri, g)


def _df_decode_body(pt_ref, q_ref, kn_ref, vn_ref, *rest, dec_t, n_blocks, n_slots, lam_init):
    kc_refs, vc_refs = rest[:n_slots], rest[n_slots:2 * n_slots]
    (bias_ref, bias_new_ref, g_ref, lq1, lk1, lq2, lk2, o_ref,
     qq_ref, m_ref, l_ref, acc_ref) = rest[2 * n_slots:]
    jj = pl.program_id(1)
    rows = 2 * dec_t
    page = bias_new_ref.shape[1]

    def update(k_refs, v_refs, bias):
        own = [pl.ds(h, page, stride=n_blocks) for h in range(n_blocks)]
        gather = lambda refs, h: jnp.concatenate([r[0, own[h], :].astype(BF16) for r in refs], axis=0)
        qq = qq_ref[...]
        s = jnp.concatenate(
            [_dot_tb(qq[h * rows:(h + 1) * rows], gather(k_refs, h)) for h in range(n_blocks)],
            axis=0) + bias[...]
        m_old = m_ref[...]
        m_new = jnp.maximum(m_old, jnp.max(s, axis=-1, keepdims=True))
        alpha = jnp.exp(m_old - m_new)
        p = jnp.exp(s - m_new)
        l_ref[...] = alpha * l_ref[...] + jnp.sum(p, axis=-1, keepdims=True)
        m_ref[...] = m_new
        p = p.astype(BF16)
        pv = jnp.concatenate(
            [_dot(p[h * rows:(h + 1) * rows], gather(v_refs, h)) for h in range(n_blocks)], axis=0)
        acc_ref[...] = alpha * acc_ref[...] + pv

    @pl.when(jj == 0)
    def _():
        qq_ref[...] = _stack_decode_queries(q_ref[0], n_blocks)
        m_ref[...] = jnp.full(m_ref.shape, NEG, F32)
        l_ref[...] = jnp.zeros_like(l_ref)
        acc_ref[...] = jnp.zeros_like(acc_ref)
        update([kn_ref], [vn_ref], bias_new_ref)

    update(kc_refs, vc_refs, bias_ref.at[0])

    @pl.when(jj == pl.num_programs(1) - 1)
    def _():
        lam = _lambda(lq1, lk1, lq2, lk2, lam_init)
        for p in range(n_blocks):
            sl = slice(p * rows, (p + 1) * rows)
            o_ref[0, :, p * LANES:(p + 1) * LANES] = _df_finish(
                acc_ref[sl, :], l_ref[sl, :], dec_t, lam, g_ref[...], lam_init)


def _df_decode(page_table, q, k_new, v_new, k_cache, v_cache, bias_s, bias_new, g, lams, lam_init,
               n_slots):
    b, dec_t, width = q.shape
    n_pages = page_table.shape[1]
    page_rows = k_cache.shape[1]
    n_blocks = width // LANES
    rows = n_blocks * 2 * dec_t
    n_steps = n_pages // n_slots
    per_b = lambda bi, jj, pt: (bi, 0, 0)
    const = lambda bi, jj, pt: (0, 0)
    vec = pl.BlockSpec((1, DH), const)

    def slot_spec(u):
        return pl.BlockSpec((1, page_rows, LANES), lambda bi, jj, pt: (pt[bi, jj * n_slots + u], 0, 0))

    grid_spec = pltpu.PrefetchScalarGridSpec(
        num_scalar_prefetch=1,
        grid=(b, n_steps),
        in_specs=[
            pl.BlockSpec((1, dec_t, width), per_b),
            pl.BlockSpec((1,) + k_new.shape[1:], per_b),
            pl.BlockSpec((1,) + v_new.shape[1:], per_b),
            *[slot_spec(u) for u in range(n_slots)],
            *[slot_spec(u) for u in range(n_slots)],
            pl.BlockSpec((1, rows, bias_s.shape[2]),
                         lambda bi, jj, pt: (jnp.where(jj == n_steps - 1, 1, 0), 0, 0)),
            pl.BlockSpec(bias_new.shape, const),
            pl.BlockSpec((1, LANES), const),
            vec, vec, vec, vec,
        ],
        out_specs=pl.BlockSpec((1, dec_t, width), per_b),
        scratch_shapes=[pltpu.VMEM((rows, LANES), BF16),
                        pltpu.VMEM((rows, 1), F32),
                        pltpu.VMEM((rows, 1), F32),
                        pltpu.VMEM((rows, LANES), F32)],
    )
    return pl.pallas_call(
        functools.partial(_df_decode_body, dec_t=dec_t, n_blocks=n_blocks, n_slots=n_slots,
                          lam_init=lam_init),
        grid_spec=grid_spec,
        out_shape=jax.ShapeDtypeStruct((b, dec_t, width), F32),
        compiler_params=pltpu.CompilerParams(
            dimension_semantics=("arbitrary", "arbitrary"),
            vmem_limit_bytes=VMEM_LIMIT),
        name="df_decode",
    )(page_table, q, k_new, v_new, *([k_cache] * n_slots), *([v_cache] * n_slots),
      bias_s, bias_new, g, *lams)


def _router_gates(logits_t, bias_col, n_experts):
    tm = logits_t.shape[1]
    gsz = n_experts // N_GROUPS
    scores = 1.0 / (1.0 + jnp.exp(-logits_t))
    sel = scores + bias_col
    sub = lax.broadcasted_iota(I32, (gsz, tm), 0)
    group_scores = []
    for g in range(N_GROUPS):
        blk = sel[g * gsz:(g + 1) * gsz]
        m1 = jnp.max(blk, axis=0, keepdims=True)
        first = jnp.min(jnp.where(blk == m1, sub, gsz), axis=0, keepdims=True)
        m2 = jnp.max(jnp.where(sub == first, -jnp.inf, blk), axis=0, keepdims=True)
        group_scores.append(m1 + m2)
    gs = jnp.concatenate(group_scores, axis=0)
    gidx = lax.broadcasted_iota(I32, (N_GROUPS, tm), 0)
    grank = jnp.zeros((N_GROUPS, tm), I32)
    for g in range(N_GROUPS):
        row = gs[g:g + 1]
        ahead = (row > gs) | ((row == gs) & (gidx > g))
        grank = grank + ahead.astype(I32)
    gkeep = grank < TOPK_GROUPS
    masked = jnp.concatenate(
        [jnp.where(jnp.broadcast_to(gkeep[g:g + 1], (gsz, tm)), sel[g * gsz:(g + 1) * gsz], -jnp.inf)
         for g in range(N_GROUPS)], axis=0)
    eidx = lax.broadcasted_iota(I32, (n_experts, tm), 0)
    erank = jnp.zeros((n_experts, tm), I32)
    for e in range(n_experts):
        row = masked[e:e + 1]
        ahead = (row > masked) | ((row == masked) & (eidx > e))
        erank = erank + ahead.astype(I32)
    w = jnp.where(erank < TOP_K, scores, 0.0)
    return w / jnp.sum(w, axis=0, keepdims=True) * ROUTED_SCALE


def _post_body(msb_ref, mdf_ref, x_ref, wo_sb_ref, wo_df_ref, g2_ref, rw_hi_ref, rw_lo_ref,
               rb_ref, x1_ref, h2_ref, gates_ref, *, n_experts):
    att = _dot(msb_ref[0].astype(BF16), wo_sb_ref[...]) + _dot(mdf_ref[0].astype(BF16), wo_df_ref[...])
    x1 = x_ref[0] + att
    x1_ref[...] = x1
    ms = jnp.mean(x1 * x1, axis=-1, keepdims=True)
    h2 = (x1 * lax.rsqrt(ms + EPS)) * g2_ref[...]
    h2_hi, h2_lo = _split_bf16(h2)
    h2_ref[...] = h2_hi
    rw_hi = rw_hi_ref[...]
    logits_t = _dot_tb(rw_hi, h2_hi) + _dot_tb(rw_hi, h2_lo) + _dot_tb(rw_lo_ref[...], h2_hi)
    gates_t = _router_gates(logits_t, rb_ref[...], n_experts)
    tm = gates_t.shape[1]
    pad = gates_ref.shape[1] - n_experts
    gates_t = jnp.concatenate([gates_t, jnp.zeros((pad, tm), F32)], axis=0)
    gates_ref[...] = gates_t.T.astype(gates_ref.dtype)


def _post_attention(msb, mdf, x, wo_sb, wo_df, g2, rw_hi, rw_lo, rb, tm):
    b, s, d = x.shape
    n = b * s
    per_b = s // tm
    n_experts = rw_hi.shape[0]
    ge = _round_up(n_experts, LANES)
    row = lambda bi, i: (bi * per_b + i, 0)
    const = lambda bi, i: (0, 0)
    return pl.pallas_call(
        functools.partial(_post_body, n_experts=n_experts),
        grid=(b, per_b),
        in_specs=[
            pl.BlockSpec((1, tm, msb.shape[2]), lambda bi, i: (bi, i, 0)),
            pl.BlockSpec((1, tm, mdf.shape[2]), lambda bi, i: (bi, i, 0)),
            pl.BlockSpec((1, tm, d), lambda bi, i: (bi, i, 0)),
            pl.BlockSpec(wo_sb.shape, const),
            pl.BlockSpec(wo_df.shape, const),
            pl.BlockSpec((1, d), const),
            pl.BlockSpec(rw_hi.shape, const),
            pl.BlockSpec(rw_lo.shape, const),
            pl.BlockSpec((n_experts, 1), const),
        ],
        out_specs=[pl.BlockSpec((tm, d), row), pl.BlockSpec((tm, d), row), pl.BlockSpec((tm, ge), row)],
        out_shape=[jax.ShapeDtypeStruct((n, d), F32), jax.ShapeDtypeStruct((n, d), BF16),
                   jax.ShapeDtypeStruct((n, ge), F32)],
        compiler_params=pltpu.CompilerParams(
            dimension_semantics=("arbitrary", "arbitrary"), vmem_limit_bytes=VMEM_LIMIT),
        name="post_attention",
    )(msb, mdf, x, wo_sb, wo_df, g2, rw_hi, rw_lo, rb)


def _silu(x):
    return x / (1.0 + jnp.exp(-x))


def _moe_body(h2_ref, x1_ref, gates_ref, wg_ref, wu_ref, wd_ref, sg_ref, su_ref, sd_ref,
              o_ref, acc_ref, *, group):
    eg = pl.program_id(1)
    h2 = h2_ref[...]

    @pl.when(eg == 0)
    def _():
        a = _silu(_dot(h2, sg_ref[...])) * _dot(h2, su_ref[...])
        acc_ref[...] = _dot(a.astype(BF16), sd_ref[...])

    f = wg_ref.shape[2]
    gates = gates_ref[...]
    lane = lax.broadcasted_iota(I32, gates.shape, 1)
    acts = []
    for j in range(group):
        gate = jnp.sum(jnp.where(lane == eg * group + j, gates, 0.0), axis=-1, keepdims=True)
        acts.append((_silu(_dot(h2, wg_ref[j])) * _dot(h2, wu_ref[j]) * gate).astype(BF16))
    a = jnp.concatenate(acts, axis=1)
    wd = wd_ref[...].reshape(group * f, wd_ref.shape[2])
    acc_ref[...] += _dot(a, wd)

    @pl.when(eg == pl.num_programs(1) - 1)
    def _():
        o_ref[...] = x1_ref[...] + acc_ref[...]


def _moe(h2, x1, gates, w_gate, w_up, w_down, sg, su, sd, tm, group):
    n, d = x1.shape
    n_experts, _, f = w_gate.shape
    row = lambda i, e: (i, 0)
    const = lambda i, e: (0, 0)
    return pl.pallas_call(
        functools.partial(_moe_body, group=group),
        grid=(n // tm, n_experts // group),
        in_specs=[
            pl.BlockSpec((tm, d), row),
            pl.BlockSpec((tm, d), row),
            pl.BlockSpec((tm, gates.shape[1]), row),
            pl.BlockSpec((group, d, f), lambda i, e: (e, 0, 0)),
            pl.BlockSpec((group, d, f), lambda i, e: (e, 0, 0)),
            pl.BlockSpec((group, f, d), lambda i, e: (e, 0, 0)),
            pl.BlockSpec(sg.shape, const),
            pl.BlockSpec(su.shape, const),
            pl.BlockSpec(sd.shape, const),
        ],
        out_specs=pl.BlockSpec((tm, d), row),
        out_shape=jax.ShapeDtypeStruct((n, d), F32),
        scratch_shapes=[pltpu.VMEM((tm, d), F32)],
        compiler_params=pltpu.CompilerParams(
            dimension_semantics=("arbitrary", "arbitrary"), vmem_limit_bytes=VMEM_LIMIT),
        name="moe",
    )(h2, x1, gates, w_gate, w_up, w_down, sg, su, sd)


def _largest_tile(n, cap):
    t = cap
    while n % t:
        t //= 2
    return t


def kernel(x_prompt, x_sample, cache_k_sb, cache_v_sb, cache_k_diff, cache_v_diff, page_table,
           meta_tokens, rel_bias, norm1_g, w_in, q_norm_g, k_norm_g, lambda_q1, lambda_k1,
           lambda_q2, lambda_k2, sb_out_g, diff_subln_g, w_out, norm2_g, router_w, router_bias,
           w_gate, w_up, w_down, w_shared_gate, w_shared_up, w_shared_down):
    assert w_in.shape[0] == 1, "single-layer step"
    b, seq, d = x_prompt.shape
    dec_b, dec_t, _ = x_sample.shape
    n_meta = meta_tokens.shape[0]
    width = sb_out_g.shape[1]
    n_pool, page = cache_k_sb.shape[1], cache_k_sb.shape[2]
    n_experts = router_w.shape[2]
    t = seq + n_meta
    blk = ATT_BLOCK
    lam_init = 0.8 - 0.6 * math.exp(-0.3 * 0)

    w_in_b = w_in[0].astype(BF16)
    g1 = norm1_g
    reps = width // DH
    qg = jnp.tile(q_norm_g, (1, reps))
    kg = jnp.tile(k_norm_g, (1, reps))
    lane = jnp.arange(width)
    gm = ((lane[:, None] // DH) == (lane[None, :] // DH)).astype(BF16) * (1.0 / DH)
    tri_p = (jnp.arange(blk)[:, None] > jnp.arange(blk)[None, :]).astype(BF16)
    tri_d = (jnp.arange(page)[:, None] > jnp.arange(page)[None, :]).astype(BF16)
    lams = (lambda_q1, lambda_k1, lambda_q2, lambda_k2)
    wo_sb = w_out[0, :width].astype(BF16)
    wo_df = w_out[0, width:].astype(BF16)
    rw_t = router_w[0].T
    rw_hi = rw_t.astype(BF16)
    rw_lo = (rw_t - rw_hi.astype(F32)).astype(BF16)
    rb = router_bias[0][:, None]
    sg = w_shared_gate[0].astype(BF16)
    su = w_shared_up[0].astype(BF16)
    sd = w_shared_down[0].astype(BF16)

    n_h_sb = width // DH
    n_h_df = width // LANES
    n_slots = DECODE_PAGES_PER_STEP
    assert blk >= MAX_DISTANCE and page >= MAX_DISTANCE
    assert page_table.shape[1] % n_slots == 0 and dec_t * n_h_df <= page
    pre = blk - n_meta
    assert 0 <= pre < blk and seq % blk == 0
    bias_p, bias_s, bias_n = _bias_tiles(rel_bias, blk, pre, dec_t, page, n_slots)
    head = jnp.concatenate([jnp.zeros((pre, d), F32), meta_tokens], axis=0)
    (k_sb_p, v_sb_p, k_df_p, v_df_p,
     qsb_b, ksb_b, vsb_b, qdf_b, kdf_b, vdf_b) = _project_prompt(head, x_prompt, n_meta, g1, w_in_b, qg, kg, gm)
    msb_p = _sb_prompt(qsb_b, ksb_b, vsb_b, tri_p, sb_out_g, blk, pre)
    mdf_p = _df_prompt(rel_bias, qdf_b, kdf_b, vdf_b, bias_p, diff_subln_g, lams, blk, pre, lam_init)

    n_s = dec_b * dec_t
    xs = x_sample.reshape(1, n_s, d)
    (k_sb_s, v_sb_s, k_df_s, v_df_s,
     qsb_s, _, _, qdf_s, _, _) = _project(xs, n_s, g1, w_in_b, qg, kg, gm, n_s)

    def new_page(rows, n_rows, n_pad):
        r = rows.reshape(dec_b, n_rows, -1)
        return jnp.pad(r, ((0, 0), (0, n_pad - n_rows), (0, 0)))

    transposed_pages = lambda c: jnp.transpose(c[0], (0, 2, 3, 1)).reshape(n_pool, width, page)
    interleaved_rows = lambda c: c.reshape(n_pool, page * n_h_df, LANES)
    msb_s = _sb_decode(page_table, qsb_s.reshape(dec_b, dec_t, width),
                       new_page(k_sb_s, dec_t, page), new_page(v_sb_s, dec_t, page),
                       transposed_pages(cache_k_sb), transposed_pages(cache_v_sb), tri_d, sb_out_g)
    mdf_s = _df_decode(page_table, qdf_s.reshape(dec_b, dec_t, width),
                       new_page(k_df_s, dec_t * n_h_df, page * n_h_df),
                       new_page(v_df_s, dec_t * n_h_df, page * n_h_df),
                       interleaved_rows(cache_k_diff), interleaved_rows(cache_v_diff),
                       bias_s, bias_n, diff_subln_g, lams, lam_init, n_slots)

    wg_b, wu_b, wd_b = w_gate[0].astype(BF16), w_up[0].astype(BF16), w_down[0].astype(BF16)
    x1_p, h2_p, gates_p = _post_attention(msb_p, mdf_p, x_prompt, wo_sb, wo_df, norm2_g,
                                          rw_hi, rw_lo, rb, _largest_tile(seq, 512))
    y_p = _moe(h2_p, x1_p, gates_p, wg_b, wu_b, wd_b, sg, su, sd,
               _largest_tile(b * seq, MOE_TOKEN_TILE), MOE_EXPERT_GROUP)
    y_prompt = y_p.reshape(b, seq, d)
    x1_s, h2_s, gates_s = _post_attention(
        msb_s.reshape(1, n_s, width), mdf_s.reshape(1, n_s, width), x_sample.reshape(1, n_s, d),
        wo_sb, wo_df, norm2_g, rw_hi, rw_lo, rb, n_s)
    y_s = _moe(h2_s, x1_s, gates_s, wg_b, wu_b, wd_b, sg, su, sd, n_s, MOE_EXPERT_GROUP)
    y_sample = y_s.reshape(dec_b, dec_t, d)

    sb_rows = lambda r, bb, tt: r.reshape(1, bb, tt, n_h_sb, DH)
    df_rows = lambda r, bb, tt: r.reshape(1, bb, tt, n_h_df, LANES)
    return (y_prompt, y_sample,
            sb_rows(k_sb_p, b, t), sb_rows(v_sb_p, b, t), df_rows(k_df_p, b, t), df_rows(v_df_p, b, t),
            sb_rows(k_sb_s, dec_b, dec_t), sb_rows(v_sb_s, dec_b, dec_t),
            df_rows(k_df_s, dec_b, dec_t), df_rows(v_df_s, dec_b, dec_t))
```
